```python
import jax
import jax.numpy as jnp
from jax import lax
import numpy as np

D_MODEL = 1024
BATCH = 8
SEQ = 2048
DEPTH = 4
DEC_BATCH = 128
DEC_SEQ = 4
PAST_LEN = 8192
PAGE_SIZE = 128

D_MIX = D_MODEL
HEAD_DIM = 64
W_A = D_MIX // 4
W_B = D_MIX // 4
W_C = D_MIX // 4
W_D = D_MIX - W_A - W_B - W_C
N_Q = W_A // HEAD_DIM
N_KV = N_Q // 2
GQA_G = N_Q // N_KV
WINDOW = 128
Q_BLOCK = 128
N_HG = W_B // HEAD_DIM
HG_K = HEAD_DIM
HG_V = HEAD_DIM
HG_CHUNK = 64
LB_FLOOR = 1e-30
CONV_W = 3
POOL_WINDOWS = (2, 4, 8, 16)
N_POOL = 4
POOL_GW = W_D // N_POOL
POOL_BUF = 15
N_MEM = 256
N_XH = 4
X_HD = 64
D_X = N_XH * X_HD
D_FF = ((8 * D_MODEL + 3 * 256 - 1) // (3 * 256)) * 256
ALPHA = (2 * DEPTH) ** 0.25
BETA = (8 * DEPTH) ** -0.25
LN_EPS = 1e-5
RMS_EPS = 1e-6
MASK_VALUE = -1e30
SPLIT_SIZES = (W_A, N_KV * HEAD_DIM, N_KV * HEAD_DIM, W_B, W_B, W_B, W_B, W_C, W_C, W_C, W_D)
D_IN = W_A + 2 * N_KV * HEAD_DIM + 4 * W_B + 3 * W_C + W_D

kernel_name = 'hybrid_parallel_heads_decode_step'


def split_points():
    pts, acc = [], 0
    for s in SPLIT_SIZES[:-1]:
        acc += s
        pts.append(acc)
    return pts


def layer_norm(x, g, b):
    xf = x.astype(jnp.float32)
    mu = xf.mean(-1, keepdims=True)
    var = jnp.square(xf - mu).mean(-1, keepdims=True)
    return ((xf - mu) * lax.rsqrt(var + LN_EPS) * g + b).astype(x.dtype)


def alibi_slopes():
    h = jnp.arange(N_Q, dtype=jnp.float32) + 1.0
    return jnp.exp2(-8.0 * h / N_Q).reshape(N_KV, GQA_G)


def sliding_window_attention(q, k, v, k_buf, v_buf, pos0, sink, slopes):
    B, T = q.shape[:2]
    k_ext = jnp.concatenate([k_buf.astype(k.dtype), k], axis=1)
    v_ext = jnp.concatenate([v_buf.astype(v.dtype), v], axis=1)
    qb_len = Q_BLOCK if T % Q_BLOCK == 0 else T
    nb = T // qb_len
    idx = (jnp.arange(nb) * qb_len)[:, None] + jnp.arange(qb_len + WINDOW)[None, :]
    kb = k_ext[:, idx]
    vb = v_ext[:, idx]
    qb = q.reshape(B, nb, qb_len, N_KV, GQA_G, HEAD_DIM)
    s = jnp.einsum('bnqkgd,bnskd->bnkgqs', qb, kb).astype(jnp.float32) * (HEAD_DIM ** -0.5)
    k_pos = pos0 - WINDOW + idx
    q_pos = pos0 + jnp.arange(T).reshape(nb, qb_len)
    rel = q_pos[:, :, None] - k_pos[:, None, :]
    valid = (rel >= 0) & (rel <= WINDOW) & (k_pos[:, None, :] >= 0)
    s = s - slopes[None, None, :, :, None, None] * rel[None, :, None, None].astype(jnp.float32)
    s = jnp.where(valid[None, :, None, None], s, MASK_VALUE)
    sk = sink.astype(jnp.float32).reshape(N_KV, GQA_G)[None, None, :, :, None]
    m = jnp.maximum(jnp.max(s, axis=-1), sk)
    p = jnp.exp(s - m[..., None])
    denom = p.sum(-1) + jnp.exp(sk - m)
    p = p / denom[..., None]
    o = jnp.einsum('bnkgqs,bnskd->bnqkgd', p.astype(v.dtype), vb).reshape(B, T, N_Q * HEAD_DIM)
    return o, k_ext[:, -WINDOW:], v_ext[:, -WINDOW:]


def hgrn_lower_bounds(lb_param):
    p = jax.nn.softmax(lb_param.astype(jnp.float32), axis=0)
    return jnp.cumsum(p, axis=0) - p[0:1]


def hgrn2(q, f_pre, i, g, s0, lb, norm_g):
    B, T, _ = q.shape
    f32 = jnp.float32
    qh = jax.nn.silu(q.astype(f32)).reshape(B, T, N_HG, HG_K)
    fp = f_pre.astype(f32).reshape(B, T, N_HG, HG_K)
    lbh = lb.astype(f32).reshape(N_HG, HG_K)
    log_f = jnp.logaddexp(jax.nn.log_sigmoid(fp), jnp.log(jnp.maximum(lbh, LB_FLOOR)) + jax.nn.log_sigmoid(-fp))
    kh = (1.0 - lbh) * jax.nn.sigmoid(-fp)
    vh = i.astype(f32).reshape(B, T, N_HG, HG_V)
    c = HG_CHUNK if T % HG_CHUNK == 0 else T
    nc = T // c

    def to_chunks(a):
        return a.reshape(B, nc, c, N_HG, a.shape[-1]).swapaxes(0, 1)

    tril = jnp.tril(jnp.ones((c, c), dtype=bool))[None, :, :, None, None]

    def step(S, inp):
        qc, kc, vc, lfc = inp
        cum = jnp.cumsum(lfc, axis=1)
        o = jnp.einsum('bthk,bhkv->bthv', qc * jnp.exp(cum), S)
        diff = cum[:, :, None] - cum[:, None, :]
        decay = jnp.where(tril, jnp.exp(jnp.where(tril, diff, 0.0)), 0.0)
        a = jnp.einsum('bthk,bshk,btshk->bhts', qc, kc, decay)
        o = o + jnp.einsum('bhts,bshv->bthv', a, vc)
        last = cum[:, -1]
        S = jnp.exp(last)[..., None] * S + jnp.einsum('bshk,bshv->bhkv', kc * jnp.exp(last[:, None] - cum), vc)
        return S, o

    S, o = lax.scan(step, s0.astype(f32), (to_chunks(qh), to_chunks(kh), to_chunks(vh), to_chunks(log_f)))
    o = o.swapaxes(0, 1).reshape(B, T, N_HG, HG_V)
    o = o * lax.rsqrt(jnp.mean(jnp.square(o), axis=-1, keepdims=True) + RMS_EPS) * norm_g.astype(f32).reshape(N_HG, HG_V)
    o = o.reshape(B, T, W_B) * jax.nn.silu(g.astype(f32))
    return o.astype(q.dtype), S.astype(s0.dtype)


def short_conv(b_gate, c_gate, h, buf, conv_w):
    T = h.shape[1]
    u = c_gate * h
    ext = jnp.concatenate([buf.astype(u.dtype), u], axis=1)
    y = ext[:, 0:T] * conv_w[0]
    for j in range(1, CONV_W):
        y = y + ext[:, j:j + T] * conv_w[j]
    return b_gate * y, ext[:, -(CONV_W - 1):]


def pool_mixer(v, buf, pos0, pool_w, pool_scale):
    B, T, _ = v.shape
    ext_raw = jnp.concatenate([buf.astype(v.dtype), v], axis=1)
    ext = ext_raw.astype(jnp.float32)
    cs = jnp.concatenate([jnp.zeros((B, 1, W_D), jnp.float32), jnp.cumsum(ext, axis=1)], axis=1)
    pos = pos0 + jnp.arange(T)
    outs = []
    for gi, w in enumerate(POOL_WINDOWS):
        lo, hi = gi * POOL_GW, (gi + 1) * POOL_GW
        win_sum = cs[:, POOL_BUF + 1:POOL_BUF + 1 + T, lo:hi] - cs[:, POOL_BUF + 1 - w:POOL_BUF + 1 - w + T, lo:hi]
        cnt = jnp.minimum(pos + 1, w).astype(jnp.float32)
        outs.append(win_sum / cnt[None, :, None])
    pooled = (jnp.concatenate(outs, axis=-1) - v.astype(jnp.float32)).reshape(B, T, N_POOL, POOL_GW)
    y = jnp.einsum('btgc,gcd->btgd', pooled, pool_w.astype(jnp.float32)).reshape(B, T, W_D) * pool_scale
    return y.astype(v.dtype), ext_raw[:, -POOL_BUF:]


def memory_attention(x, mem_k, mem_v, w_xq, w_xo):
    B, T, _ = x.shape
    q = (x @ w_xq).reshape(B, T, N_XH, X_HD)
    s = jnp.einsum('bthd,bmhd->bhtm', q, mem_k).astype(jnp.float32) * (X_HD ** -0.5)
    p = jax.nn.softmax(s, axis=-1)
    o = jnp.einsum('bhtm,bmhd->bthd', p.astype(mem_v.dtype), mem_v).reshape(B, T, D_X)
    return o @ w_xo


def decoder_layer(x, k_buf, v_buf, s_hgrn, conv_buf, pool_buf, mem_k, mem_v, pos0,
                  w_in, attn_sink, lb, hgrn_norm_g, conv_w, pool_w, pool_scale, w_o,
                  ln1_g, ln1_b, w_xq, w_xo, ln2_g, ln2_b, w_gate, w_up, w_down, ln3_g, ln3_b):
    B, T, _ = x.shape
    proj = x @ w_in
    a_q, a_k, a_v, b_q, b_f, b_i, b_g, c_b, c_c, c_h, d_v = jnp.split(proj, split_points(), axis=-1)
    o_a, k_new, v_new = sliding_window_attention(
        a_q.reshape(B, T, N_Q, HEAD_DIM), a_k.reshape(B, T, N_KV, HEAD_DIM), a_v.reshape(B, T, N_KV, HEAD_DIM),
        k_buf, v_buf, pos0, attn_sink, alibi_slopes())
    o_b, s_new = hgrn2(b_q, b_f, b_i, b_g, s_hgrn, lb, hgrn_norm_g)
    o_c, conv_new = short_conv(c_b, c_c, c_h, conv_buf, conv_w)
    o_d, pool_new = pool_mixer(d_v, pool_buf, pos0, pool_w, pool_scale)
    mix = jnp.concatenate([o_a, o_b, o_c, o_d], axis=-1) @ w_o
    x = layer_norm(ALPHA * x + mix, ln1_g, ln1_b)
    x = layer_norm(ALPHA * x + memory_attention(x, mem_k, mem_v, w_xq, w_xo), ln2_g, ln2_b)
    ffn = (jax.nn.silu(x @ w_gate) * (x @ w_up)) @ w_down
    x = layer_norm(ALPHA * x + ffn, ln3_g, ln3_b)
    return x, k_new, v_new, s_new, conv_new, pool_new


def setup_inputs(seed: int = 0) -> dict:
    key = jax.random.key(seed)
    ks = iter(jax.random.split(key, 64))
    f32 = jnp.float32

    def nrm(shape, scale=1.0):
        return jax.random.normal(next(ks), shape, f32) * scale

    def gain(shape):
        return 1.0 + nrm(shape, 0.05)

    return {
        'x_prompt': nrm((BATCH, SEQ, D_MODEL)),
        'x_sample': nrm((DEC_BATCH, DEC_SEQ, D_MODEL)),
        'cache_swa_k': nrm((DEPTH, DEC_BATCH, WINDOW, N_KV, HEAD_DIM)),
        'cache_swa_v': nrm((DEPTH, DEC_BATCH, WINDOW, N_KV, HEAD_DIM)),
        'state_hgrn': nrm((DEPTH, DEC_BATCH, N_HG, HG_K, HG_V), 0.5),
        'state_conv': nrm((DEPTH, DEC_BATCH, CONV_W - 1, W_C)),
        'state_pool': nrm((DEPTH, DEC_BATCH, POOL_BUF, W_D)),
        'cache_mem_k': nrm((DEPTH, DEC_BATCH, N_MEM, N_XH, X_HD)),
        'cache_mem_v': nrm((DEPTH, DEC_BATCH, N_MEM, N_XH, X_HD)),
        'mem_prompt': nrm((BATCH, N_MEM, D_MODEL)),
        'emb_ln_g': gain((D_MODEL,)),
        'emb_ln_b': nrm((D_MODEL,), 0.05),
        'w_in': nrm((DEPTH, D_MODEL, D_IN), D_MODEL ** -0.5),
        'attn_sink': nrm((DEPTH, N_Q), 0.5),
        'hgrn_lb': nrm((DEPTH, W_B), 0.5),
        'hgrn_norm_g': gain((DEPTH, W_B)),
        'conv_w': nrm((DEPTH, CONV_W, W_C), CONV_W ** -0.5),
        'pool_w': nrm((DEPTH, N_POOL, POOL_GW, POOL_GW), POOL_GW ** -0.5),
        'pool_scale': 1.0 + nrm((DEPTH, W_D), 0.1),
        'w_o': nrm((DEPTH, D_MIX, D_MODEL), BETA * D_MIX ** -0.5),
        'ln1_g': gain((DEPTH, D_MODEL)),
        'ln1_b': nrm((DEPTH, D_MODEL), 0.05),
        'w_xq': nrm((DEPTH, D_MODEL, D_X), D_MODEL ** -0.5),
        'w_xk': nrm((DEPTH, D_MODEL, D_X), D_MODEL ** -0.5),
        'w_xv': nrm((DEPTH, D_MODEL, D_X), D_MODEL ** -0.5),
        'w_xo': nrm((DEPTH, D_X, D_MODEL), BETA * D_X ** -0.5),
        'ln2_g': gain((DEPTH, D_MODEL)),
        'ln2_b': nrm((DEPTH, D_MODEL), 0.05),
        'w_gate': nrm((DEPTH, D_MODEL, D_FF), D_MODEL ** -0.5),
        'w_up': nrm((DEPTH, D_MODEL, D_FF), D_MODEL ** -0.5),
        'w_down': nrm((DEPTH, D_FF, D_MODEL), BETA * D_FF ** -0.5),
        'ln3_g': gain((DEPTH, D_MODEL)),
        'ln3_b': nrm((DEPTH, D_MODEL), 0.05),
    }


def reference(x_prompt, x_sample, cache_swa_k, cache_swa_v, state_hgrn, state_conv, state_pool,
              cache_mem_k, cache_mem_v, mem_prompt, emb_ln_g, emb_ln_b, w_in, attn_sink, hgrn_lb,
              hgrn_norm_g, conv_w, pool_w, pool_scale, w_o, ln1_g, ln1_b, w_xq, w_xk, w_xv, w_xo,
              ln2_g, ln2_b, w_gate, w_up, w_down, ln3_g, ln3_b):
    lb_all = hgrn_lower_bounds(hgrn_lb)
    hp = layer_norm(x_prompt, emb_ln_g, emb_ln_b)
    hs = layer_norm(x_sample, emb_ln_g, emb_ln_b)
    bp = x_prompt.shape[0]
    dt = x_prompt.dtype
    z_kv = jnp.zeros((bp, WINDOW, N_KV, HEAD_DIM), dt)
    z_s = jnp.zeros((bp, N_HG, HG_K, HG_V), jnp.float32)
    z_c = jnp.zeros((bp, CONV_W - 1, W_C), dt)
    z_p = jnp.zeros((bp, POOL_BUF, W_D), dt)
    pk, pv, ps, pc, pp, pmk, pmv = [], [], [], [], [], [], []
    sk, sv, ss, sc, sp = [], [], [], [], []
    for l in range(DEPTH):
        mk = (mem_prompt @ w_xk[l]).reshape(bp, N_MEM, N_XH, X_HD)
        mv = (mem_prompt @ w_xv[l]).reshape(bp, N_MEM, N_XH, X_HD)
        lw = (w_in[l], attn_sink[l], lb_all[l], hgrn_norm_g[l], conv_w[l], pool_w[l], pool_scale[l], w_o[l],
              ln1_g[l], ln1_b[l], w_xq[l], w_xo[l], ln2_g[l], ln2_b[l], w_gate[l], w_up[l], w_down[l],
              ln3_g[l], ln3_b[l])
        hp, k1, v1, s1, c1, p1 = decoder_layer(hp, z_kv, z_kv, z_s, z_c, z_p, mk, mv, 0, *lw)
        hs, k2, v2, s2, c2, p2 = decoder_layer(hs, cache_swa_k[l], cache_swa_v[l], state_hgrn[l], state_conv[l],
                                               state_pool[l], cache_mem_k[l], cache_mem_v[l], PAST_LEN, *lw)
        pk.append(k1); pv.append(v1); ps.append(s1); pc.append(c1); pp.append(p1); pmk.append(mk); pmv.append(mv)
        sk.append(k2); sv.append(v2); ss.append(s2); sc.append(c2); sp.append(p2)
    return (hp, hs,
            jnp.stack(pk), jnp.stack(pv), jnp.stack(ps), jnp.stack(pc), jnp.stack(pp), jnp.stack(pmk), jnp.stack(pmv),
            jnp.stack(sk), jnp.stack(sv), jnp.stack(ss), jnp.stack(sc), jnp.stack(sp))
```

```python
import functools

import jax
import jax.numpy as jnp
from jax import lax
from jax.experimental import pallas as pl
from jax.experimental.pallas import tpu as pltpu

F32 = jnp.float32
BF16 = jnp.bfloat16

D_MODEL = 1024
DEPTH = 4
HEAD_DIM = 64
W_GRP = 256
N_KV = 2
WINDOW = 128
N_HG = 4
CONV_W = 3
POOL_WINDOWS = (2, 4, 8, 16)
POOL_BUF = 15
N_MEM = 256
N_XH = 4
D_X = 256
D_FF = 2816
D_IN = 2560
ALPHA = (2 * DEPTH) ** 0.25
LN_EPS = 1e-5
RMS_EPS = 1e-6
MASK_VALUE = -1e30
LB_FLOOR = 1e-30
PAST_LEN = 8192

C_AQ, C_AK, C_AV = 0, 256, 384
C_BQ, C_BF, C_BI, C_BG = 512, 768, 1024, 1280
C_CB, C_CC, C_CH = 1536, 1792, 2048
C_DV = 2304

LANES = 128
SUBLANES = 8
VMEM_LIMIT_BYTES = 56 * 1024 * 1024

TOKEN_BLOCK = 512
HG_CHUNK = 64
HG_MID = HG_CHUNK // 2 - 1


def _nt_dot(a, b):
    return lax.dot_general(a, b, (((1,), (1,)), ((), ())), preferred_element_type=F32)


def _tn_dot(a, b):
    return lax.dot_general(a, b, (((0,), (0,)), ((), ())), preferred_element_type=F32)


def _dot(a, b):
    return jnp.dot(a, b, preferred_element_type=F32)


def _layer_norm(x, g, b):
    mu = jnp.mean(x, axis=-1, keepdims=True)
    xc = x - mu
    var = jnp.mean(xc * xc, axis=-1, keepdims=True)
    return xc * lax.rsqrt(var + LN_EPS) * g + b


def _sigmoid_pair(z):
    e = jnp.exp(-jnp.abs(z))
    inv = 1.0 / (1.0 + e)
    small = e * inv
    pos = z >= 0
    return jnp.where(pos, inv, small), jnp.where(pos, small, inv)


def _silu(z):
    s, _ = _sigmoid_pair(z)
    return z * s


def _full_spec(shape):
    nd = len(shape)
    return pl.BlockSpec(shape, lambda *_: (0,) * nd)


def _compiler_params(sem):
    return pltpu.CompilerParams(dimension_semantics=sem, vmem_limit_bytes=VMEM_LIMIT_BYTES)


def _ln_kernel(x_ref, g_ref, b_ref, o_ref):
    o_ref[...] = _layer_norm(x_ref[...], g_ref[...], b_ref[...])


def _ln_call(x2d, g, b):
    m = x2d.shape[0]
    return pl.pallas_call(
        _ln_kernel,
        grid=(m // TOKEN_BLOCK,),
        in_specs=[pl.BlockSpec((TOKEN_BLOCK, D_MODEL), lambda i: (i, 0)),
                  _full_spec((1, D_MODEL)), _full_spec((1, D_MODEL))],
        out_specs=pl.BlockSpec((TOKEN_BLOCK, D_MODEL), lambda i: (i, 0)),
        out_shape=jax.ShapeDtypeStruct((m, D_MODEL), F32),
        compiler_params=_compiler_params(("arbitrary",)),
    )(x2d, g, b)


def _matmul_kernel(x_ref, w_ref, o_ref):
    o_ref[...] = _dot(x_ref[...].astype(BF16), w_ref[...])


def _matmul_call(x2d, w):
    m, k = x2d.shape
    n = w.shape[1]
    return pl.pallas_call(
        _matmul_kernel,
        grid=(m // TOKEN_BLOCK,),
        in_specs=[pl.BlockSpec((TOKEN_BLOCK, k), lambda i: (i, 0)), _full_spec((k, n))],
        out_specs=pl.BlockSpec((TOKEN_BLOCK, n), lambda i: (i, 0)),
        out_shape=jax.ShapeDtypeStruct((m, n), F32),
        compiler_params=_compiler_params(("arbitrary",)),
    )(x2d, w)


def _ffn_kernel(x_ref, wg_ref, wu_ref, wd_ref, g_ref, b_ref, o_ref):
    x = x_ref[...]
    xb = x.astype(BF16)
    h = _silu(_dot(xb, wg_ref[...])) * _dot(xb, wu_ref[...])
    y = _dot(h.astype(BF16), wd_ref[...])
    o_ref[...] = _layer_norm(ALPHA * x + y, g_ref[...], b_ref[...])


def _ffn_call(x2d, wg, wu, wd, g, b):
    m = x2d.shape[0]
    single = pl.Buffered(1)
    return pl.pallas_call(
        _ffn_kernel,
        grid=(m // TOKEN_BLOCK,),
        in_specs=[pl.BlockSpec((TOKEN_BLOCK, D_MODEL), lambda i: (i, 0)),
                  pl.BlockSpec((D_MODEL, D_FF), lambda i: (0, 0), pipeline_mode=single),
                  pl.BlockSpec((D_MODEL, D_FF), lambda i: (0, 0), pipeline_mode=single),
                  pl.BlockSpec((D_FF, D_MODEL), lambda i: (0, 0), pipeline_mode=single),
                  _full_spec((1, D_MODEL)), _full_spec((1, D_MODEL))],
        out_specs=pl.BlockSpec((TOKEN_BLOCK, D_MODEL), lambda i: (i, 0)),
        out_shape=jax.ShapeDtypeStruct((m, D_MODEL), F32),
        compiler_params=_compiler_params(("arbitrary",)),
    )(x2d, wg, wu, wd, g, b)


def _xattn_prompt_kernel(x_ref, mk_ref, mv_ref, wq_ref, wo_ref, g_ref, b_ref, o_ref):
    x = x_ref[...]
    q = _dot(x.astype(BF16), wq_ref[...]) * (HEAD_DIM ** -0.5)
    mk = mk_ref[...].astype(BF16)
    mv = mv_ref[...].astype(BF16)
    outs = []
    for h in range(N_XH):
        sl = slice(h * HEAD_DIM, (h + 1) * HEAD_DIM)
        s = _nt_dot(q[:, sl].astype(BF16), mk[:, sl])
        m = jnp.max(s, axis=-1, keepdims=True)
        p = jnp.exp(s - m)
        den = jnp.sum(p, axis=-1, keepdims=True)
        outs.append(_dot(p.astype(BF16), mv[:, sl]) / den)
    o = jnp.concatenate(outs, axis=-1)
    y = _dot(o.astype(BF16), wo_ref[...])
    o_ref[...] = _layer_norm(ALPHA * x + y, g_ref[...], b_ref[...])


def _xattn_prompt_call(x, mk, mv, wq, wo, g, b):
    bsz, t, _ = x.shape
    tb = min(TOKEN_BLOCK, t)
    return pl.pallas_call(
        _xattn_prompt_kernel,
        grid=(bsz, t // tb),
        in_specs=[pl.BlockSpec((None, tb, D_MODEL), lambda i, j: (i, j, 0)),
                  pl.BlockSpec((None, N_MEM, D_X), lambda i, j: (i, 0, 0)),
                  pl.BlockSpec((None, N_MEM, D_X), lambda i, j: (i, 0, 0)),
                  _full_spec((D_MODEL, D_X)), _full_spec((D_X, D_MODEL)),
                  _full_spec((1, D_MODEL)), _full_spec((1, D_MODEL))],
        out_specs=pl.BlockSpec((None, tb, D_MODEL), lambda i, j: (i, j, 0)),
        out_shape=jax.ShapeDtypeStruct((bsz, t, D_MODEL), F32),
        compiler_params=_compiler_params(("arbitrary", "arbitrary")),
    )(x, mk, mv, wq, wo, g, b)


def _swa_bias_table():
    r = lax.broadcasted_iota(jnp.int32, (4 * WINDOW, 2 * WINDOW), 0)
    c = lax.broadcasted_iota(jnp.int32, (4 * WINDOW, 2 * WINDOW), 1)
    head = r >> 7
    rel = (r & (WINDOW - 1)) + WINDOW - c
    slope = jnp.exp2(-2.0 * (head.astype(F32) + 1.0))
    valid = (rel >= 0) & (rel <= WINDOW)
    return jnp.where(valid, -slope * rel.astype(F32), MASK_VALUE)


def _prompt_mixer_kernel(x_ref, w_in_ref, sink_ref, lb_ref, ng_ref, cw_ref, pw_ref, ps_ref, w_o_ref,
                         g_ref, b_ref,
                         y_ref, knew_ref, vnew_ref, snew_ref, cnew_ref, pnew_ref,
                         proj_scr, kext_scr, vext_scr, st_scr, u_scr, p_scr, bias_scr, mix_scr, hg_scr):
    tb = x_ref.shape[0]
    n_qb = tb // WINDOW
    n_ch = tb // HG_CHUNK
    bi = pl.program_id(0)
    ti = pl.program_id(1)
    last = ti == pl.num_programs(1) - 1

    @pl.when((bi == 0) & (ti == 0))
    def _():
        bias_scr[...] = _swa_bias_table()

    @pl.when(ti == 0)
    def _():
        kext_scr[0:WINDOW, :] = jnp.zeros((WINDOW, LANES), BF16)
        vext_scr[0:WINDOW, :] = jnp.zeros((WINDOW, LANES), BF16)
        st_scr[...] = jnp.zeros(st_scr.shape, F32)
        u_scr[0:SUBLANES, :] = jnp.zeros((SUBLANES, W_GRP), F32)
        p_scr[0:16, :] = jnp.zeros((16, W_GRP), F32)

    x = x_ref[...]
    proj_scr[...] = _dot(x.astype(BF16), w_in_ref[...])

    kext_scr[WINDOW:WINDOW + tb, :] = proj_scr[:, C_AK:C_AK + LANES].astype(BF16)
    vext_scr[WINDOW:WINDOW + tb, :] = proj_scr[:, C_AV:C_AV + LANES].astype(BF16)
    lane = lax.broadcasted_iota(jnp.int32, (WINDOW, LANES), 1)
    lo = lane < HEAD_DIM
    col = lax.broadcasted_iota(jnp.int32, (WINDOW, 2 * WINDOW), 1)
    for j in range(n_qb):
        rows = slice(j * WINDOW, (j + 1) * WINDOW)
        q0 = proj_scr[rows, 0:LANES] * (HEAD_DIM ** -0.5)
        q1 = proj_scr[rows, LANES:2 * LANES] * (HEAD_DIM ** -0.5)
        q0r = pltpu.roll(q0, HEAD_DIM, axis=1)
        q1r = pltpu.roll(q1, HEAD_DIM, axis=1)
        zero = jnp.zeros_like(q0)
        q4 = jnp.concatenate([jnp.where(lo, q0, zero), jnp.where(lo, q0r, zero),
                              jnp.where(lo, zero, q1r), jnp.where(lo, zero, q1)], axis=0).astype(BF16)
        kj = kext_scr[j * WINDOW:(j + 2) * WINDOW, :]
        vj = vext_scr[j * WINDOW:(j + 2) * WINDOW, :]
        s_all = _nt_dot(q4, kj)
        ps, dens = [], []
        for h in range(4):
            s = s_all[h * WINDOW:(h + 1) * WINDOW, :] + bias_scr[h * WINDOW:(h + 1) * WINDOW, :]
            if j == 0:
                s = jnp.where((ti == 0) & (col < WINDOW), MASK_VALUE, s)
            sink = sink_ref[h]
            m = jnp.maximum(jnp.max(s, axis=-1, keepdims=True), sink)
            p = jnp.exp(s - m)
            dens.append(jnp.sum(p, axis=-1, keepdims=True) + jnp.exp(sink - m))
            ps.append(p.astype(BF16))
        o_all = _dot(jnp.concatenate(ps, axis=0), vj)
        o = [o_all[h * WINDOW:(h + 1) * WINDOW, :] / dens[h] for h in range(4)]
        c0 = jnp.where(lo, o[0], pltpu.roll(o[1], HEAD_DIM, axis=1))
        c1 = jnp.where(lo, pltpu.roll(o[2], HEAD_DIM, axis=1), o[3])
        mix_scr[rows, 0:LANES] = c0.astype(BF16)
        mix_scr[rows, LANES:2 * LANES] = c1.astype(BF16)
    kext_scr[0:WINDOW, :] = kext_scr[tb:tb + WINDOW, :]
    vext_scr[0:WINDOW, :] = vext_scr[tb:tb + WINDOW, :]

    @pl.when(last)
    def _():
        knew_ref[...] = proj_scr[tb - WINDOW:tb, C_AK:C_AK + LANES]
        vnew_ref[...] = proj_scr[tb - WINDOW:tb, C_AV:C_AV + LANES]

    lb = lb_ref[...]
    lbf = jnp.maximum(lb, LB_FLOOR)
    one_m_lb = 1.0 - lb
    ng = ng_ref[...]
    r256 = lax.broadcasted_iota(jnp.int32, (W_GRP, W_GRP), 0)
    c256 = lax.broadcasted_iota(jnp.int32, (W_GRP, W_GRP), 1)
    same_head = (r256 >> 6) == (c256 >> 6)
    head_ones = jnp.where(same_head, 1.0, 0.0).astype(BF16)
    rc = lax.broadcasted_iota(jnp.int32, (HG_CHUNK, W_GRP), 0)
    cc = lax.broadcasted_iota(jnp.int32, (HG_CHUNK, W_GRP), 1)
    causal = (cc & (HG_CHUNK - 1)) <= rc

    def chunk_body(ci, carry):
        r0 = pl.multiple_of(ci * HG_CHUNK, HG_CHUNK)
        rows = pl.ds(r0, HG_CHUNK)
        q = _silu(proj_scr[rows, C_BQ:C_BQ + W_GRP])
        sig_pos, sig_neg = _sigmoid_pair(proj_scr[rows, C_BF:C_BF + W_GRP])
        g = jnp.log(sig_pos + lbf * sig_neg)
        k = one_m_lb * sig_neg
        v = proj_scr[rows, C_BI:C_BI + W_GRP]
        cum = g
        for sh in (1, 2, 4, 8, 16, 32):
            cum = cum + jnp.where(rc >= sh, pltpu.roll(cum, sh, axis=0), 0.0)
        ref = cum[HG_MID:HG_MID + 1, :]
        tot = cum[HG_CHUNK - 1:HG_CHUNK, :]
        e_fwd = jnp.exp(cum - ref)
        e_bwd = jnp.exp(ref - cum)
        qp = q * e_fwd
        kp = k * e_bwd
        qs = (qp * jnp.exp(ref)).astype(BF16)
        ks = (kp * jnp.exp(tot - ref)).astype(BF16)
        vb = v.astype(BF16)
        bk = jnp.where(same_head, jnp.concatenate([kp] * N_HG, axis=0), 0.0).astype(BF16)
        bv = jnp.where(same_head, jnp.concatenate([v] * N_HG, axis=0), 0.0).astype(BF16)
        a = jnp.where(causal, _nt_dot(qp.astype(BF16), bk), 0.0)
        st = st_scr[...]
        o = _dot(a.astype(BF16), bv) + _nt_dot(qs, st.astype(BF16))
        st_scr[...] = st * jnp.exp(tot) + jnp.where(same_head, _tn_dot(vb, ks), 0.0)
        o2 = o * o
        o2_hi = o2.astype(BF16)
        o2_lo = (o2 - o2_hi.astype(F32)).astype(BF16)
        ms = (_dot(o2_hi, head_ones) + _dot(o2_lo, head_ones)) * (1.0 / HEAD_DIM)
        o = o * lax.rsqrt(ms + RMS_EPS) * ng
        o = o * _silu(proj_scr[rows, C_BG:C_BG + W_GRP])
        hg_scr[rows, :] = o
        return carry

    lax.fori_loop(0, n_ch, chunk_body, 0)
    mix_scr[:, W_GRP:2 * W_GRP] = hg_scr[...].astype(BF16)

    @pl.when(last)
    def _():
        s_t = st_scr[...].T
        for h in range(N_HG):
            snew_ref[h * HEAD_DIM:(h + 1) * HEAD_DIM, :] = (
                s_t[h * HEAD_DIM:(h + 1) * HEAD_DIM, h * HEAD_DIM:(h + 1) * HEAD_DIM])

    u_scr[SUBLANES:SUBLANES + tb, :] = proj_scr[:, C_CC:C_CC + W_GRP] * proj_scr[:, C_CH:C_CH + W_GRP]
    cw = cw_ref[...]
    yc = (u_scr[SUBLANES - 2:SUBLANES - 2 + tb, :] * cw[0:1, :]
          + u_scr[SUBLANES - 1:SUBLANES - 1 + tb, :] * cw[1:2, :]
          + u_scr[SUBLANES:SUBLANES + tb, :] * cw[2:3, :])
    mix_scr[:, 2 * W_GRP:3 * W_GRP] = (proj_scr[:, C_CB:C_CB + W_GRP] * yc).astype(BF16)
    tail = u_scr[tb:tb + SUBLANES, :]
    u_scr[0:SUBLANES, :] = tail

    @pl.when(last)
    def _():
        cnew_ref[...] = tail[SUBLANES - 2:SUBLANES, :]

    dv = proj_scr[:, C_DV:C_DV + W_GRP]
    p_scr[16:16 + tb, :] = dv
    ext = p_scr[...]
    s2 = ext + pltpu.roll(ext, 1, axis=0)
    s4 = s2 + pltpu.roll(s2, 2, axis=0)
    s8 = s4 + pltpu.roll(s4, 4, axis=0)
    s16 = s8 + pltpu.roll(s8, 8, axis=0)
    rp = lax.broadcasted_iota(jnp.int32, (tb, W_GRP), 0)
    cp = lax.broadcasted_iota(jnp.int32, (tb, W_GRP), 1)
    grp = cp >> 6
    win = jnp.where(grp == 0, s2[16:], jnp.where(grp == 1, s4[16:], jnp.where(grp == 2, s8[16:], s16[16:])))
    width = jnp.left_shift(2, grp)
    cnt = jnp.minimum(ti * tb + rp + 1, width).astype(F32)
    pooled = win / cnt - dv
    yd = _dot(pooled.astype(BF16), pw_ref[...]) * ps_ref[...]
    mix_scr[:, 3 * W_GRP:4 * W_GRP] = yd.astype(BF16)
    ptail = p_scr[tb:tb + 16, :]
    p_scr[0:16, :] = ptail

    @pl.when(last)
    def _():
        pnew_ref[...] = ptail[1:16, :]

    y = _dot(mix_scr[...], w_o_ref[...])
    y_ref[...] = _layer_norm(ALPHA * x + y, g_ref[...], b_ref[...])


def _prompt_mixer_call(x, w_in, sink, lb, ng, cw, pw_bd, ps, w_o, g, b):
    bsz, t, _ = x.shape
    tb = min(TOKEN_BLOCK, t)
    row = lambda i, j: (i, 0, 0)
    single = pl.Buffered(1)
    out_shape = (
        jax.ShapeDtypeStruct((bsz, t, D_MODEL), F32),
        jax.ShapeDtypeStruct((bsz, WINDOW, LANES), F32),
        jax.ShapeDtypeStruct((bsz, WINDOW, LANES), F32),
        jax.ShapeDtypeStruct((bsz, W_GRP, HEAD_DIM), F32),
        jax.ShapeDtypeStruct((bsz, CONV_W - 1, W_GRP), F32),
        jax.ShapeDtypeStruct((bsz, POOL_BUF, W_GRP), F32),
    )
    return pl.pallas_call(
        _prompt_mixer_kernel,
        grid=(bsz, t // tb),
        in_specs=[pl.BlockSpec((None, tb, D_MODEL), lambda i, j: (i, j, 0)),
                  pl.BlockSpec((D_MODEL, D_IN), lambda i, j: (0, 0), pipeline_mode=single),
                  pl.BlockSpec(memory_space=pltpu.SMEM),
                  _full_spec((1, W_GRP)), _full_spec((1, W_GRP)), _full_spec((CONV_W, W_GRP)),
                  _full_spec((W_GRP, W_GRP)), _full_spec((1, W_GRP)),
                  pl.BlockSpec((D_MODEL, D_MODEL), lambda i, j: (0, 0), pipeline_mode=single),
                  _full_spec((1, D_MODEL)), _full_spec((1, D_MODEL))],
        out_specs=(pl.BlockSpec((None, tb, D_MODEL), lambda i, j: (i, j, 0)),
                   pl.BlockSpec((None, WINDOW, LANES), row),
                   pl.BlockSpec((None, WINDOW, LANES), row),
                   pl.BlockSpec((None, W_GRP, HEAD_DIM), row),
                   pl.BlockSpec((None, CONV_W - 1, W_GRP), row),
                   pl.BlockSpec((None, POOL_BUF, W_GRP), row)),
        out_shape=out_shape,
        scratch_shapes=[
            pltpu.VMEM((tb, D_IN), F32),
            pltpu.VMEM((WINDOW + tb, LANES), BF16),
            pltpu.VMEM((WINDOW + tb, LANES), BF16),
            pltpu.VMEM((W_GRP, W_GRP), F32),
            pltpu.VMEM((SUBLANES + tb, W_GRP), F32),
            pltpu.VMEM((16 + tb, W_GRP), F32),
            pltpu.VMEM((4 * WINDOW, 2 * WINDOW), F32),
            pltpu.VMEM((tb, D_MODEL), BF16),
            pltpu.VMEM((tb, W_GRP), F32),
        ],
        compiler_params=_compiler_params(("arbitrary", "arbitrary")),
    )(x, w_in, sink, lb, ng, cw, pw_bd, ps, w_o, g, b)


DEC_SEQ = 4
SEQ_BLOCK = 16
PAIR_ROWS = 2 * DEC_SEQ


def _sample_mixer_kernel(x_ref, ck_ref, cv_ref, s_ref, cs_ref, pb_ref,
                         w_in_ref, w_hgt_ref, sink_ref, lb_ref, lbc_ref, ng_ref, cw_ref, pw_ref, ps_ref,
                         w_o_ref, g_ref, b_ref,
                         y_ref, knew_ref, vnew_ref, snew_ref, cnew_ref, pnew_ref,
                         proj_scr, mix_scr, hg_scr, cd_scr, od_scr):
    sb = ck_ref.shape[0]
    m_rows = sb * DEC_SEQ
    x = x_ref[...]
    xb = x.astype(BF16)
    proj_scr[...] = _dot(xb, w_in_ref[...])

    lane = lax.broadcasted_iota(jnp.int32, (PAIR_ROWS, LANES), 1)
    lo = lane < HEAD_DIM
    sub8 = lax.broadcasted_iota(jnp.int32, (PAIR_ROWS, LANES), 0)
    n_rows = 4 * PAIR_ROWS
    r_c = lax.broadcasted_iota(jnp.int32, (n_rows, 2 * WINDOW), 0)
    c_c = lax.broadcasted_iota(jnp.int32, (n_rows, 2 * WINDOW), 1)
    step_c = r_c & (DEC_SEQ - 1)
    rel_c = step_c + WINDOW - (c_c & (WINDOW - 1))
    valid_c = (((r_c >> 2) & 1) == (c_c >> 7)) & (rel_c <= WINDOW)
    slope_c = jnp.exp2(-2.0 * ((r_c >> 3).astype(F32) + 1.0))
    bias_c = jnp.where(valid_c, -slope_c * rel_c.astype(F32), MASK_VALUE)
    r_n = lax.broadcasted_iota(jnp.int32, (n_rows, PAIR_ROWS), 0)
    c_n = lax.broadcasted_iota(jnp.int32, (n_rows, PAIR_ROWS), 1)
    rel_n = (r_n & (DEC_SEQ - 1)) - (c_n & (DEC_SEQ - 1))
    valid_n = (((r_n >> 2) & 1) == (c_n >> 2)) & (rel_n >= 0)
    slope_n = jnp.exp2(-2.0 * ((r_n >> 3).astype(F32) + 1.0))
    bias_n = jnp.where(valid_n, -slope_n * rel_n.astype(F32), MASK_VALUE)
    head_col = lax.broadcasted_iota(jnp.int32, (n_rows, 1), 0) >> 3
    sink_col = jnp.where(head_col == 0, sink_ref[0],
                         jnp.where(head_col == 1, sink_ref[1], jnp.where(head_col == 2, sink_ref[2], sink_ref[3])))
    row128 = lax.broadcasted_iota(jnp.int32, (WINDOW, LANES), 0)
    for p in range(sb // 2):
        rows = slice(p * PAIR_ROWS, (p + 1) * PAIR_ROWS)
        q0 = proj_scr[rows, 0:LANES] * (HEAD_DIM ** -0.5)
        q1 = proj_scr[rows, LANES:2 * LANES] * (HEAD_DIM ** -0.5)
        q0r = pltpu.roll(q0, HEAD_DIM, axis=1)
        q1r = pltpu.roll(q1, HEAD_DIM, axis=1)
        zero = jnp.zeros_like(q0)
        q4 = jnp.concatenate([jnp.where(lo, q0, zero), jnp.where(lo, q0r, zero),
                              jnp.where(lo, zero, q1r), jnp.where(lo, zero, q1)], axis=0).astype(BF16)
        kn8 = proj_scr[rows, C_AK:C_AK + LANES]
        vn8 = proj_scr[rows, C_AV:C_AV + LANES]
        ka, kb = ck_ref[2 * p], ck_ref[2 * p + 1]
        va, vb = cv_ref[2 * p], cv_ref[2 * p + 1]
        s_c = _nt_dot(q4, jnp.concatenate([ka, kb], axis=0).astype(BF16)) + bias_c
        s_n = _nt_dot(q4, kn8.astype(BF16)) + bias_n
        m = jnp.maximum(jnp.maximum(jnp.max(s_c, axis=-1, keepdims=True),
                                    jnp.max(s_n, axis=-1, keepdims=True)), sink_col)
        p_c = jnp.exp(s_c - m)
        p_n = jnp.exp(s_n - m)
        den = (jnp.sum(p_c, axis=-1, keepdims=True) + jnp.sum(p_n, axis=-1, keepdims=True)
               + jnp.exp(sink_col - m))
        o_all = (_dot(p_c.astype(BF16), jnp.concatenate([va, vb], axis=0).astype(BF16))
                 + _dot(p_n.astype(BF16), vn8.astype(BF16))) / den
        o = [o_all[h * PAIR_ROWS:(h + 1) * PAIR_ROWS, :] for h in range(4)]
        mix_scr[rows, 0:LANES] = jnp.where(lo, o[0], pltpu.roll(o[1], HEAD_DIM, axis=1))
        mix_scr[rows, LANES:2 * LANES] = jnp.where(lo, pltpu.roll(o[2], HEAD_DIM, axis=1), o[3])
        kn_hi = pltpu.roll(kn8, DEC_SEQ, axis=0)
        vn_hi = pltpu.roll(vn8, DEC_SEQ, axis=0)
        for idx, old, new, out_ref in ((2 * p, ka, kn_hi, knew_ref), (2 * p + 1, kb, kn8, knew_ref),
                                       (2 * p, va, vn_hi, vnew_ref), (2 * p + 1, vb, vn8, vnew_ref)):
            shifted = pltpu.roll(old, WINDOW - DEC_SEQ, axis=0)
            out_ref[idx, 0:WINDOW - SUBLANES, :] = shifted[0:WINDOW - SUBLANES, :]
            out_ref[idx, WINDOW - SUBLANES:WINDOW, :] = jnp.where(
                sub8 >= DEC_SEQ, new, shifted[WINDOW - SUBLANES:WINDOW, :])

    hgt = _nt_dot(w_hgt_ref[...], xb)
    q_t = _silu(hgt[0:W_GRP, :])
    sig_pos, sig_neg = _sigmoid_pair(hgt[W_GRP:2 * W_GRP, :])
    lbc = lbc_ref[...]
    f_t = sig_pos + jnp.maximum(lbc, LB_FLOOR) * sig_neg
    k_t = (1.0 - lbc) * sig_neg
    for b in range(sb):
        st = s_ref[b]
        for step in range(DEC_SEQ):
            c = b * DEC_SEQ + step
            v_row = proj_scr[c:c + 1, C_BI:C_BI + W_GRP]
            v_exp = jnp.concatenate(
                [jnp.broadcast_to(v_row[:, h * HEAD_DIM:(h + 1) * HEAD_DIM], (HEAD_DIM, HEAD_DIM))
                 for h in range(N_HG)], axis=0)
            st = f_t[:, c:c + 1] * st + k_t[:, c:c + 1] * v_exp
            qs = q_t[:, c:c + 1] * st
            hg_scr[c:c + 1, :] = jnp.concatenate(
                [jnp.sum(qs[h * HEAD_DIM:(h + 1) * HEAD_DIM, :], axis=0, keepdims=True) for h in range(N_HG)],
                axis=1)
        snew_ref[b] = st
    r256 = lax.broadcasted_iota(jnp.int32, (W_GRP, W_GRP), 0)
    c256 = lax.broadcasted_iota(jnp.int32, (W_GRP, W_GRP), 1)
    head_ones = jnp.where((r256 >> 6) == (c256 >> 6), 1.0, 0.0).astype(BF16)
    o = hg_scr[...]
    o2 = o * o
    o2_hi = o2.astype(BF16)
    o2_lo = (o2 - o2_hi.astype(F32)).astype(BF16)
    ms = (_dot(o2_hi, head_ones) + _dot(o2_lo, head_ones)) * (1.0 / HEAD_DIM)
    o = o * lax.rsqrt(ms + RMS_EPS) * ng_ref[...]
    mix_scr[:, W_GRP:2 * W_GRP] = o * _silu(proj_scr[:, C_BG:C_BG + W_GRP])

    for j in range(8):
        cd_scr[j] = proj_scr[:, C_CB + j * LANES:C_CB + (j + 1) * LANES]
    cw = cw_ref[...]
    ps = ps_ref[...]
    cp = lax.broadcasted_iota(jnp.int32, (sb, W_GRP), 1) >> 6
    width = jnp.left_shift(2, cp)
    step_rows = lambda step: pl.ds(step, sb, stride=DEC_SEQ)

    def step_load(tile, step):
        return jnp.concatenate([cd_scr[tile, step_rows(step), :], cd_scr[tile + 1, step_rows(step), :]], axis=1)

    def step_store(tile, step, val):
        od_scr[tile, step_rows(step), :] = val[:, 0:LANES]
        od_scr[tile + 1, step_rows(step), :] = val[:, LANES:2 * LANES]

    u = [cs_ref[0], cs_ref[1]]
    ext = [pb_ref[i] for i in range(POOL_BUF)]
    for step in range(DEC_SEQ):
        u.append(step_load(2, step) * step_load(4, step))
        ext.append(step_load(6, step))
    cnew_ref[0] = u[DEC_SEQ]
    cnew_ref[1] = u[DEC_SEQ + 1]
    for i in range(POOL_BUF):
        pnew_ref[i] = ext[DEC_SEQ + i]
    for step in range(DEC_SEQ):
        yc = u[step] * cw[0:1, :] + u[step + 1] * cw[1:2, :] + u[step + 2] * cw[2:3, :]
        step_store(0, step, step_load(0, step) * yc)
        top = POOL_BUF + step
        acc = ext[top]
        sums = {}
        for jj in range(1, 16):
            acc = acc + ext[top - jj]
            if jj + 1 in POOL_WINDOWS:
                sums[jj + 1] = acc
        win = jnp.where(cp == 0, sums[2], jnp.where(cp == 1, sums[4], jnp.where(cp == 2, sums[8], sums[16])))
        cnt = jnp.minimum(PAST_LEN + step + 1, width).astype(F32)
        pooled = win / cnt - ext[top]
        step_store(2, step, _dot(pooled.astype(BF16), pw_ref[...]) * ps)
    for j in range(4):
        mix_scr[:, 2 * W_GRP + j * LANES:2 * W_GRP + (j + 1) * LANES] = od_scr[j]

    y = _dot(mix_scr[...].astype(BF16), w_o_ref[...])
    y_ref[...] = _layer_norm(ALPHA * x + y, g_ref[...], b_ref[...])


def _sample_mixer_call(l, x2d, ck, cv, s, cs_tm, pb_tm, w):
    m = x2d.shape[0]
    nb = m // DEC_SEQ
    sb = SEQ_BLOCK
    rows = sb * DEC_SEQ
    single = pl.Buffered(1)
    out_shape = (
        jax.ShapeDtypeStruct((m, D_MODEL), F32),
        jax.ShapeDtypeStruct((nb, WINDOW, LANES), F32),
        jax.ShapeDtypeStruct((nb, WINDOW, LANES), F32),
        jax.ShapeDtypeStruct((nb, W_GRP, HEAD_DIM), F32),
        jax.ShapeDtypeStruct((CONV_W - 1, nb, W_GRP), F32),
        jax.ShapeDtypeStruct((POOL_BUF, nb, W_GRP), F32),
    )
    return pl.pallas_call(
        _sample_mixer_kernel,
        grid=(nb // sb,),
        in_specs=[pl.BlockSpec((rows, D_MODEL), lambda i: (i, 0)),
                  pl.BlockSpec((None, sb, WINDOW, LANES), lambda i: (l, i, 0, 0)),
                  pl.BlockSpec((None, sb, WINDOW, LANES), lambda i: (l, i, 0, 0)),
                  pl.BlockSpec((None, sb, W_GRP, HEAD_DIM), lambda i: (l, i, 0, 0)),
                  pl.BlockSpec((None, CONV_W - 1, sb, W_GRP), lambda i: (l, 0, i, 0)),
                  pl.BlockSpec((None, POOL_BUF, sb, W_GRP), lambda i: (l, 0, i, 0)),
                  pl.BlockSpec((D_MODEL, D_IN), lambda i: (0, 0), pipeline_mode=single),
                  _full_spec((2 * W_GRP, D_MODEL)),
                  pl.BlockSpec(memory_space=pltpu.SMEM),
                  _full_spec((1, W_GRP)), _full_spec((W_GRP, 1)), _full_spec((1, W_GRP)),
                  _full_spec((CONV_W, W_GRP)), _full_spec((W_GRP, W_GRP)), _full_spec((1, W_GRP)),
                  pl.BlockSpec((D_MODEL, D_MODEL), lambda i: (0, 0), pipeline_mode=single),
                  _full_spec((1, D_MODEL)), _full_spec((1, D_MODEL))],
        out_specs=(pl.BlockSpec((rows, D_MODEL), lambda i: (i, 0)),
                   pl.BlockSpec((sb, WINDOW, LANES), lambda i: (i, 0, 0)),
                   pl.BlockSpec((sb, WINDOW, LANES), lambda i: (i, 0, 0)),
                   pl.BlockSpec((sb, W_GRP, HEAD_DIM), lambda i: (i, 0, 0)),
                   pl.BlockSpec((CONV_W - 1, sb, W_GRP), lambda i: (0, i, 0)),
                   pl.BlockSpec((POOL_BUF, sb, W_GRP), lambda i: (0, i, 0))),
        out_shape=out_shape,
        scratch_shapes=[pltpu.VMEM((rows, D_IN), F32),
                        pltpu.VMEM((rows, D_MODEL), F32),
                        pltpu.VMEM((rows, W_GRP), F32),
                        pltpu.VMEM((8, rows, LANES), F32),
                        pltpu.VMEM((4, rows, LANES), F32)],
        compiler_params=_compiler_params(("arbitrary",)),
    )(x2d, ck, cv, s, cs_tm, pb_tm, w["w_in"], w["w_hgt"], w["sink"], w["lb"], w["lb_col"], w["ng"], w["cw"],
      w["pw"], w["ps"], w["w_o"], w["ln1_g"], w["ln1_b"])


def _xattn_sample_kernel(x_ref, mk_ref, mv_ref, wq_ref, wo_ref, g_ref, b_ref, o_ref, att_scr):
    sb = mk_ref.shape[0]
    x = x_ref[...]
    q = _dot(x.astype(BF16), wq_ref[...]) * (HEAD_DIM ** -0.5)
    n_rows = N_XH * PAIR_ROWS
    lane = lax.broadcasted_iota(jnp.int32, (PAIR_ROWS, D_X), 1) >> 6
    r_s = lax.broadcasted_iota(jnp.int32, (n_rows, 2 * N_MEM), 0)
    c_s = lax.broadcasted_iota(jnp.int32, (n_rows, 2 * N_MEM), 1)
    own = ((r_s >> 2) & 1) == (c_s >> 8)
    for p in range(sb // 2):
        rows = slice(p * PAIR_ROWS, (p + 1) * PAIR_ROWS)
        q8 = q[rows, :]
        zero = jnp.zeros_like(q8)
        q4 = jnp.concatenate([jnp.where(lane == h, q8, zero) for h in range(N_XH)], axis=0).astype(BF16)
        k2 = jnp.concatenate([mk_ref[2 * p], mk_ref[2 * p + 1]], axis=0).astype(BF16)
        v2 = jnp.concatenate([mv_ref[2 * p], mv_ref[2 * p + 1]], axis=0).astype(BF16)
        s = jnp.where(own, _nt_dot(q4, k2), MASK_VALUE)
        m = jnp.max(s, axis=-1, keepdims=True)
        pr = jnp.exp(s - m)
        den = jnp.sum(pr, axis=-1, keepdims=True)
        o_all = _dot(pr.astype(BF16), v2) / den
        o8 = jnp.zeros_like(q8)
        for h in range(N_XH):
            o8 = jnp.where(lane == h, o_all[h * PAIR_ROWS:(h + 1) * PAIR_ROWS, :], o8)
        att_scr[rows, :] = o8.astype(BF16)
    y = _dot(att_scr[...], wo_ref[...])
    o_ref[...] = _layer_norm(ALPHA * x + y, g_ref[...], b_ref[...])


def _xattn_sample_call(l, x2d, mk, mv, w):
    m = x2d.shape[0]
    nb = m // DEC_SEQ
    sb = SEQ_BLOCK
    rows = sb * DEC_SEQ
    return pl.pallas_call(
        _xattn_sample_kernel,
        grid=(nb // sb,),
        in_specs=[pl.BlockSpec((rows, D_MODEL), lambda i: (i, 0)),
                  pl.BlockSpec((None, sb, N_MEM, D_X), lambda i: (l, i, 0, 0)),
                  pl.BlockSpec((None, sb, N_MEM, D_X), lambda i: (l, i, 0, 0)),
                  _full_spec((D_MODEL, D_X)), _full_spec((D_X, D_MODEL)),
                  _full_spec((1, D_MODEL)), _full_spec((1, D_MODEL))],
        out_specs=pl.BlockSpec((rows, D_MODEL), lambda i: (i, 0)),
        out_shape=jax.ShapeDtypeStruct((m, D_MODEL), F32),
        scratch_shapes=[pltpu.VMEM((rows, D_X), BF16)],
        compiler_params=_compiler_params(("arbitrary",)),
    )(x2d, mk, mv, w["w_xq"], w["w_xo"], w["ln2_g"], w["ln2_b"])


def _sample_layer(l, x2d, ck, cv, s, cs_tm, pb_tm, mk, mv, w):
    x2d, kn, vn, sn, cn, pn = _sample_mixer_call(l, x2d, ck, cv, s, cs_tm, pb_tm, w)
    x2d = _xattn_sample_call(l, x2d, mk, mv, w)
    x2d = _ffn_call(x2d, w["w_gate"], w["w_up"], w["w_down"], w["ln3_g"], w["ln3_b"])
    return x2d, kn, vn, sn, cn, pn


def _row(v):
    return v.reshape(1, -1).astype(F32)


def _pool_block_diag(pool_w):
    z = jnp.zeros((W_GRP, W_GRP), pool_w.dtype)
    for gi in range(4):
        z = lax.dynamic_update_slice(z, pool_w[gi], (gi * HEAD_DIM, gi * HEAD_DIM))
    return z


def _hgrn_lower_bounds(lb_param):
    p = jax.nn.softmax(lb_param.astype(F32), axis=0)
    return jnp.cumsum(p, axis=0) - p[0:1]


def _prompt_layer(x, mem2d, w):
    bsz = x.shape[0]
    mk = _matmul_call(mem2d, w["w_xk"]).reshape(bsz, N_MEM, D_X)
    mv = _matmul_call(mem2d, w["w_xv"]).reshape(bsz, N_MEM, D_X)
    x, kn, vn, sn, cn, pn = _prompt_mixer_call(
        x, w["w_in"], w["sink"], w["lb"], w["ng"], w["cw"], w["pw"], w["ps"], w["w_o"], w["ln1_g"], w["ln1_b"])
    x = _xattn_prompt_call(x, mk, mv, w["w_xq"], w["w_xo"], w["ln2_g"], w["ln2_b"])
    t = x.shape[1]
    x = _ffn_call(x.reshape(bsz * t, D_MODEL), w["w_gate"], w["w_up"], w["w_down"],
                  w["ln3_g"], w["ln3_b"]).reshape(bsz, t, D_MODEL)
    return x, kn, vn, sn, cn, pn, mk, mv


def _layer_weights(l, lb_all, w_in, attn_sink, hgrn_norm_g, conv_w, pool_w, pool_scale, w_o, ln1_g, ln1_b,
                   w_xq, w_xk, w_xv, w_xo, ln2_g, ln2_b, w_gate, w_up, w_down, ln3_g, ln3_b):
    bf = lambda a: a[l].astype(BF16)
    return dict(
        w_in=bf(w_in), w_hgt=w_in[l][:, C_BQ:C_BI].T.astype(BF16),
        sink=attn_sink[l].astype(F32), lb=_row(lb_all[l]), lb_col=lb_all[l].reshape(-1, 1).astype(F32),
        ng=_row(hgrn_norm_g[l]),
        cw=conv_w[l].astype(F32), pw=_pool_block_diag(pool_w[l]).astype(BF16), ps=_row(pool_scale[l]),
        w_o=bf(w_o), ln1_g=_row(ln1_g[l]), ln1_b=_row(ln1_b[l]),
        w_xq=bf(w_xq), w_xk=bf(w_xk), w_xv=bf(w_xv), w_xo=bf(w_xo), ln2_g=_row(ln2_g[l]), ln2_b=_row(ln2_b[l]),
        w_gate=bf(w_gate), w_up=bf(w_up), w_down=bf(w_down), ln3_g=_row(ln3_g[l]), ln3_b=_row(ln3_b[l]))


def kernel(x_prompt, x_sample, cache_swa_k, cache_swa_v, state_hgrn, state_conv, state_pool, cache_mem_k,
           cache_mem_v, mem_prompt, emb_ln_g, emb_ln_b, w_in, attn_sink, hgrn_lb, hgrn_norm_g, conv_w, pool_w,
           pool_scale, w_o, ln1_g, ln1_b, w_xq, w_xk, w_xv, w_xo, ln2_g, ln2_b, w_gate, w_up, w_down, ln3_g,
           ln3_b):
    bp, t, _ = x_prompt.shape
    lb_all = _hgrn_lower_bounds(hgrn_lb)
    hp = _ln_call(x_prompt.reshape(bp * t, D_MODEL), _row(emb_ln_g), _row(emb_ln_b)).reshape(bp, t, D_MODEL)
    mem2d = mem_prompt.reshape(bp * N_MEM, D_MODEL)
    bs, ts, _ = x_sample.shape
    hs = _ln_call(x_sample.reshape(bs * ts, D_MODEL), _row(emb_ln_g), _row(emb_ln_b))
    ck = cache_swa_k.reshape(DEPTH, bs, WINDOW, LANES)
    cv = cache_swa_v.reshape(DEPTH, bs, WINDOW, LANES)
    st = state_hgrn.reshape(DEPTH, bs, W_GRP, HEAD_DIM)
    cs_tm = state_conv.transpose(0, 2, 1, 3)
    pb_tm = state_pool.transpose(0, 2, 1, 3)
    mk_s = cache_mem_k.reshape(DEPTH, bs, N_MEM, D_X)
    mv_s = cache_mem_v.reshape(DEPTH, bs, N_MEM, D_X)
    outs = [[] for _ in range(7)]
    souts = [[] for _ in range(5)]
    for l in range(DEPTH):
        w = _layer_weights(l, lb_all, w_in, attn_sink, hgrn_norm_g, conv_w, pool_w, pool_scale, w_o, ln1_g,
                           ln1_b, w_xq, w_xk, w_xv, w_xo, ln2_g, ln2_b, w_gate, w_up, w_down, ln3_g, ln3_b)
        res = _prompt_layer(hp, mem2d, w)
        hp = res[0]
        for acc, r in zip(outs, res[1:]):
            acc.append(r)
        sres = _sample_layer(l, hs, ck, cv, st, cs_tm, pb_tm, mk_s, mv_s, w)
        hs = sres[0]
        for acc, r in zip(souts, sres[1:]):
            acc.append(r)
    pk, pv, ps, pc, pp, pmk, pmv = [jnp.stack(o) for o in outs]
    sk, sv, ss, sc, sp = [jnp.stack(o) for o in souts]
    return (hp, hs.reshape(bs, ts, D_MODEL),
            pk.reshape(DEPTH, bp, WINDOW, N_KV, HEAD_DIM), pv.reshape(DEPTH, bp, WINDOW, N_KV, HEAD_DIM),
            ps.reshape(DEPTH, bp, N_HG, HEAD_DIM, HEAD_DIM), pc, pp,
            pmk.reshape(DEPTH, bp, N_MEM, N_XH, HEAD_DIM), pmv.reshape(DEPTH, bp, N_MEM, N_XH, HEAD_DIM),
            sk.reshape(DEPTH, bs, WINDOW, N_KV, HEAD_DIM), sv.reshape(DEPTH, bs, WINDOW, N_KV, HEAD_DIM),
            ss.reshape(DEPTH, bs, N_HG, HEAD_DIM, HEAD_DIM),
            sc.transpose(0, 2, 1, 3), sp.transpose(0, 2, 1, 3))
```

```python
import functools

import jax
import jax.numpy as jnp
from jax import lax
from jax.experimental import pallas as pl
from jax.experimental.pallas import tpu as pltpu

F32 = jnp.float32
BF16 = jnp.bfloat16

D_MODEL = 1024
DEPTH = 4
HEAD_DIM = 64
W_GRP = 256
N_KV = 2
WINDOW = 128
N_HG = 4
CONV_W = 3
POOL_WINDOWS = (2, 4, 8, 16)
POOL_BUF = 15
N_MEM = 256
N_XH = 4
D_X = 256
D_FF = 2816
D_IN = 2560
ALPHA = (2 * DEPTH) ** 0.25
LN_EPS = 1e-5
RMS_EPS = 1e-6
MASK_VALUE = -1e30
LB_FLOOR = 1e-30
PAST_LEN = 8192

C_AQ, C_AK, C_AV = 0, 256, 384
C_BQ, C_BF, C_BI, C_BG = 512, 768, 1024, 1280
C_CB, C_CC, C_CH = 1536, 1792, 2048
C_DV = 2304

LANES = 128
SUBLANES = 8
VMEM_LIMIT_BYTES = 56 * 1024 * 1024

TOKEN_BLOCK = 512
HG_CHUNK = 64
HG_MID = HG_CHUNK // 2 - 1


def _nt_dot(a, b):
    return lax.dot_general(a, b, (((1,), (1,)), ((), ())), preferred_element_type=F32)


def _tn_dot(a, b):
    return lax.dot_general(a, b, (((0,), (0,)), ((), ())), preferred_element_type=F32)


def _dot(a, b):
    return jnp.dot(a, b, preferred_element_type=F32)


def _layer_norm(x, g, b):
    mu = jnp.mean(x, axis=-1, keepdims=True)
    xc = x - mu
    var = jnp.mean(xc * xc, axis=-1, keepdims=True)
    return xc * lax.rsqrt(var + LN_EPS) * g + b


def _sigmoid_pair(z):
    e = jnp.exp(-jnp.abs(z))
    inv = 1.0 / (1.0 + e)
    small = e * inv
    pos = z >= 0
    return jnp.where(pos, inv, small), jnp.where(pos, small, inv)


def _silu(z):
    s, _ = _sigmoid_pair(z)
    return z * s


def _full_spec(shape):
    nd = len(shape)
    return pl.BlockSpec(shape, lambda *_: (0,) * nd)


def _compiler_params(sem, flags=None):
    return pltpu.CompilerParams(dimension_semantics=sem, vmem_limit_bytes=VMEM_LIMIT_BYTES, flags=flags)


def _ln_kernel(x_ref, g_ref, b_ref, o_ref):
    o_ref[...] = _layer_norm(x_ref[...], g_ref[...], b_ref[...])


def _ln_call(x2d, g, b):
    m = x2d.shape[0]
    return pl.pallas_call(
        _ln_kernel,
        name="input_ln",
        grid=(m // TOKEN_BLOCK,),
        in_specs=[pl.BlockSpec((TOKEN_BLOCK, D_MODEL), lambda i: (i, 0)),
                  _full_spec((1, D_MODEL)), _full_spec((1, D_MODEL))],
        out_specs=pl.BlockSpec((TOKEN_BLOCK, D_MODEL), lambda i: (i, 0)),
        out_shape=jax.ShapeDtypeStruct((m, D_MODEL), F32),
        compiler_params=_compiler_params(("arbitrary",)),
    )(x2d, g, b)


def _matmul_kernel(x_ref, w_ref, o_ref):
    o_ref[...] = _dot(x_ref[...].astype(BF16), w_ref[...])


def _matmul_call(x2d, w):
    m, k = x2d.shape
    n = w.shape[1]
    return pl.pallas_call(
        _matmul_kernel,
        name="mem_proj",
        grid=(m // TOKEN_BLOCK,),
        in_specs=[pl.BlockSpec((TOKEN_BLOCK, k), lambda i: (i, 0)), _full_spec((k, n))],
        out_specs=pl.BlockSpec((TOKEN_BLOCK, n), lambda i: (i, 0)),
        out_shape=jax.ShapeDtypeStruct((m, n), F32),
        compiler_params=_compiler_params(("arbitrary",)),
    )(x2d, w)


def _ffn_kernel(x_ref, wg_ref, wu_ref, wd_ref, g_ref, b_ref, o_ref):
    x = x_ref[...]
    xb = x.astype(BF16)
    h = _silu(_dot(xb, wg_ref[...])) * _dot(xb, wu_ref[...])
    y = _dot(h.astype(BF16), wd_ref[...])
    o_ref[...] = _layer_norm(ALPHA * x + y, g_ref[...], b_ref[...])


def _ffn_call(x2d, wg, wu, wd, g, b):
    m = x2d.shape[0]
    single = pl.Buffered(1)
    return pl.pallas_call(
        _ffn_kernel,
        name="ffn",
        grid=(m // TOKEN_BLOCK,),
        in_specs=[pl.BlockSpec((TOKEN_BLOCK, D_MODEL), lambda i: (i, 0)),
                  pl.BlockSpec((D_MODEL, D_FF), lambda i: (0, 0), pipeline_mode=single),
                  pl.BlockSpec((D_MODEL, D_FF), lambda i: (0, 0), pipeline_mode=single),
                  pl.BlockSpec((D_FF, D_MODEL), lambda i: (0, 0), pipeline_mode=single),
                  _full_spec((1, D_MODEL)), _full_spec((1, D_MODEL))],
        out_specs=pl.BlockSpec((TOKEN_BLOCK, D_MODEL), lambda i: (i, 0)),
        out_shape=jax.ShapeDtypeStruct((m, D_MODEL), F32),
        compiler_params=_compiler_params(("arbitrary",)),
    )(x2d, wg, wu, wd, g, b)


def _xattn_prompt_kernel(x_ref, mk_ref, mv_ref, wq_ref, wo_ref, g_ref, b_ref, o_ref):
    x = x_ref[...]
    q = _dot(x.astype(BF16), wq_ref[...]) * (HEAD_DIM ** -0.5)
    mk = mk_ref[...].astype(BF16)
    mv = mv_ref[...].astype(BF16)
    lane_head = lax.broadcasted_iota(jnp.int32, q.shape, 1) >> 6
    o = jnp.zeros_like(q)
    for h in range(N_XH):
        qh = jnp.where(lane_head == h, q, 0.0).astype(BF16)
        s = _nt_dot(mk, qh)
        m = jnp.max(s, axis=0, keepdims=True)
        p = jnp.exp(s - m)
        den = jnp.sum(p, axis=0, keepdims=True)
        p = (p * (1.0 / den)).astype(BF16)
        o = jnp.where(lane_head == h, _tn_dot(p, mv), o)
    y = _dot(o.astype(BF16), wo_ref[...])
    o_ref[...] = _layer_norm(ALPHA * x + y, g_ref[...], b_ref[...])


def _xattn_prompt_call(x, mk, mv, wq, wo, g, b):
    bsz, t, _ = x.shape
    tb = min(TOKEN_BLOCK, t)
    return pl.pallas_call(
        _xattn_prompt_kernel,
        name="xattn_prompt",
        grid=(bsz, t // tb),
        in_specs=[pl.BlockSpec((None, tb, D_MODEL), lambda i, j: (i, j, 0)),
                  pl.BlockSpec((None, N_MEM, D_X), lambda i, j: (i, 0, 0)),
                  pl.BlockSpec((None, N_MEM, D_X), lambda i, j: (i, 0, 0)),
                  _full_spec((D_MODEL, D_X)), _full_spec((D_X, D_MODEL)),
                  _full_spec((1, D_MODEL)), _full_spec((1, D_MODEL))],
        out_specs=pl.BlockSpec((None, tb, D_MODEL), lambda i, j: (i, j, 0)),
        out_shape=jax.ShapeDtypeStruct((bsz, t, D_MODEL), F32),
        compiler_params=_compiler_params(("arbitrary", "arbitrary")),
    )(x, mk, mv, wq, wo, g, b)


def _swa_bias_table():
    c = lax.broadcasted_iota(jnp.int32, (2 * WINDOW, 4 * WINDOW), 0)
    r = lax.broadcasted_iota(jnp.int32, (2 * WINDOW, 4 * WINDOW), 1)
    head = r >> 7
    rel = (r & (WINDOW - 1)) + WINDOW - c
    slope = jnp.exp2(-2.0 * (head.astype(F32) + 1.0))
    valid = (rel >= 0) & (rel <= WINDOW)
    return jnp.where(valid, -slope * rel.astype(F32), MASK_VALUE)


def _prompt_mixer_kernel(x_ref, w_in_ref, sink_ref, lb_ref, ng_ref, cw_ref, pw_ref, ps_ref, w_o_ref,
                         g_ref, b_ref,
                         y_ref, knew_ref, vnew_ref, snew_ref, cnew_ref, pnew_ref,
                         proj_scr, kext_scr, vext_scr, st_scr, u_scr, p_scr, bias_scr, mix_scr, hg_scr):
    tb = x_ref.shape[0]
    n_qb = tb // WINDOW
    n_ch = tb // HG_CHUNK
    bi = pl.program_id(0)
    ti = pl.program_id(1)
    last = ti == pl.num_programs(1) - 1

    @pl.when((bi == 0) & (ti == 0))
    def _():
        bias_scr[...] = _swa_bias_table()

    @pl.when(ti == 0)
    def _():
        kext_scr[0:WINDOW, :] = jnp.zeros((WINDOW, LANES), BF16)
        vext_scr[0:WINDOW, :] = jnp.zeros((WINDOW, LANES), BF16)
        st_scr[...] = jnp.zeros(st_scr.shape, F32)
        u_scr[0:SUBLANES, :] = jnp.zeros((SUBLANES, W_GRP), F32)
        p_scr[0:16, :] = jnp.zeros((16, W_GRP), F32)

    x = x_ref[...]
    xb = x.astype(BF16)
    proj_scr[:, 0:C_BQ] = _dot(xb, w_in_ref[:, 0:C_BQ])
    proj_scr[:, C_BQ:C_CB] = _dot(xb, w_in_ref[:, C_BQ:C_CB])

    kext_scr[WINDOW:WINDOW + tb, :] = proj_scr[:, C_AK:C_AK + LANES].astype(BF16)
    vext_scr[WINDOW:WINDOW + tb, :] = proj_scr[:, C_AV:C_AV + LANES].astype(BF16)
    lane = lax.broadcasted_iota(jnp.int32, (WINDOW, LANES), 1)
    lo = lane < HEAD_DIM
    key_row = lax.broadcasted_iota(jnp.int32, (2 * WINDOW, 4 * WINDOW), 0)
    head_lane = lax.broadcasted_iota(jnp.int32, (1, 4 * WINDOW), 1) >> 7
    sink_row = jnp.where(head_lane == 0, sink_ref[0],
                         jnp.where(head_lane == 1, sink_ref[1],
                                   jnp.where(head_lane == 2, sink_ref[2], sink_ref[3])))
    for j in range(n_qb):
        rows = slice(j * WINDOW, (j + 1) * WINDOW)
        q0 = proj_scr[rows, 0:LANES] * (HEAD_DIM ** -0.5)
        q1 = proj_scr[rows, LANES:2 * LANES] * (HEAD_DIM ** -0.5)
        q0r = pltpu.roll(q0, HEAD_DIM, axis=1)
        q1r = pltpu.roll(q1, HEAD_DIM, axis=1)
        zero = jnp.zeros_like(q0)
        q4 = jnp.concatenate([jnp.where(lo, q0, zero), jnp.where(lo, q0r, zero),
                              jnp.where(lo, zero, q1r), jnp.where(lo, zero, q1)], axis=0).astype(BF16)
        kj = kext_scr[j * WINDOW:(j + 2) * WINDOW, :]
        vj = vext_scr[j * WINDOW:(j + 2) * WINDOW, :]
        s = _nt_dot(kj, q4) + bias_scr[...]
        if j == 0:
            s = jnp.where((ti == 0) & (key_row < WINDOW), MASK_VALUE, s)
        m = jnp.maximum(jnp.max(s, axis=0, keepdims=True), sink_row)
        p = jnp.exp(s - m)
        den = jnp.sum(p, axis=0, keepdims=True) + jnp.exp(sink_row - m)
        p = (p * (1.0 / den)).astype(BF16)
        o_all = _tn_dot(p, vj)
        o = [o_all[h * WINDOW:(h + 1) * WINDOW, :] for h in range(4)]
        c0 = jnp.where(lo, o[0], pltpu.roll(o[1], HEAD_DIM, axis=1))
        c1 = jnp.where(lo, pltpu.roll(o[2], HEAD_DIM, axis=1), o[3])
        mix_scr[rows, 0:LANES] = c0.astype(BF16)
        mix_scr[rows, LANES:2 * LANES] = c1.astype(BF16)
    kext_scr[0:WINDOW, :] = kext_scr[tb:tb + WINDOW, :]
    vext_scr[0:WINDOW, :] = vext_scr[tb:tb + WINDOW, :]
    proj_scr[:, C_CB:D_IN] = _dot(xb, w_in_ref[:, C_CB:D_IN])

    @pl.when(last)
    def _():
        knew_ref[...] = proj_scr[tb - WINDOW:tb, C_AK:C_AK + LANES]
        vnew_ref[...] = proj_scr[tb - WINDOW:tb, C_AV:C_AV + LANES]

    lb = lb_ref[...]
    lbf = jnp.maximum(lb, LB_FLOOR)
    one_m_lb = 1.0 - lb
    ng = ng_ref[...]
    r256 = lax.broadcasted_iota(jnp.int32, (W_GRP, W_GRP), 0)
    c256 = lax.broadcasted_iota(jnp.int32, (W_GRP, W_GRP), 1)
    same_head = (r256 >> 6) == (c256 >> 6)
    head_ones = jnp.where(same_head, 1.0, 0.0).astype(BF16)
    rc = lax.broadcasted_iota(jnp.int32, (HG_CHUNK, W_GRP), 0)
    cc = lax.broadcasted_iota(jnp.int32, (HG_CHUNK, W_GRP), 1)
    causal = (cc & (HG_CHUNK - 1)) <= rc

    def chunk_body(ci, carry):
        r0 = pl.multiple_of(ci * HG_CHUNK, HG_CHUNK)
        rows = pl.ds(r0, HG_CHUNK)
        q = _silu(proj_scr[rows, C_BQ:C_BQ + W_GRP])
        sig_pos, sig_neg = _sigmoid_pair(proj_scr[rows, C_BF:C_BF + W_GRP])
        g = jnp.log(sig_pos + lbf * sig_neg)
        k = one_m_lb * sig_neg
        v = proj_scr[rows, C_BI:C_BI + W_GRP]
        cum = g
        for sh in (1, 2, 4, 8, 16, 32):
            cum = cum + jnp.where(rc >= sh, pltpu.roll(cum, sh, axis=0), 0.0)
        ref = cum[HG_MID:HG_MID + 1, :]
        tot = cum[HG_CHUNK - 1:HG_CHUNK, :]
        e_fwd = jnp.exp(cum - ref)
        e_bwd = jnp.exp(ref - cum)
        qp = q * e_fwd
        kp = k * e_bwd
        qs = (qp * jnp.exp(ref)).astype(BF16)
        ks = (kp * jnp.exp(tot - ref)).astype(BF16)
        vb = v.astype(BF16)
        bk = jnp.where(same_head, jnp.concatenate([kp] * N_HG, axis=0), 0.0).astype(BF16)
        bv = jnp.where(same_head, jnp.concatenate([v] * N_HG, axis=0), 0.0).astype(BF16)
        a = jnp.where(causal, _nt_dot(qp.astype(BF16), bk), 0.0)
        st = st_scr[...]
        o = _dot(a.astype(BF16), bv) + _nt_dot(qs, st.astype(BF16))
        st_scr[...] = st * jnp.exp(tot) + jnp.where(same_head, _tn_dot(vb, ks), 0.0)
        o2 = o * o
        o2_hi = o2.astype(BF16)
        o2_lo = (o2 - o2_hi.astype(F32)).astype(BF16)
        ms = (_dot(o2_hi, head_ones) + _dot(o2_lo, head_ones)) * (1.0 / HEAD_DIM)
        o = o * lax.rsqrt(ms + RMS_EPS) * ng
        o = o * _silu(proj_scr[rows, C_BG:C_BG + W_GRP])
        hg_scr[rows, :] = o
        return carry

    lax.fori_loop(0, n_ch, chunk_body, 0, unroll=2)
    mix_scr[:, W_GRP:2 * W_GRP] = hg_scr[...].astype(BF16)

    @pl.when(last)
    def _():
        s_t = st_scr[...].T
        for h in range(N_HG):
            snew_ref[h * HEAD_DIM:(h + 1) * HEAD_DIM, :] = (
                s_t[h * HEAD_DIM:(h + 1) * HEAD_DIM, h * HEAD_DIM:(h + 1) * HEAD_DIM])

    u_scr[SUBLANES:SUBLANES + tb, :] = proj_scr[:, C_CC:C_CC + W_GRP] * proj_scr[:, C_CH:C_CH + W_GRP]
    cw = cw_ref[...]
    yc = (u_scr[SUBLANES - 2:SUBLANES - 2 + tb, :] * cw[0:1, :]
          + u_scr[SUBLANES - 1:SUBLANES - 1 + tb, :] * cw[1:2, :]
          + u_scr[SUBLANES:SUBLANES + tb, :] * cw[2:3, :])
    mix_scr[:, 2 * W_GRP:3 * W_GRP] = (proj_scr[:, C_CB:C_CB + W_GRP] * yc).astype(BF16)
    tail = u_scr[tb:tb + SUBLANES, :]
    u_scr[0:SUBLANES, :] = tail

    @pl.when(last)
    def _():
        cnew_ref[...] = tail[SUBLANES - 2:SUBLANES, :]

    dv = proj_scr[:, C_DV:C_DV + W_GRP]
    p_scr[16:16 + tb, :] = dv
    ext = p_scr[...]
    s2 = ext + pltpu.roll(ext, 1, axis=0)
    s4 = s2 + pltpu.roll(s2, 2, axis=0)
    s8 = s4 + pltpu.roll(s4, 4, axis=0)
    s16 = s8 + pltpu.roll(s8, 8, axis=0)
    rp = lax.broadcasted_iota(jnp.int32, (tb, W_GRP), 0)
    cp = lax.broadcasted_iota(jnp.int32, (tb, W_GRP), 1)
    grp = cp >> 6
    win = jnp.where(grp == 0, s2[16:], jnp.where(grp == 1, s4[16:], jnp.where(grp == 2, s8[16:], s16[16:])))
    width = jnp.left_shift(2, grp)
    cnt = jnp.minimum(ti * tb + rp + 1, width).astype(F32)
    pooled = win / cnt - dv
    yd = _dot(pooled.astype(BF16), pw_ref[...]) * ps_ref[...]
    mix_scr[:, 3 * W_GRP:4 * W_GRP] = yd.astype(BF16)
    ptail = p_scr[tb:tb + 16, :]
    p_scr[0:16, :] = ptail

    @pl.when(last)
    def _():
        pnew_ref[...] = ptail[1:16, :]

    y = _dot(mix_scr[...], w_o_ref[...])
    y_ref[...] = _layer_norm(ALPHA * x + y, g_ref[...], b_ref[...])


def _prompt_mixer_call(x, w_in, sink, lb, ng, cw, pw_bd, ps, w_o, g, b):
    bsz, t, _ = x.shape
    tb = min(TOKEN_BLOCK, t)
    row = lambda i, j: (i, 0, 0)
    single = pl.Buffered(1)
    out_shape = (
        jax.ShapeDtypeStruct((bsz, t, D_MODEL), F32),
        jax.ShapeDtypeStruct((bsz, WINDOW, LANES), F32),
        jax.ShapeDtypeStruct((bsz, WINDOW, LANES), F32),
        jax.ShapeDtypeStruct((bsz, W_GRP, HEAD_DIM), F32),
        jax.ShapeDtypeStruct((bsz, CONV_W - 1, W_GRP), F32),
        jax.ShapeDtypeStruct((bsz, POOL_BUF, W_GRP), F32),
    )
    return pl.pallas_call(
        _prompt_mixer_kernel,
        name="mixer_prompt",
        grid=(bsz, t // tb),
        in_specs=[pl.BlockSpec((None, tb, D_MODEL), lambda i, j: (i, j, 0)),
                  pl.BlockSpec((D_MODEL, D_IN), lambda i, j: (0, 0), pipeline_mode=single),
                  pl.BlockSpec(memory_space=pltpu.SMEM),
                  _full_spec((1, W_GRP)), _full_spec((1, W_GRP)), _full_spec((CONV_W, W_GRP)),
                  _full_spec((W_GRP, W_GRP)), _full_spec((1, W_GRP)),
                  pl.BlockSpec((D_MODEL, D_MODEL), lambda i, j: (0, 0), pipeline_mode=single),
                  _full_spec((1, D_MODEL)), _full_spec((1, D_MODEL))],
        out_specs=(pl.BlockSpec((None, tb, D_MODEL), lambda i, j: (i, j, 0)),
                   pl.BlockSpec((None, WINDOW, LANES), row),
                   pl.BlockSpec((None, WINDOW, LANES), row),
                   pl.BlockSpec((None, W_GRP, HEAD_DIM), row),
                   pl.BlockSpec((None, CONV_W - 1, W_GRP), row),
                   pl.BlockSpec((None, POOL_BUF, W_GRP), row)),
        out_shape=out_shape,
        scratch_shapes=[
            pltpu.VMEM((tb, D_IN), F32),
            pltpu.VMEM((WINDOW + tb, LANES), BF16),
            pltpu.VMEM((WINDOW + tb, LANES), BF16),
            pltpu.VMEM((W_GRP, W_GRP), F32),
            pltpu.VMEM((SUBLANES + tb, W_GRP), F32),
            pltpu.VMEM((16 + tb, W_GRP), F32),
            pltpu.VMEM((2 * WINDOW, 4 * WINDOW), F32),
            pltpu.VMEM((tb, D_MODEL), BF16),
            pltpu.VMEM((tb, W_GRP), F32),
        ],
        compiler_params=_compiler_params(("arbitrary", "arbitrary")),
    )(x, w_in, sink, lb, ng, cw, pw_bd, ps, w_o, g, b)


DEC_SEQ = 4
SEQ_BLOCK = 16
PAIR_ROWS = 2 * DEC_SEQ


def _sample_mixer_kernel(x_ref, ck_ref, cv_ref, s_ref, cs_ref, pb_ref,
                         w_in_ref, w_hgt_ref, sink_ref, lb_ref, lbc_ref, ng_ref, cw_ref, pw_ref, ps_ref,
                         w_o_ref, g_ref, b_ref,
                         y_ref, knew_ref, vnew_ref, snew_ref, cnew_ref, pnew_ref,
                         proj_scr, mix_scr, hg_scr, cd_scr, od_scr):
    sb = ck_ref.shape[0]
    m_rows = sb * DEC_SEQ
    x = x_ref[...]
    xb = x.astype(BF16)
    proj_scr[...] = _dot(xb, w_in_ref[...])

    lane = lax.broadcasted_iota(jnp.int32, (PAIR_ROWS, LANES), 1)
    lo = lane < HEAD_DIM
    sub8 = lax.broadcasted_iota(jnp.int32, (PAIR_ROWS, LANES), 0)
    n_rows = 4 * PAIR_ROWS
    r_c = lax.broadcasted_iota(jnp.int32, (n_rows, 2 * WINDOW), 0)
    c_c = lax.broadcasted_iota(jnp.int32, (n_rows, 2 * WINDOW), 1)
    step_c = r_c & (DEC_SEQ - 1)
    rel_c = step_c + WINDOW - (c_c & (WINDOW - 1))
    valid_c = (((r_c >> 2) & 1) == (c_c >> 7)) & (rel_c <= WINDOW)
    slope_c = jnp.exp2(-2.0 * ((r_c >> 3).astype(F32) + 1.0))
    bias_c = jnp.where(valid_c, -slope_c * rel_c.astype(F32), MASK_VALUE)
    r_n = lax.broadcasted_iota(jnp.int32, (n_rows, PAIR_ROWS), 0)
    c_n = lax.broadcasted_iota(jnp.int32, (n_rows, PAIR_ROWS), 1)
    rel_n = (r_n & (DEC_SEQ - 1)) - (c_n & (DEC_SEQ - 1))
    valid_n = (((r_n >> 2) & 1) == (c_n >> 2)) & (rel_n >= 0)
    slope_n = jnp.exp2(-2.0 * ((r_n >> 3).astype(F32) + 1.0))
    bias_n = jnp.where(valid_n, -slope_n * rel_n.astype(F32), MASK_VALUE)
    head_col = lax.broadcasted_iota(jnp.int32, (n_rows, 1), 0) >> 3
    sink_col = jnp.where(head_col == 0, sink_ref[0],
                         jnp.where(head_col == 1, sink_ref[1], jnp.where(head_col == 2, sink_ref[2], sink_ref[3])))
    row128 = lax.broadcasted_iota(jnp.int32, (WINDOW, LANES), 0)
    for p in range(sb // 2):
        rows = slice(p * PAIR_ROWS, (p + 1) * PAIR_ROWS)
        q0 = proj_scr[rows, 0:LANES] * (HEAD_DIM ** -0.5)
        q1 = proj_scr[rows, LANES:2 * LANES] * (HEAD_DIM ** -0.5)
        q0r = pltpu.roll(q0, HEAD_DIM, axis=1)
        q1r = pltpu.roll(q1, HEAD_DIM, axis=1)
        zero = jnp.zeros_like(q0)
        q4 = jnp.concatenate([jnp.where(lo, q0, zero), jnp.where(lo, q0r, zero),
                              jnp.where(lo, zero, q1r), jnp.where(lo, zero, q1)], axis=0).astype(BF16)
        kn8 = proj_scr[rows, C_AK:C_AK + LANES]
        vn8 = proj_scr[rows, C_AV:C_AV + LANES]
        ka, kb = ck_ref[2 * p], ck_ref[2 * p + 1]
        va, vb = cv_ref[2 * p], cv_ref[2 * p + 1]
        s_c = _nt_dot(q4, jnp.concatenate([ka, kb], axis=0).astype(BF16)) + bias_c
        s_n = _nt_dot(q4, kn8.astype(BF16)) + bias_n
        m = jnp.maximum(jnp.maximum(jnp.max(s_c, axis=-1, keepdims=True),
                                    jnp.max(s_n, axis=-1, keepdims=True)), sink_col)
        p_c = jnp.exp(s_c - m)
        p_n = jnp.exp(s_n - m)
        den = (jnp.sum(p_c, axis=-1, keepdims=True) + jnp.sum(p_n, axis=-1, keepdims=True)
               + jnp.exp(sink_col - m))
        o_all = (_dot(p_c.astype(BF16), jnp.concatenate([va, vb], axis=0).astype(BF16))
                 + _dot(p_n.astype(BF16), vn8.astype(BF16))) / den
        o = [o_all[h * PAIR_ROWS:(h + 1) * PAIR_ROWS, :] for h in range(4)]
        mix_scr[rows, 0:LANES] = jnp.where(lo, o[0], pltpu.roll(o[1], HEAD_DIM, axis=1))
        mix_scr[rows, LANES:2 * LANES] = jnp.where(lo, pltpu.roll(o[2], HEAD_DIM, axis=1), o[3])
        kn_hi = pltpu.roll(kn8, DEC_SEQ, axis=0)
        vn_hi = pltpu.roll(vn8, DEC_SEQ, axis=0)
        for idx, old, new, out_ref in ((2 * p, ka, kn_hi, knew_ref), (2 * p + 1, kb, kn8, knew_ref),
                                       (2 * p, va, vn_hi, vnew_ref), (2 * p + 1, vb, vn8, vnew_ref)):
            shifted = pltpu.roll(old, WINDOW - DEC_SEQ, axis=0)
            out_ref[idx, 0:WINDOW - SUBLANES, :] = shifted[0:WINDOW - SUBLANES, :]
            out_ref[idx, WINDOW - SUBLANES:WINDOW, :] = jnp.where(
                sub8 >= DEC_SEQ, new, shifted[WINDOW - SUBLANES:WINDOW, :])

    hgt = _nt_dot(w_hgt_ref[...], xb)
    q_t = _silu(hgt[0:W_GRP, :])
    sig_pos, sig_neg = _sigmoid_pair(hgt[W_GRP:2 * W_GRP, :])
    lbc = lbc_ref[...]
    f_t = sig_pos + jnp.maximum(lbc, LB_FLOOR) * sig_neg
    k_t = (1.0 - lbc) * sig_neg
    for b in range(sb):
        st = s_ref[b]
        for step in range(DEC_SEQ):
            c = b * DEC_SEQ + step
            v_row = proj_scr[c:c + 1, C_BI:C_BI + W_GRP]
            v_exp = jnp.concatenate(
                [jnp.broadcast_to(v_row[:, h * HEAD_DIM:(h + 1) * HEAD_DIM], (HEAD_DIM, HEAD_DIM))
                 for h in range(N_HG)], axis=0)
            st = f_t[:, c:c + 1] * st + k_t[:, c:c + 1] * v_exp
            qs = q_t[:, c:c + 1] * st
            hg_scr[c:c + 1, :] = jnp.concatenate(
                [jnp.sum(qs[h * HEAD_DIM:(h + 1) * HEAD_DIM, :], axis=0, keepdims=True) for h in range(N_HG)],
                axis=1)
        snew_ref[b] = st
    r256 = lax.broadcasted_iota(jnp.int32, (W_GRP, W_GRP), 0)
    c256 = lax.broadcasted_iota(jnp.int32, (W_GRP, W_GRP), 1)
    head_ones = jnp.where((r256 >> 6) == (c256 >> 6), 1.0, 0.0).astype(BF16)
    o = hg_scr[...]
    o2 = o * o
    o2_hi = o2.astype(BF16)
    o2_lo = (o2 - o2_hi.astype(F32)).astype(BF16)
    ms = (_dot(o2_hi, head_ones) + _dot(o2_lo, head_ones)) * (1.0 / HEAD_DIM)
    o = o * lax.rsqrt(ms + RMS_EPS) * ng_ref[...]
    mix_scr[:, W_GRP:2 * W_GRP] = o * _silu(proj_scr[:, C_BG:C_BG + W_GRP])

    for j in range(8):
        cd_scr[j] = proj_scr[:, C_CB + j * LANES:C_CB + (j + 1) * LANES]
    cw = cw_ref[...]
    ps = ps_ref[...]
    cp = lax.broadcasted_iota(jnp.int32, (sb, W_GRP), 1) >> 6
    width = jnp.left_shift(2, cp)
    step_rows = lambda step: pl.ds(step, sb, stride=DEC_SEQ)

    def step_load(tile, step):
        return jnp.concatenate([cd_scr[tile, step_rows(step), :], cd_scr[tile + 1, step_rows(step), :]], axis=1)

    def step_store(tile, step, val):
        od_scr[tile, step_rows(step), :] = val[:, 0:LANES]
        od_scr[tile + 1, step_rows(step), :] = val[:, LANES:2 * LANES]

    u = [cs_ref[0], cs_ref[1]]
    ext = [pb_ref[i] for i in range(POOL_BUF)]
    for step in range(DEC_SEQ):
        u.append(step_load(2, step) * step_load(4, step))
        ext.append(step_load(6, step))
    cnew_ref[0] = u[DEC_SEQ]
    cnew_ref[1] = u[DEC_SEQ + 1]
    for i in range(POOL_BUF):
        pnew_ref[i] = ext[DEC_SEQ + i]
    for step in range(DEC_SEQ):
        yc = u[step] * cw[0:1, :] + u[step + 1] * cw[1:2, :] + u[step + 2] * cw[2:3, :]
        step_store(0, step, step_load(0, step) * yc)
        top = POOL_BUF + step
        acc = ext[top]
        sums = {}
        for jj in range(1, 16):
            acc = acc + ext[top - jj]
            if jj + 1 in POOL_WINDOWS:
                sums[jj + 1] = acc
        win = jnp.where(cp == 0, sums[2], jnp.where(cp == 1, sums[4], jnp.where(cp == 2, sums[8], sums[16])))
        cnt = jnp.minimum(PAST_LEN + step + 1, width).astype(F32)
        pooled = win / cnt - ext[top]
        step_store(2, step, _dot(pooled.astype(BF16), pw_ref[...]) * ps)
    for j in range(4):
        mix_scr[:, 2 * W_GRP + j * LANES:2 * W_GRP + (j + 1) * LANES] = od_scr[j]

    y = _dot(mix_scr[...].astype(BF16), w_o_ref[...])
    y_ref[...] = _layer_norm(ALPHA * x + y, g_ref[...], b_ref[...])


def _sample_mixer_call(l, x2d, ck, cv, s, cs_tm, pb_tm, w):
    m = x2d.shape[0]
    nb = m // DEC_SEQ
    sb = SEQ_BLOCK
    rows = sb * DEC_SEQ
    single = pl.Buffered(1)
    out_shape = (
        jax.ShapeDtypeStruct((m, D_MODEL), F32),
        jax.ShapeDtypeStruct((nb, WINDOW, LANES), F32),
        jax.ShapeDtypeStruct((nb, WINDOW, LANES), F32),
        jax.ShapeDtypeStruct((nb, W_GRP, HEAD_DIM), F32),
        jax.ShapeDtypeStruct((CONV_W - 1, nb, W_GRP), F32),
        jax.ShapeDtypeStruct((POOL_BUF, nb, W_GRP), F32),
    )
    return pl.pallas_call(
        _sample_mixer_kernel,
        name="mixer_sample",
        grid=(nb // sb,),
        in_specs=[pl.BlockSpec((rows, D_MODEL), lambda i: (i, 0)),
                  pl.BlockSpec((None, sb, WINDOW, LANES), lambda i: (l, i, 0, 0)),
                  pl.BlockSpec((None, sb, WINDOW, LANES), lambda i: (l, i, 0, 0)),
                  pl.BlockSpec((None, sb, W_GRP, HEAD_DIM), lambda i: (l, i, 0, 0)),
                  pl.BlockSpec((None, CONV_W - 1, sb, W_GRP), lambda i: (l, 0, i, 0)),
                  pl.BlockSpec((None, POOL_BUF, sb, W_GRP), lambda i: (l, 0, i, 0)),
                  pl.BlockSpec((D_MODEL, D_IN), lambda i: (0, 0), pipeline_mode=single),
                  _full_spec((2 * W_GRP, D_MODEL)),
                  pl.BlockSpec(memory_space=pltpu.SMEM),
                  _full_spec((1, W_GRP)), _full_spec((W_GRP, 1)), _full_spec((1, W_GRP)),
                  _full_spec((CONV_W, W_GRP)), _full_spec((W_GRP, W_GRP)), _full_spec((1, W_GRP)),
                  pl.BlockSpec((D_MODEL, D_MODEL), lambda i: (0, 0), pipeline_mode=single),
                  _full_spec((1, D_MODEL)), _full_spec((1, D_MODEL))],
        out_specs=(pl.BlockSpec((rows, D_MODEL), lambda i: (i, 0)),
                   pl.BlockSpec((sb, WINDOW, LANES), lambda i: (i, 0, 0)),
                   pl.BlockSpec((sb, WINDOW, LANES), lambda i: (i, 0, 0)),
                   pl.BlockSpec((sb, W_GRP, HEAD_DIM), lambda i: (i, 0, 0)),
                   pl.BlockSpec((CONV_W - 1, sb, W_GRP), lambda i: (0, i, 0)),
                   pl.BlockSpec((POOL_BUF, sb, W_GRP), lambda i: (0, i, 0))),
        out_shape=out_shape,
        scratch_shapes=[pltpu.VMEM((rows, D_IN), F32),
                        pltpu.VMEM((rows, D_MODEL), F32),
                        pltpu.VMEM((rows, W_GRP), F32),
                        pltpu.VMEM((8, rows, LANES), F32),
                        pltpu.VMEM((4, rows, LANES), F32)],
        compiler_params=_compiler_params(("arbitrary",)),
    )(x2d, ck, cv, s, cs_tm, pb_tm, w["w_in"], w["w_hgt"], w["sink"], w["lb"], w["lb_col"], w["ng"], w["cw"],
      w["pw"], w["ps"], w["w_o"], w["ln1_g"], w["ln1_b"])


def _xattn_sample_kernel(x_ref, mk_ref, mv_ref, wq_ref, wo_ref, g_ref, b_ref, o_ref, att_scr):
    sb = mk_ref.shape[0]
    x = x_ref[...]
    q = _dot(x.astype(BF16), wq_ref[...]) * (HEAD_DIM ** -0.5)
    n_rows = N_XH * PAIR_ROWS
    lane = lax.broadcasted_iota(jnp.int32, (PAIR_ROWS, D_X), 1) >> 6
    r_s = lax.broadcasted_iota(jnp.int32, (n_rows, 2 * N_MEM), 0)
    c_s = lax.broadcasted_iota(jnp.int32, (n_rows, 2 * N_MEM), 1)
    own = ((r_s >> 2) & 1) == (c_s >> 8)
    for p in range(sb // 2):
        rows = slice(p * PAIR_ROWS, (p + 1) * PAIR_ROWS)
        q8 = q[rows, :]
        zero = jnp.zeros_like(q8)
        q4 = jnp.concatenate([jnp.where(lane == h, q8, zero) for h in range(N_XH)], axis=0).astype(BF16)
        k2 = jnp.concatenate([mk_ref[2 * p], mk_ref[2 * p + 1]], axis=0).astype(BF16)
        v2 = jnp.concatenate([mv_ref[2 * p], mv_ref[2 * p + 1]], axis=0).astype(BF16)
        s = jnp.where(own, _nt_dot(q4, k2), MASK_VALUE)
        m = jnp.max(s, axis=-1, keepdims=True)
        pr = jnp.exp(s - m)
        den = jnp.sum(pr, axis=-1, keepdims=True)
        o_all = _dot(pr.astype(BF16), v2) / den
        o8 = jnp.zeros_like(q8)
        for h in range(N_XH):
            o8 = jnp.where(lane == h, o_all[h * PAIR_ROWS:(h + 1) * PAIR_ROWS, :], o8)
        att_scr[rows, :] = o8.astype(BF16)
    y = _dot(att_scr[...], wo_ref[...])
    o_ref[...] = _layer_norm(ALPHA * x + y, g_ref[...], b_ref[...])


def _xattn_sample_call(l, x2d, mk, mv, w):
    m = x2d.shape[0]
    nb = m // DEC_SEQ
    sb = SEQ_BLOCK
    rows = sb * DEC_SEQ
    return pl.pallas_call(
        _xattn_sample_kernel,
        name="xattn_sample",
        grid=(nb // sb,),
        in_specs=[pl.BlockSpec((rows, D_MODEL), lambda i: (i, 0)),
                  pl.BlockSpec((None, sb, N_MEM, D_X), lambda i: (l, i, 0, 0)),
                  pl.BlockSpec((None, sb, N_MEM, D_X), lambda i: (l, i, 0, 0)),
                  _full_spec((D_MODEL, D_X)), _full_spec((D_X, D_MODEL)),
                  _full_spec((1, D_MODEL)), _full_spec((1, D_MODEL))],
        out_specs=pl.BlockSpec((rows, D_MODEL), lambda i: (i, 0)),
        out_shape=jax.ShapeDtypeStruct((m, D_MODEL), F32),
        scratch_shapes=[pltpu.VMEM((rows, D_X), BF16)],
        compiler_params=_compiler_params(("arbitrary",)),
    )(x2d, mk, mv, w["w_xq"], w["w_xo"], w["ln2_g"], w["ln2_b"])


def _sample_layer(l, x2d, ck, cv, s, cs_tm, pb_tm, mk, mv, w):
    x2d, kn, vn, sn, cn, pn = _sample_mixer_call(l, x2d, ck, cv, s, cs_tm, pb_tm, w)
    x2d = _xattn_sample_call(l, x2d, mk, mv, w)
    x2d = _ffn_call(x2d, w["w_gate"], w["w_up"], w["w_down"], w["ln3_g"], w["ln3_b"])
    return x2d, kn, vn, sn, cn, pn


def _row(v):
    return v.reshape(1, -1).astype(F32)


def _pool_block_diag(pool_w):
    z = jnp.zeros((W_GRP, W_GRP), pool_w.dtype)
    for gi in range(4):
        z = lax.dynamic_update_slice(z, pool_w[gi], (gi * HEAD_DIM, gi * HEAD_DIM))
    return z


def _hgrn_lower_bounds(lb_param):
    p = jax.nn.softmax(lb_param.astype(F32), axis=0)
    return jnp.cumsum(p, axis=0) - p[0:1]


def _prompt_layer(x, mem2d, w):
    bsz = x.shape[0]
    mk = _matmul_call(mem2d, w["w_xk"]).reshape(bsz, N_MEM, D_X)
    mv = _matmul_call(mem2d, w["w_xv"]).reshape(bsz, N_MEM, D_X)
    x, kn, vn, sn, cn, pn = _prompt_mixer_call(
        x, w["w_in"], w["sink"], w["lb"], w["ng"], w["cw"], w["pw"], w["ps"], w["w_o"], w["ln1_g"], w["ln1_b"])
    x = _xattn_prompt_call(x, mk, mv, w["w_xq"], w["w_xo"], w["ln2_g"], w["ln2_b"])
    t = x.shape[1]
    x = _ffn_call(x.reshape(bsz * t, D_MODEL), w["w_gate"], w["w_up"], w["w_down"],
                  w["ln3_g"], w["ln3_b"]).reshape(bsz, t, D_MODEL)
    return x, kn, vn, sn, cn, pn, mk, mv


def _layer_weights(l, lb_all, w_in, attn_sink, hgrn_norm_g, conv_w, pool_w, pool_scale, w_o, ln1_g, ln1_b,
                   w_xq, w_xk, w_xv, w_xo, ln2_g, ln2_b, w_gate, w_up, w_down, ln3_g, ln3_b):
    bf = lambda a: a[l].astype(BF16)
    return dict(
        w_in=bf(w_in), w_hgt=w_in[l][:, C_BQ:C_BI].T.astype(BF16),
        sink=attn_sink[l].astype(F32), lb=_row(lb_all[l]), lb_col=lb_all[l].reshape(-1, 1).astype(F32),
        ng=_row(hgrn_norm_g[l]),
        cw=conv_w[l].astype(F32), pw=_pool_block_diag(pool_w[l]).astype(BF16), ps=_row(pool_scale[l]),
        w_o=bf(w_o), ln1_g=_row(ln1_g[l]), ln1_b=_row(ln1_b[l]),
        w_xq=bf(w_xq), w_xk=bf(w_xk), w_xv=bf(w_xv), w_xo=bf(w_xo), ln2_g=_row(ln2_g[l]), ln2_b=_row(ln2_b[l]),
        w_gate=bf(w_gate), w_up=bf(w_up), w_down=bf(w_down), ln3_g=_row(ln3_g[l]), ln3_b=_row(ln3_b[l]))


def kernel(x_prompt, x_sample, cache_swa_k, cache_swa_v, state_hgrn, state_conv, state_pool, cache_mem_k,
           cache_mem_v, mem_prompt, emb_ln_g, emb_ln_b, w_in, attn_sink, hgrn_lb, hgrn_norm_g, conv_w, pool_w,
           pool_scale, w_o, ln1_g, ln1_b, w_xq, w_xk, w_xv, w_xo, ln2_g, ln2_b, w_gate, w_up, w_down, ln3_g,
           ln3_b):
    bp, t, _ = x_prompt.shape
    lb_all = _hgrn_lower_bounds(hgrn_lb)
    hp = _ln_call(x_prompt.reshape(bp * t, D_MODEL), _row(emb_ln_g), _row(emb_ln_b)).reshape(bp, t, D_MODEL)
    mem2d = mem_prompt.reshape(bp * N_MEM, D_MODEL)
    bs, ts, _ = x_sample.shape
    hs = _ln_call(x_sample.reshape(bs * ts, D_MODEL), _row(emb_ln_g), _row(emb_ln_b))
    ck = cache_swa_k.reshape(DEPTH, bs, WINDOW, LANES)
    cv = cache_swa_v.reshape(DEPTH, bs, WINDOW, LANES)
    st = state_hgrn.reshape(DEPTH, bs, W_GRP, HEAD_DIM)
    cs_tm = state_conv.transpose(0, 2, 1, 3)
    pb_tm = state_pool.transpose(0, 2, 1, 3)
    mk_s = cache_mem_k.reshape(DEPTH, bs, N_MEM, D_X)
    mv_s = cache_mem_v.reshape(DEPTH, bs, N_MEM, D_X)
    outs = [[] for _ in range(7)]
    souts = [[] for _ in range(5)]
    for l in range(DEPTH):
        w = _layer_weights(l, lb_all, w_in, attn_sink, hgrn_norm_g, conv_w, pool_w, pool_scale, w_o, ln1_g,
                           ln1_b, w_xq, w_xk, w_xv, w_xo, ln2_g, ln2_b, w_gate, w_up, w_down, ln3_g, ln3_b)
        res = _prompt_layer(hp, mem2d, w)
        hp = res[0]
        for acc, r in zip(outs, res[1:]):
            acc.append(r)
        sres = _sample_layer(l, hs, ck, cv, st, cs_tm, pb_tm, mk_s, mv_s, w)
        hs = sres[0]
        for acc, r in zip(souts, sres[1:]):
            acc.append(r)
    pk, pv, ps, pc, pp, pmk, pmv = [jnp.stack(o) for o in outs]
    sk, sv, ss, sc, sp = [jnp.stack(o) for o in souts]
    return (hp, hs.reshape(bs, ts, D_MODEL),
            pk.reshape(DEPTH, bp, WINDOW, N_KV, HEAD_DIM), pv.reshape(DEPTH, bp, WINDOW, N_KV, HEAD_DIM),
            ps.reshape(DEPTH, bp, N_HG, HEAD_DIM, HEAD_DIM), pc, pp,
            pmk.reshape(DEPTH, bp, N_MEM, N_XH, HEAD_DIM), pmv.reshape(DEPTH, bp, N_MEM, N_XH, HEAD_DIM),
            sk.reshape(DEPTH, bs, WINDOW, N_KV, HEAD_DIM), sv.reshape(DEPTH, bs, WINDOW, N_KV, HEAD_DIM),
            ss.reshape(DEPTH, bs, N_HG, HEAD_DIM, HEAD_DIM),
            sc.transpose(0, 2, 1, 3), sp.transpose(0, 2, 1, 3))
```

```python
import functools

import jax
import jax.numpy as jnp
from jax import lax
from jax.experimental import pallas as pl
from jax.experimental.pallas import tpu as pltpu

F32 = jnp.float32
BF16 = jnp.bfloat16

D_MODEL = 1024
DEPTH = 4
HEAD_DIM = 64
W_GRP = 256
N_KV = 2
WINDOW = 128
N_HG = 4
CONV_W = 3
POOL_WINDOWS = (2, 4, 8, 16)
POOL_BUF = 15
N_MEM = 256
N_XH = 4
D_X = 256
D_FF = 2816
D_IN = 2560
ALPHA = (2 * DEPTH) ** 0.25
LN_EPS = 1e-5
RMS_EPS = 1e-6
MASK_VALUE = -1e30
LB_FLOOR = 1e-30
PAST_LEN = 8192

C_AQ, C_AK, C_AV = 0, 256, 384
C_BQ, C_BF, C_BI, C_BG = 512, 768, 1024, 1280
C_CB, C_CC, C_CH = 1536, 1792, 2048
C_DV = 2304

LANES = 128
SUBLANES = 8
VMEM_LIMIT_BYTES = 56 * 1024 * 1024

TOKEN_BLOCK = 512
HG_CHUNK = 64
HG_MID = HG_CHUNK // 2 - 1


def _nt_dot(a, b):
    return lax.dot_general(a, b, (((1,), (1,)), ((), ())), preferred_element_type=F32)


def _tn_dot(a, b):
    return lax.dot_general(a, b, (((0,), (0,)), ((), ())), preferred_element_type=F32)


def _dot(a, b):
    return jnp.dot(a, b, preferred_element_type=F32)


def _layer_norm(x, g, b):
    mu = jnp.mean(x, axis=-1, keepdims=True)
    xc = x - mu
    var = jnp.mean(xc * xc, axis=-1, keepdims=True)
    return xc * lax.rsqrt(var + LN_EPS) * g + b


def _sigmoid_pair(z):
    e = jnp.exp(-jnp.abs(z))
    inv = 1.0 / (1.0 + e)
    small = e * inv
    pos = z >= 0
    return jnp.where(pos, inv, small), jnp.where(pos, small, inv)


def _silu(z):
    s, _ = _sigmoid_pair(z)
    return z * s


def _full_spec(shape):
    nd = len(shape)
    return pl.BlockSpec(shape, lambda *_: (0,) * nd)


def _compiler_params(sem, flags=None):
    return pltpu.CompilerParams(dimension_semantics=sem, vmem_limit_bytes=VMEM_LIMIT_BYTES, flags=flags)


def _ln_kernel(x_ref, g_ref, b_ref, o_ref):
    o_ref[...] = _layer_norm(x_ref[...], g_ref[...], b_ref[...])


def _ln_call(x2d, g, b):
    m = x2d.shape[0]
    return pl.pallas_call(
        _ln_kernel,
        name="input_ln",
        grid=(m // TOKEN_BLOCK,),
        in_specs=[pl.BlockSpec((TOKEN_BLOCK, D_MODEL), lambda i: (i, 0)),
                  _full_spec((1, D_MODEL)), _full_spec((1, D_MODEL))],
        out_specs=pl.BlockSpec((TOKEN_BLOCK, D_MODEL), lambda i: (i, 0)),
        out_shape=jax.ShapeDtypeStruct((m, D_MODEL), F32),
        compiler_params=_compiler_params(("arbitrary",)),
    )(x2d, g, b)


def _matmul_kernel(x_ref, w_ref, o_ref):
    o_ref[...] = _dot(x_ref[...].astype(BF16), w_ref[...])


def _matmul_call(x2d, w):
    m, k = x2d.shape
    n = w.shape[1]
    return pl.pallas_call(
        _matmul_kernel,
        name="mem_proj",
        grid=(m // TOKEN_BLOCK,),
        in_specs=[pl.BlockSpec((TOKEN_BLOCK, k), lambda i: (i, 0)), _full_spec((k, n))],
        out_specs=pl.BlockSpec((TOKEN_BLOCK, n), lambda i: (i, 0)),
        out_shape=jax.ShapeDtypeStruct((m, n), F32),
        compiler_params=_compiler_params(("arbitrary",)),
    )(x2d, w)


def _ffn_kernel(x_ref, wg_ref, wu_ref, wd_ref, g_ref, b_ref, o_ref):
    x = x_ref[...]
    xb = x.astype(BF16)
    h = _silu(_dot(xb, wg_ref[...])) * _dot(xb, wu_ref[...])
    y = _dot(h.astype(BF16), wd_ref[...])
    o_ref[...] = _layer_norm(ALPHA * x + y, g_ref[...], b_ref[...])


def _ffn_call(x2d, wg, wu, wd, g, b):
    m = x2d.shape[0]
    single = pl.Buffered(1)
    return pl.pallas_call(
        _ffn_kernel,
        name="ffn",
        grid=(m // TOKEN_BLOCK,),
        in_specs=[pl.BlockSpec((TOKEN_BLOCK, D_MODEL), lambda i: (i, 0)),
                  pl.BlockSpec((D_MODEL, D_FF), lambda i: (0, 0), pipeline_mode=single),
                  pl.BlockSpec((D_MODEL, D_FF), lambda i: (0, 0), pipeline_mode=single),
                  pl.BlockSpec((D_FF, D_MODEL), lambda i: (0, 0), pipeline_mode=single),
                  _full_spec((1, D_MODEL)), _full_spec((1, D_MODEL))],
        out_specs=pl.BlockSpec((TOKEN_BLOCK, D_MODEL), lambda i: (i, 0)),
        out_shape=jax.ShapeDtypeStruct((m, D_MODEL), F32),
        compiler_params=_compiler_params(("arbitrary",)),
    )(x2d, wg, wu, wd, g, b)


def _xattn_prompt_kernel(x_ref, mk_ref, mv_ref, wq_ref, wo_ref, g_ref, b_ref, o_ref):
    x = x_ref[...]
    q = _dot(x.astype(BF16), wq_ref[...]) * (HEAD_DIM ** -0.5)
    mk = mk_ref[...].astype(BF16)
    mv = mv_ref[...].astype(BF16)
    lane_head = lax.broadcasted_iota(jnp.int32, q.shape, 1) >> 6
    o = jnp.zeros_like(q)
    for h in range(N_XH):
        qh = jnp.where(lane_head == h, q, 0.0).astype(BF16)
        s = _nt_dot(mk, qh)
        m = jnp.max(s, axis=0, keepdims=True)
        p = jnp.exp(s - m)
        den = jnp.sum(p, axis=0, keepdims=True)
        p = (p * (1.0 / den)).astype(BF16)
        o = jnp.where(lane_head == h, _tn_dot(p, mv), o)
    y = _dot(o.astype(BF16), wo_ref[...])
    o_ref[...] = _layer_norm(ALPHA * x + y, g_ref[...], b_ref[...])


def _xattn_prompt_call(x, mk, mv, wq, wo, g, b):
    bsz, t, _ = x.shape
    tb = min(TOKEN_BLOCK, t)
    return pl.pallas_call(
        _xattn_prompt_kernel,
        name="xattn_prompt",
        grid=(bsz, t // tb),
        in_specs=[pl.BlockSpec((None, tb, D_MODEL), lambda i, j: (i, j, 0)),
                  pl.BlockSpec((None, N_MEM, D_X), lambda i, j: (i, 0, 0)),
                  pl.BlockSpec((None, N_MEM, D_X), lambda i, j: (i, 0, 0)),
                  _full_spec((D_MODEL, D_X)), _full_spec((D_X, D_MODEL)),
                  _full_spec((1, D_MODEL)), _full_spec((1, D_MODEL))],
        out_specs=pl.BlockSpec((None, tb, D_MODEL), lambda i, j: (i, j, 0)),
        out_shape=jax.ShapeDtypeStruct((bsz, t, D_MODEL), F32),
        compiler_params=_compiler_params(("arbitrary", "arbitrary")),
    )(x, mk, mv, wq, wo, g, b)


def _swa_bias_table():
    c = lax.broadcasted_iota(jnp.int32, (2 * WINDOW, 4 * WINDOW), 0)
    r = lax.broadcasted_iota(jnp.int32, (2 * WINDOW, 4 * WINDOW), 1)
    head = r >> 7
    rel = (r & (WINDOW - 1)) + WINDOW - c
    slope = jnp.exp2(-2.0 * (head.astype(F32) + 1.0))
    valid = (rel >= 0) & (rel <= WINDOW)
    return jnp.where(valid, -slope * rel.astype(F32), MASK_VALUE)


def _prompt_mixer_kernel(x_ref, w_in_ref, sink_ref, lb_ref, ng_ref, cw_ref, pw_ref, ps_ref, w_o_ref,
                         g_ref, b_ref,
                         y_ref, knew_ref, vnew_ref, snew_ref, cnew_ref, pnew_ref,
                         proj_scr, kext_scr, vext_scr, st_scr, u_scr, p_scr, bias_scr, mix_scr, hg_scr):
    tb = x_ref.shape[0]
    n_qb = tb // WINDOW
    n_ch = tb // HG_CHUNK
    bi = pl.program_id(0)
    ti = pl.program_id(1)
    last = ti == pl.num_programs(1) - 1

    @pl.when((bi == 0) & (ti == 0))
    def _():
        bias_scr[...] = _swa_bias_table()

    @pl.when(ti == 0)
    def _():
        kext_scr[0:WINDOW, :] = jnp.zeros((WINDOW, LANES), BF16)
        vext_scr[0:WINDOW, :] = jnp.zeros((WINDOW, LANES), BF16)
        st_scr[...] = jnp.zeros(st_scr.shape, F32)
        u_scr[0:SUBLANES, :] = jnp.zeros((SUBLANES, W_GRP), F32)
        p_scr[0:16, :] = jnp.zeros((16, W_GRP), F32)

    x = x_ref[...]
    xb = x.astype(BF16)
    proj_scr[:, 0:C_BQ] = _dot(xb, w_in_ref[:, 0:C_BQ])
    proj_scr[:, C_BQ:C_CB] = _dot(xb, w_in_ref[:, C_BQ:C_CB])

    kext_scr[WINDOW:WINDOW + tb, :] = proj_scr[:, C_AK:C_AK + LANES].astype(BF16)
    vext_scr[WINDOW:WINDOW + tb, :] = proj_scr[:, C_AV:C_AV + LANES].astype(BF16)
    lane = lax.broadcasted_iota(jnp.int32, (WINDOW, LANES), 1)
    lo = lane < HEAD_DIM
    key_row = lax.broadcasted_iota(jnp.int32, (2 * WINDOW, 4 * WINDOW), 0)
    head_lane = lax.broadcasted_iota(jnp.int32, (1, 4 * WINDOW), 1) >> 7
    sink_row = jnp.where(head_lane == 0, sink_ref[0],
                         jnp.where(head_lane == 1, sink_ref[1],
                                   jnp.where(head_lane == 2, sink_ref[2], sink_ref[3])))
    for j in range(n_qb):
        rows = slice(j * WINDOW, (j + 1) * WINDOW)
        q0 = proj_scr[rows, 0:LANES] * (HEAD_DIM ** -0.5)
        q1 = proj_scr[rows, LANES:2 * LANES] * (HEAD_DIM ** -0.5)
        q0r = pltpu.roll(q0, HEAD_DIM, axis=1)
        q1r = pltpu.roll(q1, HEAD_DIM, axis=1)
        zero = jnp.zeros_like(q0)
        q4 = jnp.concatenate([jnp.where(lo, q0, zero), jnp.where(lo, q0r, zero),
                              jnp.where(lo, zero, q1r), jnp.where(lo, zero, q1)], axis=0).astype(BF16)
        kj = kext_scr[j * WINDOW:(j + 2) * WINDOW, :]
        vj = vext_scr[j * WINDOW:(j + 2) * WINDOW, :]
        s = _nt_dot(kj, q4) + bias_scr[...]
        if j == 0:
            s = jnp.where((ti == 0) & (key_row < WINDOW), MASK_VALUE, s)
        m = jnp.maximum(jnp.max(s, axis=0, keepdims=True), sink_row)
        p = jnp.exp(s - m)
        den = jnp.sum(p, axis=0, keepdims=True) + jnp.exp(sink_row - m)
        p = (p * (1.0 / den)).astype(BF16)
        o_all = _tn_dot(p, vj)
        o = [o_all[h * WINDOW:(h + 1) * WINDOW, :] for h in range(4)]
        c0 = jnp.where(lo, o[0], pltpu.roll(o[1], HEAD_DIM, axis=1))
        c1 = jnp.where(lo, pltpu.roll(o[2], HEAD_DIM, axis=1), o[3])
        mix_scr[rows, 0:LANES] = c0.astype(BF16)
        mix_scr[rows, LANES:2 * LANES] = c1.astype(BF16)
    kext_scr[0:WINDOW, :] = kext_scr[tb:tb + WINDOW, :]
    vext_scr[0:WINDOW, :] = vext_scr[tb:tb + WINDOW, :]
    proj_scr[:, C_CB:D_IN] = _dot(xb, w_in_ref[:, C_CB:D_IN])

    @pl.when(last)
    def _():
        knew_ref[...] = proj_scr[tb - WINDOW:tb, C_AK:C_AK + LANES]
        vnew_ref[...] = proj_scr[tb - WINDOW:tb, C_AV:C_AV + LANES]

    lb = lb_ref[...]
    lbf = jnp.maximum(lb, LB_FLOOR)
    one_m_lb = 1.0 - lb
    ng = ng_ref[...]
    r256 = lax.broadcasted_iota(jnp.int32, (W_GRP, W_GRP), 0)
    c256 = lax.broadcasted_iota(jnp.int32, (W_GRP, W_GRP), 1)
    same_head = (r256 >> 6) == (c256 >> 6)
    head_ones = jnp.where(same_head, 1.0, 0.0).astype(BF16)
    rc = lax.broadcasted_iota(jnp.int32, (HG_CHUNK, W_GRP), 0)
    cc = lax.broadcasted_iota(jnp.int32, (HG_CHUNK, W_GRP), 1)
    causal = (cc & (HG_CHUNK - 1)) <= rc

    def chunk_body(ci, carry):
        r0 = pl.multiple_of(ci * HG_CHUNK, HG_CHUNK)
        rows = pl.ds(r0, HG_CHUNK)
        q = _silu(proj_scr[rows, C_BQ:C_BQ + W_GRP])
        sig_pos, sig_neg = _sigmoid_pair(proj_scr[rows, C_BF:C_BF + W_GRP])
        g = jnp.log(sig_pos + lbf * sig_neg)
        k = one_m_lb * sig_neg
        v = proj_scr[rows, C_BI:C_BI + W_GRP]
        cum = g
        for sh in (1, 2, 4, 8, 16, 32):
            cum = cum + jnp.where(rc >= sh, pltpu.roll(cum, sh, axis=0), 0.0)
        ref = cum[HG_MID:HG_MID + 1, :]
        tot = cum[HG_CHUNK - 1:HG_CHUNK, :]
        e_fwd = jnp.exp(cum - ref)
        e_bwd = jnp.exp(ref - cum)
        qp = q * e_fwd
        kp = k * e_bwd
        qs = (qp * jnp.exp(ref)).astype(BF16)
        ks = (kp * jnp.exp(tot - ref)).astype(BF16)
        vb = v.astype(BF16)
        bk = jnp.where(same_head, jnp.concatenate([kp] * N_HG, axis=0), 0.0).astype(BF16)
        bv = jnp.where(same_head, jnp.concatenate([v] * N_HG, axis=0), 0.0).astype(BF16)
        a = jnp.where(causal, _nt_dot(qp.astype(BF16), bk), 0.0)
        st = st_scr[...]
        o = _dot(a.astype(BF16), bv) + _nt_dot(qs, st.astype(BF16))
        st_scr[...] = st * jnp.exp(tot) + jnp.where(same_head, _tn_dot(vb, ks), 0.0)
        o2 = o * o
        o2_hi = o2.astype(BF16)
        o2_lo = (o2 - o2_hi.astype(F32)).astype(BF16)
        ms = (_dot(o2_hi, head_ones) + _dot(o2_lo, head_ones)) * (1.0 / HEAD_DIM)
        o = o * lax.rsqrt(ms + RMS_EPS) * ng
        o = o * _silu(proj_scr[rows, C_BG:C_BG + W_GRP])
        hg_scr[rows, :] = o
        return carry

    lax.fori_loop(0, n_ch, chunk_body, 0, unroll=2)
    mix_scr[:, W_GRP:2 * W_GRP] = hg_scr[...].astype(BF16)

    @pl.when(last)
    def _():
        s_t = st_scr[...].T
        for h in range(N_HG):
            snew_ref[h * HEAD_DIM:(h + 1) * HEAD_DIM, :] = (
                s_t[h * HEAD_DIM:(h + 1) * HEAD_DIM, h * HEAD_DIM:(h + 1) * HEAD_DIM])

    u_scr[SUBLANES:SUBLANES + tb, :] = proj_scr[:, C_CC:C_CC + W_GRP] * proj_scr[:, C_CH:C_CH + W_GRP]
    cw = cw_ref[...]
    yc = (u_scr[SUBLANES - 2:SUBLANES - 2 + tb, :] * cw[0:1, :]
          + u_scr[SUBLANES - 1:SUBLANES - 1 + tb, :] * cw[1:2, :]
          + u_scr[SUBLANES:SUBLANES + tb, :] * cw[2:3, :])
    mix_scr[:, 2 * W_GRP:3 * W_GRP] = (proj_scr[:, C_CB:C_CB + W_GRP] * yc).astype(BF16)
    tail = u_scr[tb:tb + SUBLANES, :]
    u_scr[0:SUBLANES, :] = tail

    @pl.when(last)
    def _():
        cnew_ref[...] = tail[SUBLANES - 2:SUBLANES, :]

    dv = proj_scr[:, C_DV:C_DV + W_GRP]
    p_scr[16:16 + tb, :] = dv
    ext = p_scr[...]
    s2 = ext + pltpu.roll(ext, 1, axis=0)
    s4 = s2 + pltpu.roll(s2, 2, axis=0)
    s8 = s4 + pltpu.roll(s4, 4, axis=0)
    s16 = s8 + pltpu.roll(s8, 8, axis=0)
    rp = lax.broadcasted_iota(jnp.int32, (tb, W_GRP), 0)
    cp = lax.broadcasted_iota(jnp.int32, (tb, W_GRP), 1)
    grp = cp >> 6
    win = jnp.where(grp == 0, s2[16:], jnp.where(grp == 1, s4[16:], jnp.where(grp == 2, s8[16:], s16[16:])))
    width = jnp.left_shift(2, grp)
    cnt = jnp.minimum(ti * tb + rp + 1, width).astype(F32)
    pooled = win / cnt - dv
    yd = _dot(pooled.astype(BF16), pw_ref[...]) * ps_ref[...]
    mix_scr[:, 3 * W_GRP:4 * W_GRP] = yd.astype(BF16)
    ptail = p_scr[tb:tb + 16, :]
    p_scr[0:16, :] = ptail

    @pl.when(last)
    def _():
        pnew_ref[...] = ptail[1:16, :]

    y = _dot(mix_scr[...], w_o_ref[...])
    y_ref[...] = _layer_norm(ALPHA * x + y, g_ref[...], b_ref[...])


def _prompt_mixer_call(x, w_in, sink, lb, ng, cw, pw_bd, ps, w_o, g, b):
    bsz, t, _ = x.shape
    tb = min(TOKEN_BLOCK, t)
    row = lambda i, j: (i, 0, 0)
    single = pl.Buffered(1)
    out_shape = (
        jax.ShapeDtypeStruct((bsz, t, D_MODEL), F32),
        jax.ShapeDtypeStruct((bsz, WINDOW, LANES), F32),
        jax.ShapeDtypeStruct((bsz, WINDOW, LANES), F32),
        jax.ShapeDtypeStruct((bsz, W_GRP, HEAD_DIM), F32),
        jax.ShapeDtypeStruct((bsz, CONV_W - 1, W_GRP), F32),
        jax.ShapeDtypeStruct((bsz, POOL_BUF, W_GRP), F32),
    )
    return pl.pallas_call(
        _prompt_mixer_kernel,
        name="mixer_prompt",
        grid=(bsz, t // tb),
        in_specs=[pl.BlockSpec((None, tb, D_MODEL), lambda i, j: (i, j, 0)),
                  pl.BlockSpec((D_MODEL, D_IN), lambda i, j: (0, 0), pipeline_mode=single),
                  pl.BlockSpec(memory_space=pltpu.SMEM),
                  _full_spec((1, W_GRP)), _full_spec((1, W_GRP)), _full_spec((CONV_W, W_GRP)),
                  _full_spec((W_GRP, W_GRP)), _full_spec((1, W_GRP)),
                  pl.BlockSpec((D_MODEL, D_MODEL), lambda i, j: (0, 0), pipeline_mode=single),
                  _full_spec((1, D_MODEL)), _full_spec((1, D_MODEL))],
        out_specs=(pl.BlockSpec((None, tb, D_MODEL), lambda i, j: (i, j, 0)),
                   pl.BlockSpec((None, WINDOW, LANES), row),
                   pl.BlockSpec((None, WINDOW, LANES), row),
                   pl.BlockSpec((None, W_GRP, HEAD_DIM), row),
                   pl.BlockSpec((None, CONV_W - 1, W_GRP), row),
                   pl.BlockSpec((None, POOL_BUF, W_GRP), row)),
        out_shape=out_shape,
        scratch_shapes=[
            pltpu.VMEM((tb, D_IN), F32),
            pltpu.VMEM((WINDOW + tb, LANES), BF16),
            pltpu.VMEM((WINDOW + tb, LANES), BF16),
            pltpu.VMEM((W_GRP, W_GRP), F32),
            pltpu.VMEM((SUBLANES + tb, W_GRP), F32),
            pltpu.VMEM((16 + tb, W_GRP), F32),
            pltpu.VMEM((2 * WINDOW, 4 * WINDOW), F32),
            pltpu.VMEM((tb, D_MODEL), BF16),
            pltpu.VMEM((tb, W_GRP), F32),
        ],
        compiler_params=_compiler_params(("arbitrary", "arbitrary")),
    )(x, w_in, sink, lb, ng, cw, pw_bd, ps, w_o, g, b)


DEC_SEQ = 4
SEQ_BLOCK = 16
PAIR_ROWS = 2 * DEC_SEQ


def _sample_mixer_kernel(x_ref, ck_ref, cv_ref, s_ref, cs_ref, pb_ref,
                         w_in_ref, w_hgt_ref, sink_ref, lb_ref, lbc_ref, ng_ref, cw_ref, pw_ref, ps_ref,
                         w_o_ref, g_ref, b_ref,
                         y_ref, knew_ref, vnew_ref, snew_ref, cnew_ref, pnew_ref,
                         proj_scr, mix_scr, hg_scr, cd_scr, od_scr):
    sb = ck_ref.shape[0]
    m_rows = sb * DEC_SEQ
    x = x_ref[...]
    xb = x.astype(BF16)
    proj_scr[...] = _dot(xb, w_in_ref[...])

    lane = lax.broadcasted_iota(jnp.int32, (PAIR_ROWS, LANES), 1)
    lo = lane < HEAD_DIM
    sub8 = lax.broadcasted_iota(jnp.int32, (PAIR_ROWS, LANES), 0)
    n_rows = 4 * PAIR_ROWS
    r_c = lax.broadcasted_iota(jnp.int32, (n_rows, 2 * WINDOW), 0)
    c_c = lax.broadcasted_iota(jnp.int32, (n_rows, 2 * WINDOW), 1)
    step_c = r_c & (DEC_SEQ - 1)
    rel_c = step_c + WINDOW - (c_c & (WINDOW - 1))
    valid_c = (((r_c >> 2) & 1) == (c_c >> 7)) & (rel_c <= WINDOW)
    slope_c = jnp.exp2(-2.0 * ((r_c >> 3).astype(F32) + 1.0))
    bias_c = jnp.where(valid_c, -slope_c * rel_c.astype(F32), MASK_VALUE)
    r_n = lax.broadcasted_iota(jnp.int32, (n_rows, PAIR_ROWS), 0)
    c_n = lax.broadcasted_iota(jnp.int32, (n_rows, PAIR_ROWS), 1)
    rel_n = (r_n & (DEC_SEQ - 1)) - (c_n & (DEC_SEQ - 1))
    valid_n = (((r_n >> 2) & 1) == (c_n >> 2)) & (rel_n >= 0)
    slope_n = jnp.exp2(-2.0 * ((r_n >> 3).astype(F32) + 1.0))
    bias_n = jnp.where(valid_n, -slope_n * rel_n.astype(F32), MASK_VALUE)
    head_col = lax.broadcasted_iota(jnp.int32, (n_rows, 1), 0) >> 3
    sink_col = jnp.where(head_col == 0, sink_ref[0],
                         jnp.where(head_col == 1, sink_ref[1], jnp.where(head_col == 2, sink_ref[2], sink_ref[3])))
    row128 = lax.broadcasted_iota(jnp.int32, (WINDOW, LANES), 0)
    for p in range(sb // 2):
        rows = slice(p * PAIR_ROWS, (p + 1) * PAIR_ROWS)
        q0 = proj_scr[rows, 0:LANES] * (HEAD_DIM ** -0.5)
        q1 = proj_scr[rows, LANES:2 * LANES] * (HEAD_DIM ** -0.5)
        q0r = pltpu.roll(q0, HEAD_DIM, axis=1)
        q1r = pltpu.roll(q1, HEAD_DIM, axis=1)
        zero = jnp.zeros_like(q0)
        q4 = jnp.concatenate([jnp.where(lo, q0, zero), jnp.where(lo, q0r, zero),
                              jnp.where(lo, zero, q1r), jnp.where(lo, zero, q1)], axis=0).astype(BF16)
        kn8 = proj_scr[rows, C_AK:C_AK + LANES]
        vn8 = proj_scr[rows, C_AV:C_AV + LANES]
        ka, kb = ck_ref[2 * p], ck_ref[2 * p + 1]
        va, vb = cv_ref[2 * p], cv_ref[2 * p + 1]
        s_c = _nt_dot(q4, jnp.concatenate([ka, kb], axis=0).astype(BF16)) + bias_c
        s_n = _nt_dot(q4, kn8.astype(BF16)) + bias_n
        m = jnp.maximum(jnp.maximum(jnp.max(s_c, axis=-1, keepdims=True),
                                    jnp.max(s_n, axis=-1, keepdims=True)), sink_col)
        p_c = jnp.exp(s_c - m)
        p_n = jnp.exp(s_n - m)
        den = (jnp.sum(p_c, axis=-1, keepdims=True) + jnp.sum(p_n, axis=-1, keepdims=True)
               + jnp.exp(sink_col - m))
        o_all = (_dot(p_c.astype(BF16), jnp.concatenate([va, vb], axis=0).astype(BF16))
                 + _dot(p_n.astype(BF16), vn8.astype(BF16))) / den
        o = [o_all[h * PAIR_ROWS:(h + 1) * PAIR_ROWS, :] for h in range(4)]
        mix_scr[rows, 0:LANES] = jnp.where(lo, o[0], pltpu.roll(o[1], HEAD_DIM, axis=1))
        mix_scr[rows, LANES:2 * LANES] = jnp.where(lo, pltpu.roll(o[2], HEAD_DIM, axis=1), o[3])
        kn_hi = pltpu.roll(kn8, DEC_SEQ, axis=0)
        vn_hi = pltpu.roll(vn8, DEC_SEQ, axis=0)
        for idx, old, new, out_ref in ((2 * p, ka, kn_hi, knew_ref), (2 * p + 1, kb, kn8, knew_ref),
                                       (2 * p, va, vn_hi, vnew_ref), (2 * p + 1, vb, vn8, vnew_ref)):
            shifted = pltpu.roll(old, WINDOW - DEC_SEQ, axis=0)
            out_ref[idx, 0:WINDOW - SUBLANES, :] = shifted[0:WINDOW - SUBLANES, :]
            out_ref[idx, WINDOW - SUBLANES:WINDOW, :] = jnp.where(
                sub8 >= DEC_SEQ, new, shifted[WINDOW - SUBLANES:WINDOW, :])

    hgt = _nt_dot(w_hgt_ref[...], xb)
    q_t = _silu(hgt[0:W_GRP, :])
    sig_pos, sig_neg = _sigmoid_pair(hgt[W_GRP:2 * W_GRP, :])
    lbc = lbc_ref[...]
    f_t = sig_pos + jnp.maximum(lbc, LB_FLOOR) * sig_neg
    k_t = (1.0 - lbc) * sig_neg
    for b in range(sb):
        st = s_ref[b]
        for step in range(DEC_SEQ):
            c = b * DEC_SEQ + step
            v_row = proj_scr[c:c + 1, C_BI:C_BI + W_GRP]
            v_exp = jnp.concatenate(
                [jnp.broadcast_to(v_row[:, h * HEAD_DIM:(h + 1) * HEAD_DIM], (HEAD_DIM, HEAD_DIM))
                 for h in range(N_HG)], axis=0)
            st = f_t[:, c:c + 1] * st + k_t[:, c:c + 1] * v_exp
            qs = q_t[:, c:c + 1] * st
            hg_scr[c:c + 1, :] = jnp.concatenate(
                [jnp.sum(qs[h * HEAD_DIM:(h + 1) * HEAD_DIM, :], axis=0, keepdims=True) for h in range(N_HG)],
                axis=1)
        snew_ref[b] = st
    r256 = lax.broadcasted_iota(jnp.int32, (W_GRP, W_GRP), 0)
    c256 = lax.broadcasted_iota(jnp.int32, (W_GRP, W_GRP), 1)
    head_ones = jnp.where((r256 >> 6) == (c256 >> 6), 1.0, 0.0).astype(BF16)
    o = hg_scr[...]
    o2 = o * o
    o2_hi = o2.astype(BF16)
    o2_lo = (o2 - o2_hi.astype(F32)).astype(BF16)
    ms = (_dot(o2_hi, head_ones) + _dot(o2_lo, head_ones)) * (1.0 / HEAD_DIM)
    o = o * lax.rsqrt(ms + RMS_EPS) * ng_ref[...]
    mix_scr[:, W_GRP:2 * W_GRP] = o * _silu(proj_scr[:, C_BG:C_BG + W_GRP])

    for j in range(8):
        cd_scr[j] = proj_scr[:, C_CB + j * LANES:C_CB + (j + 1) * LANES]
    cw = cw_ref[...]
    ps = ps_ref[...]
    cp = lax.broadcasted_iota(jnp.int32, (sb, W_GRP), 1) >> 6
    width = jnp.left_shift(2, cp)
    step_rows = lambda step: pl.ds(step, sb, stride=DEC_SEQ)

    def step_load(tile, step):
        return jnp.concatenate([cd_scr[tile, step_rows(step), :], cd_scr[tile + 1, step_rows(step), :]], axis=1)

    def step_store(tile, step, val):
        od_scr[tile, step_rows(step), :] = val[:, 0:LANES]
        od_scr[tile + 1, step_rows(step), :] = val[:, LANES:2 * LANES]

    u = [cs_ref[0], cs_ref[1]]
    ext = [pb_ref[i] for i in range(POOL_BUF)]
    for step in range(DEC_SEQ):
        u.append(step_load(2, step) * step_load(4, step))
        ext.append(step_load(6, step))
    cnew_ref[0] = u[DEC_SEQ]
    cnew_ref[1] = u[DEC_SEQ + 1]
    for i in range(POOL_BUF):
        pnew_ref[i] = ext[DEC_SEQ + i]
    for step in range(DEC_SEQ):
        yc = u[step] * cw[0:1, :] + u[step + 1] * cw[1:2, :] + u[step + 2] * cw[2:3, :]
        step_store(0, step, step_load(0, step) * yc)
        top = POOL_BUF + step
        acc = ext[top]
        sums = {}
        for jj in range(1, 16):
            acc = acc + ext[top - jj]
            if jj + 1 in POOL_WINDOWS:
                sums[jj + 1] = acc
        win = jnp.where(cp == 0, sums[2], jnp.where(cp == 1, sums[4], jnp.where(cp == 2, sums[8], sums[16])))
        cnt = jnp.minimum(PAST_LEN + step + 1, width).astype(F32)
        pooled = win / cnt - ext[top]
        step_store(2, step, _dot(pooled.astype(BF16), pw_ref[...]) * ps)
    for j in range(4):
        mix_scr[:, 2 * W_GRP + j * LANES:2 * W_GRP + (j + 1) * LANES] = od_scr[j]

    y = _dot(mix_scr[...].astype(BF16), w_o_ref[...])
    y_ref[...] = _layer_norm(ALPHA * x + y, g_ref[...], b_ref[...])


def _sample_mixer_call(l, x2d, ck, cv, s, cs_tm, pb_tm, w):
    m = x2d.shape[0]
    nb = m // DEC_SEQ
    sb = SEQ_BLOCK
    rows = sb * DEC_SEQ
    single = pl.Buffered(1)
    out_shape = (
        jax.ShapeDtypeStruct((m, D_MODEL), F32),
        jax.ShapeDtypeStruct((nb, WINDOW, LANES), F32),
        jax.ShapeDtypeStruct((nb, WINDOW, LANES), F32),
        jax.ShapeDtypeStruct((nb, W_GRP, HEAD_DIM), F32),
        jax.ShapeDtypeStruct((CONV_W - 1, nb, W_GRP), F32),
        jax.ShapeDtypeStruct((POOL_BUF, nb, W_GRP), F32),
    )
    return pl.pallas_call(
        _sample_mixer_kernel,
        name="mixer_sample",
        grid=(nb // sb,),
        in_specs=[pl.BlockSpec((rows, D_MODEL), lambda i: (i, 0)),
                  pl.BlockSpec((None, sb, WINDOW, LANES), lambda i: (l, i, 0, 0)),
                  pl.BlockSpec((None, sb, WINDOW, LANES), lambda i: (l, i, 0, 0)),
                  pl.BlockSpec((None, sb, W_GRP, HEAD_DIM), lambda i: (l, i, 0, 0)),
                  pl.BlockSpec((None, CONV_W - 1, sb, W_GRP), lambda i: (l, 0, i, 0)),
                  pl.BlockSpec((None, POOL_BUF, sb, W_GRP), lambda i: (l, 0, i, 0)),
                  pl.BlockSpec((D_MODEL, D_IN), lambda i: (0, 0), pipeline_mode=single),
                  _full_spec((2 * W_GRP, D_MODEL)),
                  pl.BlockSpec(memory_space=pltpu.SMEM),
                  _full_spec((1, W_GRP)), _full_spec((W_GRP, 1)), _full_spec((1, W_GRP)),
                  _full_spec((CONV_W, W_GRP)), _full_spec((W_GRP, W_GRP)), _full_spec((1, W_GRP)),
                  pl.BlockSpec((D_MODEL, D_MODEL), lambda i: (0, 0), pipeline_mode=single),
                  _full_spec((1, D_MODEL)), _full_spec((1, D_MODEL))],
        out_specs=(pl.BlockSpec((rows, D_MODEL), lambda i: (i, 0)),
                   pl.BlockSpec((sb, WINDOW, LANES), lambda i: (i, 0, 0)),
                   pl.BlockSpec((sb, WINDOW, LANES), lambda i: (i, 0, 0)),
                   pl.BlockSpec((sb, W_GRP, HEAD_DIM), lambda i: (i, 0, 0)),
                   pl.BlockSpec((CONV_W - 1, sb, W_GRP), lambda i: (0, i, 0)),
                   pl.BlockSpec((POOL_BUF, sb, W_GRP), lambda i: (0, i, 0))),
        out_shape=out_shape,
        scratch_shapes=[pltpu.VMEM((rows, D_IN), F32),
                        pltpu.VMEM((rows, D_MODEL), F32),
                        pltpu.VMEM((rows, W_GRP), F32),
                        pltpu.VMEM((8, rows, LANES), F32),
                        pltpu.VMEM((4, rows, LANES), F32)],
        compiler_params=_compiler_params(("arbitrary",)),
    )(x2d, ck, cv, s, cs_tm, pb_tm, w["w_in"], w["w_hgt"], w["sink"], w["lb"], w["lb_col"], w["ng"], w["cw"],
      w["pw"], w["ps"], w["w_o"], w["ln1_g"], w["ln1_b"])


def _xattn_sample_kernel(x_ref, mk_ref, mv_ref, wq_ref, wo_ref, g_ref, b_ref, o_ref, att_scr):
    sb = mk_ref.shape[0]
    x = x_ref[...]
    q = _dot(x.astype(BF16), wq_ref[...]) * (HEAD_DIM ** -0.5)
    n_rows = N_XH * PAIR_ROWS
    lane = lax.broadcasted_iota(jnp.int32, (PAIR_ROWS, D_X), 1) >> 6
    r_s = lax.broadcasted_iota(jnp.int32, (n_rows, 2 * N_MEM), 0)
    c_s = lax.broadcasted_iota(jnp.int32, (n_rows, 2 * N_MEM), 1)
    own = ((r_s >> 2) & 1) == (c_s >> 8)
    for p in range(sb // 2):
        rows = slice(p * PAIR_ROWS, (p + 1) * PAIR_ROWS)
        q8 = q[rows, :]
        zero = jnp.zeros_like(q8)
        q4 = jnp.concatenate([jnp.where(lane == h, q8, zero) for h in range(N_XH)], axis=0).astype(BF16)
        k2 = jnp.concatenate([mk_ref[2 * p], mk_ref[2 * p + 1]], axis=0).astype(BF16)
        v2 = jnp.concatenate([mv_ref[2 * p], mv_ref[2 * p + 1]], axis=0).astype(BF16)
        s = jnp.where(own, _nt_dot(q4, k2), MASK_VALUE)
        m = jnp.max(s, axis=-1, keepdims=True)
        pr = jnp.exp(s - m)
        den = jnp.sum(pr, axis=-1, keepdims=True)
        o_all = _dot(pr.astype(BF16), v2) / den
        o8 = jnp.zeros_like(q8)
        for h in range(N_XH):
            o8 = jnp.where(lane == h, o_all[h * PAIR_ROWS:(h + 1) * PAIR_ROWS, :], o8)
        att_scr[rows, :] = o8.astype(BF16)
    y = _dot(att_scr[...], wo_ref[...])
    o_ref[...] = _layer_norm(ALPHA * x + y, g_ref[...], b_ref[...])


def _xattn_sample_call(l, x2d, mk, mv, w):
    m = x2d.shape[0]
    nb = m // DEC_SEQ
    sb = SEQ_BLOCK
    rows = sb * DEC_SEQ
    return pl.pallas_call(
        _xattn_sample_kernel,
        name="xattn_sample",
        grid=(nb // sb,),
        in_specs=[pl.BlockSpec((rows, D_MODEL), lambda i: (i, 0)),
                  pl.BlockSpec((None, sb, N_MEM, D_X), lambda i: (l, i, 0, 0)),
                  pl.BlockSpec((None, sb, N_MEM, D_X), lambda i: (l, i, 0, 0)),
                  _full_spec((D_MODEL, D_X)), _full_spec((D_X, D_MODEL)),
                  _full_spec((1, D_MODEL)), _full_spec((1, D_MODEL))],
        out_specs=pl.BlockSpec((rows, D_MODEL), lambda i: (i, 0)),
        out_shape=jax.ShapeDtypeStruct((m, D_MODEL), F32),
        scratch_shapes=[pltpu.VMEM((rows, D_X), BF16)],
        compiler_params=_compiler_params(("arbitrary",)),
    )(x2d, mk, mv, w["w_xq"], w["w_xo"], w["ln2_g"], w["ln2_b"])


SWA_ROWS = WINDOW * N_KV
NEW_ROWS = DEC_SEQ * N_KV
MEM_ROWS = N_MEM * N_XH


def _hi_dot(a, b):
    return jnp.dot(a, b, preferred_element_type=F32, precision=lax.Precision.HIGHEST)


def _head_rows(tile):
    return jnp.concatenate([tile[:, h * HEAD_DIM:(h + 1) * HEAD_DIM] for h in range(4)], axis=0)


def _head_lanes(rows32):
    return jnp.concatenate([rows32[h * PAIR_ROWS:(h + 1) * PAIR_ROWS, :] for h in range(4)], axis=1)


def _sample_mixer2_kernel(x_ref, ck_ref, cv_ref, s_ref, cs_ref, pb_ref,
                          w_in_ref, sink_ref, lb_ref, ng_ref, cw_ref, pw_ref, ps_ref, w_o_ref, g_ref, b_ref,
                          *rest):
    (y_ref, knew_ref, vnew_ref, snew_ref, cnew_ref, pnew_ref,
     proj_scr, mix_scr, cd_scr, od_scr, new_scr, sc_scr, sn_scr, pc_scr, pn_scr) = rest[-15:]
    sb = ck_ref.shape[0]
    m_rows = sb * DEC_SEQ
    n_pair = sb // 2
    x = x_ref[...]
    xb = x.astype(BF16)
    proj_scr[...] = _dot(xb, w_in_ref[...])

    r2 = lax.broadcasted_iota(jnp.int32, (2 * m_rows, m_rows), 0)
    c2 = lax.broadcasted_iota(jnp.int32, (2 * m_rows, m_rows), 1)
    pick0 = jnp.where(r2 == 2 * c2, 1.0, 0.0)
    pick1 = jnp.where(r2 == 2 * c2 + 1, 1.0, 0.0)
    for idx, col in ((0, C_AK), (1, C_AV)):
        new_scr[idx] = (_hi_dot(pick0, proj_scr[:, col:col + HEAD_DIM])
                        + _hi_dot(pick1, proj_scr[:, col + HEAD_DIM:col + 2 * HEAD_DIM]))
    n_rows = 4 * PAIR_ROWS
    r_c = lax.broadcasted_iota(jnp.int32, (n_rows, 2 * SWA_ROWS), 0)
    c_c = lax.broadcasted_iota(jnp.int32, (n_rows, 2 * SWA_ROWS), 1)
    rel_c = (r_c & 3) + WINDOW - ((c_c >> 1) & (WINDOW - 1))
    ok_c = (((r_c >> 2) & 1) == (c_c >> 8)) & ((r_c >> 4) == (c_c & 1)) & (rel_c <= WINDOW)
    slope_c = jnp.exp2(-2.0 * ((r_c >> 3).astype(F32) + 1.0))
    bias_c = jnp.where(ok_c, -slope_c * rel_c.astype(F32), MASK_VALUE)
    r_n = lax.broadcasted_iota(jnp.int32, (n_rows, 2 * NEW_ROWS), 0)
    c_n = lax.broadcasted_iota(jnp.int32, (n_rows, 2 * NEW_ROWS), 1)
    rel_n = (r_n & 3) - ((c_n >> 1) & 3)
    ok_n = (((r_n >> 2) & 1) == (c_n >> 3)) & ((r_n >> 4) == (c_n & 1)) & (rel_n >= 0)
    slope_n = jnp.exp2(-2.0 * ((r_n >> 3).astype(F32) + 1.0))
    bias_n = jnp.where(ok_n, -slope_n * rel_n.astype(F32), MASK_VALUE)
    for p in range(n_pair):
        rows = slice(p * PAIR_ROWS, (p + 1) * PAIR_ROWS)
        q32 = _head_rows(proj_scr[rows, 0:W_GRP] * (HEAD_DIM ** -0.5)).astype(BF16)
        k2 = jnp.concatenate([ck_ref[2 * p], ck_ref[2 * p + 1]], axis=0).astype(BF16)
        kn = new_scr[0, 2 * NEW_ROWS * p:2 * NEW_ROWS * (p + 1), :].astype(BF16)
        sc_scr[n_rows * p:n_rows * (p + 1), :] = _nt_dot(q32, k2) + bias_c
        sn_scr[n_rows * p:n_rows * (p + 1), :] = _nt_dot(q32, kn) + bias_n
    head_col = (lax.broadcasted_iota(jnp.int32, (n_rows * n_pair, 1), 0) >> 3) & 3
    sink_col = jnp.where(head_col == 0, sink_ref[0],
                         jnp.where(head_col == 1, sink_ref[1], jnp.where(head_col == 2, sink_ref[2], sink_ref[3])))
    s_c = sc_scr[...]
    s_n = sn_scr[...]
    m = jnp.maximum(jnp.maximum(jnp.max(s_c, axis=-1, keepdims=True), jnp.max(s_n, axis=-1, keepdims=True)),
                    sink_col)
    p_c = jnp.exp(s_c - m)
    p_n = jnp.exp(s_n - m)
    inv = 1.0 / (jnp.sum(p_c, axis=-1, keepdims=True) + jnp.sum(p_n, axis=-1, keepdims=True)
                 + jnp.exp(sink_col - m))
    pc_scr[...] = (p_c * inv).astype(BF16)
    pn_scr[...] = (p_n * inv).astype(BF16)
    for p in range(n_pair):
        rows = slice(p * PAIR_ROWS, (p + 1) * PAIR_ROWS)
        v2 = jnp.concatenate([cv_ref[2 * p], cv_ref[2 * p + 1]], axis=0).astype(BF16)
        vn = new_scr[1, 2 * NEW_ROWS * p:2 * NEW_ROWS * (p + 1), :].astype(BF16)
        o32 = (_dot(pc_scr[n_rows * p:n_rows * (p + 1), :], v2)
               + _dot(pn_scr[n_rows * p:n_rows * (p + 1), :], vn))
        mix_scr[rows, 0:W_GRP] = _head_lanes(o32)
    for b in range(sb):
        for old_ref, out_ref, idx in ((ck_ref, knew_ref, 0), (cv_ref, vnew_ref, 1)):
            out_ref[b, 0:SWA_ROWS - NEW_ROWS, :] = old_ref[b, NEW_ROWS:SWA_ROWS, :]
            out_ref[b, SWA_ROWS - NEW_ROWS:SWA_ROWS, :] = new_scr[idx, NEW_ROWS * b:NEW_ROWS * (b + 1), :]

    lb = lb_ref[...]
    lbf = jnp.maximum(lb, LB_FLOOR)
    r256 = lax.broadcasted_iota(jnp.int32, (W_GRP, W_GRP), 0)
    c256 = lax.broadcasted_iota(jnp.int32, (W_GRP, W_GRP), 1)
    same_head = (r256 >> 6) == (c256 >> 6)
    head_ones = jnp.where(same_head, 1.0, 0.0).astype(BF16)
    step = lax.broadcasted_iota(jnp.int32, (m_rows, W_GRP), 0) & (DEC_SEQ - 1)
    q = _silu(proj_scr[:, C_BQ:C_BQ + W_GRP])
    sig_pos, sig_neg = _sigmoid_pair(proj_scr[:, C_BF:C_BF + W_GRP])
    g = jnp.log(sig_pos + lbf * sig_neg)
    k = (1.0 - lb) * sig_neg
    v = proj_scr[:, C_BI:C_BI + W_GRP]
    cum = g + jnp.where(step >= 1, pltpu.roll(g, 1, axis=0), 0.0)
    cum = cum + jnp.where(step >= 2, pltpu.roll(cum, 2, axis=0), 0.0)
    o = jnp.zeros((m_rows, W_GRP), F32)
    for lag in range(DEC_SEQ):
        if lag == 0:
            w_qk = q * k
            v_l = v
        else:
            decay = jnp.exp(jnp.minimum(cum - pltpu.roll(cum, lag, axis=0), 0.0))
            w_qk = jnp.where(step >= lag, q * pltpu.roll(k, lag, axis=0) * decay, 0.0)
            v_l = pltpu.roll(v, lag, axis=0)
        o = o + _dot(w_qk.astype(BF16), head_ones) * v_l
    tot = jnp.where(step == 3, cum,
                    jnp.where(step == 2, pltpu.roll(cum, m_rows - 1, axis=0),
                              jnp.where(step == 1, pltpu.roll(cum, m_rows - 2, axis=0),
                                        pltpu.roll(cum, m_rows - 3, axis=0))))
    qs = q * jnp.exp(cum)
    ks = k * jnp.exp(tot - cum)
    dec = jnp.exp(tot)
    dec_hi = dec.astype(BF16).astype(F32)
    dec_mid = (dec - dec_hi).astype(BF16).astype(F32)
    dec_lo = dec - dec_hi - dec_mid
    dec3 = jnp.where(step == 0, dec_hi, jnp.where(step == 1, dec_mid, jnp.where(step == 2, dec_lo, 0.0)))
    sub8 = lax.broadcasted_iota(jnp.int32, (PAIR_ROWS, W_GRP), 0)
    ones3 = jnp.where((lax.broadcasted_iota(jnp.int32, (PAIR_ROWS, HEAD_DIM), 0) & 3) < 3, 1.0, 0.0)
    for p in range(n_pair):
        rows = slice(p * PAIR_ROWS, (p + 1) * PAIR_ROWS)
        qs8 = qs[rows, :].astype(BF16)
        o_inter = jnp.zeros((PAIR_ROWS, W_GRP), F32)
        for half in range(2):
            b = 2 * p + half
            own = (sub8 >> 2) == half
            own64 = own[:, 0:HEAD_DIM]
            st = s_ref[b]
            st_bd = jnp.where(same_head, jnp.concatenate([st] * N_HG, axis=1), 0.0).astype(BF16)
            o_inter = jnp.where(own, _dot(qs8, st_bd), o_inter)
            ks8 = jnp.where(own, ks[rows, :], 0.0).astype(BF16)
            upd = _tn_dot(ks8, v[rows, :].astype(BF16))
            upd = jnp.concatenate([upd[h * HEAD_DIM:(h + 1) * HEAD_DIM, h * HEAD_DIM:(h + 1) * HEAD_DIM]
                                   for h in range(N_HG)], axis=0)
            d8 = jnp.where(own, dec3[rows, :], 0.0).astype(BF16)
            dcol = _tn_dot(d8, jnp.where(own64, ones3, 0.0).astype(BF16))
            snew_ref[b] = dcol * st + upd
        hg = o[rows, :] + o_inter
        hg2 = hg * hg
        hg2_hi = hg2.astype(BF16)
        hg2_lo = (hg2 - hg2_hi.astype(F32)).astype(BF16)
        ms = (_dot(hg2_hi, head_ones) + _dot(hg2_lo, head_ones)) * (1.0 / HEAD_DIM)
        hg = hg * lax.rsqrt(ms + RMS_EPS) * ng_ref[...]
        mix_scr[rows, W_GRP:2 * W_GRP] = hg * _silu(proj_scr[rows, C_BG:C_BG + W_GRP])

    for j in range(8):
        cd_scr[j] = proj_scr[:, C_CB + j * LANES:C_CB + (j + 1) * LANES]
    cw = cw_ref[...]
    ps = ps_ref[...]
    cp = lax.broadcasted_iota(jnp.int32, (sb, W_GRP), 1) >> 6
    width = jnp.left_shift(2, cp)
    step_rows = lambda s_: pl.ds(s_, sb, stride=DEC_SEQ)

    def step_load(tile, s_):
        return jnp.concatenate([cd_scr[tile, step_rows(s_), :], cd_scr[tile + 1, step_rows(s_), :]], axis=1)

    def step_store(tile, s_, val):
        od_scr[tile, step_rows(s_), :] = val[:, 0:LANES]
        od_scr[tile + 1, step_rows(s_), :] = val[:, LANES:2 * LANES]

    u = [cs_ref[0], cs_ref[1]]
    ext = [pb_ref[i] for i in range(POOL_BUF)]
    for s_ in range(DEC_SEQ):
        u.append(step_load(2, s_) * step_load(4, s_))
        ext.append(step_load(6, s_))
    cnew_ref[0] = u[DEC_SEQ]
    cnew_ref[1] = u[DEC_SEQ + 1]
    for i in range(POOL_BUF):
        pnew_ref[i] = ext[DEC_SEQ + i]
    for s_ in range(DEC_SEQ):
        yc = u[s_] * cw[0:1, :] + u[s_ + 1] * cw[1:2, :] + u[s_ + 2] * cw[2:3, :]
        step_store(0, s_, step_load(0, s_) * yc)
        top = POOL_BUF + s_
        acc = ext[top]
        sums = {}
        for jj in range(1, 16):
            acc = acc + ext[top - jj]
            if jj + 1 in POOL_WINDOWS:
                sums[jj + 1] = acc
        win = jnp.where(cp == 0, sums[2], jnp.where(cp == 1, sums[4], jnp.where(cp == 2, sums[8], sums[16])))
        cnt = jnp.minimum(PAST_LEN + s_ + 1, width).astype(F32)
        pooled = win / cnt - ext[top]
        step_store(2, s_, _dot(pooled.astype(BF16), pw_ref[...]) * ps)
    for j in range(4):
        mix_scr[:, 2 * W_GRP + j * LANES:2 * W_GRP + (j + 1) * LANES] = od_scr[j]

    y = _dot(mix_scr[...].astype(BF16), w_o_ref[...])
    y_ref[...] = _layer_norm(ALPHA * x + y, g_ref[...], b_ref[...])


def _sample_mixer2_call(l, x2d, ck, cv, s, cs_tm, pb_tm, w, prev):
    m = x2d.shape[0]
    nb = m // DEC_SEQ
    sb = SEQ_BLOCK
    rows = sb * DEC_SEQ
    n_sc = 4 * PAIR_ROWS * (sb // 2)
    single = pl.Buffered(1)
    out_shape = (
        jax.ShapeDtypeStruct((m, D_MODEL), F32),
        jax.ShapeDtypeStruct((DEPTH, nb, SWA_ROWS, HEAD_DIM), F32),
        jax.ShapeDtypeStruct((DEPTH, nb, SWA_ROWS, HEAD_DIM), F32),
        jax.ShapeDtypeStruct((DEPTH, nb, W_GRP, HEAD_DIM), F32),
        jax.ShapeDtypeStruct((CONV_W - 1, nb, W_GRP), F32),
        jax.ShapeDtypeStruct((POOL_BUF, nb, W_GRP), F32),
    )
    layer_blk = lambda i: (l, i, 0, 0)
    in_specs = [pl.BlockSpec((rows, D_MODEL), lambda i: (i, 0)),
                pl.BlockSpec((None, sb, SWA_ROWS, HEAD_DIM), layer_blk),
                pl.BlockSpec((None, sb, SWA_ROWS, HEAD_DIM), layer_blk),
                pl.BlockSpec((None, sb, W_GRP, HEAD_DIM), layer_blk),
                pl.BlockSpec((None, CONV_W - 1, sb, W_GRP), lambda i: (l, 0, i, 0)),
                pl.BlockSpec((None, POOL_BUF, sb, W_GRP), lambda i: (l, 0, i, 0)),
                pl.BlockSpec((D_MODEL, D_IN), lambda i: (0, 0), pipeline_mode=single),
                pl.BlockSpec(memory_space=pltpu.SMEM),
                _full_spec((1, W_GRP)), _full_spec((1, W_GRP)),
                _full_spec((CONV_W, W_GRP)), _full_spec((W_GRP, W_GRP)), _full_spec((1, W_GRP)),
                pl.BlockSpec((D_MODEL, D_MODEL), lambda i: (0, 0), pipeline_mode=single),
                _full_spec((1, D_MODEL)), _full_spec((1, D_MODEL))]
    args = [x2d, ck, cv, s, cs_tm, pb_tm, w["w_in"], w["sink"], w["lb"], w["ng"], w["cw"], w["pw"], w["ps"],
            w["w_o"], w["ln1_g"], w["ln1_b"]]
    aliases = {}
    if prev is not None:
        for j, buf in enumerate(prev):
            aliases[len(args)] = 1 + j
            args.append(buf)
            in_specs.append(pl.BlockSpec(memory_space=pl.ANY))
    return pl.pallas_call(
        _sample_mixer2_kernel,
        name="mixer_sample",
        grid=(nb // sb,),
        in_specs=in_specs,
        out_specs=(pl.BlockSpec((rows, D_MODEL), lambda i: (i, 0)),
                   pl.BlockSpec((None, sb, SWA_ROWS, HEAD_DIM), layer_blk),
                   pl.BlockSpec((None, sb, SWA_ROWS, HEAD_DIM), layer_blk),
                   pl.BlockSpec((None, sb, W_GRP, HEAD_DIM), layer_blk),
                   pl.BlockSpec((CONV_W - 1, sb, W_GRP), lambda i: (0, i, 0)),
                   pl.BlockSpec((POOL_BUF, sb, W_GRP), lambda i: (0, i, 0))),
        out_shape=out_shape,
        input_output_aliases=aliases,
        scratch_shapes=[pltpu.VMEM((rows, D_IN), F32),
                        pltpu.VMEM((rows, D_MODEL), F32),
                        pltpu.VMEM((8, rows, LANES), F32),
                        pltpu.VMEM((4, rows, LANES), F32),
                        pltpu.VMEM((2, 2 * rows, HEAD_DIM), F32),
                        pltpu.VMEM((n_sc, 2 * SWA_ROWS), F32),
                        pltpu.VMEM((n_sc, 2 * NEW_ROWS), F32),
                        pltpu.VMEM((n_sc, 2 * SWA_ROWS), BF16),
                        pltpu.VMEM((n_sc, 2 * NEW_ROWS), BF16)],
        compiler_params=_compiler_params(("arbitrary",)),
    )(*args)


def _xattn_sample2_kernel(x_ref, mk_ref, mv_ref, wq_ref, wo_ref, g_ref, b_ref, o_ref, s_scr, p_scr, att_scr):
    sb = mk_ref.shape[0]
    n_rows = N_XH * PAIR_ROWS
    x = x_ref[...]
    q = _dot(x.astype(BF16), wq_ref[...]) * (HEAD_DIM ** -0.5)
    r_s = lax.broadcasted_iota(jnp.int32, (n_rows, MEM_ROWS), 0)
    c_s = lax.broadcasted_iota(jnp.int32, (n_rows, MEM_ROWS), 1)
    head_ok = (r_s >> 3) == (c_s & 3)
    member = (r_s >> 2) & 1
    for b in range(sb):
        rows = slice((b // 2) * PAIR_ROWS, (b // 2 + 1) * PAIR_ROWS)
        q32 = _head_rows(q[rows, :]).astype(BF16)
        s = _nt_dot(q32, mk_ref[b].astype(BF16))
        s_scr[b] = jnp.where(head_ok & (member == (b & 1)), s, MASK_VALUE)
    s_all = s_scr[...]
    m = jnp.max(s_all, axis=-1, keepdims=True)
    pr = jnp.exp(s_all - m)
    p_scr[...] = (pr * (1.0 / jnp.sum(pr, axis=-1, keepdims=True))).astype(BF16)
    own_rows = (lax.broadcasted_iota(jnp.int32, (n_rows, HEAD_DIM), 0) >> 2) & 1
    for p in range(sb // 2):
        rows = slice(p * PAIR_ROWS, (p + 1) * PAIR_ROWS)
        o_a = _dot(p_scr[2 * p], mv_ref[2 * p].astype(BF16))
        o_b = _dot(p_scr[2 * p + 1], mv_ref[2 * p + 1].astype(BF16))
        att_scr[rows, :] = _head_lanes(jnp.where(own_rows == 0, o_a, o_b)).astype(BF16)
    y = _dot(att_scr[...], wo_ref[...])
    o_ref[...] = _layer_norm(ALPHA * x + y, g_ref[...], b_ref[...])


def _xattn_sample2_call(l, x2d, mk, mv, w):
    m = x2d.shape[0]
    nb = m // DEC_SEQ
    sb = SEQ_BLOCK
    rows = sb * DEC_SEQ
    return pl.pallas_call(
        _xattn_sample2_kernel,
        name="xattn_sample",
        grid=(nb // sb,),
        in_specs=[pl.BlockSpec((rows, D_MODEL), lambda i: (i, 0)),
                  pl.BlockSpec((None, sb, MEM_ROWS, HEAD_DIM), lambda i: (l, i, 0, 0)),
                  pl.BlockSpec((None, sb, MEM_ROWS, HEAD_DIM), lambda i: (l, i, 0, 0)),
                  _full_spec((D_MODEL, D_X)), _full_spec((D_X, D_MODEL)),
                  _full_spec((1, D_MODEL)), _full_spec((1, D_MODEL))],
        out_specs=pl.BlockSpec((rows, D_MODEL), lambda i: (i, 0)),
        out_shape=jax.ShapeDtypeStruct((m, D_MODEL), F32),
        scratch_shapes=[pltpu.VMEM((sb, N_XH * PAIR_ROWS, MEM_ROWS), F32),
                        pltpu.VMEM((sb, N_XH * PAIR_ROWS, MEM_ROWS), BF16),
                        pltpu.VMEM((rows, D_X), BF16)],
        compiler_params=_compiler_params(("arbitrary",)),
    )(x2d, mk, mv, w["w_xq"], w["w_xo"], w["ln2_g"], w["ln2_b"])


def _sample_layer(l, x2d, ck, cv, s, cs_tm, pb_tm, mk, mv, w, prev):
    x2d, kn, vn, sn, cn, pn = _sample_mixer2_call(l, x2d, ck, cv, s, cs_tm, pb_tm, w, prev)
    x2d = _xattn_sample2_call(l, x2d, mk, mv, w)
    x2d = _ffn_call(x2d, w["w_gate"], w["w_up"], w["w_down"], w["ln3_g"], w["ln3_b"])
    return x2d, kn, vn, sn, cn, pn


def _row(v):
    return v.reshape(1, -1).astype(F32)


def _pool_block_diag(pool_w):
    z = jnp.zeros((W_GRP, W_GRP), pool_w.dtype)
    for gi in range(4):
        z = lax.dynamic_update_slice(z, pool_w[gi], (gi * HEAD_DIM, gi * HEAD_DIM))
    return z


def _hgrn_lower_bounds(lb_param):
    p = jax.nn.softmax(lb_param.astype(F32), axis=0)
    return jnp.cumsum(p, axis=0) - p[0:1]


def _prompt_layer(x, mem2d, w):
    bsz = x.shape[0]
    mk = _matmul_call(mem2d, w["w_xk"]).reshape(bsz, N_MEM, D_X)
    mv = _matmul_call(mem2d, w["w_xv"]).reshape(bsz, N_MEM, D_X)
    x, kn, vn, sn, cn, pn = _prompt_mixer_call(
        x, w["w_in"], w["sink"], w["lb"], w["ng"], w["cw"], w["pw"], w["ps"], w["w_o"], w["ln1_g"], w["ln1_b"])
    x = _xattn_prompt_call(x, mk, mv, w["w_xq"], w["w_xo"], w["ln2_g"], w["ln2_b"])
    t = x.shape[1]
    x = _ffn_call(x.reshape(bsz * t, D_MODEL), w["w_gate"], w["w_up"], w["w_down"],
                  w["ln3_g"], w["ln3_b"]).reshape(bsz, t, D_MODEL)
    return x, kn, vn, sn, cn, pn, mk, mv


def _layer_weights(l, lb_all, w_in, attn_sink, hgrn_norm_g, conv_w, pool_w, pool_scale, w_o, ln1_g, ln1_b,
                   w_xq, w_xk, w_xv, w_xo, ln2_g, ln2_b, w_gate, w_up, w_down, ln3_g, ln3_b):
    bf = lambda a: a[l].astype(BF16)
    return dict(
        w_in=bf(w_in), w_hgt=w_in[l][:, C_BQ:C_BI].T.astype(BF16),
        sink=attn_sink[l].astype(F32), lb=_row(lb_all[l]), lb_col=lb_all[l].reshape(-1, 1).astype(F32),
        ng=_row(hgrn_norm_g[l]),
        cw=conv_w[l].astype(F32), pw=_pool_block_diag(pool_w[l]).astype(BF16), ps=_row(pool_scale[l]),
        w_o=bf(w_o), ln1_g=_row(ln1_g[l]), ln1_b=_row(ln1_b[l]),
        w_xq=bf(w_xq), w_xk=bf(w_xk), w_xv=bf(w_xv), w_xo=bf(w_xo), ln2_g=_row(ln2_g[l]), ln2_b=_row(ln2_b[l]),
        w_gate=bf(w_gate), w_up=bf(w_up), w_down=bf(w_down), ln3_g=_row(ln3_g[l]), ln3_b=_row(ln3_b[l]))


def kernel(x_prompt, x_sample, cache_swa_k, cache_swa_v, state_hgrn, state_conv, state_pool, cache_mem_k,
           cache_mem_v, mem_prompt, emb_ln_g, emb_ln_b, w_in, attn_sink, hgrn_lb, hgrn_norm_g, conv_w, pool_w,
           pool_scale, w_o, ln1_g, ln1_b, w_xq, w_xk, w_xv, w_xo, ln2_g, ln2_b, w_gate, w_up, w_down, ln3_g,
           ln3_b):
    bp, t, _ = x_prompt.shape
    lb_all = _hgrn_lower_bounds(hgrn_lb)
    hp = _ln_call(x_prompt.reshape(bp * t, D_MODEL), _row(emb_ln_g), _row(emb_ln_b)).reshape(bp, t, D_MODEL)
    mem2d = mem_prompt.reshape(bp * N_MEM, D_MODEL)
    bs, ts, _ = x_sample.shape
    hs = _ln_call(x_sample.reshape(bs * ts, D_MODEL), _row(emb_ln_g), _row(emb_ln_b))
    ck = cache_swa_k.reshape(DEPTH, bs, SWA_ROWS, HEAD_DIM)
    cv = cache_swa_v.reshape(DEPTH, bs, SWA_ROWS, HEAD_DIM)
    st = state_hgrn.reshape(DEPTH, bs, W_GRP, HEAD_DIM)
    cs_tm = state_conv.transpose(0, 2, 1, 3)
    pb_tm = state_pool.transpose(0, 2, 1, 3)
    mk_s = cache_mem_k.reshape(DEPTH, bs, MEM_ROWS, HEAD_DIM)
    mv_s = cache_mem_v.reshape(DEPTH, bs, MEM_ROWS, HEAD_DIM)
    outs = [[] for _ in range(7)]
    souts = [[] for _ in range(2)]
    prev = None
    for l in range(DEPTH):
        w = _layer_weights(l, lb_all, w_in, attn_sink, hgrn_norm_g, conv_w, pool_w, pool_scale, w_o, ln1_g,
                           ln1_b, w_xq, w_xk, w_xv, w_xo, ln2_g, ln2_b, w_gate, w_up, w_down, ln3_g, ln3_b)
        res = _prompt_layer(hp, mem2d, w)
        hp = res[0]
        for acc, r in zip(outs, res[1:]):
            acc.append(r)
        sres = _sample_layer(l, hs, ck, cv, st, cs_tm, pb_tm, mk_s, mv_s, w, prev)
        hs = sres[0]
        prev = sres[1:4]
        for acc, r in zip(souts, sres[4:]):
            acc.append(r)
    pk, pv, ps, pc, pp, pmk, pmv = [jnp.stack(o) for o in outs]
    sk, sv, ss = prev
    sc, sp = [jnp.stack(o) for o in souts]
    return (hp, hs.reshape(bs, ts, D_MODEL),
            pk.reshape(DEPTH, bp, WINDOW, N_KV, HEAD_DIM), pv.reshape(DEPTH, bp, WINDOW, N_KV, HEAD_DIM),
            ps.reshape(DEPTH, bp, N_HG, HEAD_DIM, HEAD_DIM), pc, pp,
            pmk.reshape(DEPTH, bp, N_MEM, N_XH, HEAD_DIM), pmv.reshape(DEPTH, bp, N_MEM, N_XH, HEAD_DIM),
            sk.reshape(DEPTH, bs, WINDOW, N_KV, HEAD_DIM), sv.reshape(DEPTH, bs, WINDOW, N_KV, HEAD_DIM),
            ss.reshape(DEPTH, bs, N_HG, HEAD_DIM, HEAD_DIM),
            sc.transpose(0, 2, 1, 3), sp.transpose(0, 2, 1, 3))
```

```python
import functools

import jax
import jax.numpy as jnp
from jax import lax
from jax.experimental import pallas as pl
from jax.experimental.pallas import tpu as pltpu

F32 = jnp.float32
BF16 = jnp.bfloat16

D_MODEL = 1024
DEPTH = 4
HEAD_DIM = 64
W_GRP = 256
N_KV = 2
WINDOW = 128
N_HG = 4
CONV_W = 3
POOL_WINDOWS = (2, 4, 8, 16)
POOL_BUF = 15
N_MEM = 256
N_XH = 4
D_X = 256
D_FF = 2816
D_IN = 2560
ALPHA = (2 * DEPTH) ** 0.25
LN_EPS = 1e-5
RMS_EPS = 1e-6
MASK_VALUE = -1e30
LB_FLOOR = 1e-30
PAST_LEN = 8192

C_AQ, C_AK, C_AV = 0, 256, 384
C_BQ, C_BF, C_BI, C_BG = 512, 768, 1024, 1280
C_CB, C_CC, C_CH = 1536, 1792, 2048
C_DV = 2304

LANES = 128
SUBLANES = 8
VMEM_LIMIT_BYTES = 56 * 1024 * 1024

TOKEN_BLOCK = 512
HG_CHUNK = 64
HG_MID = HG_CHUNK // 2 - 1


def _nt_dot(a, b):
    return lax.dot_general(a, b, (((1,), (1,)), ((), ())), preferred_element_type=F32)


def _tn_dot(a, b):
    return lax.dot_general(a, b, (((0,), (0,)), ((), ())), preferred_element_type=F32)


def _dot(a, b):
    return jnp.dot(a, b, preferred_element_type=F32)


def _layer_norm(x, g, b):
    mu = jnp.mean(x, axis=-1, keepdims=True)
    xc = x - mu
    var = jnp.mean(xc * xc, axis=-1, keepdims=True)
    return xc * lax.rsqrt(var + LN_EPS) * g + b


def _sigmoid_pair(z):
    e = jnp.exp(-jnp.abs(z))
    inv = 1.0 / (1.0 + e)
    small = e * inv
    pos = z >= 0
    return jnp.where(pos, inv, small), jnp.where(pos, small, inv)


def _silu(z):
    s, _ = _sigmoid_pair(z)
    return z * s


def _full_spec(shape):
    nd = len(shape)
    return pl.BlockSpec(shape, lambda *_: (0,) * nd)


def _compiler_params(sem, flags=None):
    return pltpu.CompilerParams(dimension_semantics=sem, vmem_limit_bytes=VMEM_LIMIT_BYTES, flags=flags)


def _ln_kernel(x_ref, g_ref, b_ref, o_ref):
    o_ref[...] = _layer_norm(x_ref[...], g_ref[...], b_ref[...])


def _ln_call(x2d, g, b):
    m = x2d.shape[0]
    return pl.pallas_call(
        _ln_kernel,
        name="input_ln",
        grid=(m // TOKEN_BLOCK,),
        in_specs=[pl.BlockSpec((TOKEN_BLOCK, D_MODEL), lambda i: (i, 0)),
                  _full_spec((1, D_MODEL)), _full_spec((1, D_MODEL))],
        out_specs=pl.BlockSpec((TOKEN_BLOCK, D_MODEL), lambda i: (i, 0)),
        out_shape=jax.ShapeDtypeStruct((m, D_MODEL), F32),
        compiler_params=_compiler_params(("arbitrary",)),
    )(x2d, g, b)


def _matmul_kernel(x_ref, w_ref, o_ref):
    o_ref[...] = _dot(x_ref[...].astype(BF16), w_ref[...])


def _matmul_call(x2d, w):
    m, k = x2d.shape
    n = w.shape[1]
    return pl.pallas_call(
        _matmul_kernel,
        name="mem_proj",
        grid=(m // TOKEN_BLOCK,),
        in_specs=[pl.BlockSpec((TOKEN_BLOCK, k), lambda i: (i, 0)), _full_spec((k, n))],
        out_specs=pl.BlockSpec((TOKEN_BLOCK, n), lambda i: (i, 0)),
        out_shape=jax.ShapeDtypeStruct((m, n), F32),
        compiler_params=_compiler_params(("arbitrary",)),
    )(x2d, w)


def _ffn_kernel(x_ref, wg_ref, wu_ref, wd_ref, g_ref, b_ref, o_ref):
    x = x_ref[...]
    xb = x.astype(BF16)
    h = _silu(_dot(xb, wg_ref[...])) * _dot(xb, wu_ref[...])
    y = _dot(h.astype(BF16), wd_ref[...])
    o_ref[...] = _layer_norm(ALPHA * x + y, g_ref[...], b_ref[...])


def _ffn_call(x2d, wg, wu, wd, g, b):
    m = x2d.shape[0]
    single = pl.Buffered(1)
    return pl.pallas_call(
        _ffn_kernel,
        name="ffn",
        grid=(m // TOKEN_BLOCK,),
        in_specs=[pl.BlockSpec((TOKEN_BLOCK, D_MODEL), lambda i: (i, 0)),
                  pl.BlockSpec((D_MODEL, D_FF), lambda i: (0, 0), pipeline_mode=single),
                  pl.BlockSpec((D_MODEL, D_FF), lambda i: (0, 0), pipeline_mode=single),
                  pl.BlockSpec((D_FF, D_MODEL), lambda i: (0, 0), pipeline_mode=single),
                  _full_spec((1, D_MODEL)), _full_spec((1, D_MODEL))],
        out_specs=pl.BlockSpec((TOKEN_BLOCK, D_MODEL), lambda i: (i, 0)),
        out_shape=jax.ShapeDtypeStruct((m, D_MODEL), F32),
        compiler_params=_compiler_params(("arbitrary",)),
    )(x2d, wg, wu, wd, g, b)


def _xattn_prompt_kernel(x_ref, mk_ref, mv_ref, wq_ref, wo_ref, g_ref, b_ref, o_ref):
    x = x_ref[...]
    q = _dot(x.astype(BF16), wq_ref[...]) * (HEAD_DIM ** -0.5)
    mk = mk_ref[...].astype(BF16)
    mv = mv_ref[...].astype(BF16)
    lane_head = lax.broadcasted_iota(jnp.int32, q.shape, 1) >> 6
    o = jnp.zeros_like(q)
    for h in range(N_XH):
        qh = jnp.where(lane_head == h, q, 0.0).astype(BF16)
        s = _nt_dot(mk, qh)
        m = jnp.max(s, axis=0, keepdims=True)
        p = jnp.exp(s - m)
        den = jnp.sum(p, axis=0, keepdims=True)
        p = (p * (1.0 / den)).astype(BF16)
        o = jnp.where(lane_head == h, _tn_dot(p, mv), o)
    y = _dot(o.astype(BF16), wo_ref[...])
    o_ref[...] = _layer_norm(ALPHA * x + y, g_ref[...], b_ref[...])


def _xattn_prompt_call(x, mk, mv, wq, wo, g, b):
    bsz, t, _ = x.shape
    tb = min(TOKEN_BLOCK, t)
    return pl.pallas_call(
        _xattn_prompt_kernel,
        name="xattn_prompt",
        grid=(bsz, t // tb),
        in_specs=[pl.BlockSpec((None, tb, D_MODEL), lambda i, j: (i, j, 0)),
                  pl.BlockSpec((None, N_MEM, D_X), lambda i, j: (i, 0, 0)),
                  pl.BlockSpec((None, N_MEM, D_X), lambda i, j: (i, 0, 0)),
                  _full_spec((D_MODEL, D_X)), _full_spec((D_X, D_MODEL)),
                  _full_spec((1, D_MODEL)), _full_spec((1, D_MODEL))],
        out_specs=pl.BlockSpec((None, tb, D_MODEL), lambda i, j: (i, j, 0)),
        out_shape=jax.ShapeDtypeStruct((bsz, t, D_MODEL), F32),
        compiler_params=_compiler_params(("arbitrary", "arbitrary")),
    )(x, mk, mv, wq, wo, g, b)


def _swa_bias_table():
    c = lax.broadcasted_iota(jnp.int32, (2 * WINDOW, 4 * WINDOW), 0)
    r = lax.broadcasted_iota(jnp.int32, (2 * WINDOW, 4 * WINDOW), 1)
    head = r >> 7
    rel = (r & (WINDOW - 1)) + WINDOW - c
    slope = jnp.exp2(-2.0 * (head.astype(F32) + 1.0))
    valid = (rel >= 0) & (rel <= WINDOW)
    return jnp.where(valid, -slope * rel.astype(F32), MASK_VALUE)


def _prompt_mixer_kernel(x_ref, w_in_ref, sink_ref, lb_ref, ng_ref, cw_ref, pw_ref, ps_ref, w_o_ref,
                         g_ref, b_ref,
                         y_ref, knew_ref, vnew_ref, snew_ref, cnew_ref, pnew_ref,
                         proj_scr, kext_scr, vext_scr, st_scr, u_scr, p_scr, bias_scr, mix_scr, hg_scr):
    tb = x_ref.shape[0]
    n_qb = tb // WINDOW
    n_ch = tb // HG_CHUNK
    bi = pl.program_id(0)
    ti = pl.program_id(1)
    last = ti == pl.num_programs(1) - 1

    @pl.when((bi == 0) & (ti == 0))
    def _():
        bias_scr[...] = _swa_bias_table()

    @pl.when(ti == 0)
    def _():
        kext_scr[0:WINDOW, :] = jnp.zeros((WINDOW, LANES), BF16)
        vext_scr[0:WINDOW, :] = jnp.zeros((WINDOW, LANES), BF16)
        st_scr[...] = jnp.zeros(st_scr.shape, F32)
        u_scr[0:SUBLANES, :] = jnp.zeros((SUBLANES, W_GRP), F32)
        p_scr[0:16, :] = jnp.zeros((16, W_GRP), F32)

    x = x_ref[...]
    xb = x.astype(BF16)
    proj_scr[:, 0:C_BQ] = _dot(xb, w_in_ref[:, 0:C_BQ])
    proj_scr[:, C_BQ:C_CB] = _dot(xb, w_in_ref[:, C_BQ:C_CB])

    kext_scr[WINDOW:WINDOW + tb, :] = proj_scr[:, C_AK:C_AK + LANES].astype(BF16)
    vext_scr[WINDOW:WINDOW + tb, :] = proj_scr[:, C_AV:C_AV + LANES].astype(BF16)
    lane = lax.broadcasted_iota(jnp.int32, (WINDOW, LANES), 1)
    lo = lane < HEAD_DIM
    key_row = lax.broadcasted_iota(jnp.int32, (2 * WINDOW, 4 * WINDOW), 0)
    head_lane = lax.broadcasted_iota(jnp.int32, (1, 4 * WINDOW), 1) >> 7
    sink_row = jnp.where(head_lane == 0, sink_ref[0],
                         jnp.where(head_lane == 1, sink_ref[1],
                                   jnp.where(head_lane == 2, sink_ref[2], sink_ref[3])))
    for j in range(n_qb):
        rows = slice(j * WINDOW, (j + 1) * WINDOW)
        q0 = proj_scr[rows, 0:LANES] * (HEAD_DIM ** -0.5)
        q1 = proj_scr[rows, LANES:2 * LANES] * (HEAD_DIM ** -0.5)
        q0r = pltpu.roll(q0, HEAD_DIM, axis=1)
        q1r = pltpu.roll(q1, HEAD_DIM, axis=1)
        zero = jnp.zeros_like(q0)
        q4 = jnp.concatenate([jnp.where(lo, q0, zero), jnp.where(lo, q0r, zero),
                              jnp.where(lo, zero, q1r), jnp.where(lo, zero, q1)], axis=0).astype(BF16)
        kj = kext_scr[j * WINDOW:(j + 2) * WINDOW, :]
        vj = vext_scr[j * WINDOW:(j + 2) * WINDOW, :]
        s = _nt_dot(kj, q4) + bias_scr[...]
        if j == 0:
            s = jnp.where((ti == 0) & (key_row < WINDOW), MASK_VALUE, s)
        m = jnp.maximum(jnp.max(s, axis=0, keepdims=True), sink_row)
        p = jnp.exp(s - m)
        den = jnp.sum(p, axis=0, keepdims=True) + jnp.exp(sink_row - m)
        p = (p * (1.0 / den)).astype(BF16)
        o_all = _tn_dot(p, vj)
        o = [o_all[h * WINDOW:(h + 1) * WINDOW, :] for h in range(4)]
        c0 = jnp.where(lo, o[0], pltpu.roll(o[1], HEAD_DIM, axis=1))
        c1 = jnp.where(lo, pltpu.roll(o[2], HEAD_DIM, axis=1), o[3])
        mix_scr[rows, 0:LANES] = c0.astype(BF16)
        mix_scr[rows, LANES:2 * LANES] = c1.astype(BF16)
    kext_scr[0:WINDOW, :] = kext_scr[tb:tb + WINDOW, :]
    vext_scr[0:WINDOW, :] = vext_scr[tb:tb + WINDOW, :]
    proj_scr[:, C_CB:D_IN] = _dot(xb, w_in_ref[:, C_CB:D_IN])

    @pl.when(last)
    def _():
        knew_ref[...] = proj_scr[tb - WINDOW:tb, C_AK:C_AK + LANES]
        vnew_ref[...] = proj_scr[tb - WINDOW:tb, C_AV:C_AV + LANES]

    lb = lb_ref[...]
    lbf = jnp.maximum(lb, LB_FLOOR)
    one_m_lb = 1.0 - lb
    ng = ng_ref[...]
    r256 = lax.broadcasted_iota(jnp.int32, (W_GRP, W_GRP), 0)
    c256 = lax.broadcasted_iota(jnp.int32, (W_GRP, W_GRP), 1)
    same_head = (r256 >> 6) == (c256 >> 6)
    head_ones = jnp.where(same_head, 1.0, 0.0).astype(BF16)
    rc = lax.broadcasted_iota(jnp.int32, (HG_CHUNK, W_GRP), 0)
    cc = lax.broadcasted_iota(jnp.int32, (HG_CHUNK, W_GRP), 1)
    causal = (cc & (HG_CHUNK - 1)) <= rc

    def chunk_body(ci, carry):
        r0 = pl.multiple_of(ci * HG_CHUNK, HG_CHUNK)
        rows = pl.ds(r0, HG_CHUNK)
        q = _silu(proj_scr[rows, C_BQ:C_BQ + W_GRP])
        sig_pos, sig_neg = _sigmoid_pair(proj_scr[rows, C_BF:C_BF + W_GRP])
        g = jnp.log(sig_pos + lbf * sig_neg)
        k = one_m_lb * sig_neg
        v = proj_scr[rows, C_BI:C_BI + W_GRP]
        cum = g
        for sh in (1, 2, 4, 8, 16, 32):
            cum = cum + jnp.where(rc >= sh, pltpu.roll(cum, sh, axis=0), 0.0)
        ref = cum[HG_MID:HG_MID + 1, :]
        tot = cum[HG_CHUNK - 1:HG_CHUNK, :]
        e_fwd = jnp.exp(cum - ref)
        e_bwd = jnp.exp(ref - cum)
        qp = q * e_fwd
        kp = k * e_bwd
        qs = (qp * jnp.exp(ref)).astype(BF16)
        ks = (kp * jnp.exp(tot - ref)).astype(BF16)
        vb = v.astype(BF16)
        bk = jnp.where(same_head, jnp.concatenate([kp] * N_HG, axis=0), 0.0).astype(BF16)
        bv = jnp.where(same_head, jnp.concatenate([v] * N_HG, axis=0), 0.0).astype(BF16)
        a = jnp.where(causal, _nt_dot(qp.astype(BF16), bk), 0.0)
        st = st_scr[...]
        o = _dot(a.astype(BF16), bv) + _nt_dot(qs, st.astype(BF16))
        st_scr[...] = st * jnp.exp(tot) + jnp.where(same_head, _tn_dot(vb, ks), 0.0)
        o2 = o * o
        o2_hi = o2.astype(BF16)
        o2_lo = (o2 - o2_hi.astype(F32)).astype(BF16)
        ms = (_dot(o2_hi, head_ones) + _dot(o2_lo, head_ones)) * (1.0 / HEAD_DIM)
        o = o * lax.rsqrt(ms + RMS_EPS) * ng
        o = o * _silu(proj_scr[rows, C_BG:C_BG + W_GRP])
        hg_scr[rows, :] = o
        return carry

    lax.fori_loop(0, n_ch, chunk_body, 0, unroll=2)
    mix_scr[:, W_GRP:2 * W_GRP] = hg_scr[...].astype(BF16)

    @pl.when(last)
    def _():
        s_t = st_scr[...].T
        for h in range(N_HG):
            snew_ref[h * HEAD_DIM:(h + 1) * HEAD_DIM, :] = (
                s_t[h * HEAD_DIM:(h + 1) * HEAD_DIM, h * HEAD_DIM:(h + 1) * HEAD_DIM])

    u_scr[SUBLANES:SUBLANES + tb, :] = proj_scr[:, C_CC:C_CC + W_GRP] * proj_scr[:, C_CH:C_CH + W_GRP]
    cw = cw_ref[...]
    yc = (u_scr[SUBLANES - 2:SUBLANES - 2 + tb, :] * cw[0:1, :]
          + u_scr[SUBLANES - 1:SUBLANES - 1 + tb, :] * cw[1:2, :]
          + u_scr[SUBLANES:SUBLANES + tb, :] * cw[2:3, :])
    mix_scr[:, 2 * W_GRP:3 * W_GRP] = (proj_scr[:, C_CB:C_CB + W_GRP] * yc).astype(BF16)
    tail = u_scr[tb:tb + SUBLANES, :]
    u_scr[0:SUBLANES, :] = tail

    @pl.when(last)
    def _():
        cnew_ref[...] = tail[SUBLANES - 2:SUBLANES, :]

    dv = proj_scr[:, C_DV:C_DV + W_GRP]
    p_scr[16:16 + tb, :] = dv
    ext = p_scr[...]
    s2 = ext + pltpu.roll(ext, 1, axis=0)
    s4 = s2 + pltpu.roll(s2, 2, axis=0)
    s8 = s4 + pltpu.roll(s4, 4, axis=0)
    s16 = s8 + pltpu.roll(s8, 8, axis=0)
    rp = lax.broadcasted_iota(jnp.int32, (tb, W_GRP), 0)
    cp = lax.broadcasted_iota(jnp.int32, (tb, W_GRP), 1)
    grp = cp >> 6
    win = jnp.where(grp == 0, s2[16:], jnp.where(grp == 1, s4[16:], jnp.where(grp == 2, s8[16:], s16[16:])))
    width = jnp.left_shift(2, grp)
    cnt = jnp.minimum(ti * tb + rp + 1, width).astype(F32)
    pooled = win / cnt - dv
    yd = _dot(pooled.astype(BF16), pw_ref[...]) * ps_ref[...]
    mix_scr[:, 3 * W_GRP:4 * W_GRP] = yd.astype(BF16)
    ptail = p_scr[tb:tb + 16, :]
    p_scr[0:16, :] = ptail

    @pl.when(last)
    def _():
        pnew_ref[...] = ptail[1:16, :]

    y = _dot(mix_scr[...], w_o_ref[...])
    y_ref[...] = _layer_norm(ALPHA * x + y, g_ref[...], b_ref[...])


def _prompt_mixer_call(x, w_in, sink, lb, ng, cw, pw_bd, ps, w_o, g, b):
    bsz, t, _ = x.shape
    tb = min(TOKEN_BLOCK, t)
    row = lambda i, j: (i, 0, 0)
    single = pl.Buffered(1)
    out_shape = (
        jax.ShapeDtypeStruct((bsz, t, D_MODEL), F32),
        jax.ShapeDtypeStruct((bsz, WINDOW, LANES), F32),
        jax.ShapeDtypeStruct((bsz, WINDOW, LANES), F32),
        jax.ShapeDtypeStruct((bsz, W_GRP, HEAD_DIM), F32),
        jax.ShapeDtypeStruct((bsz, CONV_W - 1, W_GRP), F32),
        jax.ShapeDtypeStruct((bsz, POOL_BUF, W_GRP), F32),
    )
    return pl.pallas_call(
        _prompt_mixer_kernel,
        name="mixer_prompt",
        grid=(bsz, t // tb),
        in_specs=[pl.BlockSpec((None, tb, D_MODEL), lambda i, j: (i, j, 0)),
                  pl.BlockSpec((D_MODEL, D_IN), lambda i, j: (0, 0), pipeline_mode=single),
                  pl.BlockSpec(memory_space=pltpu.SMEM),
                  _full_spec((1, W_GRP)), _full_spec((1, W_GRP)), _full_spec((CONV_W, W_GRP)),
                  _full_spec((W_GRP, W_GRP)), _full_spec((1, W_GRP)),
                  pl.BlockSpec((D_MODEL, D_MODEL), lambda i, j: (0, 0), pipeline_mode=single),
                  _full_spec((1, D_MODEL)), _full_spec((1, D_MODEL))],
        out_specs=(pl.BlockSpec((None, tb, D_MODEL), lambda i, j: (i, j, 0)),
                   pl.BlockSpec((None, WINDOW, LANES), row),
                   pl.BlockSpec((None, WINDOW, LANES), row),
                   pl.BlockSpec((None, W_GRP, HEAD_DIM), row),
                   pl.BlockSpec((None, CONV_W - 1, W_GRP), row),
                   pl.BlockSpec((None, POOL_BUF, W_GRP), row)),
        out_shape=out_shape,
        scratch_shapes=[
            pltpu.VMEM((tb, D_IN), F32),
            pltpu.VMEM((WINDOW + tb, LANES), BF16),
            pltpu.VMEM((WINDOW + tb, LANES), BF16),
            pltpu.VMEM((W_GRP, W_GRP), F32),
            pltpu.VMEM((SUBLANES + tb, W_GRP), F32),
            pltpu.VMEM((16 + tb, W_GRP), F32),
            pltpu.VMEM((2 * WINDOW, 4 * WINDOW), F32),
            pltpu.VMEM((tb, D_MODEL), BF16),
            pltpu.VMEM((tb, W_GRP), F32),
        ],
        compiler_params=_compiler_params(("arbitrary", "arbitrary")),
    )(x, w_in, sink, lb, ng, cw, pw_bd, ps, w_o, g, b)


DEC_SEQ = 4
SEQ_BLOCK = 16
PAIR_ROWS = 2 * DEC_SEQ


def _sample_mixer_kernel(x_ref, ck_ref, cv_ref, s_ref, cs_ref, pb_ref,
                         w_in_ref, w_hgt_ref, sink_ref, lb_ref, lbc_ref, ng_ref, cw_ref, pw_ref, ps_ref,
                         w_o_ref, g_ref, b_ref,
                         y_ref, knew_ref, vnew_ref, snew_ref, cnew_ref, pnew_ref,
                         proj_scr, mix_scr, hg_scr, cd_scr, od_scr):
    sb = ck_ref.shape[0]
    m_rows = sb * DEC_SEQ
    x = x_ref[...]
    xb = x.astype(BF16)
    proj_scr[...] = _dot(xb, w_in_ref[...])

    lane = lax.broadcasted_iota(jnp.int32, (PAIR_ROWS, LANES), 1)
    lo = lane < HEAD_DIM
    sub8 = lax.broadcasted_iota(jnp.int32, (PAIR_ROWS, LANES), 0)
    n_rows = 4 * PAIR_ROWS
    r_c = lax.broadcasted_iota(jnp.int32, (n_rows, 2 * WINDOW), 0)
    c_c = lax.broadcasted_iota(jnp.int32, (n_rows, 2 * WINDOW), 1)
    step_c = r_c & (DEC_SEQ - 1)
    rel_c = step_c + WINDOW - (c_c & (WINDOW - 1))
    valid_c = (((r_c >> 2) & 1) == (c_c >> 7)) & (rel_c <= WINDOW)
    slope_c = jnp.exp2(-2.0 * ((r_c >> 3).astype(F32) + 1.0))
    bias_c = jnp.where(valid_c, -slope_c * rel_c.astype(F32), MASK_VALUE)
    r_n = lax.broadcasted_iota(jnp.int32, (n_rows, PAIR_ROWS), 0)
    c_n = lax.broadcasted_iota(jnp.int32, (n_rows, PAIR_ROWS), 1)
    rel_n = (r_n & (DEC_SEQ - 1)) - (c_n & (DEC_SEQ - 1))
    valid_n = (((r_n >> 2) & 1) == (c_n >> 2)) & (rel_n >= 0)
    slope_n = jnp.exp2(-2.0 * ((r_n >> 3).astype(F32) + 1.0))
    bias_n = jnp.where(valid_n, -slope_n * rel_n.astype(F32), MASK_VALUE)
    head_col = lax.broadcasted_iota(jnp.int32, (n_rows, 1), 0) >> 3
    sink_col = jnp.where(head_col == 0, sink_ref[0],
                         jnp.where(head_col == 1, sink_ref[1], jnp.where(head_col == 2, sink_ref[2], sink_ref[3])))
    row128 = lax.broadcasted_iota(jnp.int32, (WINDOW, LANES), 0)
    for p in range(sb // 2):
        rows = slice(p * PAIR_ROWS, (p + 1) * PAIR_ROWS)
        q0 = proj_scr[rows, 0:LANES] * (HEAD_DIM ** -0.5)
        q1 = proj_scr[rows, LANES:2 * LANES] * (HEAD_DIM ** -0.5)
        q0r = pltpu.roll(q0, HEAD_DIM, axis=1)
        q1r = pltpu.roll(q1, HEAD_DIM, axis=1)
        zero = jnp.zeros_like(q0)
        q4 = jnp.concatenate([jnp.where(lo, q0, zero), jnp.where(lo, q0r, zero),
                              jnp.where(lo, zero, q1r), jnp.where(lo, zero, q1)], axis=0).astype(BF16)
        kn8 = proj_scr[rows, C_AK:C_AK + LANES]
        vn8 = proj_scr[rows, C_AV:C_AV + LANES]
        ka, kb = ck_ref[2 * p], ck_ref[2 * p + 1]
        va, vb = cv_ref[2 * p], cv_ref[2 * p + 1]
        s_c = _nt_dot(q4, jnp.concatenate([ka, kb], axis=0).astype(BF16)) + bias_c
        s_n = _nt_dot(q4, kn8.astype(BF16)) + bias_n
        m = jnp.maximum(jnp.maximum(jnp.max(s_c, axis=-1, keepdims=True),
                                    jnp.max(s_n, axis=-1, keepdims=True)), sink_col)
        p_c = jnp.exp(s_c - m)
        p_n = jnp.exp(s_n - m)
        den = (jnp.sum(p_c, axis=-1, keepdims=True) + jnp.sum(p_n, axis=-1, keepdims=True)
               + jnp.exp(sink_col - m))
        o_all = (_dot(p_c.astype(BF16), jnp.concatenate([va, vb], axis=0).astype(BF16))
                 + _dot(p_n.astype(BF16), vn8.astype(BF16))) / den
        o = [o_all[h * PAIR_ROWS:(h + 1) * PAIR_ROWS, :] for h in range(4)]
        mix_scr[rows, 0:LANES] = jnp.where(lo, o[0], pltpu.roll(o[1], HEAD_DIM, axis=1))
        mix_scr[rows, LANES:2 * LANES] = jnp.where(lo, pltpu.roll(o[2], HEAD_DIM, axis=1), o[3])
        kn_hi = pltpu.roll(kn8, DEC_SEQ, axis=0)
        vn_hi = pltpu.roll(vn8, DEC_SEQ, axis=0)
        for idx, old, new, out_ref in ((2 * p, ka, kn_hi, knew_ref), (2 * p + 1, kb, kn8, knew_ref),
                                       (2 * p, va, vn_hi, vnew_ref), (2 * p + 1, vb, vn8, vnew_ref)):
            shifted = pltpu.roll(old, WINDOW - DEC_SEQ, axis=0)
            out_ref[idx, 0:WINDOW - SUBLANES, :] = shifted[0:WINDOW - SUBLANES, :]
            out_ref[idx, WINDOW - SUBLANES:WINDOW, :] = jnp.where(
                sub8 >= DEC_SEQ, new, shifted[WINDOW - SUBLANES:WINDOW, :])

    hgt = _nt_dot(w_hgt_ref[...], xb)
    q_t = _silu(hgt[0:W_GRP, :])
    sig_pos, sig_neg = _sigmoid_pair(hgt[W_GRP:2 * W_GRP, :])
    lbc = lbc_ref[...]
    f_t = sig_pos + jnp.maximum(lbc, LB_FLOOR) * sig_neg
    k_t = (1.0 - lbc) * sig_neg
    for b in range(sb):
        st = s_ref[b]
        for step in range(DEC_SEQ):
            c = b * DEC_SEQ + step
            v_row = proj_scr[c:c + 1, C_BI:C_BI + W_GRP]
            v_exp = jnp.concatenate(
                [jnp.broadcast_to(v_row[:, h * HEAD_DIM:(h + 1) * HEAD_DIM], (HEAD_DIM, HEAD_DIM))
                 for h in range(N_HG)], axis=0)
            st = f_t[:, c:c + 1] * st + k_t[:, c:c + 1] * v_exp
            qs = q_t[:, c:c + 1] * st
            hg_scr[c:c + 1, :] = jnp.concatenate(
                [jnp.sum(qs[h * HEAD_DIM:(h + 1) * HEAD_DIM, :], axis=0, keepdims=True) for h in range(N_HG)],
                axis=1)
        snew_ref[b] = st
    r256 = lax.broadcasted_iota(jnp.int32, (W_GRP, W_GRP), 0)
    c256 = lax.broadcasted_iota(jnp.int32, (W_GRP, W_GRP), 1)
    head_ones = jnp.where((r256 >> 6) == (c256 >> 6), 1.0, 0.0).astype(BF16)
    o = hg_scr[...]
    o2 = o * o
    o2_hi = o2.astype(BF16)
    o2_lo = (o2 - o2_hi.astype(F32)).astype(BF16)
    ms = (_dot(o2_hi, head_ones) + _dot(o2_lo, head_ones)) * (1.0 / HEAD_DIM)
    o = o * lax.rsqrt(ms + RMS_EPS) * ng_ref[...]
    mix_scr[:, W_GRP:2 * W_GRP] = o * _silu(proj_scr[:, C_BG:C_BG + W_GRP])

    for j in range(8):
        cd_scr[j] = proj_scr[:, C_CB + j * LANES:C_CB + (j + 1) * LANES]
    cw = cw_ref[...]
    ps = ps_ref[...]
    cp = lax.broadcasted_iota(jnp.int32, (sb, W_GRP), 1) >> 6
    width = jnp.left_shift(2, cp)
    step_rows = lambda step: pl.ds(step, sb, stride=DEC_SEQ)

    def step_load(tile, step):
        return jnp.concatenate([cd_scr[tile, step_rows(step), :], cd_scr[tile + 1, step_rows(step), :]], axis=1)

    def step_store(tile, step, val):
        od_scr[tile, step_rows(step), :] = val[:, 0:LANES]
        od_scr[tile + 1, step_rows(step), :] = val[:, LANES:2 * LANES]

    u = [cs_ref[0], cs_ref[1]]
    ext = [pb_ref[i] for i in range(POOL_BUF)]
    for step in range(DEC_SEQ):
        u.append(step_load(2, step) * step_load(4, step))
        ext.append(step_load(6, step))
    cnew_ref[0] = u[DEC_SEQ]
    cnew_ref[1] = u[DEC_SEQ + 1]
    for i in range(POOL_BUF):
        pnew_ref[i] = ext[DEC_SEQ + i]
    for step in range(DEC_SEQ):
        yc = u[step] * cw[0:1, :] + u[step + 1] * cw[1:2, :] + u[step + 2] * cw[2:3, :]
        step_store(0, step, step_load(0, step) * yc)
        top = POOL_BUF + step
        acc = ext[top]
        sums = {}
        for jj in range(1, 16):
            acc = acc + ext[top - jj]
            if jj + 1 in POOL_WINDOWS:
                sums[jj + 1] = acc
        win = jnp.where(cp == 0, sums[2], jnp.where(cp == 1, sums[4], jnp.where(cp == 2, sums[8], sums[16])))
        cnt = jnp.minimum(PAST_LEN + step + 1, width).astype(F32)
        pooled = win / cnt - ext[top]
        step_store(2, step, _dot(pooled.astype(BF16), pw_ref[...]) * ps)
    for j in range(4):
        mix_scr[:, 2 * W_GRP + j * LANES:2 * W_GRP + (j + 1) * LANES] = od_scr[j]

    y = _dot(mix_scr[...].astype(BF16), w_o_ref[...])
    y_ref[...] = _layer_norm(ALPHA * x + y, g_ref[...], b_ref[...])


def _sample_mixer_call(l, x2d, ck, cv, s, cs_tm, pb_tm, w):
    m = x2d.shape[0]
    nb = m // DEC_SEQ
    sb = SEQ_BLOCK
    rows = sb * DEC_SEQ
    single = pl.Buffered(1)
    out_shape = (
        jax.ShapeDtypeStruct((m, D_MODEL), F32),
        jax.ShapeDtypeStruct((nb, WINDOW, LANES), F32),
        jax.ShapeDtypeStruct((nb, WINDOW, LANES), F32),
        jax.ShapeDtypeStruct((nb, W_GRP, HEAD_DIM), F32),
        jax.ShapeDtypeStruct((CONV_W - 1, nb, W_GRP), F32),
        jax.ShapeDtypeStruct((POOL_BUF, nb, W_GRP), F32),
    )
    return pl.pallas_call(
        _sample_mixer_kernel,
        name="mixer_sample",
        grid=(nb // sb,),
        in_specs=[pl.BlockSpec((rows, D_MODEL), lambda i: (i, 0)),
                  pl.BlockSpec((None, sb, WINDOW, LANES), lambda i: (l, i, 0, 0)),
                  pl.BlockSpec((None, sb, WINDOW, LANES), lambda i: (l, i, 0, 0)),
                  pl.BlockSpec((None, sb, W_GRP, HEAD_DIM), lambda i: (l, i, 0, 0)),
                  pl.BlockSpec((None, CONV_W - 1, sb, W_GRP), lambda i: (l, 0, i, 0)),
                  pl.BlockSpec((None, POOL_BUF, sb, W_GRP), lambda i: (l, 0, i, 0)),
                  pl.BlockSpec((D_MODEL, D_IN), lambda i: (0, 0), pipeline_mode=single),
                  _full_spec((2 * W_GRP, D_MODEL)),
                  pl.BlockSpec(memory_space=pltpu.SMEM),
                  _full_spec((1, W_GRP)), _full_spec((W_GRP, 1)), _full_spec((1, W_GRP)),
                  _full_spec((CONV_W, W_GRP)), _full_spec((W_GRP, W_GRP)), _full_spec((1, W_GRP)),
                  pl.BlockSpec((D_MODEL, D_MODEL), lambda i: (0, 0), pipeline_mode=single),
                  _full_spec((1, D_MODEL)), _full_spec((1, D_MODEL))],
        out_specs=(pl.BlockSpec((rows, D_MODEL), lambda i: (i, 0)),
                   pl.BlockSpec((sb, WINDOW, LANES), lambda i: (i, 0, 0)),
                   pl.BlockSpec((sb, WINDOW, LANES), lambda i: (i, 0, 0)),
                   pl.BlockSpec((sb, W_GRP, HEAD_DIM), lambda i: (i, 0, 0)),
                   pl.BlockSpec((CONV_W - 1, sb, W_GRP), lambda i: (0, i, 0)),
                   pl.BlockSpec((POOL_BUF, sb, W_GRP), lambda i: (0, i, 0))),
        out_shape=out_shape,
        scratch_shapes=[pltpu.VMEM((rows, D_IN), F32),
                        pltpu.VMEM((rows, D_MODEL), F32),
                        pltpu.VMEM((rows, W_GRP), F32),
                        pltpu.VMEM((8, rows, LANES), F32),
                        pltpu.VMEM((4, rows, LANES), F32)],
        compiler_params=_compiler_params(("arbitrary",)),
    )(x2d, ck, cv, s, cs_tm, pb_tm, w["w_in"], w["w_hgt"], w["sink"], w["lb"], w["lb_col"], w["ng"], w["cw"],
      w["pw"], w["ps"], w["w_o"], w["ln1_g"], w["ln1_b"])


def _xattn_sample_kernel(x_ref, mk_ref, mv_ref, wq_ref, wo_ref, g_ref, b_ref, o_ref, att_scr):
    sb = mk_ref.shape[0]
    x = x_ref[...]
    q = _dot(x.astype(BF16), wq_ref[...]) * (HEAD_DIM ** -0.5)
    n_rows = N_XH * PAIR_ROWS
    lane = lax.broadcasted_iota(jnp.int32, (PAIR_ROWS, D_X), 1) >> 6
    r_s = lax.broadcasted_iota(jnp.int32, (n_rows, 2 * N_MEM), 0)
    c_s = lax.broadcasted_iota(jnp.int32, (n_rows, 2 * N_MEM), 1)
    own = ((r_s >> 2) & 1) == (c_s >> 8)
    for p in range(sb // 2):
        rows = slice(p * PAIR_ROWS, (p + 1) * PAIR_ROWS)
        q8 = q[rows, :]
        zero = jnp.zeros_like(q8)
        q4 = jnp.concatenate([jnp.where(lane == h, q8, zero) for h in range(N_XH)], axis=0).astype(BF16)
        k2 = jnp.concatenate([mk_ref[2 * p], mk_ref[2 * p + 1]], axis=0).astype(BF16)
        v2 = jnp.concatenate([mv_ref[2 * p], mv_ref[2 * p + 1]], axis=0).astype(BF16)
        s = jnp.where(own, _nt_dot(q4, k2), MASK_VALUE)
        m = jnp.max(s, axis=-1, keepdims=True)
        pr = jnp.exp(s - m)
        den = jnp.sum(pr, axis=-1, keepdims=True)
        o_all = _dot(pr.astype(BF16), v2) / den
        o8 = jnp.zeros_like(q8)
        for h in range(N_XH):
            o8 = jnp.where(lane == h, o_all[h * PAIR_ROWS:(h + 1) * PAIR_ROWS, :], o8)
        att_scr[rows, :] = o8.astype(BF16)
    y = _dot(att_scr[...], wo_ref[...])
    o_ref[...] = _layer_norm(ALPHA * x + y, g_ref[...], b_ref[...])


def _xattn_sample_call(l, x2d, mk, mv, w):
    m = x2d.shape[0]
    nb = m // DEC_SEQ
    sb = SEQ_BLOCK
    rows = sb * DEC_SEQ
    return pl.pallas_call(
        _xattn_sample_kernel,
        name="xattn_sample",
        grid=(nb // sb,),
        in_specs=[pl.BlockSpec((rows, D_MODEL), lambda i: (i, 0)),
                  pl.BlockSpec((None, sb, N_MEM, D_X), lambda i: (l, i, 0, 0)),
                  pl.BlockSpec((None, sb, N_MEM, D_X), lambda i: (l, i, 0, 0)),
                  _full_spec((D_MODEL, D_X)), _full_spec((D_X, D_MODEL)),
                  _full_spec((1, D_MODEL)), _full_spec((1, D_MODEL))],
        out_specs=pl.BlockSpec((rows, D_MODEL), lambda i: (i, 0)),
        out_shape=jax.ShapeDtypeStruct((m, D_MODEL), F32),
        scratch_shapes=[pltpu.VMEM((rows, D_X), BF16)],
        compiler_params=_compiler_params(("arbitrary",)),
    )(x2d, mk, mv, w["w_xq"], w["w_xo"], w["ln2_g"], w["ln2_b"])


SWA_ROWS = WINDOW * N_KV
NEW_ROWS = DEC_SEQ * N_KV
MEM_ROWS = N_MEM * N_XH


def _hi_dot(a, b):
    return jnp.dot(a, b, preferred_element_type=F32, precision=lax.Precision.HIGHEST)


def _head_rows(tile):
    return jnp.concatenate([tile[:, h * HEAD_DIM:(h + 1) * HEAD_DIM] for h in range(4)], axis=0)


def _head_lanes(rows32):
    return jnp.concatenate([rows32[h * PAIR_ROWS:(h + 1) * PAIR_ROWS, :] for h in range(4)], axis=1)


def _sample_mixer2_kernel(x_ref, ck_ref, cv_ref, s_ref, cs_ref, pb_ref,
                          w_in_ref, sink_ref, lb_ref, ng_ref, cw_ref, pw_ref, ps_ref, w_o_ref, g_ref, b_ref,
                          *rest):
    (y_ref, knew_ref, vnew_ref, snew_ref, cnew_ref, pnew_ref,
     proj_scr, mix_scr, cd_scr, od_scr, new_scr, sc_scr, sn_scr, pc_scr, pn_scr) = rest[-15:]
    sb = ck_ref.shape[0]
    m_rows = sb * DEC_SEQ
    n_pair = sb // 2
    x = x_ref[...]
    xb = x.astype(BF16)
    proj_scr[...] = _dot(xb, w_in_ref[...])

    r2 = lax.broadcasted_iota(jnp.int32, (2 * m_rows, m_rows), 0)
    c2 = lax.broadcasted_iota(jnp.int32, (2 * m_rows, m_rows), 1)
    pick0 = jnp.where(r2 == 2 * c2, 1.0, 0.0)
    pick1 = jnp.where(r2 == 2 * c2 + 1, 1.0, 0.0)
    for idx, col in ((0, C_AK), (1, C_AV)):
        new_scr[idx] = (_hi_dot(pick0, proj_scr[:, col:col + HEAD_DIM])
                        + _hi_dot(pick1, proj_scr[:, col + HEAD_DIM:col + 2 * HEAD_DIM]))
    n_rows = 4 * PAIR_ROWS
    r_c = lax.broadcasted_iota(jnp.int32, (n_rows, 2 * SWA_ROWS), 0)
    c_c = lax.broadcasted_iota(jnp.int32, (n_rows, 2 * SWA_ROWS), 1)
    rel_c = (r_c & 3) + WINDOW - ((c_c >> 1) & (WINDOW - 1))
    ok_c = (((r_c >> 2) & 1) == (c_c >> 8)) & ((r_c >> 4) == (c_c & 1)) & (rel_c <= WINDOW)
    slope_c = jnp.exp2(-2.0 * ((r_c >> 3).astype(F32) + 1.0))
    bias_c = jnp.where(ok_c, -slope_c * rel_c.astype(F32), MASK_VALUE)
    r_n = lax.broadcasted_iota(jnp.int32, (n_rows, 2 * NEW_ROWS), 0)
    c_n = lax.broadcasted_iota(jnp.int32, (n_rows, 2 * NEW_ROWS), 1)
    rel_n = (r_n & 3) - ((c_n >> 1) & 3)
    ok_n = (((r_n >> 2) & 1) == (c_n >> 3)) & ((r_n >> 4) == (c_n & 1)) & (rel_n >= 0)
    slope_n = jnp.exp2(-2.0 * ((r_n >> 3).astype(F32) + 1.0))
    bias_n = jnp.where(ok_n, -slope_n * rel_n.astype(F32), MASK_VALUE)
    for p in range(n_pair):
        rows = slice(p * PAIR_ROWS, (p + 1) * PAIR_ROWS)
        q32 = _head_rows(proj_scr[rows, 0:W_GRP] * (HEAD_DIM ** -0.5)).astype(BF16)
        k2 = jnp.concatenate([ck_ref[2 * p], ck_ref[2 * p + 1]], axis=0).astype(BF16)
        kn = new_scr[0, 2 * NEW_ROWS * p:2 * NEW_ROWS * (p + 1), :].astype(BF16)
        sc_scr[n_rows * p:n_rows * (p + 1), :] = _nt_dot(q32, k2) + bias_c
        sn_scr[n_rows * p:n_rows * (p + 1), :] = _nt_dot(q32, kn) + bias_n
    head_col = (lax.broadcasted_iota(jnp.int32, (n_rows * n_pair, 1), 0) >> 3) & 3
    sink_col = jnp.where(head_col == 0, sink_ref[0],
                         jnp.where(head_col == 1, sink_ref[1], jnp.where(head_col == 2, sink_ref[2], sink_ref[3])))
    s_c = sc_scr[...]
    s_n = sn_scr[...]
    m = jnp.maximum(jnp.maximum(jnp.max(s_c, axis=-1, keepdims=True), jnp.max(s_n, axis=-1, keepdims=True)),
                    sink_col)
    p_c = jnp.exp(s_c - m)
    p_n = jnp.exp(s_n - m)
    inv = 1.0 / (jnp.sum(p_c, axis=-1, keepdims=True) + jnp.sum(p_n, axis=-1, keepdims=True)
                 + jnp.exp(sink_col - m))
    pc_scr[...] = (p_c * inv).astype(BF16)
    pn_scr[...] = (p_n * inv).astype(BF16)
    for p in range(n_pair):
        rows = slice(p * PAIR_ROWS, (p + 1) * PAIR_ROWS)
        v2 = jnp.concatenate([cv_ref[2 * p], cv_ref[2 * p + 1]], axis=0).astype(BF16)
        vn = new_scr[1, 2 * NEW_ROWS * p:2 * NEW_ROWS * (p + 1), :].astype(BF16)
        o32 = (_dot(pc_scr[n_rows * p:n_rows * (p + 1), :], v2)
               + _dot(pn_scr[n_rows * p:n_rows * (p + 1), :], vn))
        mix_scr[rows, 0:W_GRP] = _head_lanes(o32)
    for b in range(sb):
        for old_ref, out_ref, idx in ((ck_ref, knew_ref, 0), (cv_ref, vnew_ref, 1)):
            out_ref[b, 0:SWA_ROWS - NEW_ROWS, :] = old_ref[b, NEW_ROWS:SWA_ROWS, :]
            out_ref[b, SWA_ROWS - NEW_ROWS:SWA_ROWS, :] = new_scr[idx, NEW_ROWS * b:NEW_ROWS * (b + 1), :]

    lb = lb_ref[...]
    lbf = jnp.maximum(lb, LB_FLOOR)
    r256 = lax.broadcasted_iota(jnp.int32, (W_GRP, W_GRP), 0)
    c256 = lax.broadcasted_iota(jnp.int32, (W_GRP, W_GRP), 1)
    same_head = (r256 >> 6) == (c256 >> 6)
    head_ones = jnp.where(same_head, 1.0, 0.0).astype(BF16)
    step = lax.broadcasted_iota(jnp.int32, (m_rows, W_GRP), 0) & (DEC_SEQ - 1)
    q = _silu(proj_scr[:, C_BQ:C_BQ + W_GRP])
    sig_pos, sig_neg = _sigmoid_pair(proj_scr[:, C_BF:C_BF + W_GRP])
    g = jnp.log(sig_pos + lbf * sig_neg)
    k = (1.0 - lb) * sig_neg
    v = proj_scr[:, C_BI:C_BI + W_GRP]
    cum = g + jnp.where(step >= 1, pltpu.roll(g, 1, axis=0), 0.0)
    cum = cum + jnp.where(step >= 2, pltpu.roll(cum, 2, axis=0), 0.0)
    o = jnp.zeros((m_rows, W_GRP), F32)
    for lag in range(DEC_SEQ):
        if lag == 0:
            w_qk = q * k
            v_l = v
        else:
            decay = jnp.exp(jnp.minimum(cum - pltpu.roll(cum, lag, axis=0), 0.0))
            w_qk = jnp.where(step >= lag, q * pltpu.roll(k, lag, axis=0) * decay, 0.0)
            v_l = pltpu.roll(v, lag, axis=0)
        o = o + _dot(w_qk.astype(BF16), head_ones) * v_l
    tot = jnp.where(step == 3, cum,
                    jnp.where(step == 2, pltpu.roll(cum, m_rows - 1, axis=0),
                              jnp.where(step == 1, pltpu.roll(cum, m_rows - 2, axis=0),
                                        pltpu.roll(cum, m_rows - 3, axis=0))))
    qs = q * jnp.exp(cum)
    ks = k * jnp.exp(tot - cum)
    dec = jnp.exp(tot)
    dec_hi = dec.astype(BF16).astype(F32)
    dec_mid = (dec - dec_hi).astype(BF16).astype(F32)
    dec_lo = dec - dec_hi - dec_mid
    dec3 = jnp.where(step == 0, dec_hi, jnp.where(step == 1, dec_mid, jnp.where(step == 2, dec_lo, 0.0)))
    sub8 = lax.broadcasted_iota(jnp.int32, (PAIR_ROWS, W_GRP), 0)
    ones3 = jnp.where((lax.broadcasted_iota(jnp.int32, (PAIR_ROWS, HEAD_DIM), 0) & 3) < 3, 1.0, 0.0)
    for p in range(n_pair):
        rows = slice(p * PAIR_ROWS, (p + 1) * PAIR_ROWS)
        qs8 = qs[rows, :].astype(BF16)
        o_inter = jnp.zeros((PAIR_ROWS, W_GRP), F32)
        for half in range(2):
            b = 2 * p + half
            own = (sub8 >> 2) == half
            own64 = own[:, 0:HEAD_DIM]
            st = s_ref[b]
            st_bd = jnp.where(same_head, jnp.concatenate([st] * N_HG, axis=1), 0.0).astype(BF16)
            o_inter = jnp.where(own, _dot(qs8, st_bd), o_inter)
            ks8 = jnp.where(own, ks[rows, :], 0.0).astype(BF16)
            upd = _tn_dot(ks8, v[rows, :].astype(BF16))
            upd = jnp.concatenate([upd[h * HEAD_DIM:(h + 1) * HEAD_DIM, h * HEAD_DIM:(h + 1) * HEAD_DIM]
                                   for h in range(N_HG)], axis=0)
            d8 = jnp.where(own, dec3[rows, :], 0.0).astype(BF16)
            dcol = _tn_dot(d8, jnp.where(own64, ones3, 0.0).astype(BF16))
            snew_ref[b] = dcol * st + upd
        hg = o[rows, :] + o_inter
        hg2 = hg * hg
        hg2_hi = hg2.astype(BF16)
        hg2_lo = (hg2 - hg2_hi.astype(F32)).astype(BF16)
        ms = (_dot(hg2_hi, head_ones) + _dot(hg2_lo, head_ones)) * (1.0 / HEAD_DIM)
        hg = hg * lax.rsqrt(ms + RMS_EPS) * ng_ref[...]
        mix_scr[rows, W_GRP:2 * W_GRP] = hg * _silu(proj_scr[rows, C_BG:C_BG + W_GRP])

    for j in range(8):
        cd_scr[j] = proj_scr[:, C_CB + j * LANES:C_CB + (j + 1) * LANES]
    cw = cw_ref[...]
    ps = ps_ref[...]
    cp = lax.broadcasted_iota(jnp.int32, (sb, W_GRP), 1) >> 6
    width = jnp.left_shift(2, cp)
    step_rows = lambda s_: pl.ds(s_, sb, stride=DEC_SEQ)

    def step_load(tile, s_):
        return jnp.concatenate([cd_scr[tile, step_rows(s_), :], cd_scr[tile + 1, step_rows(s_), :]], axis=1)

    def step_store(tile, s_, val):
        od_scr[tile, step_rows(s_), :] = val[:, 0:LANES]
        od_scr[tile + 1, step_rows(s_), :] = val[:, LANES:2 * LANES]

    u = [cs_ref[0], cs_ref[1]]
    ext = [pb_ref[i] for i in range(POOL_BUF)]
    for s_ in range(DEC_SEQ):
        u.append(step_load(2, s_) * step_load(4, s_))
        ext.append(step_load(6, s_))
    cnew_ref[0] = u[DEC_SEQ]
    cnew_ref[1] = u[DEC_SEQ + 1]
    for i in range(POOL_BUF):
        pnew_ref[i] = ext[DEC_SEQ + i]
    for s_ in range(DEC_SEQ):
        yc = u[s_] * cw[0:1, :] + u[s_ + 1] * cw[1:2, :] + u[s_ + 2] * cw[2:3, :]
        step_store(0, s_, step_load(0, s_) * yc)
        top = POOL_BUF + s_
        acc = ext[top]
        sums = {}
        for jj in range(1, 16):
            acc = acc + ext[top - jj]
            if jj + 1 in POOL_WINDOWS:
                sums[jj + 1] = acc
        win = jnp.where(cp == 0, sums[2], jnp.where(cp == 1, sums[4], jnp.where(cp == 2, sums[8], sums[16])))
        cnt = jnp.minimum(PAST_LEN + s_ + 1, width).astype(F32)
        pooled = win / cnt - ext[top]
        step_store(2, s_, _dot(pooled.astype(BF16), pw_ref[...]) * ps)
    for j in range(4):
        mix_scr[:, 2 * W_GRP + j * LANES:2 * W_GRP + (j + 1) * LANES] = od_scr[j]

    y = _dot(mix_scr[...].astype(BF16), w_o_ref[...])
    y_ref[...] = _layer_norm(ALPHA * x + y, g_ref[...], b_ref[...])


def _sample_mixer2_call(l, x2d, ck, cv, s, cs_tm, pb_tm, w, prev):
    m = x2d.shape[0]
    nb = m // DEC_SEQ
    sb = SEQ_BLOCK
    rows = sb * DEC_SEQ
    n_sc = 4 * PAIR_ROWS * (sb // 2)
    single = pl.Buffered(1)
    out_shape = (
        jax.ShapeDtypeStruct((m, D_MODEL), F32),
        jax.ShapeDtypeStruct((DEPTH, nb, SWA_ROWS, HEAD_DIM), F32),
        jax.ShapeDtypeStruct((DEPTH, nb, SWA_ROWS, HEAD_DIM), F32),
        jax.ShapeDtypeStruct((DEPTH, nb, W_GRP, HEAD_DIM), F32),
        jax.ShapeDtypeStruct((CONV_W - 1, nb, W_GRP), F32),
        jax.ShapeDtypeStruct((POOL_BUF, nb, W_GRP), F32),
    )
    layer_blk = lambda i: (l, i, 0, 0)
    in_specs = [pl.BlockSpec((rows, D_MODEL), lambda i: (i, 0)),
                pl.BlockSpec((None, sb, SWA_ROWS, HEAD_DIM), layer_blk),
                pl.BlockSpec((None, sb, SWA_ROWS, HEAD_DIM), layer_blk),
                pl.BlockSpec((None, sb, W_GRP, HEAD_DIM), layer_blk),
                pl.BlockSpec((None, CONV_W - 1, sb, W_GRP), lambda i: (l, 0, i, 0)),
                pl.BlockSpec((None, POOL_BUF, sb, W_GRP), lambda i: (l, 0, i, 0)),
                pl.BlockSpec((D_MODEL, D_IN), lambda i: (0, 0), pipeline_mode=single),
                pl.BlockSpec(memory_space=pltpu.SMEM),
                _full_spec((1, W_GRP)), _full_spec((1, W_GRP)),
                _full_spec((CONV_W, W_GRP)), _full_spec((W_GRP, W_GRP)), _full_spec((1, W_GRP)),
                pl.BlockSpec((D_MODEL, D_MODEL), lambda i: (0, 0), pipeline_mode=single),
                _full_spec((1, D_MODEL)), _full_spec((1, D_MODEL))]
    args = [x2d, ck, cv, s, cs_tm, pb_tm, w["w_in"], w["sink"], w["lb"], w["ng"], w["cw"], w["pw"], w["ps"],
            w["w_o"], w["ln1_g"], w["ln1_b"]]
    aliases = {}
    if prev is not None:
        for j, buf in enumerate(prev):
            aliases[len(args)] = 1 + j
            args.append(buf)
            in_specs.append(pl.BlockSpec(memory_space=pl.ANY))
    return pl.pallas_call(
        _sample_mixer2_kernel,
        name="mixer_sample",
        grid=(nb // sb,),
        in_specs=in_specs,
        out_specs=(pl.BlockSpec((rows, D_MODEL), lambda i: (i, 0)),
                   pl.BlockSpec((None, sb, SWA_ROWS, HEAD_DIM), layer_blk),
                   pl.BlockSpec((None, sb, SWA_ROWS, HEAD_DIM), layer_blk),
                   pl.BlockSpec((None, sb, W_GRP, HEAD_DIM), layer_blk),
                   pl.BlockSpec((CONV_W - 1, sb, W_GRP), lambda i: (0, i, 0)),
                   pl.BlockSpec((POOL_BUF, sb, W_GRP), lambda i: (0, i, 0))),
        out_shape=out_shape,
        input_output_aliases=aliases,
        scratch_shapes=[pltpu.VMEM((rows, D_IN), F32),
                        pltpu.VMEM((rows, D_MODEL), F32),
                        pltpu.VMEM((8, rows, LANES), F32),
                        pltpu.VMEM((4, rows, LANES), F32),
                        pltpu.VMEM((2, 2 * rows, HEAD_DIM), F32),
                        pltpu.VMEM((n_sc, 2 * SWA_ROWS), F32),
                        pltpu.VMEM((n_sc, 2 * NEW_ROWS), F32),
                        pltpu.VMEM((n_sc, 2 * SWA_ROWS), BF16),
                        pltpu.VMEM((n_sc, 2 * NEW_ROWS), BF16)],
        compiler_params=_compiler_params(("arbitrary",)),
    )(*args)


def _xattn_sample2_kernel(x_ref, mk_ref, mv_ref, wq_ref, wo_ref, g_ref, b_ref, o_ref, s_scr, p_scr, att_scr):
    sb = mk_ref.shape[0]
    n_rows = N_XH * PAIR_ROWS
    x = x_ref[...]
    q = _dot(x.astype(BF16), wq_ref[...]) * (HEAD_DIM ** -0.5)
    r_s = lax.broadcasted_iota(jnp.int32, (n_rows, MEM_ROWS), 0)
    c_s = lax.broadcasted_iota(jnp.int32, (n_rows, MEM_ROWS), 1)
    head_ok = (r_s >> 3) == (c_s & 3)
    member = (r_s >> 2) & 1
    for b in range(sb):
        rows = slice((b // 2) * PAIR_ROWS, (b // 2 + 1) * PAIR_ROWS)
        q32 = _head_rows(q[rows, :]).astype(BF16)
        s = _nt_dot(q32, mk_ref[b].astype(BF16))
        s_scr[b] = jnp.where(head_ok & (member == (b & 1)), s, MASK_VALUE)
    s_all = s_scr[...]
    m = jnp.max(s_all, axis=-1, keepdims=True)
    pr = jnp.exp(s_all - m)
    p_scr[...] = (pr * (1.0 / jnp.sum(pr, axis=-1, keepdims=True))).astype(BF16)
    own_rows = (lax.broadcasted_iota(jnp.int32, (n_rows, HEAD_DIM), 0) >> 2) & 1
    for p in range(sb // 2):
        rows = slice(p * PAIR_ROWS, (p + 1) * PAIR_ROWS)
        o_a = _dot(p_scr[2 * p], mv_ref[2 * p].astype(BF16))
        o_b = _dot(p_scr[2 * p + 1], mv_ref[2 * p + 1].astype(BF16))
        att_scr[rows, :] = _head_lanes(jnp.where(own_rows == 0, o_a, o_b)).astype(BF16)
    y = _dot(att_scr[...], wo_ref[...])
    o_ref[...] = _layer_norm(ALPHA * x + y, g_ref[...], b_ref[...])


def _xattn_sample2_call(l, x2d, mk, mv, w):
    m = x2d.shape[0]
    nb = m // DEC_SEQ
    sb = SEQ_BLOCK
    rows = sb * DEC_SEQ
    return pl.pallas_call(
        _xattn_sample2_kernel,
        name="xattn_sample",
        grid=(nb // sb,),
        in_specs=[pl.BlockSpec((rows, D_MODEL), lambda i: (i, 0)),
                  pl.BlockSpec((None, sb, MEM_ROWS, HEAD_DIM), lambda i: (l, i, 0, 0)),
                  pl.BlockSpec((None, sb, MEM_ROWS, HEAD_DIM), lambda i: (l, i, 0, 0)),
                  _full_spec((D_MODEL, D_X)), _full_spec((D_X, D_MODEL)),
                  _full_spec((1, D_MODEL)), _full_spec((1, D_MODEL))],
        out_specs=pl.BlockSpec((rows, D_MODEL), lambda i: (i, 0)),
        out_shape=jax.ShapeDtypeStruct((m, D_MODEL), F32),
        scratch_shapes=[pltpu.VMEM((sb, N_XH * PAIR_ROWS, MEM_ROWS), F32),
                        pltpu.VMEM((sb, N_XH * PAIR_ROWS, MEM_ROWS), BF16),
                        pltpu.VMEM((rows, D_X), BF16)],
        compiler_params=_compiler_params(("arbitrary",)),
    )(x2d, mk, mv, w["w_xq"], w["w_xo"], w["ln2_g"], w["ln2_b"])


HG_ROWS = N_HG * HEAD_DIM * HEAD_DIM
K_PER_STEP = 32


def _gather_block(dst_scr, src_scr, n_tiles, gi, sb, nb):
    for t in range(DEC_SEQ):
        start = pl.multiple_of(t * nb + gi * sb, sb)
        for j in range(n_tiles):
            dst_scr[j, t * sb:(t + 1) * sb, :] = src_scr[j, pl.ds(start, sb), :]


def _scatter_block(dst_scr, src_scr, n_tiles, gi, sb, nb):
    for t in range(DEC_SEQ):
        start = pl.multiple_of(t * nb + gi * sb, sb)
        for j in range(n_tiles):
            dst_scr[pl.ds(start, sb), j * LANES:(j + 1) * LANES] = src_scr[j, t * sb:(t + 1) * sb, :]


def _sample_mixer3_kernel(xt_ref, ck_ref, cv_ref, s_ref, cs_ref, pb_ref,
                          w_in_ref, w_hgt_ref, w_kvt_ref, sink_ref, lbc_ref, ngc_ref, cw_ref, pw_ref, ps_ref,
                          w_o_ref, g_ref, b_ref, *rest):
    (y_ref, knew_ref, vnew_ref, snew_ref, cnew_ref, pnew_ref,
     proj_scr, a_scr, kvt_scr, q_scr, f_scr, k_scr, v_scr, gate_scr, o_scr, mix_scr,
     ab_scr, oa_scr, sc_scr, sn_scr, pc_scr, pn_scr) = rest[-22:]
    gi = pl.program_id(0)
    sb = ck_ref.shape[0]
    nb = xt_ref.shape[0] // DEC_SEQ
    half = sb // 2

    @pl.when(gi == 0)
    def _():
        xt = xt_ref[...].astype(BF16)
        pa = _dot(xt, w_in_ref[:, 0:C_BQ])
        for j in range(4):
            a_scr[j] = pa[:, j * LANES:(j + 1) * LANES]
        kvt_scr[...] = _nt_dot(w_kvt_ref[...], xt)
        hgt = _nt_dot(w_hgt_ref[...], xt)
        lbc = lbc_ref[...]
        q_scr[...] = _silu(hgt[0:W_GRP, :])
        sig_pos, sig_neg = _sigmoid_pair(hgt[W_GRP:2 * W_GRP, :])
        f_scr[...] = sig_pos + jnp.maximum(lbc, LB_FLOOR) * sig_neg
        k_scr[...] = (1.0 - lbc) * sig_neg
        v_scr[...] = hgt[2 * W_GRP:3 * W_GRP, :]
        gate_scr[...] = _silu(hgt[3 * W_GRP:4 * W_GRP, :])
        o_scr[...] = jnp.zeros(o_scr.shape, F32)
        proj_scr[...] = _dot(xt, w_in_ref[:, C_CB:D_IN])
        cw = cw_ref[...]
        ps = ps_ref[...]
        cp = lax.broadcasted_iota(jnp.int32, (nb, W_GRP), 1) >> 6
        width = jnp.left_shift(2, cp)
        u = [cs_ref[0], cs_ref[1]]
        ext = [pb_ref[i] for i in range(POOL_BUF)]
        for t in range(DEC_SEQ):
            rows = slice(t * nb, (t + 1) * nb)
            u.append(proj_scr[rows, W_GRP:2 * W_GRP] * proj_scr[rows, 2 * W_GRP:3 * W_GRP])
            ext.append(proj_scr[rows, 3 * W_GRP:4 * W_GRP])
        cnew_ref[0] = u[DEC_SEQ]
        cnew_ref[1] = u[DEC_SEQ + 1]
        for i in range(POOL_BUF):
            pnew_ref[i] = ext[DEC_SEQ + i]
        for t in range(DEC_SEQ):
            rows = slice(t * nb, (t + 1) * nb)
            yc = u[t] * cw[0:1, :] + u[t + 1] * cw[1:2, :] + u[t + 2] * cw[2:3, :]
            mix_scr[rows, 2 * W_GRP:3 * W_GRP] = proj_scr[rows, 0:W_GRP] * yc
            top = POOL_BUF + t
            acc = ext[top]
            sums = {}
            for jj in range(1, 16):
                acc = acc + ext[top - jj]
                if jj + 1 in POOL_WINDOWS:
                    sums[jj + 1] = acc
            win = jnp.where(cp == 0, sums[2], jnp.where(cp == 1, sums[4], jnp.where(cp == 2, sums[8], sums[16])))
            cnt = jnp.minimum(PAST_LEN + t + 1, width).astype(F32)
            pooled = win / cnt - ext[top]
            mix_scr[rows, 3 * W_GRP:4 * W_GRP] = _dot(pooled.astype(BF16), pw_ref[...]) * ps

    _gather_block(ab_scr, a_scr, 4, gi, sb, nb)
    lane = lax.broadcasted_iota(jnp.int32, (PAIR_ROWS, LANES), 1)
    lo = lane < HEAD_DIM
    n_rows = 4 * PAIR_ROWS
    r_c = lax.broadcasted_iota(jnp.int32, (n_rows, 2 * WINDOW), 0)
    c_c = lax.broadcasted_iota(jnp.int32, (n_rows, 2 * WINDOW), 1)
    rel_c = ((r_c >> 1) & 3) + WINDOW - (c_c & (WINDOW - 1))
    ok_c = ((r_c & 1) == (c_c >> 7)) & (rel_c <= WINDOW)
    bias_c = jnp.where(ok_c, -jnp.exp2(-2.0 * ((r_c >> 3).astype(F32) + 1.0)) * rel_c.astype(F32), MASK_VALUE)
    r_n = lax.broadcasted_iota(jnp.int32, (n_rows, PAIR_ROWS), 0)
    c_n = lax.broadcasted_iota(jnp.int32, (n_rows, PAIR_ROWS), 1)
    rel_n = ((r_n >> 1) & 3) - (c_n >> 1)
    ok_n = ((r_n & 1) == (c_n & 1)) & (rel_n >= 0)
    bias_n = jnp.where(ok_n, -jnp.exp2(-2.0 * ((r_n >> 3).astype(F32) + 1.0)) * rel_n.astype(F32), MASK_VALUE)
    pair_rows = lambda p: pl.ds(p, PAIR_ROWS, stride=half)
    for p in range(half):
        q0 = ab_scr[0, pair_rows(p), :] * (HEAD_DIM ** -0.5)
        q1 = ab_scr[1, pair_rows(p), :] * (HEAD_DIM ** -0.5)
        q0r = pltpu.roll(q0, HEAD_DIM, axis=1)
        q1r = pltpu.roll(q1, HEAD_DIM, axis=1)
        zero = jnp.zeros_like(q0)
        q4 = jnp.concatenate([jnp.where(lo, q0, zero), jnp.where(lo, q0r, zero),
                              jnp.where(lo, zero, q1r), jnp.where(lo, zero, q1)], axis=0).astype(BF16)
        kt2 = jnp.concatenate([ck_ref[p], ck_ref[p + half]], axis=1).astype(BF16)
        kn8 = ab_scr[2, pair_rows(p), :].astype(BF16)
        sc_scr[n_rows * p:n_rows * (p + 1), :] = _dot(q4, kt2) + bias_c
        sn_scr[n_rows * p:n_rows * (p + 1), :] = _nt_dot(q4, kn8) + bias_n
    head_col = (lax.broadcasted_iota(jnp.int32, (n_rows * half, 1), 0) >> 3) & 3
    sink_col = jnp.where(head_col == 0, sink_ref[0],
                         jnp.where(head_col == 1, sink_ref[1], jnp.where(head_col == 2, sink_ref[2], sink_ref[3])))
    s_c = sc_scr[...]
    s_n = sn_scr[...]
    m = jnp.maximum(jnp.maximum(jnp.max(s_c, axis=-1, keepdims=True), jnp.max(s_n, axis=-1, keepdims=True)),
                    sink_col)
    p_c = jnp.exp(s_c - m)
    p_n = jnp.exp(s_n - m)
    inv = 1.0 / (jnp.sum(p_c, axis=-1, keepdims=True) + jnp.sum(p_n, axis=-1, keepdims=True)
                 + jnp.exp(sink_col - m))
    pc_scr[...] = (p_c * inv).astype(BF16)
    pn_scr[...] = (p_n * inv).astype(BF16)
    for p in range(half):
        vt2 = jnp.concatenate([cv_ref[p], cv_ref[p + half]], axis=1).astype(BF16)
        vn8 = ab_scr[3, pair_rows(p), :].astype(BF16)
        o_all = (_nt_dot(pc_scr[n_rows * p:n_rows * (p + 1), :], vt2)
                 + _dot(pn_scr[n_rows * p:n_rows * (p + 1), :], vn8))
        o = [o_all[h * PAIR_ROWS:(h + 1) * PAIR_ROWS, :] for h in range(4)]
        oa_scr[0, pair_rows(p), :] = jnp.where(lo, o[0], pltpu.roll(o[1], HEAD_DIM, axis=1))
        oa_scr[1, pair_rows(p), :] = jnp.where(lo, pltpu.roll(o[2], HEAD_DIM, axis=1), o[3])
    _scatter_block(mix_scr, oa_scr, 2, gi, sb, nb)
    lane_w = lax.broadcasted_iota(jnp.int32, (LANES, LANES), 1)
    for b in range(sb):
        seq = gi * sb + b
        for old_ref, out_ref, r0 in ((ck_ref, knew_ref, 0), (cv_ref, vnew_ref, LANES)):
            cols = pltpu.roll(old_ref[b], WINDOW - DEC_SEQ, axis=1)
            for t in range(DEC_SEQ):
                shift = jnp.bitwise_and(WINDOW - DEC_SEQ + t - seq, LANES - 1)
                new_t = pltpu.roll(kvt_scr[r0:r0 + LANES, t * nb:(t + 1) * nb], shift, axis=1)
                cols = jnp.where(lane_w == WINDOW - DEC_SEQ + t, new_t, cols)
            out_ref[b] = cols

    head = gi // 2
    k_base = head * HEAD_DIM + (gi % 2) * K_PER_STEP
    v_rows = pl.ds(pl.multiple_of(head * HEAD_DIM, HEAD_DIM), HEAD_DIM)

    def k_body(k8, accs):
        rows8 = pl.ds(pl.multiple_of(k_base + k8 * SUBLANES, SUBLANES), SUBLANES)
        accs = list(accs)
        for j in range(SUBLANES):
            st_rows = pl.ds(pl.multiple_of((k8 * SUBLANES + j) * HEAD_DIM, HEAD_DIM), HEAD_DIM)
            st = s_ref[st_rows, :]
            for t in range(DEC_SEQ):
                cols = slice(t * nb, (t + 1) * nb)
                f8, k8v, q8 = f_scr[rows8, cols], k_scr[rows8, cols], q_scr[rows8, cols]
                st = f8[j:j + 1, :] * st + k8v[j:j + 1, :] * v_scr[v_rows, cols]
                accs[t] = accs[t] + q8[j:j + 1, :] * st
            snew_ref[st_rows, :] = st
        return tuple(accs)

    zero_acc = jnp.zeros((HEAD_DIM, nb), F32)
    accs = lax.fori_loop(0, K_PER_STEP // SUBLANES, k_body, (zero_acc,) * DEC_SEQ)
    for t in range(DEC_SEQ):
        cols = slice(t * nb, (t + 1) * nb)
        o_scr[v_rows, cols] = o_scr[v_rows, cols] + accs[t]

    @pl.when(gi == pl.num_programs(0) - 1)
    def _():
        o = o_scr[...]
        parts = []
        for h in range(N_HG):
            oh = o[h * HEAD_DIM:(h + 1) * HEAD_DIM, :]
            parts.append(oh * lax.rsqrt(jnp.mean(oh * oh, axis=0, keepdims=True) + RMS_EPS))
        ob = jnp.concatenate(parts, axis=0) * ngc_ref[...] * gate_scr[...]
        y = (_dot(mix_scr[:, 0:W_GRP].astype(BF16), w_o_ref[0:W_GRP, :])
             + _tn_dot(ob.astype(BF16), w_o_ref[W_GRP:2 * W_GRP, :])
             + _dot(mix_scr[:, 2 * W_GRP:4 * W_GRP].astype(BF16), w_o_ref[2 * W_GRP:4 * W_GRP, :]))
        y_ref[...] = _layer_norm(ALPHA * xt_ref[...] + y, g_ref[...], b_ref[...])


def _sample_mixer3_call(l, xt, ck, cv, s, cs_tm, pb_tm, w, prev):
    m = xt.shape[0]
    nb = m // DEC_SEQ
    sb = SEQ_BLOCK
    n_steps = nb // sb
    n_sc = 4 * PAIR_ROWS * (sb // 2)
    s_rows = HG_ROWS // n_steps
    single = pl.Buffered(1)
    const2 = lambda i: (0, 0)
    out_shape = (
        jax.ShapeDtypeStruct((m, D_MODEL), F32),
        jax.ShapeDtypeStruct((DEPTH, nb, LANES, WINDOW), F32),
        jax.ShapeDtypeStruct((DEPTH, nb, LANES, WINDOW), F32),
        jax.ShapeDtypeStruct((DEPTH, HG_ROWS, nb), F32),
        jax.ShapeDtypeStruct((CONV_W - 1, nb, W_GRP), F32),
        jax.ShapeDtypeStruct((POOL_BUF, nb, W_GRP), F32),
    )
    seq_blk = lambda i: (l, i, 0, 0)
    in_specs = [pl.BlockSpec((m, D_MODEL), const2, pipeline_mode=single),
                pl.BlockSpec((None, sb, LANES, WINDOW), seq_blk),
                pl.BlockSpec((None, sb, LANES, WINDOW), seq_blk),
                pl.BlockSpec((None, s_rows, nb), lambda i: (l, i, 0)),
                pl.BlockSpec((None, CONV_W - 1, nb, W_GRP), lambda i: (l, 0, 0, 0), pipeline_mode=single),
                pl.BlockSpec((None, POOL_BUF, nb, W_GRP), lambda i: (l, 0, 0, 0), pipeline_mode=single),
                pl.BlockSpec((D_MODEL, D_IN), const2, pipeline_mode=single),
                pl.BlockSpec((4 * W_GRP, D_MODEL), const2, pipeline_mode=single),
                pl.BlockSpec((2 * LANES, D_MODEL), const2, pipeline_mode=single),
                pl.BlockSpec(memory_space=pltpu.SMEM),
                _full_spec((W_GRP, 1)), _full_spec((W_GRP, 1)),
                _full_spec((CONV_W, W_GRP)), _full_spec((W_GRP, W_GRP)), _full_spec((1, W_GRP)),
                pl.BlockSpec((D_MODEL, D_MODEL), const2, pipeline_mode=single),
                _full_spec((1, D_MODEL)), _full_spec((1, D_MODEL))]
    args = [xt, ck, cv, s, cs_tm, pb_tm, w["w_in"], w["w_hgt"], w["w_kvt"], w["sink"], w["lb_col"], w["ng_col"],
            w["cw"], w["pw"], w["ps"], w["w_o"], w["ln1_g"], w["ln1_b"]]
    aliases = {}
    if prev is not None:
        for j, buf in enumerate(prev):
            aliases[len(args)] = 1 + j
            args.append(buf)
            in_specs.append(pl.BlockSpec(memory_space=pl.ANY))
    f32 = lambda *shape: pltpu.VMEM(shape, F32)
    return pl.pallas_call(
        _sample_mixer3_kernel,
        name="mixer_sample",
        grid=(n_steps,),
        in_specs=in_specs,
        out_specs=(pl.BlockSpec((m, D_MODEL), const2),
                   pl.BlockSpec((None, sb, LANES, WINDOW), seq_blk),
                   pl.BlockSpec((None, sb, LANES, WINDOW), seq_blk),
                   pl.BlockSpec((None, s_rows, nb), lambda i: (l, i, 0)),
                   pl.BlockSpec((CONV_W - 1, nb, W_GRP), lambda i: (0, 0, 0)),
                   pl.BlockSpec((POOL_BUF, nb, W_GRP), lambda i: (0, 0, 0))),
        out_shape=out_shape,
        input_output_aliases=aliases,
        scratch_shapes=[f32(m, 4 * W_GRP),
                        f32(4, m, LANES),
                        f32(2 * LANES, m),
                        f32(W_GRP, m), f32(W_GRP, m), f32(W_GRP, m), f32(W_GRP, m), f32(W_GRP, m),
                        f32(W_GRP, m),
                        f32(m, D_MODEL),
                        f32(4, sb * DEC_SEQ, LANES), f32(2, sb * DEC_SEQ, LANES),
                        f32(n_sc, 2 * WINDOW), f32(n_sc, PAIR_ROWS),
                        pltpu.VMEM((n_sc, 2 * WINDOW), BF16), pltpu.VMEM((n_sc, PAIR_ROWS), BF16)],
        compiler_params=_compiler_params(("arbitrary",)),
    )(*args)


def _xattn_sample3_kernel(xt_ref, mk_ref, mv_ref, wq_ref, wo_ref, g_ref, b_ref, o_ref,
                          q_scr, qb_scr, ob_scr, att_scr, s_scr, p_scr):
    gi = pl.program_id(0)
    sb = mk_ref.shape[0]
    nb = xt_ref.shape[0] // DEC_SEQ
    half = sb // 2
    n_rows = N_XH * PAIR_ROWS

    @pl.when(gi == 0)
    def _():
        q = _dot(xt_ref[...].astype(BF16), wq_ref[...]) * (HEAD_DIM ** -0.5)
        q_scr[0] = q[:, 0:LANES]
        q_scr[1] = q[:, LANES:2 * LANES]

    _gather_block(qb_scr, q_scr, 2, gi, sb, nb)
    lane_head = lax.broadcasted_iota(jnp.int32, (PAIR_ROWS, D_X), 1) >> 6
    r_s = lax.broadcasted_iota(jnp.int32, (n_rows, 2 * N_MEM), 0)
    c_s = lax.broadcasted_iota(jnp.int32, (n_rows, 2 * N_MEM), 1)
    own = (r_s & 1) == (c_s >> 8)
    pair_rows = lambda p: pl.ds(p, PAIR_ROWS, stride=half)
    for p in range(half):
        q8 = jnp.concatenate([qb_scr[0, pair_rows(p), :], qb_scr[1, pair_rows(p), :]], axis=1)
        zero = jnp.zeros_like(q8)
        q4 = jnp.concatenate([jnp.where(lane_head == h, q8, zero) for h in range(N_XH)], axis=0).astype(BF16)
        kt2 = jnp.concatenate([mk_ref[p], mk_ref[p + half]], axis=1).astype(BF16)
        s_scr[n_rows * p:n_rows * (p + 1), :] = jnp.where(own, _dot(q4, kt2), MASK_VALUE)
    s_all = s_scr[...]
    m = jnp.max(s_all, axis=-1, keepdims=True)
    pr = jnp.exp(s_all - m)
    p_scr[...] = (pr * (1.0 / jnp.sum(pr, axis=-1, keepdims=True))).astype(BF16)
    for p in range(half):
        vt2 = jnp.concatenate([mv_ref[p], mv_ref[p + half]], axis=1).astype(BF16)
        o_all = _nt_dot(p_scr[n_rows * p:n_rows * (p + 1), :], vt2)
        o8 = jnp.zeros((PAIR_ROWS, D_X), F32)
        for h in range(N_XH):
            o8 = jnp.where(lane_head == h, o_all[h * PAIR_ROWS:(h + 1) * PAIR_ROWS, :], o8)
        ob_scr[0, pair_rows(p), :] = o8[:, 0:LANES]
        ob_scr[1, pair_rows(p), :] = o8[:, LANES:2 * LANES]
    _scatter_block(att_scr, ob_scr, 2, gi, sb, nb)

    @pl.when(gi == pl.num_programs(0) - 1)
    def _():
        y = _dot(att_scr[...].astype(BF16), wo_ref[...])
        o_ref[...] = _layer_norm(ALPHA * xt_ref[...] + y, g_ref[...], b_ref[...])


def _xattn_sample3_call(l, xt, mk, mv, w):
    m = xt.shape[0]
    nb = m // DEC_SEQ
    sb = SEQ_BLOCK
    n_sc = N_XH * PAIR_ROWS * (sb // 2)
    const2 = lambda i: (0, 0)
    return pl.pallas_call(
        _xattn_sample3_kernel,
        name="xattn_sample",
        grid=(nb // sb,),
        in_specs=[pl.BlockSpec((m, D_MODEL), const2, pipeline_mode=pl.Buffered(1)),
                  pl.BlockSpec((None, sb, D_X, N_MEM), lambda i: (l, i, 0, 0)),
                  pl.BlockSpec((None, sb, D_X, N_MEM), lambda i: (l, i, 0, 0)),
                  _full_spec((D_MODEL, D_X)), _full_spec((D_X, D_MODEL)),
                  _full_spec((1, D_MODEL)), _full_spec((1, D_MODEL))],
        out_specs=pl.BlockSpec((m, D_MODEL), const2),
        out_shape=jax.ShapeDtypeStruct((m, D_MODEL), F32),
        scratch_shapes=[pltpu.VMEM((2, m, LANES), F32),
                        pltpu.VMEM((2, sb * DEC_SEQ, LANES), F32),
                        pltpu.VMEM((2, sb * DEC_SEQ, LANES), F32),
                        pltpu.VMEM((m, D_X), F32),
                        pltpu.VMEM((n_sc, 2 * N_MEM), F32),
                        pltpu.VMEM((n_sc, 2 * N_MEM), BF16)],
        compiler_params=_compiler_params(("arbitrary",)),
    )(xt, mk, mv, w["w_xq"], w["w_xo"], w["ln2_g"], w["ln2_b"])


def _sample_layer(l, xt, ck, cv, s, cs_tm, pb_tm, mk, mv, w, prev):
    xt, kn, vn, sn, cn, pn = _sample_mixer3_call(l, xt, ck, cv, s, cs_tm, pb_tm, w, prev)
    xt = _xattn_sample3_call(l, xt, mk, mv, w)
    xt = _ffn_call(xt, w["w_gate"], w["w_up"], w["w_down"], w["ln3_g"], w["ln3_b"])
    return xt, kn, vn, sn, cn, pn


def _row(v):
    return v.reshape(1, -1).astype(F32)


def _pool_block_diag(pool_w):
    z = jnp.zeros((W_GRP, W_GRP), pool_w.dtype)
    for gi in range(4):
        z = lax.dynamic_update_slice(z, pool_w[gi], (gi * HEAD_DIM, gi * HEAD_DIM))
    return z


def _hgrn_lower_bounds(lb_param):
    p = jax.nn.softmax(lb_param.astype(F32), axis=0)
    return jnp.cumsum(p, axis=0) - p[0:1]


def _prompt_layer(x, mem2d, w):
    bsz = x.shape[0]
    mk = _matmul_call(mem2d, w["w_xk"]).reshape(bsz, N_MEM, D_X)
    mv = _matmul_call(mem2d, w["w_xv"]).reshape(bsz, N_MEM, D_X)
    x, kn, vn, sn, cn, pn = _prompt_mixer_call(
        x, w["w_in"], w["sink"], w["lb"], w["ng"], w["cw"], w["pw"], w["ps"], w["w_o"], w["ln1_g"], w["ln1_b"])
    x = _xattn_prompt_call(x, mk, mv, w["w_xq"], w["w_xo"], w["ln2_g"], w["ln2_b"])
    t = x.shape[1]
    x = _ffn_call(x.reshape(bsz * t, D_MODEL), w["w_gate"], w["w_up"], w["w_down"],
                  w["ln3_g"], w["ln3_b"]).reshape(bsz, t, D_MODEL)
    return x, kn, vn, sn, cn, pn, mk, mv


def _layer_weights(l, lb_all, w_in, attn_sink, hgrn_norm_g, conv_w, pool_w, pool_scale, w_o, ln1_g, ln1_b,
                   w_xq, w_xk, w_xv, w_xo, ln2_g, ln2_b, w_gate, w_up, w_down, ln3_g, ln3_b):
    bf = lambda a: a[l].astype(BF16)
    return dict(
        w_in=bf(w_in), w_hgt=w_in[l][:, C_BQ:C_CB].T.astype(BF16), w_kvt=w_in[l][:, C_AK:C_BQ].T.astype(BF16),
        sink=attn_sink[l].astype(F32), lb=_row(lb_all[l]), lb_col=lb_all[l].reshape(-1, 1).astype(F32),
        ng=_row(hgrn_norm_g[l]), ng_col=hgrn_norm_g[l].reshape(-1, 1).astype(F32),
        cw=conv_w[l].astype(F32), pw=_pool_block_diag(pool_w[l]).astype(BF16), ps=_row(pool_scale[l]),
        w_o=bf(w_o), ln1_g=_row(ln1_g[l]), ln1_b=_row(ln1_b[l]),
        w_xq=bf(w_xq), w_xk=bf(w_xk), w_xv=bf(w_xv), w_xo=bf(w_xo), ln2_g=_row(ln2_g[l]), ln2_b=_row(ln2_b[l]),
        w_gate=bf(w_gate), w_up=bf(w_up), w_down=bf(w_down), ln3_g=_row(ln3_g[l]), ln3_b=_row(ln3_b[l]))


def kernel(x_prompt, x_sample, cache_swa_k, cache_swa_v, state_hgrn, state_conv, state_pool, cache_mem_k,
           cache_mem_v, mem_prompt, emb_ln_g, emb_ln_b, w_in, attn_sink, hgrn_lb, hgrn_norm_g, conv_w, pool_w,
           pool_scale, w_o, ln1_g, ln1_b, w_xq, w_xk, w_xv, w_xo, ln2_g, ln2_b, w_gate, w_up, w_down, ln3_g,
           ln3_b):
    bp, t, _ = x_prompt.shape
    lb_all = _hgrn_lower_bounds(hgrn_lb)
    hp = _ln_call(x_prompt.reshape(bp * t, D_MODEL), _row(emb_ln_g), _row(emb_ln_b)).reshape(bp, t, D_MODEL)
    mem2d = mem_prompt.reshape(bp * N_MEM, D_MODEL)
    bs, ts, _ = x_sample.shape
    hs = _ln_call(x_sample.transpose(1, 0, 2).reshape(ts * bs, D_MODEL), _row(emb_ln_g), _row(emb_ln_b))
    ck = cache_swa_k.transpose(0, 1, 3, 4, 2).reshape(DEPTH, bs, LANES, WINDOW)
    cv = cache_swa_v.transpose(0, 1, 3, 4, 2).reshape(DEPTH, bs, LANES, WINDOW)
    st = state_hgrn.transpose(0, 2, 3, 4, 1).reshape(DEPTH, HG_ROWS, bs)
    cs_tm = state_conv.transpose(0, 2, 1, 3)
    pb_tm = state_pool.transpose(0, 2, 1, 3)
    mk_s = cache_mem_k.transpose(0, 1, 3, 4, 2).reshape(DEPTH, bs, D_X, N_MEM)
    mv_s = cache_mem_v.transpose(0, 1, 3, 4, 2).reshape(DEPTH, bs, D_X, N_MEM)
    outs = [[] for _ in range(7)]
    souts = [[] for _ in range(2)]
    prev = None
    for l in range(DEPTH):
        w = _layer_weights(l, lb_all, w_in, attn_sink, hgrn_norm_g, conv_w, pool_w, pool_scale, w_o, ln1_g,
                           ln1_b, w_xq, w_xk, w_xv, w_xo, ln2_g, ln2_b, w_gate, w_up, w_down, ln3_g, ln3_b)
        res = _prompt_layer(hp, mem2d, w)
        hp = res[0]
        for acc, r in zip(outs, res[1:]):
            acc.append(r)
        sres = _sample_layer(l, hs, ck, cv, st, cs_tm, pb_tm, mk_s, mv_s, w, prev)
        hs = sres[0]
        prev = sres[1:4]
        for acc, r in zip(souts, sres[4:]):
            acc.append(r)
    pk, pv, ps, pc, pp, pmk, pmv = [jnp.stack(o) for o in outs]
    sk, sv, ss = prev
    sc, sp = [jnp.stack(o) for o in souts]
    swa_out = lambda a: a.reshape(DEPTH, bs, N_KV, HEAD_DIM, WINDOW).transpose(0, 1, 4, 2, 3)
    return (hp, hs.reshape(ts, bs, D_MODEL).transpose(1, 0, 2),
            pk.reshape(DEPTH, bp, WINDOW, N_KV, HEAD_DIM), pv.reshape(DEPTH, bp, WINDOW, N_KV, HEAD_DIM),
            ps.reshape(DEPTH, bp, N_HG, HEAD_DIM, HEAD_DIM), pc, pp,
            pmk.reshape(DEPTH, bp, N_MEM, N_XH, HEAD_DIM), pmv.reshape(DEPTH, bp, N_MEM, N_XH, HEAD_DIM),
            swa_out(sk), swa_out(sv),
            ss.reshape(DEPTH, N_HG, HEAD_DIM, HEAD_DIM, bs).transpose(0, 4, 1, 2, 3),
            sc.transpose(0, 2, 1, 3), sp.transpose(0, 2, 1, 3))
```

```python
import functools

import jax
import jax.numpy as jnp
from jax import lax
from jax.experimental import pallas as pl
from jax.experimental.pallas import tpu as pltpu

F32 = jnp.float32
BF16 = jnp.bfloat16

D_MODEL = 1024
DEPTH = 4
HEAD_DIM = 64
W_GRP = 256
N_KV = 2
WINDOW = 128
N_HG = 4
CONV_W = 3
POOL_WINDOWS = (2, 4, 8, 16)
POOL_BUF = 15
N_MEM = 256
N_XH = 4
D_X = 256
D_FF = 2816
D_IN = 2560
ALPHA = (2 * DEPTH) ** 0.25
LN_EPS = 1e-5
RMS_EPS = 1e-6
MASK_VALUE = -1e30
LB_FLOOR = 1e-30
PAST_LEN = 8192

C_AQ, C_AK, C_AV = 0, 256, 384
C_BQ, C_BF, C_BI, C_BG = 512, 768, 1024, 1280
C_CB, C_CC, C_CH = 1536, 1792, 2048
C_DV = 2304

LANES = 128
SUBLANES = 8
VMEM_LIMIT_BYTES = 56 * 1024 * 1024

TOKEN_BLOCK = 512
HG_CHUNK = 64
HG_MID = HG_CHUNK // 2 - 1


def _nt_dot(a, b):
    return lax.dot_general(a, b, (((1,), (1,)), ((), ())), preferred_element_type=F32)


def _tn_dot(a, b):
    return lax.dot_general(a, b, (((0,), (0,)), ((), ())), preferred_element_type=F32)


def _dot(a, b):
    return jnp.dot(a, b, preferred_element_type=F32)


def _layer_norm(x, g, b):
    mu = jnp.mean(x, axis=-1, keepdims=True)
    xc = x - mu
    var = jnp.mean(xc * xc, axis=-1, keepdims=True)
    return xc * lax.rsqrt(var + LN_EPS) * g + b


def _sigmoid_pair(z):
    e = jnp.exp(-jnp.abs(z))
    inv = 1.0 / (1.0 + e)
    small = e * inv
    pos = z >= 0
    return jnp.where(pos, inv, small), jnp.where(pos, small, inv)


def _silu(z):
    s, _ = _sigmoid_pair(z)
    return z * s


def _silu_tanh(z):
    return z * (0.5 + 0.5 * jnp.tanh(0.5 * z))


def _full_spec(shape):
    nd = len(shape)
    return pl.BlockSpec(shape, lambda *_: (0,) * nd)


def _compiler_params(sem, flags=None):
    return pltpu.CompilerParams(dimension_semantics=sem, vmem_limit_bytes=VMEM_LIMIT_BYTES, flags=flags)


def _ln_kernel(x_ref, g_ref, b_ref, o_ref):
    o_ref[...] = _layer_norm(x_ref[...], g_ref[...], b_ref[...])


def _ln_call(x2d, g, b):
    m = x2d.shape[0]
    return pl.pallas_call(
        _ln_kernel,
        name="input_ln",
        grid=(m // TOKEN_BLOCK,),
        in_specs=[pl.BlockSpec((TOKEN_BLOCK, D_MODEL), lambda i: (i, 0)),
                  _full_spec((1, D_MODEL)), _full_spec((1, D_MODEL))],
        out_specs=pl.BlockSpec((TOKEN_BLOCK, D_MODEL), lambda i: (i, 0)),
        out_shape=jax.ShapeDtypeStruct((m, D_MODEL), F32),
        compiler_params=_compiler_params(("arbitrary",)),
    )(x2d, g, b)


def _matmul_kernel(x_ref, w_ref, o_ref):
    o_ref[...] = _dot(x_ref[...].astype(BF16), w_ref[...])


def _matmul_call(x2d, w):
    m, k = x2d.shape
    n = w.shape[1]
    return pl.pallas_call(
        _matmul_kernel,
        name="mem_proj",
        grid=(m // TOKEN_BLOCK,),
        in_specs=[pl.BlockSpec((TOKEN_BLOCK, k), lambda i: (i, 0)), _full_spec((k, n))],
        out_specs=pl.BlockSpec((TOKEN_BLOCK, n), lambda i: (i, 0)),
        out_shape=jax.ShapeDtypeStruct((m, n), F32),
        compiler_params=_compiler_params(("arbitrary",)),
    )(x2d, w)


def _ffn_kernel(x_ref, wg_ref, wu_ref, wd_ref, g_ref, b_ref, o_ref):
    x = x_ref[...]
    xb = x.astype(BF16)
    h = _silu(_dot(xb, wg_ref[...])) * _dot(xb, wu_ref[...])
    y = _dot(h.astype(BF16), wd_ref[...])
    o_ref[...] = _layer_norm(ALPHA * x + y, g_ref[...], b_ref[...])


def _ffn_call(x2d, wg, wu, wd, g, b):
    m = x2d.shape[0]
    single = pl.Buffered(1)
    return pl.pallas_call(
        _ffn_kernel,
        name="ffn",
        grid=(m // TOKEN_BLOCK,),
        in_specs=[pl.BlockSpec((TOKEN_BLOCK, D_MODEL), lambda i: (i, 0)),
                  pl.BlockSpec((D_MODEL, D_FF), lambda i: (0, 0), pipeline_mode=single),
                  pl.BlockSpec((D_MODEL, D_FF), lambda i: (0, 0), pipeline_mode=single),
                  pl.BlockSpec((D_FF, D_MODEL), lambda i: (0, 0), pipeline_mode=single),
                  _full_spec((1, D_MODEL)), _full_spec((1, D_MODEL))],
        out_specs=pl.BlockSpec((TOKEN_BLOCK, D_MODEL), lambda i: (i, 0)),
        out_shape=jax.ShapeDtypeStruct((m, D_MODEL), F32),
        compiler_params=_compiler_params(("arbitrary",)),
    )(x2d, wg, wu, wd, g, b)


def _xattn_prompt_kernel(x_ref, mk_ref, mv_ref, wq_ref, wo_ref, g_ref, b_ref, o_ref):
    x = x_ref[...]
    q = _dot(x.astype(BF16), wq_ref[...]) * (HEAD_DIM ** -0.5)
    mk = mk_ref[...].astype(BF16)
    mv = mv_ref[...].astype(BF16)
    lane_head = lax.broadcasted_iota(jnp.int32, q.shape, 1) >> 6
    o = jnp.zeros_like(q)
    for h in range(N_XH):
        qh = jnp.where(lane_head == h, q, 0.0).astype(BF16)
        s = _nt_dot(mk, qh)
        m = jnp.max(s, axis=0, keepdims=True)
        p = jnp.exp(s - m)
        den = jnp.sum(p, axis=0, keepdims=True)
        p = (p * (1.0 / den)).astype(BF16)
        o = jnp.where(lane_head == h, _tn_dot(p, mv), o)
    y = _dot(o.astype(BF16), wo_ref[...])
    o_ref[...] = _layer_norm(ALPHA * x + y, g_ref[...], b_ref[...])


def _xattn_prompt_call(x, mk, mv, wq, wo, g, b):
    bsz, t, _ = x.shape
    tb = min(TOKEN_BLOCK, t)
    return pl.pallas_call(
        _xattn_prompt_kernel,
        name="xattn_prompt",
        grid=(bsz, t // tb),
        in_specs=[pl.BlockSpec((None, tb, D_MODEL), lambda i, j: (i, j, 0)),
                  pl.BlockSpec((None, N_MEM, D_X), lambda i, j: (i, 0, 0)),
                  pl.BlockSpec((None, N_MEM, D_X), lambda i, j: (i, 0, 0)),
                  _full_spec((D_MODEL, D_X)), _full_spec((D_X, D_MODEL)),
                  _full_spec((1, D_MODEL)), _full_spec((1, D_MODEL))],
        out_specs=pl.BlockSpec((None, tb, D_MODEL), lambda i, j: (i, j, 0)),
        out_shape=jax.ShapeDtypeStruct((bsz, t, D_MODEL), F32),
        compiler_params=_compiler_params(("arbitrary", "arbitrary")),
    )(x, mk, mv, wq, wo, g, b)


def _swa_bias_table():
    c = lax.broadcasted_iota(jnp.int32, (2 * WINDOW, 4 * WINDOW), 0)
    r = lax.broadcasted_iota(jnp.int32, (2 * WINDOW, 4 * WINDOW), 1)
    head = r >> 7
    rel = (r & (WINDOW - 1)) + WINDOW - c
    slope = jnp.exp2(-2.0 * (head.astype(F32) + 1.0))
    valid = (rel >= 0) & (rel <= WINDOW)
    return jnp.where(valid, -slope * rel.astype(F32), MASK_VALUE)


def _prompt_mixer_kernel(x_ref, w_in_ref, sink_ref, lb_ref, ng_ref, cw_ref, pw_ref, ps_ref, w_o_ref,
                         g_ref, b_ref,
                         y_ref, knew_ref, vnew_ref, snew_ref, cnew_ref, pnew_ref,
                         proj_scr, kext_scr, vext_scr, st_scr, u_scr, p_scr, bias_scr, mix_scr, hg_scr,
                         hq_scr, hk_scr, hv_scr, ghl_scr, cum_scr, qp_scr, kp_scr, qs_scr, ks_scr, dec_scr, inc_scr,
                         stb_scr, a_scr):
    tb = x_ref.shape[0]
    n_qb = tb // WINDOW
    n_ch = tb // HG_CHUNK
    bi = pl.program_id(0)
    ti = pl.program_id(1)
    last = ti == pl.num_programs(1) - 1

    @pl.when((bi == 0) & (ti == 0))
    def _():
        bias_scr[...] = _swa_bias_table()

    @pl.when(ti == 0)
    def _():
        kext_scr[0:WINDOW, :] = jnp.zeros((WINDOW, LANES), BF16)
        vext_scr[0:WINDOW, :] = jnp.zeros((WINDOW, LANES), BF16)
        st_scr[...] = jnp.zeros(st_scr.shape, F32)
        u_scr[0:SUBLANES, :] = jnp.zeros((SUBLANES, W_GRP), F32)
        p_scr[0:16, :] = jnp.zeros((16, W_GRP), F32)

    x = x_ref[...]
    xb = x.astype(BF16)
    proj_scr[:, 0:C_BQ] = _dot(xb, w_in_ref[:, 0:C_BQ])
    proj_scr[:, C_BQ:C_CB] = _dot(xb, w_in_ref[:, C_BQ:C_CB])

    kext_scr[WINDOW:WINDOW + tb, :] = proj_scr[:, C_AK:C_AK + LANES].astype(BF16)
    vext_scr[WINDOW:WINDOW + tb, :] = proj_scr[:, C_AV:C_AV + LANES].astype(BF16)
    lane = lax.broadcasted_iota(jnp.int32, (WINDOW, LANES), 1)
    lo = lane < HEAD_DIM
    key_row = lax.broadcasted_iota(jnp.int32, (2 * WINDOW, 4 * WINDOW), 0)
    head_lane = lax.broadcasted_iota(jnp.int32, (1, 4 * WINDOW), 1) >> 7
    sink_row = jnp.where(head_lane == 0, sink_ref[0],
                         jnp.where(head_lane == 1, sink_ref[1],
                                   jnp.where(head_lane == 2, sink_ref[2], sink_ref[3])))
    for j in range(n_qb):
        rows = slice(j * WINDOW, (j + 1) * WINDOW)
        q0 = proj_scr[rows, 0:LANES] * (HEAD_DIM ** -0.5)
        q1 = proj_scr[rows, LANES:2 * LANES] * (HEAD_DIM ** -0.5)
        q0r = pltpu.roll(q0, HEAD_DIM, axis=1)
        q1r = pltpu.roll(q1, HEAD_DIM, axis=1)
        zero = jnp.zeros_like(q0)
        q4 = jnp.concatenate([jnp.where(lo, q0, zero), jnp.where(lo, q0r, zero),
                              jnp.where(lo, zero, q1r), jnp.where(lo, zero, q1)], axis=0).astype(BF16)
        kj = kext_scr[j * WINDOW:(j + 2) * WINDOW, :]
        vj = vext_scr[j * WINDOW:(j + 2) * WINDOW, :]
        s = _nt_dot(kj, q4) + bias_scr[...]
        if j == 0:
            s = jnp.where((ti == 0) & (key_row < WINDOW), MASK_VALUE, s)
        m = jnp.maximum(jnp.max(s, axis=0, keepdims=True), sink_row)
        p = jnp.exp(s - m)
        den = jnp.sum(p, axis=0, keepdims=True) + jnp.exp(sink_row - m)
        p = (p * (1.0 / den)).astype(BF16)
        o_all = _tn_dot(p, vj)
        o = [o_all[h * WINDOW:(h + 1) * WINDOW, :] for h in range(4)]
        c0 = jnp.where(lo, o[0], pltpu.roll(o[1], HEAD_DIM, axis=1))
        c1 = jnp.where(lo, pltpu.roll(o[2], HEAD_DIM, axis=1), o[3])
        mix_scr[rows, 0:LANES] = c0.astype(BF16)
        mix_scr[rows, LANES:2 * LANES] = c1.astype(BF16)
    kext_scr[0:WINDOW, :] = kext_scr[tb:tb + WINDOW, :]
    vext_scr[0:WINDOW, :] = vext_scr[tb:tb + WINDOW, :]
    proj_scr[:, C_CB:D_IN] = _dot(xb, w_in_ref[:, C_CB:D_IN])

    @pl.when(last)
    def _():
        knew_ref[...] = proj_scr[tb - WINDOW:tb, C_AK:C_AK + LANES]
        vnew_ref[...] = proj_scr[tb - WINDOW:tb, C_AV:C_AV + LANES]

    lb = lb_ref[...]
    lbf = jnp.maximum(lb, LB_FLOOR)
    one_m_lb = 1.0 - lb
    ng = ng_ref[...]
    r256 = lax.broadcasted_iota(jnp.int32, (W_GRP, W_GRP), 0)
    c256 = lax.broadcasted_iota(jnp.int32, (W_GRP, W_GRP), 1)
    same_head = (r256 >> 6) == (c256 >> 6)
    head_ones = jnp.where(same_head, 1.0, 0.0).astype(BF16)
    same_head_b = head_ones > 0
    zero_b = jnp.zeros((W_GRP, W_GRP), BF16)
    rc = lax.broadcasted_iota(jnp.int32, (HG_CHUNK, W_GRP), 0)
    cc = lax.broadcasted_iota(jnp.int32, (HG_CHUNK, W_GRP), 1)
    causal = (cc & (HG_CHUNK - 1)) <= rc
    r64 = lax.broadcasted_iota(jnp.int32, (HG_CHUNK, HG_CHUNK), 0)
    c64 = lax.broadcasted_iota(jnp.int32, (HG_CHUNK, HG_CHUNK), 1)
    tril = jnp.where(c64 <= r64, 1.0, 0.0).astype(BF16)

    chunk = lambda c: slice(c * HG_CHUNK, (c + 1) * HG_CHUNK)
    hq_scr[...] = _silu_tanh(proj_scr[:, C_BQ:C_BQ + W_GRP])
    sig_pos, sig_neg = _sigmoid_pair(proj_scr[:, C_BF:C_BF + W_GRP])
    g = jnp.log(sig_pos + lbf * sig_neg)
    hk_scr[...] = one_m_lb * sig_neg
    g_hi = g.astype(BF16)
    ghl_scr[:, 0:W_GRP] = g_hi
    ghl_scr[:, W_GRP:2 * W_GRP] = (g - g_hi.astype(F32)).astype(BF16)
    hv_scr[...] = proj_scr[:, C_BI:C_BI + W_GRP].astype(BF16)
    for c in range(n_ch):
        cum2 = _dot(tril, ghl_scr[chunk(c), :])
        cum_scr[chunk(c), :] = cum2[:, 0:W_GRP] + cum2[:, W_GRP:2 * W_GRP]
    for c in range(n_ch):
        cum = cum_scr[chunk(c), :]
        ref = cum[HG_MID:HG_MID + 1, :]
        tot = cum[HG_CHUNK - 1:HG_CHUNK, :]
        qp = hq_scr[chunk(c), :] * jnp.exp(cum - ref)
        kp = hk_scr[chunk(c), :] * jnp.exp(ref - cum)
        qp_scr[chunk(c), :] = qp.astype(BF16)
        kp_scr[chunk(c), :] = kp.astype(BF16)
        qs_scr[chunk(c), :] = (qp * jnp.exp(ref)).astype(BF16)
        ks_scr[chunk(c), :] = (kp * jnp.exp(tot - ref)).astype(BF16)
        dec_scr[c] = jnp.broadcast_to(jnp.exp(tot), (SUBLANES, W_GRP))
    for c in range(n_ch):
        bk = jnp.where(same_head_b, jnp.concatenate([kp_scr[chunk(c), :]] * N_HG, axis=0), zero_b)
        a = jnp.where(causal, _nt_dot(qp_scr[chunk(c), :], bk), 0.0)
        a_scr[chunk(c), :] = a.astype(BF16)
    for c in range(n_ch):
        inc_scr[c] = jnp.where(same_head, _tn_dot(hv_scr[chunk(c), :], ks_scr[chunk(c), :]), 0.0)
    for c in range(n_ch):
        bv = jnp.where(same_head_b, jnp.concatenate([hv_scr[chunk(c), :]] * N_HG, axis=0), zero_b)
        hg_scr[chunk(c), :] = _dot(a_scr[chunk(c), :], bv)
    st = st_scr[...]
    for c in range(n_ch):
        stb_scr[c] = st.astype(BF16)
        st = st * dec_scr[c, 0:1, :] + inc_scr[c]
    st_scr[...] = st
    for c in range(n_ch):
        hg_scr[chunk(c), :] = hg_scr[chunk(c), :] + _nt_dot(qs_scr[chunk(c), :], stb_scr[c])
    o = hg_scr[...]
    ms = _dot((o * o).astype(BF16), head_ones) * (1.0 / HEAD_DIM)
    o = o * lax.rsqrt(ms + RMS_EPS) * ng
    mix_scr[:, W_GRP:2 * W_GRP] = (o * _silu_tanh(proj_scr[:, C_BG:C_BG + W_GRP])).astype(BF16)

    @pl.when(last)
    def _():
        s_t = st_scr[...].T
        for h in range(N_HG):
            snew_ref[h * HEAD_DIM:(h + 1) * HEAD_DIM, :] = (
                s_t[h * HEAD_DIM:(h + 1) * HEAD_DIM, h * HEAD_DIM:(h + 1) * HEAD_DIM])

    u_scr[SUBLANES:SUBLANES + tb, :] = proj_scr[:, C_CC:C_CC + W_GRP] * proj_scr[:, C_CH:C_CH + W_GRP]
    cw = cw_ref[...]
    yc = (u_scr[SUBLANES - 2:SUBLANES - 2 + tb, :] * cw[0:1, :]
          + u_scr[SUBLANES - 1:SUBLANES - 1 + tb, :] * cw[1:2, :]
          + u_scr[SUBLANES:SUBLANES + tb, :] * cw[2:3, :])
    mix_scr[:, 2 * W_GRP:3 * W_GRP] = (proj_scr[:, C_CB:C_CB + W_GRP] * yc).astype(BF16)
    tail = u_scr[tb:tb + SUBLANES, :]
    u_scr[0:SUBLANES, :] = tail

    @pl.when(last)
    def _():
        cnew_ref[...] = tail[SUBLANES - 2:SUBLANES, :]

    dv = proj_scr[:, C_DV:C_DV + W_GRP]
    p_scr[16:16 + tb, :] = dv
    ext = p_scr[...]
    s2 = ext + pltpu.roll(ext, 1, axis=0)
    s4 = s2 + pltpu.roll(s2, 2, axis=0)
    s8 = s4 + pltpu.roll(s4, 4, axis=0)
    s16 = s8 + pltpu.roll(s8, 8, axis=0)
    rp = lax.broadcasted_iota(jnp.int32, (tb, W_GRP), 0)
    cp = lax.broadcasted_iota(jnp.int32, (tb, W_GRP), 1)
    grp = cp >> 6
    win = jnp.where(grp == 0, s2[16:], jnp.where(grp == 1, s4[16:], jnp.where(grp == 2, s8[16:], s16[16:])))
    width = jnp.left_shift(2, grp)
    cnt = jnp.minimum(ti * tb + rp + 1, width).astype(F32)
    pooled = win / cnt - dv
    yd = _dot(pooled.astype(BF16), pw_ref[...]) * ps_ref[...]
    mix_scr[:, 3 * W_GRP:4 * W_GRP] = yd.astype(BF16)
    ptail = p_scr[tb:tb + 16, :]
    p_scr[0:16, :] = ptail

    @pl.when(last)
    def _():
        pnew_ref[...] = ptail[1:16, :]

    hb = tb // 2
    for r in range(2):
        rows = slice(r * hb, (r + 1) * hb)
        y = _dot(mix_scr[rows, :], w_o_ref[...])
        y_ref[rows, :] = _layer_norm(ALPHA * x_ref[rows, :] + y, g_ref[...], b_ref[...])


SUB_BLOCKS = 2


def _prompt_mixer2_kernel(x_ref, w_in_ref, sink_ref, lb_ref, ng_ref, cw_ref, pw_ref, ps_ref, w_o_ref,
                          g_ref, b_ref,
                          y_ref, knew_ref, vnew_ref, snew_ref, cnew_ref, pnew_ref,
                          proj_scr, kext_scr, vext_scr, st_scr, u_scr, p_scr, bias_scr, mix_scr, hg_scr,
                          hq_scr, hk_scr, hv_scr, ghl_scr, cum_scr, qp_scr, kp_scr, qs_scr, ks_scr, dec_scr, inc_scr,
                          stb_scr, a_scr, yo_scr):
    tb = x_ref.shape[0]
    hb = tb // SUB_BLOCKS
    n_qb = hb // WINDOW
    n_ch = hb // HG_CHUNK
    bi = pl.program_id(0)
    ti = pl.program_id(1)
    last = ti == pl.num_programs(1) - 1

    @pl.when((bi == 0) & (ti == 0))
    def _():
        bias_scr[...] = _swa_bias_table()

    @pl.when(ti == 0)
    def _():
        kext_scr[0:WINDOW, :] = jnp.zeros((WINDOW, LANES), BF16)
        vext_scr[0:WINDOW, :] = jnp.zeros((WINDOW, LANES), BF16)
        st_scr[...] = jnp.zeros(st_scr.shape, F32)
        u_scr[0:SUBLANES, :] = jnp.zeros((SUBLANES, W_GRP), F32)
        p_scr[0:16, :] = jnp.zeros((16, W_GRP), F32)

    def project(r, c0, c1):
        rows = slice(r * hb, (r + 1) * hb)
        proj_scr[rows, c0:c1] = _dot(x_ref[rows, :].astype(BF16), w_in_ref[:, c0:c1])

    def finish(r, c0, c1):
        rows = slice(r * hb, (r + 1) * hb)
        yo_scr[rows, c0:c1] = _dot(mix_scr[rows, :], w_o_ref[:, c0:c1])

    proj_cols = ((0, C_BQ), (C_BQ, C_BI), (C_BI, C_CB), (C_CB, C_CH), (C_CH, D_IN))
    out_cols = ((0, 512), (512, D_MODEL))
    for cols in proj_cols:
        project(0, *cols)
    queue = []

    def issue(n=1):
        for _ in range(n):
            if queue:
                fn, args = queue.pop(0)
                fn(*args)

    lane = lax.broadcasted_iota(jnp.int32, (WINDOW, LANES), 1)
    lo = lane < HEAD_DIM
    key_row = lax.broadcasted_iota(jnp.int32, (2 * WINDOW, 4 * WINDOW), 0)
    head_lane = lax.broadcasted_iota(jnp.int32, (1, 4 * WINDOW), 1) >> 7
    sink_row = jnp.where(head_lane == 0, sink_ref[0],
                         jnp.where(head_lane == 1, sink_ref[1],
                                   jnp.where(head_lane == 2, sink_ref[2], sink_ref[3])))
    lb = lb_ref[...]
    lbf = jnp.maximum(lb, LB_FLOOR)
    one_m_lb = 1.0 - lb
    ng = ng_ref[...]
    r256 = lax.broadcasted_iota(jnp.int32, (W_GRP, W_GRP), 0)
    c256 = lax.broadcasted_iota(jnp.int32, (W_GRP, W_GRP), 1)
    same_head = (r256 >> 6) == (c256 >> 6)
    head_ones = jnp.where(same_head, 1.0, 0.0).astype(BF16)
    same_head_b = head_ones > 0
    zero_b = jnp.zeros((W_GRP, W_GRP), BF16)
    rc = lax.broadcasted_iota(jnp.int32, (HG_CHUNK, W_GRP), 0)
    cc = lax.broadcasted_iota(jnp.int32, (HG_CHUNK, W_GRP), 1)
    causal = (cc & (HG_CHUNK - 1)) <= rc
    r64 = lax.broadcasted_iota(jnp.int32, (HG_CHUNK, HG_CHUNK), 0)
    c64 = lax.broadcasted_iota(jnp.int32, (HG_CHUNK, HG_CHUNK), 1)
    tril = jnp.where(c64 <= r64, 1.0, 0.0).astype(BF16)
    cw = cw_ref[...]
    rp = lax.broadcasted_iota(jnp.int32, (hb, W_GRP), 0)
    grp = lax.broadcasted_iota(jnp.int32, (hb, W_GRP), 1) >> 6
    width = jnp.left_shift(2, grp)
    chunk = lambda c: slice(c * HG_CHUNK, (c + 1) * HG_CHUNK)

    def finish_ln(r):
        rows = slice(r * hb, (r + 1) * hb)
        y_ref[rows, :] = _layer_norm(ALPHA * x_ref[rows, :] + yo_scr[rows, :], g_ref[...], b_ref[...])

    for r in range(SUB_BLOCKS):
        r0 = r * hb
        sub = slice(r0, r0 + hb)
        if r >= 1:
            queue.extend((finish, (r - 1, *cols)) for cols in out_cols)
        if r + 1 < SUB_BLOCKS:
            queue.extend((project, (r + 1, *cols)) for cols in proj_cols)

        kext_scr[WINDOW:WINDOW + hb, :] = proj_scr[sub, C_AK:C_AK + LANES].astype(BF16)
        vext_scr[WINDOW:WINDOW + hb, :] = proj_scr[sub, C_AV:C_AV + LANES].astype(BF16)
        for j in range(n_qb):
            rows = slice(r0 + j * WINDOW, r0 + (j + 1) * WINDOW)
            q0 = proj_scr[rows, 0:LANES] * (HEAD_DIM ** -0.5)
            q1 = proj_scr[rows, LANES:2 * LANES] * (HEAD_DIM ** -0.5)
            q0r = pltpu.roll(q0, HEAD_DIM, axis=1)
            q1r = pltpu.roll(q1, HEAD_DIM, axis=1)
            zero = jnp.zeros_like(q0)
            q4 = jnp.concatenate([jnp.where(lo, q0, zero), jnp.where(lo, q0r, zero),
                                  jnp.where(lo, zero, q1r), jnp.where(lo, zero, q1)], axis=0).astype(BF16)
            kj = kext_scr[j * WINDOW:(j + 2) * WINDOW, :]
            vj = vext_scr[j * WINDOW:(j + 2) * WINDOW, :]
            s = _nt_dot(kj, q4) + bias_scr[...]
            if r == 0 and j == 0:
                s = jnp.where((ti == 0) & (key_row < WINDOW), MASK_VALUE, s)
            m = jnp.maximum(jnp.max(s, axis=0, keepdims=True), sink_row)
            p = jnp.exp(s - m)
            den = jnp.sum(p, axis=0, keepdims=True) + jnp.exp(sink_row - m)
            p = (p * (1.0 / den)).astype(BF16)
            o_all = _tn_dot(p, vj)
            o = [o_all[h * WINDOW:(h + 1) * WINDOW, :] for h in range(4)]
            mix_scr[rows, 0:LANES] = jnp.where(lo, o[0], pltpu.roll(o[1], HEAD_DIM, axis=1)).astype(BF16)
            mix_scr[rows, LANES:2 * LANES] = jnp.where(lo, pltpu.roll(o[2], HEAD_DIM, axis=1), o[3]).astype(BF16)
            issue()
        kext_scr[0:WINDOW, :] = kext_scr[hb:hb + WINDOW, :]
        vext_scr[0:WINDOW, :] = vext_scr[hb:hb + WINDOW, :]

        hq_scr[...] = _silu_tanh(proj_scr[sub, C_BQ:C_BQ + W_GRP])
        issue()
        sig_pos, sig_neg = _sigmoid_pair(proj_scr[sub, C_BF:C_BF + W_GRP])
        g = jnp.log(sig_pos + lbf * sig_neg)
        hk_scr[...] = one_m_lb * sig_neg
        g_hi = g.astype(BF16)
        ghl_scr[:, 0:W_GRP] = g_hi
        ghl_scr[:, W_GRP:2 * W_GRP] = (g - g_hi.astype(F32)).astype(BF16)
        hv_scr[...] = proj_scr[sub, C_BI:C_BI + W_GRP].astype(BF16)
        for c in range(n_ch):
            cum2 = _dot(tril, ghl_scr[chunk(c), :])
            cum_scr[chunk(c), :] = cum2[:, 0:W_GRP] + cum2[:, W_GRP:2 * W_GRP]
        for c in range(n_ch):
            cum = cum_scr[chunk(c), :]
            ref = cum[HG_MID:HG_MID + 1, :]
            tot = cum[HG_CHUNK - 1:HG_CHUNK, :]
            qp = hq_scr[chunk(c), :] * jnp.exp(cum - ref)
            kp = hk_scr[chunk(c), :] * jnp.exp(ref - cum)
            qp_scr[chunk(c), :] = qp.astype(BF16)
            kp_scr[chunk(c), :] = kp.astype(BF16)
            qs_scr[chunk(c), :] = (qp * jnp.exp(ref)).astype(BF16)
            ks_scr[chunk(c), :] = (kp * jnp.exp(tot - ref)).astype(BF16)
            dec_scr[c] = jnp.broadcast_to(jnp.exp(tot), (SUBLANES, W_GRP))
        issue()
        for c in range(n_ch):
            bk = jnp.where(same_head_b, jnp.concatenate([kp_scr[chunk(c), :]] * N_HG, axis=0), zero_b)
            a = jnp.where(causal, _nt_dot(qp_scr[chunk(c), :], bk), 0.0)
            a_scr[chunk(c), :] = a.astype(BF16)
        for c in range(n_ch):
            inc_scr[c] = jnp.where(same_head, _tn_dot(hv_scr[chunk(c), :], ks_scr[chunk(c), :]), 0.0)
        for c in range(n_ch):
            bv = jnp.where(same_head_b, jnp.concatenate([hv_scr[chunk(c), :]] * N_HG, axis=0), zero_b)
            hg_scr[chunk(c), :] = _dot(a_scr[chunk(c), :], bv)
        st = st_scr[...]
        for c in range(n_ch):
            stb_scr[c] = st.astype(BF16)
            st = st * dec_scr[c, 0:1, :] + inc_scr[c]
        st_scr[...] = st
        issue()
        for c in range(n_ch):
            hg_scr[chunk(c), :] = hg_scr[chunk(c), :] + _nt_dot(qs_scr[chunk(c), :], stb_scr[c])
        o = hg_scr[...]
        ms = _dot((o * o).astype(BF16), head_ones) * (1.0 / HEAD_DIM)
        o = o * lax.rsqrt(ms + RMS_EPS) * ng
        mix_scr[sub, W_GRP:2 * W_GRP] = (o * _silu_tanh(proj_scr[sub, C_BG:C_BG + W_GRP])).astype(BF16)
        issue()

        u_scr[SUBLANES:SUBLANES + hb, :] = proj_scr[sub, C_CC:C_CC + W_GRP] * proj_scr[sub, C_CH:C_CH + W_GRP]
        yc = (u_scr[SUBLANES - 2:SUBLANES - 2 + hb, :] * cw[0:1, :]
              + u_scr[SUBLANES - 1:SUBLANES - 1 + hb, :] * cw[1:2, :]
              + u_scr[SUBLANES:SUBLANES + hb, :] * cw[2:3, :])
        mix_scr[sub, 2 * W_GRP:3 * W_GRP] = (proj_scr[sub, C_CB:C_CB + W_GRP] * yc).astype(BF16)
        u_scr[0:SUBLANES, :] = u_scr[hb:hb + SUBLANES, :]
        issue()

        dv = proj_scr[sub, C_DV:C_DV + W_GRP]
        p_scr[16:16 + hb, :] = dv
        ext = p_scr[...]
        s2 = ext + pltpu.roll(ext, 1, axis=0)
        s4 = s2 + pltpu.roll(s2, 2, axis=0)
        s8 = s4 + pltpu.roll(s4, 4, axis=0)
        s16 = s8 + pltpu.roll(s8, 8, axis=0)
        win = jnp.where(grp == 0, s2[16:], jnp.where(grp == 1, s4[16:], jnp.where(grp == 2, s8[16:], s16[16:])))
        cnt = jnp.minimum(ti * tb + r0 + rp + 1, width).astype(F32)
        pooled = win / cnt - dv
        yd = _dot(pooled.astype(BF16), pw_ref[...]) * ps_ref[...]
        mix_scr[sub, 3 * W_GRP:4 * W_GRP] = yd.astype(BF16)
        p_scr[0:16, :] = p_scr[hb:hb + 16, :]
        issue(len(queue))
        if r >= 1:
            finish_ln(r - 1)

    for cols in out_cols:
        finish(SUB_BLOCKS - 1, *cols)
    finish_ln(SUB_BLOCKS - 1)

    @pl.when(last)
    def _():
        knew_ref[...] = proj_scr[tb - WINDOW:tb, C_AK:C_AK + LANES]
        vnew_ref[...] = proj_scr[tb - WINDOW:tb, C_AV:C_AV + LANES]
        s_t = st_scr[...].T
        for h in range(N_HG):
            snew_ref[h * HEAD_DIM:(h + 1) * HEAD_DIM, :] = (
                s_t[h * HEAD_DIM:(h + 1) * HEAD_DIM, h * HEAD_DIM:(h + 1) * HEAD_DIM])
        cnew_ref[...] = u_scr[SUBLANES - 2:SUBLANES, :]
        pnew_ref[...] = p_scr[1:16, :]


def _prompt_mixer_call(x, w_in, sink, lb, ng, cw, pw_bd, ps, w_o, g, b):
    bsz, t, _ = x.shape
    tb = min(TOKEN_BLOCK, t)
    hb = tb // SUB_BLOCKS
    row = lambda i, j: (i, 0, 0)
    single = pl.Buffered(1)
    out_shape = (
        jax.ShapeDtypeStruct((bsz, t, D_MODEL), F32),
        jax.ShapeDtypeStruct((bsz, WINDOW, LANES), F32),
        jax.ShapeDtypeStruct((bsz, WINDOW, LANES), F32),
        jax.ShapeDtypeStruct((bsz, W_GRP, HEAD_DIM), F32),
        jax.ShapeDtypeStruct((bsz, CONV_W - 1, W_GRP), F32),
        jax.ShapeDtypeStruct((bsz, POOL_BUF, W_GRP), F32),
    )
    return pl.pallas_call(
        _prompt_mixer2_kernel,
        name="mixer_prompt",
        grid=(bsz, t // tb),
        in_specs=[pl.BlockSpec((None, tb, D_MODEL), lambda i, j: (i, j, 0)),
                  pl.BlockSpec((D_MODEL, D_IN), lambda i, j: (0, 0), pipeline_mode=single),
                  pl.BlockSpec(memory_space=pltpu.SMEM),
                  _full_spec((1, W_GRP)), _full_spec((1, W_GRP)), _full_spec((CONV_W, W_GRP)),
                  _full_spec((W_GRP, W_GRP)), _full_spec((1, W_GRP)),
                  pl.BlockSpec((D_MODEL, D_MODEL), lambda i, j: (0, 0), pipeline_mode=single),
                  _full_spec((1, D_MODEL)), _full_spec((1, D_MODEL))],
        out_specs=(pl.BlockSpec((None, tb, D_MODEL), lambda i, j: (i, j, 0)),
                   pl.BlockSpec((None, WINDOW, LANES), row),
                   pl.BlockSpec((None, WINDOW, LANES), row),
                   pl.BlockSpec((None, W_GRP, HEAD_DIM), row),
                   pl.BlockSpec((None, CONV_W - 1, W_GRP), row),
                   pl.BlockSpec((None, POOL_BUF, W_GRP), row)),
        out_shape=out_shape,
        scratch_shapes=[
            pltpu.VMEM((tb, D_IN), F32),
            pltpu.VMEM((WINDOW + hb, LANES), BF16),
            pltpu.VMEM((WINDOW + hb, LANES), BF16),
            pltpu.VMEM((W_GRP, W_GRP), F32),
            pltpu.VMEM((SUBLANES + hb, W_GRP), F32),
            pltpu.VMEM((16 + hb, W_GRP), F32),
            pltpu.VMEM((2 * WINDOW, 4 * WINDOW), F32),
            pltpu.VMEM((tb, D_MODEL), BF16),
            pltpu.VMEM((hb, W_GRP), F32),
            pltpu.VMEM((hb, W_GRP), F32),
            pltpu.VMEM((hb, W_GRP), F32),
            pltpu.VMEM((hb, W_GRP), BF16),
            pltpu.VMEM((hb, 2 * W_GRP), BF16),
            pltpu.VMEM((hb, W_GRP), F32),
            pltpu.VMEM((hb, W_GRP), BF16),
            pltpu.VMEM((hb, W_GRP), BF16),
            pltpu.VMEM((hb, W_GRP), BF16),
            pltpu.VMEM((hb, W_GRP), BF16),
            pltpu.VMEM((hb // HG_CHUNK, SUBLANES, W_GRP), F32),
            pltpu.VMEM((hb // HG_CHUNK, W_GRP, W_GRP), F32),
            pltpu.VMEM((hb // HG_CHUNK, W_GRP, W_GRP), BF16),
            pltpu.VMEM((hb, W_GRP), BF16),
            pltpu.VMEM((tb, D_MODEL), F32),
        ],
        compiler_params=_compiler_params(("arbitrary", "arbitrary")),
    )(x, w_in, sink, lb, ng, cw, pw_bd, ps, w_o, g, b)


DEC_SEQ = 4
SEQ_BLOCK = 16
PAIR_ROWS = 2 * DEC_SEQ


def _sample_mixer_kernel(x_ref, ck_ref, cv_ref, s_ref, cs_ref, pb_ref,
                         w_in_ref, w_hgt_ref, sink_ref, lb_ref, lbc_ref, ng_ref, cw_ref, pw_ref, ps_ref,
                         w_o_ref, g_ref, b_ref,
                         y_ref, knew_ref, vnew_ref, snew_ref, cnew_ref, pnew_ref,
                         proj_scr, mix_scr, hg_scr, cd_scr, od_scr):
    sb = ck_ref.shape[0]
    m_rows = sb * DEC_SEQ
    x = x_ref[...]
    xb = x.astype(BF16)
    proj_scr[...] = _dot(xb, w_in_ref[...])

    lane = lax.broadcasted_iota(jnp.int32, (PAIR_ROWS, LANES), 1)
    lo = lane < HEAD_DIM
    sub8 = lax.broadcasted_iota(jnp.int32, (PAIR_ROWS, LANES), 0)
    n_rows = 4 * PAIR_ROWS
    r_c = lax.broadcasted_iota(jnp.int32, (n_rows, 2 * WINDOW), 0)
    c_c = lax.broadcasted_iota(jnp.int32, (n_rows, 2 * WINDOW), 1)
    step_c = r_c & (DEC_SEQ - 1)
    rel_c = step_c + WINDOW - (c_c & (WINDOW - 1))
    valid_c = (((r_c >> 2) & 1) == (c_c >> 7)) & (rel_c <= WINDOW)
    slope_c = jnp.exp2(-2.0 * ((r_c >> 3).astype(F32) + 1.0))
    bias_c = jnp.where(valid_c, -slope_c * rel_c.astype(F32), MASK_VALUE)
    r_n = lax.broadcasted_iota(jnp.int32, (n_rows, PAIR_ROWS), 0)
    c_n = lax.broadcasted_iota(jnp.int32, (n_rows, PAIR_ROWS), 1)
    rel_n = (r_n & (DEC_SEQ - 1)) - (c_n & (DEC_SEQ - 1))
    valid_n = (((r_n >> 2) & 1) == (c_n >> 2)) & (rel_n >= 0)
    slope_n = jnp.exp2(-2.0 * ((r_n >> 3).astype(F32) + 1.0))
    bias_n = jnp.where(valid_n, -slope_n * rel_n.astype(F32), MASK_VALUE)
    head_col = lax.broadcasted_iota(jnp.int32, (n_rows, 1), 0) >> 3
    sink_col = jnp.where(head_col == 0, sink_ref[0],
                         jnp.where(head_col == 1, sink_ref[1], jnp.where(head_col == 2, sink_ref[2], sink_ref[3])))
    row128 = lax.broadcasted_iota(jnp.int32, (WINDOW, LANES), 0)
    for p in range(sb // 2):
        rows = slice(p * PAIR_ROWS, (p + 1) * PAIR_ROWS)
        q0 = proj_scr[rows, 0:LANES] * (HEAD_DIM ** -0.5)
        q1 = proj_scr[rows, LANES:2 * LANES] * (HEAD_DIM ** -0.5)
        q0r = pltpu.roll(q0, HEAD_DIM, axis=1)
        q1r = pltpu.roll(q1, HEAD_DIM, axis=1)
        zero = jnp.zeros_like(q0)
        q4 = jnp.concatenate([jnp.where(lo, q0, zero), jnp.where(lo, q0r, zero),
                              jnp.where(lo, zero, q1r), jnp.where(lo, zero, q1)], axis=0).astype(BF16)
        kn8 = proj_scr[rows, C_AK:C_AK + LANES]
        vn8 = proj_scr[rows, C_AV:C_AV + LANES]
        ka, kb = ck_ref[2 * p], ck_ref[2 * p + 1]
        va, vb = cv_ref[2 * p], cv_ref[2 * p + 1]
        s_c = _nt_dot(q4, jnp.concatenate([ka, kb], axis=0).astype(BF16)) + bias_c
        s_n = _nt_dot(q4, kn8.astype(BF16)) + bias_n
        m = jnp.maximum(jnp.maximum(jnp.max(s_c, axis=-1, keepdims=True),
                                    jnp.max(s_n, axis=-1, keepdims=True)), sink_col)
        p_c = jnp.exp(s_c - m)
        p_n = jnp.exp(s_n - m)
        den = (jnp.sum(p_c, axis=-1, keepdims=True) + jnp.sum(p_n, axis=-1, keepdims=True)
               + jnp.exp(sink_col - m))
        o_all = (_dot(p_c.astype(BF16), jnp.concatenate([va, vb], axis=0).astype(BF16))
                 + _dot(p_n.astype(BF16), vn8.astype(BF16))) / den
        o = [o_all[h * PAIR_ROWS:(h + 1) * PAIR_ROWS, :] for h in range(4)]
        mix_scr[rows, 0:LANES] = jnp.where(lo, o[0], pltpu.roll(o[1], HEAD_DIM, axis=1))
        mix_scr[rows, LANES:2 * LANES] = jnp.where(lo, pltpu.roll(o[2], HEAD_DIM, axis=1), o[3])
        kn_hi = pltpu.roll(kn8, DEC_SEQ, axis=0)
        vn_hi = pltpu.roll(vn8, DEC_SEQ, axis=0)
        for idx, old, new, out_ref in ((2 * p, ka, kn_hi, knew_ref), (2 * p + 1, kb, kn8, knew_ref),
                                       (2 * p, va, vn_hi, vnew_ref), (2 * p + 1, vb, vn8, vnew_ref)):
            shifted = pltpu.roll(old, WINDOW - DEC_SEQ, axis=0)
            out_ref[idx, 0:WINDOW - SUBLANES, :] = shifted[0:WINDOW - SUBLANES, :]
            out_ref[idx, WINDOW - SUBLANES:WINDOW, :] = jnp.where(
                sub8 >= DEC_SEQ, new, shifted[WINDOW - SUBLANES:WINDOW, :])

    hgt = _nt_dot(w_hgt_ref[...], xb)
    q_t = _silu(hgt[0:W_GRP, :])
    sig_pos, sig_neg = _sigmoid_pair(hgt[W_GRP:2 * W_GRP, :])
    lbc = lbc_ref[...]
    f_t = sig_pos + jnp.maximum(lbc, LB_FLOOR) * sig_neg
    k_t = (1.0 - lbc) * sig_neg
    for b in range(sb):
        st = s_ref[b]
        for step in range(DEC_SEQ):
            c = b * DEC_SEQ + step
            v_row = proj_scr[c:c + 1, C_BI:C_BI + W_GRP]
            v_exp = jnp.concatenate(
                [jnp.broadcast_to(v_row[:, h * HEAD_DIM:(h + 1) * HEAD_DIM], (HEAD_DIM, HEAD_DIM))
                 for h in range(N_HG)], axis=0)
            st = f_t[:, c:c + 1] * st + k_t[:, c:c + 1] * v_exp
            qs = q_t[:, c:c + 1] * st
            hg_scr[c:c + 1, :] = jnp.concatenate(
                [jnp.sum(qs[h * HEAD_DIM:(h + 1) * HEAD_DIM, :], axis=0, keepdims=True) for h in range(N_HG)],
                axis=1)
        snew_ref[b] = st
    r256 = lax.broadcasted_iota(jnp.int32, (W_GRP, W_GRP), 0)
    c256 = lax.broadcasted_iota(jnp.int32, (W_GRP, W_GRP), 1)
    head_ones = jnp.where((r256 >> 6) == (c256 >> 6), 1.0, 0.0).astype(BF16)
    o = hg_scr[...]
    o2 = o * o
    o2_hi = o2.astype(BF16)
    o2_lo = (o2 - o2_hi.astype(F32)).astype(BF16)
    ms = (_dot(o2_hi, head_ones) + _dot(o2_lo, head_ones)) * (1.0 / HEAD_DIM)
    o = o * lax.rsqrt(ms + RMS_EPS) * ng_ref[...]
    mix_scr[:, W_GRP:2 * W_GRP] = o * _silu(proj_scr[:, C_BG:C_BG + W_GRP])

    for j in range(8):
        cd_scr[j] = proj_scr[:, C_CB + j * LANES:C_CB + (j + 1) * LANES]
    cw = cw_ref[...]
    ps = ps_ref[...]
    cp = lax.broadcasted_iota(jnp.int32, (sb, W_GRP), 1) >> 6
    width = jnp.left_shift(2, cp)
    step_rows = lambda step: pl.ds(step, sb, stride=DEC_SEQ)

    def step_load(tile, step):
        return jnp.concatenate([cd_scr[tile, step_rows(step), :], cd_scr[tile + 1, step_rows(step), :]], axis=1)

    def step_store(tile, step, val):
        od_scr[tile, step_rows(step), :] = val[:, 0:LANES]
        od_scr[tile + 1, step_rows(step), :] = val[:, LANES:2 * LANES]

    u = [cs_ref[0], cs_ref[1]]
    ext = [pb_ref[i] for i in range(POOL_BUF)]
    for step in range(DEC_SEQ):
        u.append(step_load(2, step) * step_load(4, step))
        ext.append(step_load(6, step))
    cnew_ref[0] = u[DEC_SEQ]
    cnew_ref[1] = u[DEC_SEQ + 1]
    for i in range(POOL_BUF):
        pnew_ref[i] = ext[DEC_SEQ + i]
    for step in range(DEC_SEQ):
        yc = u[step] * cw[0:1, :] + u[step + 1] * cw[1:2, :] + u[step + 2] * cw[2:3, :]
        step_store(0, step, step_load(0, step) * yc)
        top = POOL_BUF + step
        acc = ext[top]
        sums = {}
        for jj in range(1, 16):
            acc = acc + ext[top - jj]
            if jj + 1 in POOL_WINDOWS:
                sums[jj + 1] = acc
        win = jnp.where(cp == 0, sums[2], jnp.where(cp == 1, sums[4], jnp.where(cp == 2, sums[8], sums[16])))
        cnt = jnp.minimum(PAST_LEN + step + 1, width).astype(F32)
        pooled = win / cnt - ext[top]
        step_store(2, step, _dot(pooled.astype(BF16), pw_ref[...]) * ps)
    for j in range(4):
        mix_scr[:, 2 * W_GRP + j * LANES:2 * W_GRP + (j + 1) * LANES] = od_scr[j]

    y = _dot(mix_scr[...].astype(BF16), w_o_ref[...])
    y_ref[...] = _layer_norm(ALPHA * x + y, g_ref[...], b_ref[...])


def _sample_mixer_call(l, x2d, ck, cv, s, cs_tm, pb_tm, w):
    m = x2d.shape[0]
    nb = m // DEC_SEQ
    sb = SEQ_BLOCK
    rows = sb * DEC_SEQ
    single = pl.Buffered(1)
    out_shape = (
        jax.ShapeDtypeStruct((m, D_MODEL), F32),
        jax.ShapeDtypeStruct((nb, WINDOW, LANES), F32),
        jax.ShapeDtypeStruct((nb, WINDOW, LANES), F32),
        jax.ShapeDtypeStruct((nb, W_GRP, HEAD_DIM), F32),
        jax.ShapeDtypeStruct((CONV_W - 1, nb, W_GRP), F32),
        jax.ShapeDtypeStruct((POOL_BUF, nb, W_GRP), F32),
    )
    return pl.pallas_call(
        _sample_mixer_kernel,
        name="mixer_sample",
        grid=(nb // sb,),
        in_specs=[pl.BlockSpec((rows, D_MODEL), lambda i: (i, 0)),
                  pl.BlockSpec((None, sb, WINDOW, LANES), lambda i: (l, i, 0, 0)),
                  pl.BlockSpec((None, sb, WINDOW, LANES), lambda i: (l, i, 0, 0)),
                  pl.BlockSpec((None, sb, W_GRP, HEAD_DIM), lambda i: (l, i, 0, 0)),
                  pl.BlockSpec((None, CONV_W - 1, sb, W_GRP), lambda i: (l, 0, i, 0)),
                  pl.BlockSpec((None, POOL_BUF, sb, W_GRP), lambda i: (l, 0, i, 0)),
                  pl.BlockSpec((D_MODEL, D_IN), lambda i: (0, 0), pipeline_mode=single),
                  _full_spec((2 * W_GRP, D_MODEL)),
                  pl.BlockSpec(memory_space=pltpu.SMEM),
                  _full_spec((1, W_GRP)), _full_spec((W_GRP, 1)), _full_spec((1, W_GRP)),
                  _full_spec((CONV_W, W_GRP)), _full_spec((W_GRP, W_GRP)), _full_spec((1, W_GRP)),
                  pl.BlockSpec((D_MODEL, D_MODEL), lambda i: (0, 0), pipeline_mode=single),
                  _full_spec((1, D_MODEL)), _full_spec((1, D_MODEL))],
        out_specs=(pl.BlockSpec((rows, D_MODEL), lambda i: (i, 0)),
                   pl.BlockSpec((sb, WINDOW, LANES), lambda i: (i, 0, 0)),
                   pl.BlockSpec((sb, WINDOW, LANES), lambda i: (i, 0, 0)),
                   pl.BlockSpec((sb, W_GRP, HEAD_DIM), lambda i: (i, 0, 0)),
                   pl.BlockSpec((CONV_W - 1, sb, W_GRP), lambda i: (0, i, 0)),
                   pl.BlockSpec((POOL_BUF, sb, W_GRP), lambda i: (0, i, 0))),
        out_shape=out_shape,
        scratch_shapes=[pltpu.VMEM((rows, D_IN), F32),
                        pltpu.VMEM((rows, D_MODEL), F32),
                        pltpu.VMEM((rows, W_GRP), F32),
                        pltpu.VMEM((8, rows, LANES), F32),
                        pltpu.VMEM((4, rows, LANES), F32)],
        compiler_params=_compiler_params(("arbitrary",)),
    )(x2d, ck, cv, s, cs_tm, pb_tm, w["w_in"], w["w_hgt"], w["sink"], w["lb"], w["lb_col"], w["ng"], w["cw"],
      w["pw"], w["ps"], w["w_o"], w["ln1_g"], w["ln1_b"])


def _xattn_sample_kernel(x_ref, mk_ref, mv_ref, wq_ref, wo_ref, g_ref, b_ref, o_ref, att_scr):
    sb = mk_ref.shape[0]
    x = x_ref[...]
    q = _dot(x.astype(BF16), wq_ref[...]) * (HEAD_DIM ** -0.5)
    n_rows = N_XH * PAIR_ROWS
    lane = lax.broadcasted_iota(jnp.int32, (PAIR_ROWS, D_X), 1) >> 6
    r_s = lax.broadcasted_iota(jnp.int32, (n_rows, 2 * N_MEM), 0)
    c_s = lax.broadcasted_iota(jnp.int32, (n_rows, 2 * N_MEM), 1)
    own = ((r_s >> 2) & 1) == (c_s >> 8)
    for p in range(sb // 2):
        rows = slice(p * PAIR_ROWS, (p + 1) * PAIR_ROWS)
        q8 = q[rows, :]
        zero = jnp.zeros_like(q8)
        q4 = jnp.concatenate([jnp.where(lane == h, q8, zero) for h in range(N_XH)], axis=0).astype(BF16)
        k2 = jnp.concatenate([mk_ref[2 * p], mk_ref[2 * p + 1]], axis=0).astype(BF16)
        v2 = jnp.concatenate([mv_ref[2 * p], mv_ref[2 * p + 1]], axis=0).astype(BF16)
        s = jnp.where(own, _nt_dot(q4, k2), MASK_VALUE)
        m = jnp.max(s, axis=-1, keepdims=True)
        pr = jnp.exp(s - m)
        den = jnp.sum(pr, axis=-1, keepdims=True)
        o_all = _dot(pr.astype(BF16), v2) / den
        o8 = jnp.zeros_like(q8)
        for h in range(N_XH):
            o8 = jnp.where(lane == h, o_all[h * PAIR_ROWS:(h + 1) * PAIR_ROWS, :], o8)
        att_scr[rows, :] = o8.astype(BF16)
    y = _dot(att_scr[...], wo_ref[...])
    o_ref[...] = _layer_norm(ALPHA * x + y, g_ref[...], b_ref[...])


def _xattn_sample_call(l, x2d, mk, mv, w):
    m = x2d.shape[0]
    nb = m // DEC_SEQ
    sb = SEQ_BLOCK
    rows = sb * DEC_SEQ
    return pl.pallas_call(
        _xattn_sample_kernel,
        name="xattn_sample",
        grid=(nb // sb,),
        in_specs=[pl.BlockSpec((rows, D_MODEL), lambda i: (i, 0)),
                  pl.BlockSpec((None, sb, N_MEM, D_X), lambda i: (l, i, 0, 0)),
                  pl.BlockSpec((None, sb, N_MEM, D_X), lambda i: (l, i, 0, 0)),
                  _full_spec((D_MODEL, D_X)), _full_spec((D_X, D_MODEL)),
                  _full_spec((1, D_MODEL)), _full_spec((1, D_MODEL))],
        out_specs=pl.BlockSpec((rows, D_MODEL), lambda i: (i, 0)),
        out_shape=jax.ShapeDtypeStruct((m, D_MODEL), F32),
        scratch_shapes=[pltpu.VMEM((rows, D_X), BF16)],
        compiler_params=_compiler_params(("arbitrary",)),
    )(x2d, mk, mv, w["w_xq"], w["w_xo"], w["ln2_g"], w["ln2_b"])


SWA_ROWS = WINDOW * N_KV
NEW_ROWS = DEC_SEQ * N_KV
MEM_ROWS = N_MEM * N_XH


def _hi_dot(a, b):
    return jnp.dot(a, b, preferred_element_type=F32, precision=lax.Precision.HIGHEST)


def _head_rows(tile):
    return jnp.concatenate([tile[:, h * HEAD_DIM:(h + 1) * HEAD_DIM] for h in range(4)], axis=0)


def _head_lanes(rows32):
    return jnp.concatenate([rows32[h * PAIR_ROWS:(h + 1) * PAIR_ROWS, :] for h in range(4)], axis=1)


def _sample_mixer2_kernel(x_ref, ck_ref, cv_ref, s_ref, cs_ref, pb_ref,
                          w_in_ref, sink_ref, lb_ref, ng_ref, cw_ref, pw_ref, ps_ref, w_o_ref, g_ref, b_ref,
                          *rest):
    (y_ref, knew_ref, vnew_ref, snew_ref, cnew_ref, pnew_ref,
     proj_scr, mix_scr, cd_scr, od_scr, new_scr, sc_scr, sn_scr, pc_scr, pn_scr) = rest[-15:]
    sb = ck_ref.shape[0]
    m_rows = sb * DEC_SEQ
    n_pair = sb // 2
    x = x_ref[...]
    xb = x.astype(BF16)
    proj_scr[...] = _dot(xb, w_in_ref[...])

    r2 = lax.broadcasted_iota(jnp.int32, (2 * m_rows, m_rows), 0)
    c2 = lax.broadcasted_iota(jnp.int32, (2 * m_rows, m_rows), 1)
    pick0 = jnp.where(r2 == 2 * c2, 1.0, 0.0)
    pick1 = jnp.where(r2 == 2 * c2 + 1, 1.0, 0.0)
    for idx, col in ((0, C_AK), (1, C_AV)):
        new_scr[idx] = (_hi_dot(pick0, proj_scr[:, col:col + HEAD_DIM])
                        + _hi_dot(pick1, proj_scr[:, col + HEAD_DIM:col + 2 * HEAD_DIM]))
    n_rows = 4 * PAIR_ROWS
    r_c = lax.broadcasted_iota(jnp.int32, (n_rows, 2 * SWA_ROWS), 0)
    c_c = lax.broadcasted_iota(jnp.int32, (n_rows, 2 * SWA_ROWS), 1)
    rel_c = (r_c & 3) + WINDOW - ((c_c >> 1) & (WINDOW - 1))
    ok_c = (((r_c >> 2) & 1) == (c_c >> 8)) & ((r_c >> 4) == (c_c & 1)) & (rel_c <= WINDOW)
    slope_c = jnp.exp2(-2.0 * ((r_c >> 3).astype(F32) + 1.0))
    bias_c = jnp.where(ok_c, -slope_c * rel_c.astype(F32), MASK_VALUE)
    r_n = lax.broadcasted_iota(jnp.int32, (n_rows, 2 * NEW_ROWS), 0)
    c_n = lax.broadcasted_iota(jnp.int32, (n_rows, 2 * NEW_ROWS), 1)
    rel_n = (r_n & 3) - ((c_n >> 1) & 3)
    ok_n = (((r_n >> 2) & 1) == (c_n >> 3)) & ((r_n >> 4) == (c_n & 1)) & (rel_n >= 0)
    slope_n = jnp.exp2(-2.0 * ((r_n >> 3).astype(F32) + 1.0))
    bias_n = jnp.where(ok_n, -slope_n * rel_n.astype(F32), MASK_VALUE)
    for p in range(n_pair):
        rows = slice(p * PAIR_ROWS, (p + 1) * PAIR_ROWS)
        q32 = _head_rows(proj_scr[rows, 0:W_GRP] * (HEAD_DIM ** -0.5)).astype(BF16)
        k2 = jnp.concatenate([ck_ref[2 * p], ck_ref[2 * p + 1]], axis=0).astype(BF16)
        kn = new_scr[0, 2 * NEW_ROWS * p:2 * NEW_ROWS * (p + 1), :].astype(BF16)
        sc_scr[n_rows * p:n_rows * (p + 1), :] = _nt_dot(q32, k2) + bias_c
        sn_scr[n_rows * p:n_rows * (p + 1), :] = _nt_dot(q32, kn) + bias_n
    head_col = (lax.broadcasted_iota(jnp.int32, (n_rows * n_pair, 1), 0) >> 3) & 3
    sink_col = jnp.where(head_col == 0, sink_ref[0],
                         jnp.where(head_col == 1, sink_ref[1], jnp.where(head_col == 2, sink_ref[2], sink_ref[3])))
    s_c = sc_scr[...]
    s_n = sn_scr[...]
    m = jnp.maximum(jnp.maximum(jnp.max(s_c, axis=-1, keepdims=True), jnp.max(s_n, axis=-1, keepdims=True)),
                    sink_col)
    p_c = jnp.exp(s_c - m)
    p_n = jnp.exp(s_n - m)
    inv = 1.0 / (jnp.sum(p_c, axis=-1, keepdims=True) + jnp.sum(p_n, axis=-1, keepdims=True)
                 + jnp.exp(sink_col - m))
    pc_scr[...] = (p_c * inv).astype(BF16)
    pn_scr[...] = (p_n * inv).astype(BF16)
    for p in range(n_pair):
        rows = slice(p * PAIR_ROWS, (p + 1) * PAIR_ROWS)
        v2 = jnp.concatenate([cv_ref[2 * p], cv_ref[2 * p + 1]], axis=0).astype(BF16)
        vn = new_scr[1, 2 * NEW_ROWS * p:2 * NEW_ROWS * (p + 1), :].astype(BF16)
        o32 = (_dot(pc_scr[n_rows * p:n_rows * (p + 1), :], v2)
               + _dot(pn_scr[n_rows * p:n_rows * (p + 1), :], vn))
        mix_scr[rows, 0:W_GRP] = _head_lanes(o32)
    for b in range(sb):
        for old_ref, out_ref, idx in ((ck_ref, knew_ref, 0), (cv_ref, vnew_ref, 1)):
            out_ref[b, 0:SWA_ROWS - NEW_ROWS, :] = old_ref[b, NEW_ROWS:SWA_ROWS, :]
            out_ref[b, SWA_ROWS - NEW_ROWS:SWA_ROWS, :] = new_scr[idx, NEW_ROWS * b:NEW_ROWS * (b + 1), :]

    lb = lb_ref[...]
    lbf = jnp.maximum(lb, LB_FLOOR)
    r256 = lax.broadcasted_iota(jnp.int32, (W_GRP, W_GRP), 0)
    c256 = lax.broadcasted_iota(jnp.int32, (W_GRP, W_GRP), 1)
    same_head = (r256 >> 6) == (c256 >> 6)
    head_ones = jnp.where(same_head, 1.0, 0.0).astype(BF16)
    step = lax.broadcasted_iota(jnp.int32, (m_rows, W_GRP), 0) & (DEC_SEQ - 1)
    q = _silu(proj_scr[:, C_BQ:C_BQ + W_GRP])
    sig_pos, sig_neg = _sigmoid_pair(proj_scr[:, C_BF:C_BF + W_GRP])
    g = jnp.log(sig_pos + lbf * sig_neg)
    k = (1.0 - lb) * sig_neg
    v = proj_scr[:, C_BI:C_BI + W_GRP]
    cum = g + jnp.where(step >= 1, pltpu.roll(g, 1, axis=0), 0.0)
    cum = cum + jnp.where(step >= 2, pltpu.roll(cum, 2, axis=0), 0.0)
    o = jnp.zeros((m_rows, W_GRP), F32)
    for lag in range(DEC_SEQ):
        if lag == 0:
            w_qk = q * k
            v_l = v
        else:
            decay = jnp.exp(jnp.minimum(cum - pltpu.roll(cum, lag, axis=0), 0.0))
            w_qk = jnp.where(step >= lag, q * pltpu.roll(k, lag, axis=0) * decay, 0.0)
            v_l = pltpu.roll(v, lag, axis=0)
        o = o + _dot(w_qk.astype(BF16), head_ones) * v_l
    tot = jnp.where(step == 3, cum,
                    jnp.where(step == 2, pltpu.roll(cum, m_rows - 1, axis=0),
                              jnp.where(step == 1, pltpu.roll(cum, m_rows - 2, axis=0),
                                        pltpu.roll(cum, m_rows - 3, axis=0))))
    qs = q * jnp.exp(cum)
    ks = k * jnp.exp(tot - cum)
    dec = jnp.exp(tot)
    dec_hi = dec.astype(BF16).astype(F32)
    dec_mid = (dec - dec_hi).astype(BF16).astype(F32)
    dec_lo = dec - dec_hi - dec_mid
    dec3 = jnp.where(step == 0, dec_hi, jnp.where(step == 1, dec_mid, jnp.where(step == 2, dec_lo, 0.0)))
    sub8 = lax.broadcasted_iota(jnp.int32, (PAIR_ROWS, W_GRP), 0)
    ones3 = jnp.where((lax.broadcasted_iota(jnp.int32, (PAIR_ROWS, HEAD_DIM), 0) & 3) < 3, 1.0, 0.0)
    for p in range(n_pair):
        rows = slice(p * PAIR_ROWS, (p + 1) * PAIR_ROWS)
        qs8 = qs[rows, :].astype(BF16)
        o_inter = jnp.zeros((PAIR_ROWS, W_GRP), F32)
        for half in range(2):
            b = 2 * p + half
            own = (sub8 >> 2) == half
            own64 = own[:, 0:HEAD_DIM]
            st = s_ref[b]
            st_bd = jnp.where(same_head, jnp.concatenate([st] * N_HG, axis=1), 0.0).astype(BF16)
            o_inter = jnp.where(own, _dot(qs8, st_bd), o_inter)
            ks8 = jnp.where(own, ks[rows, :], 0.0).astype(BF16)
            upd = _tn_dot(ks8, v[rows, :].astype(BF16))
            upd = jnp.concatenate([upd[h * HEAD_DIM:(h + 1) * HEAD_DIM, h * HEAD_DIM:(h + 1) * HEAD_DIM]
                                   for h in range(N_HG)], axis=0)
            d8 = jnp.where(own, dec3[rows, :], 0.0).astype(BF16)
            dcol = _tn_dot(d8, jnp.where(own64, ones3, 0.0).astype(BF16))
            snew_ref[b] = dcol * st + upd
        hg = o[rows, :] + o_inter
        hg2 = hg * hg
        hg2_hi = hg2.astype(BF16)
        hg2_lo = (hg2 - hg2_hi.astype(F32)).astype(BF16)
        ms = (_dot(hg2_hi, head_ones) + _dot(hg2_lo, head_ones)) * (1.0 / HEAD_DIM)
        hg = hg * lax.rsqrt(ms + RMS_EPS) * ng_ref[...]
        mix_scr[rows, W_GRP:2 * W_GRP] = hg * _silu(proj_scr[rows, C_BG:C_BG + W_GRP])

    for j in range(8):
        cd_scr[j] = proj_scr[:, C_CB + j * LANES:C_CB + (j + 1) * LANES]
    cw = cw_ref[...]
    ps = ps_ref[...]
    cp = lax.broadcasted_iota(jnp.int32, (sb, W_GRP), 1) >> 6
    width = jnp.left_shift(2, cp)
    step_rows = lambda s_: pl.ds(s_, sb, stride=DEC_SEQ)

    def step_load(tile, s_):
        return jnp.concatenate([cd_scr[tile, step_rows(s_), :], cd_scr[tile + 1, step_rows(s_), :]], axis=1)

    def step_store(tile, s_, val):
        od_scr[tile, step_rows(s_), :] = val[:, 0:LANES]
        od_scr[tile + 1, step_rows(s_), :] = val[:, LANES:2 * LANES]

    u = [cs_ref[0], cs_ref[1]]
    ext = [pb_ref[i] for i in range(POOL_BUF)]
    for s_ in range(DEC_SEQ):
        u.append(step_load(2, s_) * step_load(4, s_))
        ext.append(step_load(6, s_))
    cnew_ref[0] = u[DEC_SEQ]
    cnew_ref[1] = u[DEC_SEQ + 1]
    for i in range(POOL_BUF):
        pnew_ref[i] = ext[DEC_SEQ + i]
    for s_ in range(DEC_SEQ):
        yc = u[s_] * cw[0:1, :] + u[s_ + 1] * cw[1:2, :] + u[s_ + 2] * cw[2:3, :]
        step_store(0, s_, step_load(0, s_) * yc)
        top = POOL_BUF + s_
        acc = ext[top]
        sums = {}
        for jj in range(1, 16):
            acc = acc + ext[top - jj]
            if jj + 1 in POOL_WINDOWS:
                sums[jj + 1] = acc
        win = jnp.where(cp == 0, sums[2], jnp.where(cp == 1, sums[4], jnp.where(cp == 2, sums[8], sums[16])))
        cnt = jnp.minimum(PAST_LEN + s_ + 1, width).astype(F32)
        pooled = win / cnt - ext[top]
        step_store(2, s_, _dot(pooled.astype(BF16), pw_ref[...]) * ps)
    for j in range(4):
        mix_scr[:, 2 * W_GRP + j * LANES:2 * W_GRP + (j + 1) * LANES] = od_scr[j]

    y = _dot(mix_scr[...].astype(BF16), w_o_ref[...])
    y_ref[...] = _layer_norm(ALPHA * x + y, g_ref[...], b_ref[...])


def _sample_mixer2_call(l, x2d, ck, cv, s, cs_tm, pb_tm, w, prev):
    m = x2d.shape[0]
    nb = m // DEC_SEQ
    sb = SEQ_BLOCK
    rows = sb * DEC_SEQ
    n_sc = 4 * PAIR_ROWS * (sb // 2)
    single = pl.Buffered(1)
    out_shape = (
        jax.ShapeDtypeStruct((m, D_MODEL), F32),
        jax.ShapeDtypeStruct((DEPTH, nb, SWA_ROWS, HEAD_DIM), F32),
        jax.ShapeDtypeStruct((DEPTH, nb, SWA_ROWS, HEAD_DIM), F32),
        jax.ShapeDtypeStruct((DEPTH, nb, W_GRP, HEAD_DIM), F32),
        jax.ShapeDtypeStruct((CONV_W - 1, nb, W_GRP), F32),
        jax.ShapeDtypeStruct((POOL_BUF, nb, W_GRP), F32),
    )
    layer_blk = lambda i: (l, i, 0, 0)
    in_specs = [pl.BlockSpec((rows, D_MODEL), lambda i: (i, 0)),
                pl.BlockSpec((None, sb, SWA_ROWS, HEAD_DIM), layer_blk),
                pl.BlockSpec((None, sb, SWA_ROWS, HEAD_DIM), layer_blk),
                pl.BlockSpec((None, sb, W_GRP, HEAD_DIM), layer_blk),
                pl.BlockSpec((None, CONV_W - 1, sb, W_GRP), lambda i: (l, 0, i, 0)),
                pl.BlockSpec((None, POOL_BUF, sb, W_GRP), lambda i: (l, 0, i, 0)),
                pl.BlockSpec((D_MODEL, D_IN), lambda i: (0, 0), pipeline_mode=single),
                pl.BlockSpec(memory_space=pltpu.SMEM),
                _full_spec((1, W_GRP)), _full_spec((1, W_GRP)),
                _full_spec((CONV_W, W_GRP)), _full_spec((W_GRP, W_GRP)), _full_spec((1, W_GRP)),
                pl.BlockSpec((D_MODEL, D_MODEL), lambda i: (0, 0), pipeline_mode=single),
                _full_spec((1, D_MODEL)), _full_spec((1, D_MODEL))]
    args = [x2d, ck, cv, s, cs_tm, pb_tm, w["w_in"], w["sink"], w["lb"], w["ng"], w["cw"], w["pw"], w["ps"],
            w["w_o"], w["ln1_g"], w["ln1_b"]]
    aliases = {}
    if prev is not None:
        for j, buf in enumerate(prev):
            aliases[len(args)] = 1 + j
            args.append(buf)
            in_specs.append(pl.BlockSpec(memory_space=pl.ANY))
    return pl.pallas_call(
        _sample_mixer2_kernel,
        name="mixer_sample",
        grid=(nb // sb,),
        in_specs=in_specs,
        out_specs=(pl.BlockSpec((rows, D_MODEL), lambda i: (i, 0)),
                   pl.BlockSpec((None, sb, SWA_ROWS, HEAD_DIM), layer_blk),
                   pl.BlockSpec((None, sb, SWA_ROWS, HEAD_DIM), layer_blk),
                   pl.BlockSpec((None, sb, W_GRP, HEAD_DIM), layer_blk),
                   pl.BlockSpec((CONV_W - 1, sb, W_GRP), lambda i: (0, i, 0)),
                   pl.BlockSpec((POOL_BUF, sb, W_GRP), lambda i: (0, i, 0))),
        out_shape=out_shape,
        input_output_aliases=aliases,
        scratch_shapes=[pltpu.VMEM((rows, D_IN), F32),
                        pltpu.VMEM((rows, D_MODEL), F32),
                        pltpu.VMEM((8, rows, LANES), F32),
                        pltpu.VMEM((4, rows, LANES), F32),
                        pltpu.VMEM((2, 2 * rows, HEAD_DIM), F32),
                        pltpu.VMEM((n_sc, 2 * SWA_ROWS), F32),
                        pltpu.VMEM((n_sc, 2 * NEW_ROWS), F32),
                        pltpu.VMEM((n_sc, 2 * SWA_ROWS), BF16),
                        pltpu.VMEM((n_sc, 2 * NEW_ROWS), BF16)],
        compiler_params=_compiler_params(("arbitrary",)),
    )(*args)


def _xattn_sample2_kernel(x_ref, mk_ref, mv_ref, wq_ref, wo_ref, g_ref, b_ref, o_ref, s_scr, p_scr, att_scr):
    sb = mk_ref.shape[0]
    n_rows = N_XH * PAIR_ROWS
    x = x_ref[...]
    q = _dot(x.astype(BF16), wq_ref[...]) * (HEAD_DIM ** -0.5)
    r_s = lax.broadcasted_iota(jnp.int32, (n_rows, MEM_ROWS), 0)
    c_s = lax.broadcasted_iota(jnp.int32, (n_rows, MEM_ROWS), 1)
    head_ok = (r_s >> 3) == (c_s & 3)
    member = (r_s >> 2) & 1
    for b in range(sb):
        rows = slice((b // 2) * PAIR_ROWS, (b // 2 + 1) * PAIR_ROWS)
        q32 = _head_rows(q[rows, :]).astype(BF16)
        s = _nt_dot(q32, mk_ref[b].astype(BF16))
        s_scr[b] = jnp.where(head_ok & (member == (b & 1)), s, MASK_VALUE)
    s_all = s_scr[...]
    m = jnp.max(s_all, axis=-1, keepdims=True)
    pr = jnp.exp(s_all - m)
    p_scr[...] = (pr * (1.0 / jnp.sum(pr, axis=-1, keepdims=True))).astype(BF16)
    own_rows = (lax.broadcasted_iota(jnp.int32, (n_rows, HEAD_DIM), 0) >> 2) & 1
    for p in range(sb // 2):
        rows = slice(p * PAIR_ROWS, (p + 1) * PAIR_ROWS)
        o_a = _dot(p_scr[2 * p], mv_ref[2 * p].astype(BF16))
        o_b = _dot(p_scr[2 * p + 1], mv_ref[2 * p + 1].astype(BF16))
        att_scr[rows, :] = _head_lanes(jnp.where(own_rows == 0, o_a, o_b)).astype(BF16)
    y = _dot(att_scr[...], wo_ref[...])
    o_ref[...] = _layer_norm(ALPHA * x + y, g_ref[...], b_ref[...])


def _xattn_sample2_call(l, x2d, mk, mv, w):
    m = x2d.shape[0]
    nb = m // DEC_SEQ
    sb = SEQ_BLOCK
    rows = sb * DEC_SEQ
    return pl.pallas_call(
        _xattn_sample2_kernel,
        name="xattn_sample",
        grid=(nb // sb,),
        in_specs=[pl.BlockSpec((rows, D_MODEL), lambda i: (i, 0)),
                  pl.BlockSpec((None, sb, MEM_ROWS, HEAD_DIM), lambda i: (l, i, 0, 0)),
                  pl.BlockSpec((None, sb, MEM_ROWS, HEAD_DIM), lambda i: (l, i, 0, 0)),
                  _full_spec((D_MODEL, D_X)), _full_spec((D_X, D_MODEL)),
                  _full_spec((1, D_MODEL)), _full_spec((1, D_MODEL))],
        out_specs=pl.BlockSpec((rows, D_MODEL), lambda i: (i, 0)),
        out_shape=jax.ShapeDtypeStruct((m, D_MODEL), F32),
        scratch_shapes=[pltpu.VMEM((sb, N_XH * PAIR_ROWS, MEM_ROWS), F32),
                        pltpu.VMEM((sb, N_XH * PAIR_ROWS, MEM_ROWS), BF16),
                        pltpu.VMEM((rows, D_X), BF16)],
        compiler_params=_compiler_params(("arbitrary",)),
    )(x2d, mk, mv, w["w_xq"], w["w_xo"], w["ln2_g"], w["ln2_b"])


HG_ROWS = N_HG * HEAD_DIM * HEAD_DIM
K_PER_STEP = 32


def _gather_block(dst_scr, src_scr, n_tiles, gi, sb, nb):
    for t in range(DEC_SEQ):
        start = pl.multiple_of(t * nb + gi * sb, sb)
        for j in range(n_tiles):
            dst_scr[j, t * sb:(t + 1) * sb, :] = src_scr[j, pl.ds(start, sb), :]


def _scatter_block(dst_scr, src_scr, n_tiles, gi, sb, nb):
    for t in range(DEC_SEQ):
        start = pl.multiple_of(t * nb + gi * sb, sb)
        for j in range(n_tiles):
            dst_scr[pl.ds(start, sb), j * LANES:(j + 1) * LANES] = src_scr[j, t * sb:(t + 1) * sb, :]


def _sample_mixer3_kernel(xt_ref, ck_ref, cv_ref, s_ref, cs_ref, pb_ref,
                          w_in_ref, w_hgt_ref, w_kvt_ref, sink_ref, lbc_ref, ngc_ref, cw_ref, pw_ref, ps_ref,
                          w_o_ref, g_ref, b_ref, *rest):
    (y_ref, knew_ref, vnew_ref, snew_ref, cnew_ref, pnew_ref,
     proj_scr, a_scr, kvt_scr, q_scr, f_scr, k_scr, v_scr, gate_scr, o_scr, mix_scr,
     ab_scr, oa_scr, sc_scr, sn_scr, pc_scr, pn_scr) = rest[-22:]
    gi = pl.program_id(0)
    sb = ck_ref.shape[0]
    nb = xt_ref.shape[0] // DEC_SEQ
    half = sb // 2

    @pl.when(gi == 0)
    def _():
        xt = xt_ref[...].astype(BF16)
        pa = _dot(xt, w_in_ref[:, 0:C_BQ])
        for j in range(4):
            a_scr[j] = pa[:, j * LANES:(j + 1) * LANES]
        kvt_scr[...] = _nt_dot(w_kvt_ref[...], xt)
        hgt = _nt_dot(w_hgt_ref[...], xt)
        lbc = lbc_ref[...]
        q_scr[...] = _silu(hgt[0:W_GRP, :])
        sig_pos, sig_neg = _sigmoid_pair(hgt[W_GRP:2 * W_GRP, :])
        f_scr[...] = sig_pos + jnp.maximum(lbc, LB_FLOOR) * sig_neg
        k_scr[...] = (1.0 - lbc) * sig_neg
        v_scr[...] = hgt[2 * W_GRP:3 * W_GRP, :]
        gate_scr[...] = _silu(hgt[3 * W_GRP:4 * W_GRP, :])
        o_scr[...] = jnp.zeros(o_scr.shape, F32)
        proj_scr[...] = _dot(xt, w_in_ref[:, C_CB:D_IN])
        cw = cw_ref[...]
        ps = ps_ref[...]
        cp = lax.broadcasted_iota(jnp.int32, (nb, W_GRP), 1) >> 6
        width = jnp.left_shift(2, cp)
        u = [cs_ref[0], cs_ref[1]]
        ext = [pb_ref[i] for i in range(POOL_BUF)]
        for t in range(DEC_SEQ):
            rows = slice(t * nb, (t + 1) * nb)
            u.append(proj_scr[rows, W_GRP:2 * W_GRP] * proj_scr[rows, 2 * W_GRP:3 * W_GRP])
            ext.append(proj_scr[rows, 3 * W_GRP:4 * W_GRP])
        cnew_ref[0] = u[DEC_SEQ]
        cnew_ref[1] = u[DEC_SEQ + 1]
        for i in range(POOL_BUF):
            pnew_ref[i] = ext[DEC_SEQ + i]
        for t in range(DEC_SEQ):
            rows = slice(t * nb, (t + 1) * nb)
            yc = u[t] * cw[0:1, :] + u[t + 1] * cw[1:2, :] + u[t + 2] * cw[2:3, :]
            mix_scr[rows, 2 * W_GRP:3 * W_GRP] = proj_scr[rows, 0:W_GRP] * yc
            top = POOL_BUF + t
            acc = ext[top]
            sums = {}
            for jj in range(1, 16):
                acc = acc + ext[top - jj]
                if jj + 1 in POOL_WINDOWS:
                    sums[jj + 1] = acc
            win = jnp.where(cp == 0, sums[2], jnp.where(cp == 1, sums[4], jnp.where(cp == 2, sums[8], sums[16])))
            cnt = jnp.minimum(PAST_LEN + t + 1, width).astype(F32)
            pooled = win / cnt - ext[top]
            mix_scr[rows, 3 * W_GRP:4 * W_GRP] = _dot(pooled.astype(BF16), pw_ref[...]) * ps

    _gather_block(ab_scr, a_scr, 4, gi, sb, nb)
    lane = lax.broadcasted_iota(jnp.int32, (PAIR_ROWS, LANES), 1)
    lo = lane < HEAD_DIM
    n_rows = 4 * PAIR_ROWS
    r_c = lax.broadcasted_iota(jnp.int32, (n_rows, 2 * WINDOW), 0)
    c_c = lax.broadcasted_iota(jnp.int32, (n_rows, 2 * WINDOW), 1)
    rel_c = ((r_c >> 1) & 3) + WINDOW - (c_c & (WINDOW - 1))
    ok_c = ((r_c & 1) == (c_c >> 7)) & (rel_c <= WINDOW)
    bias_c = jnp.where(ok_c, -jnp.exp2(-2.0 * ((r_c >> 3).astype(F32) + 1.0)) * rel_c.astype(F32), MASK_VALUE)
    r_n = lax.broadcasted_iota(jnp.int32, (n_rows, PAIR_ROWS), 0)
    c_n = lax.broadcasted_iota(jnp.int32, (n_rows, PAIR_ROWS), 1)
    rel_n = ((r_n >> 1) & 3) - (c_n >> 1)
    ok_n = ((r_n & 1) == (c_n & 1)) & (rel_n >= 0)
    bias_n = jnp.where(ok_n, -jnp.exp2(-2.0 * ((r_n >> 3).astype(F32) + 1.0)) * rel_n.astype(F32), MASK_VALUE)
    pair_rows = lambda p: pl.ds(p, PAIR_ROWS, stride=half)
    for p in range(half):
        q0 = ab_scr[0, pair_rows(p), :] * (HEAD_DIM ** -0.5)
        q1 = ab_scr[1, pair_rows(p), :] * (HEAD_DIM ** -0.5)
        q0r = pltpu.roll(q0, HEAD_DIM, axis=1)
        q1r = pltpu.roll(q1, HEAD_DIM, axis=1)
        zero = jnp.zeros_like(q0)
        q4 = jnp.concatenate([jnp.where(lo, q0, zero), jnp.where(lo, q0r, zero),
                              jnp.where(lo, zero, q1r), jnp.where(lo, zero, q1)], axis=0).astype(BF16)
        kt2 = jnp.concatenate([ck_ref[p], ck_ref[p + half]], axis=1).astype(BF16)
        kn8 = ab_scr[2, pair_rows(p), :].astype(BF16)
        sc_scr[n_rows * p:n_rows * (p + 1), :] = _dot(q4, kt2) + bias_c
        sn_scr[n_rows * p:n_rows * (p + 1), :] = _nt_dot(q4, kn8) + bias_n
    head_col = (lax.broadcasted_iota(jnp.int32, (n_rows * half, 1), 0) >> 3) & 3
    sink_col = jnp.where(head_col == 0, sink_ref[0],
                         jnp.where(head_col == 1, sink_ref[1], jnp.where(head_col == 2, sink_ref[2], sink_ref[3])))
    s_c = sc_scr[...]
    s_n = sn_scr[...]
    m = jnp.maximum(jnp.maximum(jnp.max(s_c, axis=-1, keepdims=True), jnp.max(s_n, axis=-1, keepdims=True)),
                    sink_col)
    p_c = jnp.exp(s_c - m)
    p_n = jnp.exp(s_n - m)
    inv = 1.0 / (jnp.sum(p_c, axis=-1, keepdims=True) + jnp.sum(p_n, axis=-1, keepdims=True)
                 + jnp.exp(sink_col - m))
    pc_scr[...] = (p_c * inv).astype(BF16)
    pn_scr[...] = (p_n * inv).astype(BF16)
    for p in range(half):
        vt2 = jnp.concatenate([cv_ref[p], cv_ref[p + half]], axis=1).astype(BF16)
        vn8 = ab_scr[3, pair_rows(p), :].astype(BF16)
        o_all = (_nt_dot(pc_scr[n_rows * p:n_rows * (p + 1), :], vt2)
                 + _dot(pn_scr[n_rows * p:n_rows * (p + 1), :], vn8))
        o = [o_all[h * PAIR_ROWS:(h + 1) * PAIR_ROWS, :] for h in range(4)]
        oa_scr[0, pair_rows(p), :] = jnp.where(lo, o[0], pltpu.roll(o[1], HEAD_DIM, axis=1))
        oa_scr[1, pair_rows(p), :] = jnp.where(lo, pltpu.roll(o[2], HEAD_DIM, axis=1), o[3])
    _scatter_block(mix_scr, oa_scr, 2, gi, sb, nb)
    lane_w = lax.broadcasted_iota(jnp.int32, (LANES, LANES), 1)
    for b in range(sb):
        seq = gi * sb + b
        for old_ref, out_ref, r0 in ((ck_ref, knew_ref, 0), (cv_ref, vnew_ref, LANES)):
            cols = pltpu.roll(old_ref[b], WINDOW - DEC_SEQ, axis=1)
            for t in range(DEC_SEQ):
                shift = jnp.bitwise_and(WINDOW - DEC_SEQ + t - seq, LANES - 1)
                new_t = pltpu.roll(kvt_scr[r0:r0 + LANES, t * nb:(t + 1) * nb], shift, axis=1)
                cols = jnp.where(lane_w == WINDOW - DEC_SEQ + t, new_t, cols)
            out_ref[b] = cols

    head = gi // 2
    k_base = head * HEAD_DIM + (gi % 2) * K_PER_STEP
    v_rows = pl.ds(pl.multiple_of(head * HEAD_DIM, HEAD_DIM), HEAD_DIM)

    def k_body(k8, accs):
        rows8 = pl.ds(pl.multiple_of(k_base + k8 * SUBLANES, SUBLANES), SUBLANES)
        accs = list(accs)
        for j in range(SUBLANES):
            st_rows = pl.ds(pl.multiple_of((k8 * SUBLANES + j) * HEAD_DIM, HEAD_DIM), HEAD_DIM)
            st = s_ref[st_rows, :]
            for t in range(DEC_SEQ):
                cols = slice(t * nb, (t + 1) * nb)
                f8, k8v, q8 = f_scr[rows8, cols], k_scr[rows8, cols], q_scr[rows8, cols]
                st = f8[j:j + 1, :] * st + k8v[j:j + 1, :] * v_scr[v_rows, cols]
                accs[t] = accs[t] + q8[j:j + 1, :] * st
            snew_ref[st_rows, :] = st
        return tuple(accs)

    zero_acc = jnp.zeros((HEAD_DIM, nb), F32)
    accs = lax.fori_loop(0, K_PER_STEP // SUBLANES, k_body, (zero_acc,) * DEC_SEQ)
    for t in range(DEC_SEQ):
        cols = slice(t * nb, (t + 1) * nb)
        o_scr[v_rows, cols] = o_scr[v_rows, cols] + accs[t]

    @pl.when(gi == pl.num_programs(0) - 1)
    def _():
        o = o_scr[...]
        parts = []
        for h in range(N_HG):
            oh = o[h * HEAD_DIM:(h + 1) * HEAD_DIM, :]
            parts.append(oh * lax.rsqrt(jnp.mean(oh * oh, axis=0, keepdims=True) + RMS_EPS))
        ob = jnp.concatenate(parts, axis=0) * ngc_ref[...] * gate_scr[...]
        y = (_dot(mix_scr[:, 0:W_GRP].astype(BF16), w_o_ref[0:W_GRP, :])
             + _tn_dot(ob.astype(BF16), w_o_ref[W_GRP:2 * W_GRP, :])
             + _dot(mix_scr[:, 2 * W_GRP:4 * W_GRP].astype(BF16), w_o_ref[2 * W_GRP:4 * W_GRP, :]))
        y_ref[...] = _layer_norm(ALPHA * xt_ref[...] + y, g_ref[...], b_ref[...])


def _sample_mixer3_call(l, xt, ck, cv, s, cs_tm, pb_tm, w, prev):
    m = xt.shape[0]
    nb = m // DEC_SEQ
    sb = SEQ_BLOCK
    n_steps = nb // sb
    n_sc = 4 * PAIR_ROWS * (sb // 2)
    s_rows = HG_ROWS // n_steps
    single = pl.Buffered(1)
    const2 = lambda i: (0, 0)
    out_shape = (
        jax.ShapeDtypeStruct((m, D_MODEL), F32),
        jax.ShapeDtypeStruct((DEPTH, nb, LANES, WINDOW), F32),
        jax.ShapeDtypeStruct((DEPTH, nb, LANES, WINDOW), F32),
        jax.ShapeDtypeStruct((DEPTH, HG_ROWS, nb), F32),
        jax.ShapeDtypeStruct((CONV_W - 1, nb, W_GRP), F32),
        jax.ShapeDtypeStruct((POOL_BUF, nb, W_GRP), F32),
    )
    seq_blk = lambda i: (l, i, 0, 0)
    in_specs = [pl.BlockSpec((m, D_MODEL), const2, pipeline_mode=single),
                pl.BlockSpec((None, sb, LANES, WINDOW), seq_blk),
                pl.BlockSpec((None, sb, LANES, WINDOW), seq_blk),
                pl.BlockSpec((None, s_rows, nb), lambda i: (l, i, 0)),
                pl.BlockSpec((None, CONV_W - 1, nb, W_GRP), lambda i: (l, 0, 0, 0), pipeline_mode=single),
                pl.BlockSpec((None, POOL_BUF, nb, W_GRP), lambda i: (l, 0, 0, 0), pipeline_mode=single),
                pl.BlockSpec((D_MODEL, D_IN), const2, pipeline_mode=single),
                pl.BlockSpec((4 * W_GRP, D_MODEL), const2, pipeline_mode=single),
                pl.BlockSpec((2 * LANES, D_MODEL), const2, pipeline_mode=single),
                pl.BlockSpec(memory_space=pltpu.SMEM),
                _full_spec((W_GRP, 1)), _full_spec((W_GRP, 1)),
                _full_spec((CONV_W, W_GRP)), _full_spec((W_GRP, W_GRP)), _full_spec((1, W_GRP)),
                pl.BlockSpec((D_MODEL, D_MODEL), const2, pipeline_mode=single),
                _full_spec((1, D_MODEL)), _full_spec((1, D_MODEL))]
    args = [xt, ck, cv, s, cs_tm, pb_tm, w["w_in"], w["w_hgt"], w["w_kvt"], w["sink"], w["lb_col"], w["ng_col"],
            w["cw"], w["pw"], w["ps"], w["w_o"], w["ln1_g"], w["ln1_b"]]
    aliases = {}
    if prev is not None:
        for j, buf in enumerate(prev):
            aliases[len(args)] = 1 + j
            args.append(buf)
            in_specs.append(pl.BlockSpec(memory_space=pl.ANY))
    f32 = lambda *shape: pltpu.VMEM(shape, F32)
    return pl.pallas_call(
        _sample_mixer3_kernel,
        name="mixer_sample",
        grid=(n_steps,),
        in_specs=in_specs,
        out_specs=(pl.BlockSpec((m, D_MODEL), const2),
                   pl.BlockSpec((None, sb, LANES, WINDOW), seq_blk),
                   pl.BlockSpec((None, sb, LANES, WINDOW), seq_blk),
                   pl.BlockSpec((None, s_rows, nb), lambda i: (l, i, 0)),
                   pl.BlockSpec((CONV_W - 1, nb, W_GRP), lambda i: (0, 0, 0)),
                   pl.BlockSpec((POOL_BUF, nb, W_GRP), lambda i: (0, 0, 0))),
        out_shape=out_shape,
        input_output_aliases=aliases,
        scratch_shapes=[f32(m, 4 * W_GRP),
                        f32(4, m, LANES),
                        f32(2 * LANES, m),
                        f32(W_GRP, m), f32(W_GRP, m), f32(W_GRP, m), f32(W_GRP, m), f32(W_GRP, m),
                        f32(W_GRP, m),
                        f32(m, D_MODEL),
                        f32(4, sb * DEC_SEQ, LANES), f32(2, sb * DEC_SEQ, LANES),
                        f32(n_sc, 2 * WINDOW), f32(n_sc, PAIR_ROWS),
                        pltpu.VMEM((n_sc, 2 * WINDOW), BF16), pltpu.VMEM((n_sc, PAIR_ROWS), BF16)],
        compiler_params=_compiler_params(("arbitrary",)),
    )(*args)


def _xattn_sample3_kernel(xt_ref, mk_ref, mv_ref, wq_ref, wo_ref, g_ref, b_ref, o_ref,
                          q_scr, qb_scr, ob_scr, att_scr, s_scr, p_scr):
    gi = pl.program_id(0)
    sb = mk_ref.shape[0]
    nb = xt_ref.shape[0] // DEC_SEQ
    half = sb // 2
    n_rows = N_XH * PAIR_ROWS

    @pl.when(gi == 0)
    def _():
        q = _dot(xt_ref[...].astype(BF16), wq_ref[...]) * (HEAD_DIM ** -0.5)
        q_scr[0] = q[:, 0:LANES]
        q_scr[1] = q[:, LANES:2 * LANES]

    _gather_block(qb_scr, q_scr, 2, gi, sb, nb)
    lane_head = lax.broadcasted_iota(jnp.int32, (PAIR_ROWS, D_X), 1) >> 6
    r_s = lax.broadcasted_iota(jnp.int32, (n_rows, 2 * N_MEM), 0)
    c_s = lax.broadcasted_iota(jnp.int32, (n_rows, 2 * N_MEM), 1)
    own = (r_s & 1) == (c_s >> 8)
    pair_rows = lambda p: pl.ds(p, PAIR_ROWS, stride=half)
    for p in range(half):
        q8 = jnp.concatenate([qb_scr[0, pair_rows(p), :], qb_scr[1, pair_rows(p), :]], axis=1)
        zero = jnp.zeros_like(q8)
        q4 = jnp.concatenate([jnp.where(lane_head == h, q8, zero) for h in range(N_XH)], axis=0).astype(BF16)
        kt2 = jnp.concatenate([mk_ref[p], mk_ref[p + half]], axis=1).astype(BF16)
        s_scr[n_rows * p:n_rows * (p + 1), :] = jnp.where(own, _dot(q4, kt2), MASK_VALUE)
    s_all = s_scr[...]
    m = jnp.max(s_all, axis=-1, keepdims=True)
    pr = jnp.exp(s_all - m)
    p_scr[...] = (pr * (1.0 / jnp.sum(pr, axis=-1, keepdims=True))).astype(BF16)
    for p in range(half):
        vt2 = jnp.concatenate([mv_ref[p], mv_ref[p + half]], axis=1).astype(BF16)
        o_all = _nt_dot(p_scr[n_rows * p:n_rows * (p + 1), :], vt2)
        o8 = jnp.zeros((PAIR_ROWS, D_X), F32)
        for h in range(N_XH):
            o8 = jnp.where(lane_head == h, o_all[h * PAIR_ROWS:(h + 1) * PAIR_ROWS, :], o8)
        ob_scr[0, pair_rows(p), :] = o8[:, 0:LANES]
        ob_scr[1, pair_rows(p), :] = o8[:, LANES:2 * LANES]
    _scatter_block(att_scr, ob_scr, 2, gi, sb, nb)

    @pl.when(gi == pl.num_programs(0) - 1)
    def _():
        y = _dot(att_scr[...].astype(BF16), wo_ref[...])
        o_ref[...] = _layer_norm(ALPHA * xt_ref[...] + y, g_ref[...], b_ref[...])


def _xattn_sample3_call(l, xt, mk, mv, w):
    m = xt.shape[0]
    nb = m // DEC_SEQ
    sb = SEQ_BLOCK
    n_sc = N_XH * PAIR_ROWS * (sb // 2)
    const2 = lambda i: (0, 0)
    return pl.pallas_call(
        _xattn_sample3_kernel,
        name="xattn_sample",
        grid=(nb // sb,),
        in_specs=[pl.BlockSpec((m, D_MODEL), const2, pipeline_mode=pl.Buffered(1)),
                  pl.BlockSpec((None, sb, D_X, N_MEM), lambda i: (l, i, 0, 0)),
                  pl.BlockSpec((None, sb, D_X, N_MEM), lambda i: (l, i, 0, 0)),
                  _full_spec((D_MODEL, D_X)), _full_spec((D_X, D_MODEL)),
                  _full_spec((1, D_MODEL)), _full_spec((1, D_MODEL))],
        out_specs=pl.BlockSpec((m, D_MODEL), const2),
        out_shape=jax.ShapeDtypeStruct((m, D_MODEL), F32),
        scratch_shapes=[pltpu.VMEM((2, m, LANES), F32),
                        pltpu.VMEM((2, sb * DEC_SEQ, LANES), F32),
                        pltpu.VMEM((2, sb * DEC_SEQ, LANES), F32),
                        pltpu.VMEM((m, D_X), F32),
                        pltpu.VMEM((n_sc, 2 * N_MEM), F32),
                        pltpu.VMEM((n_sc, 2 * N_MEM), BF16)],
        compiler_params=_compiler_params(("arbitrary",)),
    )(xt, mk, mv, w["w_xq"], w["w_xo"], w["ln2_g"], w["ln2_b"])


def _sample_layer(l, xt, ck, cv, s, cs_tm, pb_tm, mk, mv, w, prev):
    xt, kn, vn, sn, cn, pn = _sample_mixer3_call(l, xt, ck, cv, s, cs_tm, pb_tm, w, prev)
    xt = _xattn_sample3_call(l, xt, mk, mv, w)
    xt = _ffn_call(xt, w["w_gate"], w["w_up"], w["w_down"], w["ln3_g"], w["ln3_b"])
    return xt, kn, vn, sn, cn, pn


def _row(v):
    return v.reshape(1, -1).astype(F32)


def _pool_block_diag(pool_w):
    z = jnp.zeros((W_GRP, W_GRP), pool_w.dtype)
    for gi in range(4):
        z = lax.dynamic_update_slice(z, pool_w[gi], (gi * HEAD_DIM, gi * HEAD_DIM))
    return z


def _hgrn_lower_bounds(lb_param):
    p = jax.nn.softmax(lb_param.astype(F32), axis=0)
    return jnp.cumsum(p, axis=0) - p[0:1]


def _prompt_layer(x, mem2d, w):
    bsz = x.shape[0]
    mk = _matmul_call(mem2d, w["w_xk"]).reshape(bsz, N_MEM, D_X)
    mv = _matmul_call(mem2d, w["w_xv"]).reshape(bsz, N_MEM, D_X)
    x, kn, vn, sn, cn, pn = _prompt_mixer_call(
        x, w["w_in"], w["sink"], w["lb"], w["ng"], w["cw"], w["pw"], w["ps"], w["w_o"], w["ln1_g"], w["ln1_b"])
    x = _xattn_prompt_call(x, mk, mv, w["w_xq"], w["w_xo"], w["ln2_g"], w["ln2_b"])
    t = x.shape[1]
    x = _ffn_call(x.reshape(bsz * t, D_MODEL), w["w_gate"], w["w_up"], w["w_down"],
                  w["ln3_g"], w["ln3_b"]).reshape(bsz, t, D_MODEL)
    return x, kn, vn, sn, cn, pn, mk, mv


def _layer_weights(l, lb_all, w_in, attn_sink, hgrn_norm_g, conv_w, pool_w, pool_scale, w_o, ln1_g, ln1_b,
                   w_xq, w_xk, w_xv, w_xo, ln2_g, ln2_b, w_gate, w_up, w_down, ln3_g, ln3_b):
    bf = lambda a: a[l].astype(BF16)
    return dict(
        w_in=bf(w_in), w_hgt=w_in[l][:, C_BQ:C_CB].T.astype(BF16), w_kvt=w_in[l][:, C_AK:C_BQ].T.astype(BF16),
        sink=attn_sink[l].astype(F32), lb=_row(lb_all[l]), lb_col=lb_all[l].reshape(-1, 1).astype(F32),
        ng=_row(hgrn_norm_g[l]), ng_col=hgrn_norm_g[l].reshape(-1, 1).astype(F32),
        cw=conv_w[l].astype(F32), pw=_pool_block_diag(pool_w[l]).astype(BF16), ps=_row(pool_scale[l]),
        w_o=bf(w_o), ln1_g=_row(ln1_g[l]), ln1_b=_row(ln1_b[l]),
        w_xq=bf(w_xq), w_xk=bf(w_xk), w_xv=bf(w_xv), w_xo=bf(w_xo), ln2_g=_row(ln2_g[l]), ln2_b=_row(ln2_b[l]),
        w_gate=bf(w_gate), w_up=bf(w_up), w_down=bf(w_down), ln3_g=_row(ln3_g[l]), ln3_b=_row(ln3_b[l]))


def kernel(x_prompt, x_sample, cache_swa_k, cache_swa_v, state_hgrn, state_conv, state_pool, cache_mem_k,
           cache_mem_v, mem_prompt, emb_ln_g, emb_ln_b, w_in, attn_sink, hgrn_lb, hgrn_norm_g, conv_w, pool_w,
           pool_scale, w_o, ln1_g, ln1_b, w_xq, w_xk, w_xv, w_xo, ln2_g, ln2_b, w_gate, w_up, w_down, ln3_g,
           ln3_b):
    bp, t, _ = x_prompt.shape
    lb_all = _hgrn_lower_bounds(hgrn_lb)
    hp = _ln_call(x_prompt.reshape(bp * t, D_MODEL), _row(emb_ln_g), _row(emb_ln_b)).reshape(bp, t, D_MODEL)
    mem2d = mem_prompt.reshape(bp * N_MEM, D_MODEL)
    bs, ts, _ = x_sample.shape
    hs = _ln_call(x_sample.transpose(1, 0, 2).reshape(ts * bs, D_MODEL), _row(emb_ln_g), _row(emb_ln_b))
    ck = cache_swa_k.transpose(0, 1, 3, 4, 2).reshape(DEPTH, bs, LANES, WINDOW)
    cv = cache_swa_v.transpose(0, 1, 3, 4, 2).reshape(DEPTH, bs, LANES, WINDOW)
    st = state_hgrn.transpose(0, 2, 3, 4, 1).reshape(DEPTH, HG_ROWS, bs)
    cs_tm = state_conv.transpose(0, 2, 1, 3)
    pb_tm = state_pool.transpose(0, 2, 1, 3)
    mk_s = cache_mem_k.transpose(0, 1, 3, 4, 2).reshape(DEPTH, bs, D_X, N_MEM)
    mv_s = cache_mem_v.transpose(0, 1, 3, 4, 2).reshape(DEPTH, bs, D_X, N_MEM)
    outs = [[] for _ in range(7)]
    souts = [[] for _ in range(2)]
    prev = None
    for l in range(DEPTH):
        w = _layer_weights(l, lb_all, w_in, attn_sink, hgrn_norm_g, conv_w, pool_w, pool_scale, w_o, ln1_g,
                           ln1_b, w_xq, w_xk, w_xv, w_xo, ln2_g, ln2_b, w_gate, w_up, w_down, ln3_g, ln3_b)
        res = _prompt_layer(hp, mem2d, w)
        hp = res[0]
        for acc, r in zip(outs, res[1:]):
            acc.append(r)
        sres = _sample_layer(l, hs, ck, cv, st, cs_tm, pb_tm, mk_s, mv_s, w, prev)
        hs = sres[0]
        prev = sres[1:4]
        for acc, r in zip(souts, sres[4:]):
            acc.append(r)
    pk, pv, ps, pc, pp, pmk, pmv = [jnp.stack(o) for o in outs]
    sk, sv, ss = prev
    sc, sp = [jnp.stack(o) for o in souts]
    swa_out = lambda a: a.reshape(DEPTH, bs, N_KV, HEAD_DIM, WINDOW).transpose(0, 1, 4, 2, 3)
    return (hp, hs.reshape(ts, bs, D_MODEL).transpose(1, 0, 2),
            pk.reshape(DEPTH, bp, WINDOW, N_KV, HEAD_DIM), pv.reshape(DEPTH, bp, WINDOW, N_KV, HEAD_DIM),
            ps.reshape(DEPTH, bp, N_HG, HEAD_DIM, HEAD_DIM), pc, pp,
            pmk.reshape(DEPTH, bp, N_MEM, N_XH, HEAD_DIM), pmv.reshape(DEPTH, bp, N_MEM, N_XH, HEAD_DIM),
            swa_out(sk), swa_out(sv),
            ss.reshape(DEPTH, N_HG, HEAD_DIM, HEAD_DIM, bs).transpose(0, 4, 1, 2, 3),
            sc.transpose(0, 2, 1, 3), sp.transpose(0, 2, 1, 3))
```

```python
import functools

import jax
import jax.numpy as jnp
from jax import lax
from jax.experimental import pallas as pl
from jax.experimental.pallas import tpu as pltpu

F32 = jnp.float32
BF16 = jnp.bfloat16

D_MODEL = 1024
DEPTH = 4
HEAD_DIM = 64
W_GRP = 256
N_KV = 2
WINDOW = 128
N_HG = 4
CONV_W = 3
POOL_WINDOWS = (2, 4, 8, 16)
POOL_BUF = 15
N_MEM = 256
N_XH = 4
D_X = 256
D_FF = 2816
D_IN = 2560
DEC_SEQ = 4
ALPHA = (2 * DEPTH) ** 0.25
LN_EPS = 1e-5
RMS_EPS = 1e-6
MASK_VALUE = -1e30
LB_FLOOR = 1e-30
PAST_LEN = 8192

C_AQ, C_AK, C_AV = 0, 256, 384
C_BQ, C_BF, C_BI, C_BG = 512, 768, 1024, 1280
C_CB, C_CC, C_CH = 1536, 1792, 2048
C_DV = 2304

LANES = 128
SUBLANES = 8
VMEM_LIMIT_BYTES = 56 * 1024 * 1024

TOKEN_BLOCK = 512
SUB_BLOCKS = 2
HG_CHUNK = 64
HG_MID = HG_CHUNK // 2 - 1
HG_ROWS = N_HG * HEAD_DIM * HEAD_DIM
SEQ_BLOCK = 16
PAIR_ROWS = 2 * DEC_SEQ
K_PER_STEP = HG_ROWS // HEAD_DIM * SEQ_BLOCK // 128


def _nt_dot(a, b):
    return lax.dot_general(a, b, (((1,), (1,)), ((), ())), preferred_element_type=F32)


def _tn_dot(a, b):
    return lax.dot_general(a, b, (((0,), (0,)), ((), ())), preferred_element_type=F32)


def _dot(a, b):
    return jnp.dot(a, b, preferred_element_type=F32)


def _layer_norm(x, g, b):
    mu = jnp.mean(x, axis=-1, keepdims=True)
    xc = x - mu
    var = jnp.mean(xc * xc, axis=-1, keepdims=True)
    return xc * lax.rsqrt(var + LN_EPS) * g + b


def _sigmoid_pair(z):
    e = jnp.exp(-jnp.abs(z))
    inv = 1.0 / (1.0 + e)
    small = e * inv
    pos = z >= 0
    return jnp.where(pos, inv, small), jnp.where(pos, small, inv)


def _silu(z):
    s, _ = _sigmoid_pair(z)
    return z * s


def _silu_tanh(z):
    return z * (0.5 + 0.5 * jnp.tanh(0.5 * z))


def _full_spec(shape):
    nd = len(shape)
    return pl.BlockSpec(shape, lambda *_: (0,) * nd)


def _layer_spec(shape, l, single=False):
    nd = len(shape)
    index = lambda *_: (l,) + (0,) * nd
    if single:
        return pl.BlockSpec((None,) + tuple(shape), index, pipeline_mode=pl.Buffered(1))
    return pl.BlockSpec((None,) + tuple(shape), index)


def _compiler_params(sem):
    return pltpu.CompilerParams(dimension_semantics=sem, vmem_limit_bytes=VMEM_LIMIT_BYTES)


class _MatmulQueue:
    def __init__(self):
        self.items = []

    def add(self, fn, *args):
        self.items.append((fn, args))

    def issue(self, n=1):
        for _ in range(min(n, len(self.items))):
            fn, args = self.items.pop(0)
            fn(*args)

    def flush(self):
        self.issue(len(self.items))


def _ln_kernel(x_ref, g_ref, b_ref, o_ref):
    o_ref[...] = _layer_norm(x_ref[...], g_ref[...], b_ref[...])


def _ln_call(x2d, g, b):
    m = x2d.shape[0]
    return pl.pallas_call(
        _ln_kernel,
        name="input_ln",
        grid=(m // TOKEN_BLOCK,),
        in_specs=[pl.BlockSpec((TOKEN_BLOCK, D_MODEL), lambda i: (i, 0)),
                  _full_spec((1, D_MODEL)), _full_spec((1, D_MODEL))],
        out_specs=pl.BlockSpec((TOKEN_BLOCK, D_MODEL), lambda i: (i, 0)),
        out_shape=jax.ShapeDtypeStruct((m, D_MODEL), F32),
        compiler_params=_compiler_params(("arbitrary",)),
    )(x2d, g, b)


def _matmul_kernel(x_ref, w_ref, o_ref):
    o_ref[...] = _dot(x_ref[...].astype(BF16), w_ref[...])


def _matmul_call(x2d, w):
    m, k = x2d.shape
    n = w.shape[1]
    return pl.pallas_call(
        _matmul_kernel,
        name="mem_proj",
        grid=(m // TOKEN_BLOCK,),
        in_specs=[pl.BlockSpec((TOKEN_BLOCK, k), lambda i: (i, 0)), _full_spec((k, n))],
        out_specs=pl.BlockSpec((TOKEN_BLOCK, n), lambda i: (i, 0)),
        out_shape=jax.ShapeDtypeStruct((m, n), F32),
        compiler_params=_compiler_params(("arbitrary",)),
    )(x2d, w)


def _ffn_kernel(x_ref, wg_ref, wu_ref, wd_ref, g_ref, b_ref, o_ref):
    x = x_ref[...]
    xb = x.astype(BF16)
    h = _silu(_dot(xb, wg_ref[...])) * _dot(xb, wu_ref[...])
    y = _dot(h.astype(BF16), wd_ref[...])
    o_ref[...] = _layer_norm(ALPHA * x + y, g_ref[...], b_ref[...])


def _ffn_call(l, x2d, w):
    m = x2d.shape[0]
    return pl.pallas_call(
        _ffn_kernel,
        name="ffn",
        grid=(m // TOKEN_BLOCK,),
        in_specs=[pl.BlockSpec((TOKEN_BLOCK, D_MODEL), lambda i: (i, 0)),
                  _layer_spec((D_MODEL, D_FF), l, single=True),
                  _layer_spec((D_MODEL, D_FF), l, single=True),
                  _layer_spec((D_FF, D_MODEL), l, single=True),
                  _layer_spec((1, D_MODEL), l), _layer_spec((1, D_MODEL), l)],
        out_specs=pl.BlockSpec((TOKEN_BLOCK, D_MODEL), lambda i: (i, 0)),
        out_shape=jax.ShapeDtypeStruct((m, D_MODEL), F32),
        compiler_params=_compiler_params(("arbitrary",)),
    )(x2d, w["w_gate"], w["w_up"], w["w_down"], w["ln3_g"], w["ln3_b"])


def _xattn_prompt_kernel(x_ref, mk_ref, mv_ref, wq_ref, wo_ref, g_ref, b_ref, o_ref, q_scr, att_scr, yo_scr):
    tb = x_ref.shape[0]
    hb = tb // SUB_BLOCKS
    mk = mk_ref[...].astype(BF16)
    mv = mv_ref[...].astype(BF16)
    lane_head = lax.broadcasted_iota(jnp.int32, (hb, D_X), 1) >> 6
    sub = lambda r: slice(r * hb, (r + 1) * hb)

    def project(r):
        q_scr[sub(r), :] = _dot(x_ref[sub(r), :].astype(BF16), wq_ref[...]) * (HEAD_DIM ** -0.5)

    def finish(r, c0, c1):
        yo_scr[sub(r), c0:c1] = _dot(att_scr[sub(r), :], wo_ref[:, c0:c1])

    def finish_ln(r):
        o_ref[sub(r), :] = _layer_norm(ALPHA * x_ref[sub(r), :] + yo_scr[sub(r), :], g_ref[...], b_ref[...])

    out_cols = ((0, D_MODEL // 2), (D_MODEL // 2, D_MODEL))
    project(0)
    queue = _MatmulQueue()
    for r in range(SUB_BLOCKS):
        if r >= 1:
            for cols in out_cols:
                queue.add(finish, r - 1, *cols)
        if r + 1 < SUB_BLOCKS:
            queue.add(project, r + 1)
        q = q_scr[sub(r), :]
        o = jnp.zeros_like(q)
        for h in range(N_XH):
            qh = jnp.where(lane_head == h, q, 0.0).astype(BF16)
            s = _nt_dot(mk, qh)
            m = jnp.max(s, axis=0, keepdims=True)
            p = jnp.exp(s - m)
            den = jnp.sum(p, axis=0, keepdims=True)
            p = (p * (1.0 / den)).astype(BF16)
            o = jnp.where(lane_head == h, _tn_dot(p, mv), o)
            queue.issue()
        att_scr[sub(r), :] = o.astype(BF16)
        queue.flush()
        if r >= 1:
            finish_ln(r - 1)
    for cols in out_cols:
        finish(SUB_BLOCKS - 1, *cols)
    finish_ln(SUB_BLOCKS - 1)


def _xattn_prompt_call(l, x, mkv, w):
    bsz, t, _ = x.shape
    tb = min(TOKEN_BLOCK, t)
    return pl.pallas_call(
        _xattn_prompt_kernel,
        name="xattn_prompt",
        grid=(bsz, t // tb),
        in_specs=[pl.BlockSpec((None, tb, D_MODEL), lambda i, j: (i, j, 0)),
                  pl.BlockSpec((None, N_MEM, D_X), lambda i, j: (i, 0, 2 * l)),
                  pl.BlockSpec((None, N_MEM, D_X), lambda i, j: (i, 0, 2 * l + 1)),
                  _layer_spec((D_MODEL, D_X), l), _layer_spec((D_X, D_MODEL), l),
                  _layer_spec((1, D_MODEL), l), _layer_spec((1, D_MODEL), l)],
        out_specs=pl.BlockSpec((None, tb, D_MODEL), lambda i, j: (i, j, 0)),
        out_shape=jax.ShapeDtypeStruct((bsz, t, D_MODEL), F32),
        scratch_shapes=[pltpu.VMEM((tb, D_X), F32), pltpu.VMEM((tb, D_X), BF16), pltpu.VMEM((tb, D_MODEL), F32)],
        compiler_params=_compiler_params(("arbitrary", "arbitrary")),
    )(x, mkv, mkv, w["w_xq"], w["w_xo"], w["ln2_g"], w["ln2_b"])


def _swa_bias_table():
    c = lax.broadcasted_iota(jnp.int32, (2 * WINDOW, 4 * WINDOW), 0)
    r = lax.broadcasted_iota(jnp.int32, (2 * WINDOW, 4 * WINDOW), 1)
    head = r >> 7
    rel = (r & (WINDOW - 1)) + WINDOW - c
    slope = jnp.exp2(-2.0 * (head.astype(F32) + 1.0))
    valid = (rel >= 0) & (rel <= WINDOW)
    return jnp.where(valid, -slope * rel.astype(F32), MASK_VALUE)


def _sink_select(sink_ref, l, head):
    return jnp.where(head == 0, sink_ref[l, 0],
                     jnp.where(head == 1, sink_ref[l, 1], jnp.where(head == 2, sink_ref[l, 2], sink_ref[l, 3])))


def _prompt_mixer_kernel(l, x_ref, w_in_ref, sink_ref, lb_ref, ng_ref, cw_ref, pw_ref, ps_ref, w_o_ref,
                         g_ref, b_ref,
                         y_ref, knew_ref, vnew_ref, snew_ref, cnew_ref, pnew_ref,
                         proj_scr, kext_scr, vext_scr, st_scr, u_scr, p_scr, bias_scr, mix_scr, hg_scr,
                         hq_scr, hk_scr, hv_scr, ghl_scr, cum_scr, qp_scr, kp_scr, qs_scr, ks_scr, dec_scr, inc_scr,
                         stb_scr, a_scr, yo_scr):
    tb = x_ref.shape[0]
    hb = tb // SUB_BLOCKS
    n_qb = hb // WINDOW
    n_ch = hb // HG_CHUNK
    bi = pl.program_id(0)
    ti = pl.program_id(1)
    last = ti == pl.num_programs(1) - 1

    @pl.when((bi == 0) & (ti == 0))
    def _():
        bias_scr[...] = _swa_bias_table()

    @pl.when(ti == 0)
    def _():
        kext_scr[0:WINDOW, :] = jnp.zeros((WINDOW, LANES), BF16)
        vext_scr[0:WINDOW, :] = jnp.zeros((WINDOW, LANES), BF16)
        st_scr[...] = jnp.zeros(st_scr.shape, F32)
        u_scr[0:SUBLANES, :] = jnp.zeros((SUBLANES, W_GRP), F32)
        p_scr[0:16, :] = jnp.zeros((16, W_GRP), F32)

    def project(r, c0, c1):
        rows = slice(r * hb, (r + 1) * hb)
        proj_scr[rows, c0:c1] = _dot(x_ref[rows, :].astype(BF16), w_in_ref[:, c0:c1])

    def finish(r, c0, c1):
        rows = slice(r * hb, (r + 1) * hb)
        yo_scr[rows, c0:c1] = _dot(mix_scr[rows, :], w_o_ref[:, c0:c1])

    def finish_ln(r):
        rows = slice(r * hb, (r + 1) * hb)
        y_ref[rows, :] = _layer_norm(ALPHA * x_ref[rows, :] + yo_scr[rows, :], g_ref[...], b_ref[...])

    proj_cols = ((0, C_BQ), (C_BQ, C_BI), (C_BI, C_CB), (C_CB, C_CH), (C_CH, D_IN))
    out_cols = ((0, D_MODEL // 2), (D_MODEL // 2, D_MODEL))
    for cols in proj_cols:
        project(0, *cols)
    queue = _MatmulQueue()

    lane = lax.broadcasted_iota(jnp.int32, (WINDOW, LANES), 1)
    lo = lane < HEAD_DIM
    key_row = lax.broadcasted_iota(jnp.int32, (2 * WINDOW, 4 * WINDOW), 0)
    sink_row = _sink_select(sink_ref, l, lax.broadcasted_iota(jnp.int32, (1, 4 * WINDOW), 1) >> 7)
    lb = lb_ref[...]
    lbf = jnp.maximum(lb, LB_FLOOR)
    one_m_lb = 1.0 - lb
    ng = ng_ref[...]
    r256 = lax.broadcasted_iota(jnp.int32, (W_GRP, W_GRP), 0)
    c256 = lax.broadcasted_iota(jnp.int32, (W_GRP, W_GRP), 1)
    same_head = (r256 >> 6) == (c256 >> 6)
    head_ones = jnp.where(same_head, 1.0, 0.0).astype(BF16)
    same_head_b = head_ones > 0
    zero_b = jnp.zeros((W_GRP, W_GRP), BF16)
    rc = lax.broadcasted_iota(jnp.int32, (HG_CHUNK, W_GRP), 0)
    cc = lax.broadcasted_iota(jnp.int32, (HG_CHUNK, W_GRP), 1)
    causal = (cc & (HG_CHUNK - 1)) <= rc
    r64 = lax.broadcasted_iota(jnp.int32, (HG_CHUNK, HG_CHUNK), 0)
    c64 = lax.broadcasted_iota(jnp.int32, (HG_CHUNK, HG_CHUNK), 1)
    tril = jnp.where(c64 <= r64, 1.0, 0.0).astype(BF16)
    cw = cw_ref[...]
    rp = lax.broadcasted_iota(jnp.int32, (hb, W_GRP), 0)
    grp = lax.broadcasted_iota(jnp.int32, (hb, W_GRP), 1) >> 6
    width = jnp.left_shift(2, grp)
    chunk = lambda c: slice(c * HG_CHUNK, (c + 1) * HG_CHUNK)

    for r in range(SUB_BLOCKS):
        r0 = r * hb
        sub = slice(r0, r0 + hb)
        if r >= 1:
            for cols in out_cols:
                queue.add(finish, r - 1, *cols)
        if r + 1 < SUB_BLOCKS:
            for cols in proj_cols:
                queue.add(project, r + 1, *cols)

        kext_scr[WINDOW:WINDOW + hb, :] = proj_scr[sub, C_AK:C_AK + LANES].astype(BF16)
        vext_scr[WINDOW:WINDOW + hb, :] = proj_scr[sub, C_AV:C_AV + LANES].astype(BF16)
        for j in range(n_qb):
            rows = slice(r0 + j * WINDOW, r0 + (j + 1) * WINDOW)
            q0 = proj_scr[rows, 0:LANES] * (HEAD_DIM ** -0.5)
            q1 = proj_scr[rows, LANES:2 * LANES] * (HEAD_DIM ** -0.5)
            q0r = pltpu.roll(q0, HEAD_DIM, axis=1)
            q1r = pltpu.roll(q1, HEAD_DIM, axis=1)
            zero = jnp.zeros_like(q0)
            q4 = jnp.concatenate([jnp.where(lo, q0, zero), jnp.where(lo, q0r, zero),
                                  jnp.where(lo, zero, q1r), jnp.where(lo, zero, q1)], axis=0).astype(BF16)
            kj = kext_scr[j * WINDOW:(j + 2) * WINDOW, :]
            vj = vext_scr[j * WINDOW:(j + 2) * WINDOW, :]
            s = _nt_dot(kj, q4) + bias_scr[...]
            if r == 0 and j == 0:
                s = jnp.where((ti == 0) & (key_row < WINDOW), MASK_VALUE, s)
            m = jnp.maximum(jnp.max(s, axis=0, keepdims=True), sink_row)
            p = jnp.exp(s - m)
            den = jnp.sum(p, axis=0, keepdims=True) + jnp.exp(sink_row - m)
            p = (p * (1.0 / den)).astype(BF16)
            o_all = _tn_dot(p, vj)
            o = [o_all[h * WINDOW:(h + 1) * WINDOW, :] for h in range(4)]
            mix_scr[rows, 0:LANES] = jnp.where(lo, o[0], pltpu.roll(o[1], HEAD_DIM, axis=1)).astype(BF16)
            mix_scr[rows, LANES:2 * LANES] = jnp.where(lo, pltpu.roll(o[2], HEAD_DIM, axis=1), o[3]).astype(BF16)
            queue.issue()
        kext_scr[0:WINDOW, :] = kext_scr[hb:hb + WINDOW, :]
        vext_scr[0:WINDOW, :] = vext_scr[hb:hb + WINDOW, :]

        hq_scr[...] = _silu_tanh(proj_scr[sub, C_BQ:C_BQ + W_GRP])
        queue.issue()
        sig_pos, sig_neg = _sigmoid_pair(proj_scr[sub, C_BF:C_BF + W_GRP])
        g = jnp.log(sig_pos + lbf * sig_neg)
        hk_scr[...] = one_m_lb * sig_neg
        g_hi = g.astype(BF16)
        ghl_scr[:, 0:W_GRP] = g_hi
        ghl_scr[:, W_GRP:2 * W_GRP] = (g - g_hi.astype(F32)).astype(BF16)
        hv_scr[...] = proj_scr[sub, C_BI:C_BI + W_GRP].astype(BF16)
        for c in range(n_ch):
            cum2 = _dot(tril, ghl_scr[chunk(c), :])
            cum_scr[chunk(c), :] = cum2[:, 0:W_GRP] + cum2[:, W_GRP:2 * W_GRP]
        for c in range(n_ch):
            cum = cum_scr[chunk(c), :]
            ref = cum[HG_MID:HG_MID + 1, :]
            tot = cum[HG_CHUNK - 1:HG_CHUNK, :]
            qp = hq_scr[chunk(c), :] * jnp.exp(cum - ref)
            kp = hk_scr[chunk(c), :] * jnp.exp(ref - cum)
            qp_scr[chunk(c), :] = qp.astype(BF16)
            kp_scr[chunk(c), :] = kp.astype(BF16)
            qs_scr[chunk(c), :] = (qp * jnp.exp(ref)).astype(BF16)
            ks_scr[chunk(c), :] = (kp * jnp.exp(tot - ref)).astype(BF16)
            dec_scr[c] = jnp.broadcast_to(jnp.exp(tot), (SUBLANES, W_GRP))
        queue.issue()
        for c in range(n_ch):
            bk = jnp.where(same_head_b, jnp.concatenate([kp_scr[chunk(c), :]] * N_HG, axis=0), zero_b)
            a = jnp.where(causal, _nt_dot(qp_scr[chunk(c), :], bk), 0.0)
            a_scr[chunk(c), :] = a.astype(BF16)
        for c in range(n_ch):
            inc_scr[c] = jnp.where(same_head, _tn_dot(hv_scr[chunk(c), :], ks_scr[chunk(c), :]), 0.0)
        for c in range(n_ch):
            bv = jnp.where(same_head_b, jnp.concatenate([hv_scr[chunk(c), :]] * N_HG, axis=0), zero_b)
            hg_scr[chunk(c), :] = _dot(a_scr[chunk(c), :], bv)
        st = st_scr[...]
        for c in range(n_ch):
            stb_scr[c] = st.astype(BF16)
            st = st * dec_scr[c, 0:1, :] + inc_scr[c]
        st_scr[...] = st
        queue.issue()
        for c in range(n_ch):
            hg_scr[chunk(c), :] = hg_scr[chunk(c), :] + _nt_dot(qs_scr[chunk(c), :], stb_scr[c])
        o = hg_scr[...]
        ms = _dot((o * o).astype(BF16), head_ones) * (1.0 / HEAD_DIM)
        o = o * lax.rsqrt(ms + RMS_EPS) * ng
        mix_scr[sub, W_GRP:2 * W_GRP] = (o * _silu_tanh(proj_scr[sub, C_BG:C_BG + W_GRP])).astype(BF16)
        queue.issue()

        u_scr[SUBLANES:SUBLANES + hb, :] = proj_scr[sub, C_CC:C_CC + W_GRP] * proj_scr[sub, C_CH:C_CH + W_GRP]
        yc = (u_scr[SUBLANES - 2:SUBLANES - 2 + hb, :] * cw[0:1, :]
              + u_scr[SUBLANES - 1:SUBLANES - 1 + hb, :] * cw[1:2, :]
              + u_scr[SUBLANES:SUBLANES + hb, :] * cw[2:3, :])
        mix_scr[sub, 2 * W_GRP:3 * W_GRP] = (proj_scr[sub, C_CB:C_CB + W_GRP] * yc).astype(BF16)
        u_scr[0:SUBLANES, :] = u_scr[hb:hb + SUBLANES, :]
        queue.issue()

        dv = proj_scr[sub, C_DV:C_DV + W_GRP]
        p_scr[16:16 + hb, :] = dv
        ext = p_scr[...]
        s2 = ext + pltpu.roll(ext, 1, axis=0)
        s4 = s2 + pltpu.roll(s2, 2, axis=0)
        s8 = s4 + pltpu.roll(s4, 4, axis=0)
        s16 = s8 + pltpu.roll(s8, 8, axis=0)
        win = jnp.where(grp == 0, s2[16:], jnp.where(grp == 1, s4[16:], jnp.where(grp == 2, s8[16:], s16[16:])))
        cnt = jnp.minimum(ti * tb + r0 + rp + 1, width).astype(F32)
        pooled = win / cnt - dv
        yd = _dot(pooled.astype(BF16), pw_ref[...]) * ps_ref[...]
        mix_scr[sub, 3 * W_GRP:4 * W_GRP] = yd.astype(BF16)
        p_scr[0:16, :] = p_scr[hb:hb + 16, :]
        queue.flush()
        if r >= 1:
            finish_ln(r - 1)

    for cols in out_cols:
        finish(SUB_BLOCKS - 1, *cols)
    finish_ln(SUB_BLOCKS - 1)

    @pl.when(last)
    def _():
        knew_ref[...] = proj_scr[tb - WINDOW:tb, C_AK:C_AK + LANES]
        vnew_ref[...] = proj_scr[tb - WINDOW:tb, C_AV:C_AV + LANES]
        s_t = st_scr[...].T
        for h in range(N_HG):
            snew_ref[h * HEAD_DIM:(h + 1) * HEAD_DIM, :] = (
                s_t[h * HEAD_DIM:(h + 1) * HEAD_DIM, h * HEAD_DIM:(h + 1) * HEAD_DIM])
        cnew_ref[...] = u_scr[SUBLANES - 2:SUBLANES, :]
        pnew_ref[...] = p_scr[1:16, :]


def _prompt_mixer_call(l, x, w):
    bsz, t, _ = x.shape
    tb = min(TOKEN_BLOCK, t)
    hb = tb // SUB_BLOCKS
    n_ch = hb // HG_CHUNK
    row = lambda i, j: (i, 0, 0)
    out_shape = (
        jax.ShapeDtypeStruct((bsz, t, D_MODEL), F32),
        jax.ShapeDtypeStruct((bsz, WINDOW, LANES), F32),
        jax.ShapeDtypeStruct((bsz, WINDOW, LANES), F32),
        jax.ShapeDtypeStruct((bsz, W_GRP, HEAD_DIM), F32),
        jax.ShapeDtypeStruct((bsz, CONV_W - 1, W_GRP), F32),
        jax.ShapeDtypeStruct((bsz, POOL_BUF, W_GRP), F32),
    )
    f32 = lambda *shape: pltpu.VMEM(shape, F32)
    bf16 = lambda *shape: pltpu.VMEM(shape, BF16)
    return pl.pallas_call(
        functools.partial(_prompt_mixer_kernel, l),
        name="mixer_prompt",
        grid=(bsz, t // tb),
        in_specs=[pl.BlockSpec((None, tb, D_MODEL), lambda i, j: (i, j, 0)),
                  _layer_spec((D_MODEL, D_IN), l, single=True),
                  pl.BlockSpec(memory_space=pltpu.SMEM),
                  _layer_spec((1, W_GRP), l), _layer_spec((1, W_GRP), l), _layer_spec((CONV_W, W_GRP), l),
                  _layer_spec((W_GRP, W_GRP), l), _layer_spec((1, W_GRP), l),
                  _layer_spec((D_MODEL, D_MODEL), l, single=True),
                  _layer_spec((1, D_MODEL), l), _layer_spec((1, D_MODEL), l)],
        out_specs=(pl.BlockSpec((None, tb, D_MODEL), lambda i, j: (i, j, 0)),
                   pl.BlockSpec((None, WINDOW, LANES), row),
                   pl.BlockSpec((None, WINDOW, LANES), row),
                   pl.BlockSpec((None, W_GRP, HEAD_DIM), row),
                   pl.BlockSpec((None, CONV_W - 1, W_GRP), row),
                   pl.BlockSpec((None, POOL_BUF, W_GRP), row)),
        out_shape=out_shape,
        scratch_shapes=[
            f32(tb, D_IN),
            bf16(WINDOW + hb, LANES),
            bf16(WINDOW + hb, LANES),
            f32(W_GRP, W_GRP),
            f32(SUBLANES + hb, W_GRP),
            f32(16 + hb, W_GRP),
            f32(2 * WINDOW, 4 * WINDOW),
            bf16(tb, D_MODEL),
            f32(hb, W_GRP),
            f32(hb, W_GRP), f32(hb, W_GRP),
            bf16(hb, W_GRP),
            bf16(hb, 2 * W_GRP),
            f32(hb, W_GRP),
            bf16(hb, W_GRP), bf16(hb, W_GRP),
            bf16(hb, W_GRP), bf16(hb, W_GRP),
            f32(n_ch, SUBLANES, W_GRP),
            f32(n_ch, W_GRP, W_GRP),
            bf16(n_ch, W_GRP, W_GRP),
            bf16(hb, W_GRP),
            f32(tb, D_MODEL),
        ],
        compiler_params=_compiler_params(("arbitrary", "arbitrary")),
    )(x, w["w_in"], w["sink"], w["lb"], w["ng"], w["cw"], w["pw"], w["ps"], w["w_o"], w["ln1_g"], w["ln1_b"])


def _gather_block(dst_scr, src_scr, n_tiles, gi, sb, nb):
    for t in range(DEC_SEQ):
        start = pl.multiple_of(t * nb + gi * sb, sb)
        for j in range(n_tiles):
            dst_scr[j, t * sb:(t + 1) * sb, :] = src_scr[j, pl.ds(start, sb), :]


def _scatter_block(dst_scr, src_scr, n_tiles, gi, sb, nb):
    for t in range(DEC_SEQ):
        start = pl.multiple_of(t * nb + gi * sb, sb)
        for j in range(n_tiles):
            dst_scr[pl.ds(start, sb), j * LANES:(j + 1) * LANES] = src_scr[j, t * sb:(t + 1) * sb, :]


def _sample_mixer_kernel(l, xt_ref, ck_ref, cv_ref, s_ref, cs_ref, pb_ref,
                         w_in_ref, w_hgt_ref, w_kvt_ref, sink_ref, lbc_ref, ngc_ref, cw_ref, pw_ref, ps_ref,
                         w_o_ref, g_ref, b_ref, *rest):
    (y_ref, knew_ref, vnew_ref, snew_ref, cnew_ref, pnew_ref,
     proj_scr, a_scr, kvt_scr, q_scr, f_scr, k_scr, v_scr, gate_scr, o_scr, mix_scr,
     ab_scr, oa_scr, sc_scr, sn_scr, pc_scr, pn_scr) = rest[-22:]
    gi = pl.program_id(0)
    sb = ck_ref.shape[0]
    nb = xt_ref.shape[0] // DEC_SEQ
    half = sb // 2

    @pl.when(gi == 0)
    def _():
        xt = xt_ref[...].astype(BF16)
        pa = _dot(xt, w_in_ref[:, 0:C_BQ])
        for j in range(4):
            a_scr[j] = pa[:, j * LANES:(j + 1) * LANES]
        kvt_scr[...] = _nt_dot(w_kvt_ref[...], xt)
        hgt = _nt_dot(w_hgt_ref[...], xt)
        lbc = lbc_ref[...]
        q_scr[...] = _silu(hgt[0:W_GRP, :])
        sig_pos, sig_neg = _sigmoid_pair(hgt[W_GRP:2 * W_GRP, :])
        f_scr[...] = sig_pos + jnp.maximum(lbc, LB_FLOOR) * sig_neg
        k_scr[...] = (1.0 - lbc) * sig_neg
        v_scr[...] = hgt[2 * W_GRP:3 * W_GRP, :]
        gate_scr[...] = _silu(hgt[3 * W_GRP:4 * W_GRP, :])
        o_scr[...] = jnp.zeros(o_scr.shape, F32)
        proj_scr[...] = _dot(xt, w_in_ref[:, C_CB:D_IN])
        cw = cw_ref[...]
        ps = ps_ref[...]
        cp = lax.broadcasted_iota(jnp.int32, (nb, W_GRP), 1) >> 6
        width = jnp.left_shift(2, cp)
        u = [cs_ref[0], cs_ref[1]]
        ext = [pb_ref[i] for i in range(POOL_BUF)]
        for t in range(DEC_SEQ):
            rows = slice(t * nb, (t + 1) * nb)
            u.append(proj_scr[rows, W_GRP:2 * W_GRP] * proj_scr[rows, 2 * W_GRP:3 * W_GRP])
            ext.append(proj_scr[rows, 3 * W_GRP:4 * W_GRP])
        cnew_ref[0] = u[DEC_SEQ]
        cnew_ref[1] = u[DEC_SEQ + 1]
        for i in range(POOL_BUF):
            pnew_ref[i] = ext[DEC_SEQ + i]
        for t in range(DEC_SEQ):
            rows = slice(t * nb, (t + 1) * nb)
            yc = u[t] * cw[0:1, :] + u[t + 1] * cw[1:2, :] + u[t + 2] * cw[2:3, :]
            mix_scr[rows, 2 * W_GRP:3 * W_GRP] = proj_scr[rows, 0:W_GRP] * yc
            top = POOL_BUF + t
            acc = ext[top]
            sums = {}
            for jj in range(1, 16):
                acc = acc + ext[top - jj]
                if jj + 1 in POOL_WINDOWS:
                    sums[jj + 1] = acc
            win = jnp.where(cp == 0, sums[2], jnp.where(cp == 1, sums[4], jnp.where(cp == 2, sums[8], sums[16])))
            cnt = jnp.minimum(PAST_LEN + t + 1, width).astype(F32)
            pooled = win / cnt - ext[top]
            mix_scr[rows, 3 * W_GRP:4 * W_GRP] = _dot(pooled.astype(BF16), pw_ref[...]) * ps

    _gather_block(ab_scr, a_scr, 4, gi, sb, nb)
    lane = lax.broadcasted_iota(jnp.int32, (PAIR_ROWS, LANES), 1)
    lo = lane < HEAD_DIM
    n_rows = 4 * PAIR_ROWS
    r_c = lax.broadcasted_iota(jnp.int32, (n_rows, 2 * WINDOW), 0)
    c_c = lax.broadcasted_iota(jnp.int32, (n_rows, 2 * WINDOW), 1)
    rel_c = ((r_c >> 1) & 3) + WINDOW - (c_c & (WINDOW - 1))
    ok_c = ((r_c & 1) == (c_c >> 7)) & (rel_c <= WINDOW)
    bias_c = jnp.where(ok_c, -jnp.exp2(-2.0 * ((r_c >> 3).astype(F32) + 1.0)) * rel_c.astype(F32), MASK_VALUE)
    r_n = lax.broadcasted_iota(jnp.int32, (n_rows, PAIR_ROWS), 0)
    c_n = lax.broadcasted_iota(jnp.int32, (n_rows, PAIR_ROWS), 1)
    rel_n = ((r_n >> 1) & 3) - (c_n >> 1)
    ok_n = ((r_n & 1) == (c_n & 1)) & (rel_n >= 0)
    bias_n = jnp.where(ok_n, -jnp.exp2(-2.0 * ((r_n >> 3).astype(F32) + 1.0)) * rel_n.astype(F32), MASK_VALUE)
    pair_rows = lambda p: pl.ds(p, PAIR_ROWS, stride=half)
    for p in range(half):
        q0 = ab_scr[0, pair_rows(p), :] * (HEAD_DIM ** -0.5)
        q1 = ab_scr[1, pair_rows(p), :] * (HEAD_DIM ** -0.5)
        q0r = pltpu.roll(q0, HEAD_DIM, axis=1)
        q1r = pltpu.roll(q1, HEAD_DIM, axis=1)
        zero = jnp.zeros_like(q0)
        q4 = jnp.concatenate([jnp.where(lo, q0, zero), jnp.where(lo, q0r, zero),
                              jnp.where(lo, zero, q1r), jnp.where(lo, zero, q1)], axis=0).astype(BF16)
        kt2 = jnp.concatenate([ck_ref[p], ck_ref[p + half]], axis=1).astype(BF16)
        kn8 = ab_scr[2, pair_rows(p), :].astype(BF16)
        sc_scr[n_rows * p:n_rows * (p + 1), :] = _dot(q4, kt2) + bias_c
        sn_scr[n_rows * p:n_rows * (p + 1), :] = _nt_dot(q4, kn8) + bias_n
    sink_col = _sink_select(sink_ref, l, (lax.broadcasted_iota(jnp.int32, (n_rows * half, 1), 0) >> 3) & 3)
    s_c = sc_scr[...]
    s_n = sn_scr[...]
    m = jnp.maximum(jnp.maximum(jnp.max(s_c, axis=-1, keepdims=True), jnp.max(s_n, axis=-1, keepdims=True)),
                    sink_col)
    p_c = jnp.exp(s_c - m)
    p_n = jnp.exp(s_n - m)
    inv = 1.0 / (jnp.sum(p_c, axis=-1, keepdims=True) + jnp.sum(p_n, axis=-1, keepdims=True)
                 + jnp.exp(sink_col - m))
    pc_scr[...] = (p_c * inv).astype(BF16)
    pn_scr[...] = (p_n * inv).astype(BF16)
    for p in range(half):
        vt2 = jnp.concatenate([cv_ref[p], cv_ref[p + half]], axis=1).astype(BF16)
        vn8 = ab_scr[3, pair_rows(p), :].astype(BF16)
        o_all = (_nt_dot(pc_scr[n_rows * p:n_rows * (p + 1), :], vt2)
                 + _dot(pn_scr[n_rows * p:n_rows * (p + 1), :], vn8))
        o = [o_all[h * PAIR_ROWS:(h + 1) * PAIR_ROWS, :] for h in range(4)]
        oa_scr[0, pair_rows(p), :] = jnp.where(lo, o[0], pltpu.roll(o[1], HEAD_DIM, axis=1))
        oa_scr[1, pair_rows(p), :] = jnp.where(lo, pltpu.roll(o[2], HEAD_DIM, axis=1), o[3])
    _scatter_block(mix_scr, oa_scr, 2, gi, sb, nb)
    lane_w = lax.broadcasted_iota(jnp.int32, (LANES, LANES), 1)
    for b in range(sb):
        seq = gi * sb + b
        for old_ref, out_ref, r0 in ((ck_ref, knew_ref, 0), (cv_ref, vnew_ref, LANES)):
            cols = pltpu.roll(old_ref[b], WINDOW - DEC_SEQ, axis=1)
            for t in range(DEC_SEQ):
                shift = jnp.bitwise_and(WINDOW - DEC_SEQ + t - seq, LANES - 1)
                new_t = pltpu.roll(kvt_scr[r0:r0 + LANES, t * nb:(t + 1) * nb], shift, axis=1)
                cols = jnp.where(lane_w == WINDOW - DEC_SEQ + t, new_t, cols)
            out_ref[b] = cols

    steps_per_head = pl.num_programs(0) // N_HG
    head = gi // steps_per_head
    k_base = head * HEAD_DIM + (gi % steps_per_head) * K_PER_STEP
    v_rows = pl.ds(pl.multiple_of(head * HEAD_DIM, HEAD_DIM), HEAD_DIM)

    def k_body(k8, accs):
        rows8 = pl.ds(pl.multiple_of(k_base + k8 * SUBLANES, SUBLANES), SUBLANES)
        accs = list(accs)
        for j in range(SUBLANES):
            st_rows = pl.ds(pl.multiple_of((k8 * SUBLANES + j) * HEAD_DIM, HEAD_DIM), HEAD_DIM)
            st = s_ref[st_rows, :]
            for t in range(DEC_SEQ):
                cols = slice(t * nb, (t + 1) * nb)
                f8, k8v, q8 = f_scr[rows8, cols], k_scr[rows8, cols], q_scr[rows8, cols]
                st = f8[j:j + 1, :] * st + k8v[j:j + 1, :] * v_scr[v_rows, cols]
                accs[t] = accs[t] + q8[j:j + 1, :] * st
            snew_ref[st_rows, :] = st
        return tuple(accs)

    zero_acc = jnp.zeros((HEAD_DIM, nb), F32)
    accs = lax.fori_loop(0, K_PER_STEP // SUBLANES, k_body, (zero_acc,) * DEC_SEQ)
    for t in range(DEC_SEQ):
        cols = slice(t * nb, (t + 1) * nb)
        o_scr[v_rows, cols] = o_scr[v_rows, cols] + accs[t]

    @pl.when(gi == pl.num_programs(0) - 1)
    def _():
        o = o_scr[...]
        parts = []
        for h in range(N_HG):
            oh = o[h * HEAD_DIM:(h + 1) * HEAD_DIM, :]
            parts.append(oh * lax.rsqrt(jnp.mean(oh * oh, axis=0, keepdims=True) + RMS_EPS))
        ob = jnp.concatenate(parts, axis=0) * ngc_ref[...] * gate_scr[...]
        y = (_dot(mix_scr[:, 0:W_GRP].astype(BF16), w_o_ref[0:W_GRP, :])
             + _tn_dot(ob.astype(BF16), w_o_ref[W_GRP:2 * W_GRP, :])
             + _dot(mix_scr[:, 2 * W_GRP:4 * W_GRP].astype(BF16), w_o_ref[2 * W_GRP:4 * W_GRP, :]))
        y_ref[...] = _layer_norm(ALPHA * xt_ref[...] + y, g_ref[...], b_ref[...])


def _sample_mixer_call(l, xt, ck, cv, s, cs_tm, pb_tm, w, prev):
    m = xt.shape[0]
    nb = m // DEC_SEQ
    sb = SEQ_BLOCK
    n_steps = nb // sb
    n_sc = 4 * PAIR_ROWS * (sb // 2)
    s_rows = HG_ROWS // n_steps
    single = pl.Buffered(1)
    const2 = lambda i: (0, 0)
    out_shape = (
        jax.ShapeDtypeStruct((m, D_MODEL), F32),
        jax.ShapeDtypeStruct((DEPTH, nb, LANES, WINDOW), F32),
        jax.ShapeDtypeStruct((DEPTH, nb, LANES, WINDOW), F32),
        jax.ShapeDtypeStruct((DEPTH, HG_ROWS, nb), F32),
        jax.ShapeDtypeStruct((CONV_W - 1, nb, W_GRP), F32),
        jax.ShapeDtypeStruct((POOL_BUF, nb, W_GRP), F32),
    )
    seq_blk = lambda i: (l, i, 0, 0)
    in_specs = [pl.BlockSpec((m, D_MODEL), const2, pipeline_mode=single),
                pl.BlockSpec((None, sb, LANES, WINDOW), seq_blk),
                pl.BlockSpec((None, sb, LANES, WINDOW), seq_blk),
                pl.BlockSpec((None, s_rows, nb), lambda i: (l, i, 0)),
                _layer_spec((CONV_W - 1, nb, W_GRP), l, single=True),
                _layer_spec((POOL_BUF, nb, W_GRP), l, single=True),
                _layer_spec((D_MODEL, D_IN), l, single=True),
                _layer_spec((4 * W_GRP, D_MODEL), l, single=True),
                _layer_spec((2 * LANES, D_MODEL), l, single=True),
                pl.BlockSpec(memory_space=pltpu.SMEM),
                _layer_spec((W_GRP, 1), l), _layer_spec((W_GRP, 1), l),
                _layer_spec((CONV_W, W_GRP), l), _layer_spec((W_GRP, W_GRP), l), _layer_spec((1, W_GRP), l),
                _layer_spec((D_MODEL, D_MODEL), l, single=True),
                _layer_spec((1, D_MODEL), l), _layer_spec((1, D_MODEL), l)]
    args = [xt, ck, cv, s, cs_tm, pb_tm, w["w_in"], w["w_hgt"], w["w_kvt"], w["sink"], w["lb_col"], w["ng_col"],
            w["cw"], w["pw"], w["ps"], w["w_o"], w["ln1_g"], w["ln1_b"]]
    aliases = {}
    if prev is not None:
        for j, buf in enumerate(prev):
            aliases[len(args)] = 1 + j
            args.append(buf)
            in_specs.append(pl.BlockSpec(memory_space=pl.ANY))
    f32 = lambda *shape: pltpu.VMEM(shape, F32)
    return pl.pallas_call(
        functools.partial(_sample_mixer_kernel, l),
        name="mixer_sample",
        grid=(n_steps,),
        in_specs=in_specs,
        out_specs=(pl.BlockSpec((m, D_MODEL), const2),
                   pl.BlockSpec((None, sb, LANES, WINDOW), seq_blk),
                   pl.BlockSpec((None, sb, LANES, WINDOW), seq_blk),
                   pl.BlockSpec((None, s_rows, nb), lambda i: (l, i, 0)),
                   pl.BlockSpec((CONV_W - 1, nb, W_GRP), lambda i: (0, 0, 0)),
                   pl.BlockSpec((POOL_BUF, nb, W_GRP), lambda i: (0, 0, 0))),
        out_shape=out_shape,
        input_output_aliases=aliases,
        scratch_shapes=[f32(m, 4 * W_GRP),
                        f32(4, m, LANES),
                        f32(2 * LANES, m),
                        f32(W_GRP, m), f32(W_GRP, m), f32(W_GRP, m), f32(W_GRP, m), f32(W_GRP, m),
                        f32(W_GRP, m),
                        f32(m, D_MODEL),
                        f32(4, sb * DEC_SEQ, LANES), f32(2, sb * DEC_SEQ, LANES),
                        f32(n_sc, 2 * WINDOW), f32(n_sc, PAIR_ROWS),
                        pltpu.VMEM((n_sc, 2 * WINDOW), BF16), pltpu.VMEM((n_sc, PAIR_ROWS), BF16)],
        compiler_params=_compiler_params(("arbitrary",)),
    )(*args)


def _xattn_sample_kernel(xt_ref, mk_ref, mv_ref, wq_ref, wo_ref, g_ref, b_ref, o_ref,
                         q_scr, qb_scr, ob_scr, att_scr, s_scr, p_scr):
    gi = pl.program_id(0)
    sb = mk_ref.shape[0]
    nb = xt_ref.shape[0] // DEC_SEQ
    half = sb // 2
    n_rows = N_XH * PAIR_ROWS

    @pl.when(gi == 0)
    def _():
        q = _dot(xt_ref[...].astype(BF16), wq_ref[...]) * (HEAD_DIM ** -0.5)
        q_scr[0] = q[:, 0:LANES]
        q_scr[1] = q[:, LANES:2 * LANES]

    _gather_block(qb_scr, q_scr, 2, gi, sb, nb)
    lane_head = lax.broadcasted_iota(jnp.int32, (PAIR_ROWS, D_X), 1) >> 6
    r_s = lax.broadcasted_iota(jnp.int32, (n_rows, 2 * N_MEM), 0)
    c_s = lax.broadcasted_iota(jnp.int32, (n_rows, 2 * N_MEM), 1)
    own = (r_s & 1) == (c_s >> 8)
    pair_rows = lambda p: pl.ds(p, PAIR_ROWS, stride=half)
    for p in range(half):
        q8 = jnp.concatenate([qb_scr[0, pair_rows(p), :], qb_scr[1, pair_rows(p), :]], axis=1)
        zero = jnp.zeros_like(q8)
        q4 = jnp.concatenate([jnp.where(lane_head == h, q8, zero) for h in range(N_XH)], axis=0).astype(BF16)
        kt2 = jnp.concatenate([mk_ref[p], mk_ref[p + half]], axis=1).astype(BF16)
        s_scr[n_rows * p:n_rows * (p + 1), :] = jnp.where(own, _dot(q4, kt2), MASK_VALUE)
    s_all = s_scr[...]
    m = jnp.max(s_all, axis=-1, keepdims=True)
    pr = jnp.exp(s_all - m)
    p_scr[...] = (pr * (1.0 / jnp.sum(pr, axis=-1, keepdims=True))).astype(BF16)
    for p in range(half):
        vt2 = jnp.concatenate([mv_ref[p], mv_ref[p + half]], axis=1).astype(BF16)
        o_all = _nt_dot(p_scr[n_rows * p:n_rows * (p + 1), :], vt2)
        o8 = jnp.zeros((PAIR_ROWS, D_X), F32)
        for h in range(N_XH):
            o8 = jnp.where(lane_head == h, o_all[h * PAIR_ROWS:(h + 1) * PAIR_ROWS, :], o8)
        ob_scr[0, pair_rows(p), :] = o8[:, 0:LANES]
        ob_scr[1, pair_rows(p), :] = o8[:, LANES:2 * LANES]
    _scatter_block(att_scr, ob_scr, 2, gi, sb, nb)

    @pl.when(gi == pl.num_programs(0) - 1)
    def _():
        y = _dot(att_scr[...].astype(BF16), wo_ref[...])
        o_ref[...] = _layer_norm(ALPHA * xt_ref[...] + y, g_ref[...], b_ref[...])


def _xattn_sample_call(l, xt, mk, mv, w):
    m = xt.shape[0]
    nb = m // DEC_SEQ
    sb = SEQ_BLOCK
    n_sc = N_XH * PAIR_ROWS * (sb // 2)
    const2 = lambda i: (0, 0)
    return pl.pallas_call(
        _xattn_sample_kernel,
        name="xattn_sample",
        grid=(nb // sb,),
        in_specs=[pl.BlockSpec((m, D_MODEL), const2, pipeline_mode=pl.Buffered(1)),
                  pl.BlockSpec((None, sb, D_X, N_MEM), lambda i: (l, i, 0, 0)),
                  pl.BlockSpec((None, sb, D_X, N_MEM), lambda i: (l, i, 0, 0)),
                  _layer_spec((D_MODEL, D_X), l), _layer_spec((D_X, D_MODEL), l),
                  _layer_spec((1, D_MODEL), l), _layer_spec((1, D_MODEL), l)],
        out_specs=pl.BlockSpec((m, D_MODEL), const2),
        out_shape=jax.ShapeDtypeStruct((m, D_MODEL), F32),
        scratch_shapes=[pltpu.VMEM((2, m, LANES), F32),
                        pltpu.VMEM((2, sb * DEC_SEQ, LANES), F32),
                        pltpu.VMEM((2, sb * DEC_SEQ, LANES), F32),
                        pltpu.VMEM((m, D_X), F32),
                        pltpu.VMEM((n_sc, 2 * N_MEM), F32),
                        pltpu.VMEM((n_sc, 2 * N_MEM), BF16)],
        compiler_params=_compiler_params(("arbitrary",)),
    )(xt, mk, mv, w["w_xq"], w["w_xo"], w["ln2_g"], w["ln2_b"])


def _hgrn_lower_bounds(lb_param):
    p = jax.nn.softmax(lb_param.astype(F32), axis=0)
    return jnp.cumsum(p, axis=0) - p[0:1]


def _prepare_weights(w_in, attn_sink, hgrn_lb, hgrn_norm_g, conv_w, pool_w, pool_scale, w_o, ln1_g, ln1_b,
                     w_xq, w_xo, ln2_g, ln2_b, w_gate, w_up, w_down, ln3_g, ln3_b):
    bf = lambda a: a.astype(BF16)
    rows = lambda a: a.astype(F32)[:, None, :]
    cols = lambda a: a.astype(F32)[:, :, None]
    lb_all = _hgrn_lower_bounds(hgrn_lb)
    eye = jnp.eye(len(POOL_WINDOWS), dtype=pool_w.dtype)
    pw = (pool_w[:, :, :, None, :] * eye[None, :, None, :, None]).reshape(DEPTH, W_GRP, W_GRP)
    return dict(
        w_in=bf(w_in),
        w_hgt=bf(w_in[:, :, C_BQ:C_CB].transpose(0, 2, 1)),
        w_kvt=bf(w_in[:, :, C_AK:C_BQ].transpose(0, 2, 1)),
        sink=attn_sink.astype(F32),
        lb=rows(lb_all), lb_col=cols(lb_all), ng=rows(hgrn_norm_g), ng_col=cols(hgrn_norm_g),
        cw=conv_w.astype(F32), pw=bf(pw), ps=rows(pool_scale),
        w_o=bf(w_o), ln1_g=rows(ln1_g), ln1_b=rows(ln1_b),
        w_xq=bf(w_xq), w_xo=bf(w_xo), ln2_g=rows(ln2_g), ln2_b=rows(ln2_b),
        w_gate=bf(w_gate), w_up=bf(w_up), w_down=bf(w_down), ln3_g=rows(ln3_g), ln3_b=rows(ln3_b))


def _prompt_layer(l, x, mkv, w):
    bsz, t, _ = x.shape
    x, kn, vn, sn, cn, pn = _prompt_mixer_call(l, x, w)
    x = _xattn_prompt_call(l, x, mkv, w)
    x = _ffn_call(l, x.reshape(bsz * t, D_MODEL), w).reshape(bsz, t, D_MODEL)
    return x, kn, vn, sn, cn, pn


def _sample_layer(l, xt, ck, cv, s, cs_tm, pb_tm, mk, mv, w, prev):
    xt, kn, vn, sn, cn, pn = _sample_mixer_call(l, xt, ck, cv, s, cs_tm, pb_tm, w, prev)
    xt = _xattn_sample_call(l, xt, mk, mv, w)
    xt = _ffn_call(l, xt, w)
    return xt, kn, vn, sn, cn, pn


def kernel(x_prompt, x_sample, cache_swa_k, cache_swa_v, state_hgrn, state_conv, state_pool, cache_mem_k,
           cache_mem_v, mem_prompt, emb_ln_g, emb_ln_b, w_in, attn_sink, hgrn_lb, hgrn_norm_g, conv_w, pool_w,
           pool_scale, w_o, ln1_g, ln1_b, w_xq, w_xk, w_xv, w_xo, ln2_g, ln2_b, w_gate, w_up, w_down, ln3_g,
           ln3_b):
    bp, t, _ = x_prompt.shape
    bs, ts, _ = x_sample.shape
    w = _prepare_weights(w_in, attn_sink, hgrn_lb, hgrn_norm_g, conv_w, pool_w, pool_scale, w_o, ln1_g, ln1_b,
                         w_xq, w_xo, ln2_g, ln2_b, w_gate, w_up, w_down, ln3_g, ln3_b)
    emb_g = emb_ln_g.reshape(1, D_MODEL).astype(F32)
    emb_b = emb_ln_b.reshape(1, D_MODEL).astype(F32)
    hp = _ln_call(x_prompt.reshape(bp * t, D_MODEL), emb_g, emb_b).reshape(bp, t, D_MODEL)
    w_kv = jnp.concatenate([w_xk, w_xv], axis=2).transpose(1, 0, 2).reshape(D_MODEL, DEPTH * 2 * D_X).astype(BF16)
    mkv = _matmul_call(mem_prompt.reshape(bp * N_MEM, D_MODEL), w_kv).reshape(bp, N_MEM, DEPTH * 2 * D_X)
    hs = _ln_call(x_sample.transpose(1, 0, 2).reshape(ts * bs, D_MODEL), emb_g, emb_b)
    ck = cache_swa_k.transpose(0, 1, 3, 4, 2).reshape(DEPTH, bs, LANES, WINDOW)
    cv = cache_swa_v.transpose(0, 1, 3, 4, 2).reshape(DEPTH, bs, LANES, WINDOW)
    st = state_hgrn.transpose(0, 2, 3, 4, 1).reshape(DEPTH, HG_ROWS, bs)
    cs_tm = state_conv.transpose(0, 2, 1, 3)
    pb_tm = state_pool.transpose(0, 2, 1, 3)
    mk_s = cache_mem_k.transpose(0, 1, 3, 4, 2).reshape(DEPTH, bs, D_X, N_MEM)
    mv_s = cache_mem_v.transpose(0, 1, 3, 4, 2).reshape(DEPTH, bs, D_X, N_MEM)
    outs = [[] for _ in range(5)]
    souts = [[] for _ in range(2)]
    prev = None
    for l in range(DEPTH):
        res = _prompt_layer(l, hp, mkv, w)
        hp = res[0]
        for acc, r in zip(outs, res[1:]):
            acc.append(r)
        sres = _sample_layer(l, hs, ck, cv, st, cs_tm, pb_tm, mk_s, mv_s, w, prev)
        hs = sres[0]
        prev = sres[1:4]
        for acc, r in zip(souts, sres[4:]):
            acc.append(r)
    pk, pv, ps, pc, pp = [jnp.stack(o) for o in outs]
    sk, sv, ss = prev
    sc, sp = [jnp.stack(o) for o in souts]
    mem_out = mkv.reshape(bp, N_MEM, DEPTH, 2, N_XH, HEAD_DIM).transpose(3, 2, 0, 1, 4, 5)
    swa_out = lambda a: a.reshape(DEPTH, bs, N_KV, HEAD_DIM, WINDOW).transpose(0, 1, 4, 2, 3)
    return (hp, hs.reshape(ts, bs, D_MODEL).transpose(1, 0, 2),
            pk.reshape(DEPTH, bp, WINDOW, N_KV, HEAD_DIM), pv.reshape(DEPTH, bp, WINDOW, N_KV, HEAD_DIM),
            ps.reshape(DEPTH, bp, N_HG, HEAD_DIM, HEAD_DIM), pc, pp,
            mem_out[0], mem_out[1],
            swa_out(sk), swa_out(sv),
            ss.reshape(DEPTH, N_HG, HEAD_DIM, HEAD_DIM, bs).transpose(0, 4, 1, 2, 3),
            sc.transpose(0, 2, 1, 3), sp.transpose(0, 2, 1, 3))
```

```python
import functools

import jax
import jax.numpy as jnp
from jax import lax
from jax.experimental import pallas as pl
from jax.experimental.pallas import tpu as pltpu

F32 = jnp.float32
BF16 = jnp.bfloat16

D_MODEL = 1024
DEPTH = 4
HEAD_DIM = 64
W_GRP = 256
N_KV = 2
WINDOW = 128
N_HG = 4
CONV_W = 3
POOL_WINDOWS = (2, 4, 8, 16)
POOL_BUF = 15
N_MEM = 256
N_XH = 4
D_X = 256
D_FF = 2816
D_IN = 2560
DEC_SEQ = 4
ALPHA = (2 * DEPTH) ** 0.25
LN_EPS = 1e-5
RMS_EPS = 1e-6
MASK_VALUE = -1e30
LB_FLOOR = 1e-30
PAST_LEN = 8192

C_AQ, C_AK, C_AV = 0, 256, 384
C_BQ, C_BF, C_BI, C_BG = 512, 768, 1024, 1280
C_CB, C_CC, C_CH = 1536, 1792, 2048
C_DV = 2304

LANES = 128
SUBLANES = 8
VMEM_LIMIT_BYTES = 56 * 1024 * 1024

TOKEN_BLOCK = 512
SUB_BLOCKS = 2
HG_CHUNK = 64
HG_MID = HG_CHUNK // 2 - 1
HG_MAX_EXPONENT = 80.0
HG_ROWS = N_HG * HEAD_DIM * HEAD_DIM
SEQ_BLOCK = 16
PAIR_ROWS = 2 * DEC_SEQ
K_PER_STEP = HG_ROWS // HEAD_DIM * SEQ_BLOCK // 128


def _nt_dot(a, b):
    return lax.dot_general(a, b, (((1,), (1,)), ((), ())), preferred_element_type=F32)


def _tn_dot(a, b):
    return lax.dot_general(a, b, (((0,), (0,)), ((), ())), preferred_element_type=F32)


def _dot(a, b):
    return jnp.dot(a, b, preferred_element_type=F32)


def _layer_norm(x, g, b):
    mu = jnp.mean(x, axis=-1, keepdims=True)
    xc = x - mu
    var = jnp.mean(xc * xc, axis=-1, keepdims=True)
    return xc * lax.rsqrt(var + LN_EPS) * g + b


def _sigmoid_pair(z):
    e = jnp.exp(-jnp.abs(z))
    inv = 1.0 / (1.0 + e)
    small = e * inv
    pos = z >= 0
    return jnp.where(pos, inv, small), jnp.where(pos, small, inv)


def _silu(z):
    s, _ = _sigmoid_pair(z)
    return z * s


def _silu_tanh(z):
    return z * (0.5 + 0.5 * jnp.tanh(0.5 * z))


def _full_spec(shape):
    nd = len(shape)
    return pl.BlockSpec(shape, lambda *_: (0,) * nd)


def _layer_spec(shape, l, single=False):
    nd = len(shape)
    index = lambda *_: (l,) + (0,) * nd
    if single:
        return pl.BlockSpec((None,) + tuple(shape), index, pipeline_mode=pl.Buffered(1))
    return pl.BlockSpec((None,) + tuple(shape), index)


def _compiler_params(sem):
    return pltpu.CompilerParams(dimension_semantics=sem, vmem_limit_bytes=VMEM_LIMIT_BYTES)


class _MatmulQueue:
    def __init__(self):
        self.items = []

    def add(self, fn, *args):
        self.items.append((fn, args))

    def issue(self, n=1):
        for _ in range(min(n, len(self.items))):
            fn, args = self.items.pop(0)
            fn(*args)

    def flush(self):
        self.issue(len(self.items))


def _ln_kernel(x_ref, g_ref, b_ref, o_ref):
    o_ref[...] = _layer_norm(x_ref[...], g_ref[...], b_ref[...])


def _ln_call(x2d, g, b):
    m = x2d.shape[0]
    return pl.pallas_call(
        _ln_kernel,
        name="input_ln",
        grid=(m // TOKEN_BLOCK,),
        in_specs=[pl.BlockSpec((TOKEN_BLOCK, D_MODEL), lambda i: (i, 0)),
                  _full_spec((1, D_MODEL)), _full_spec((1, D_MODEL))],
        out_specs=pl.BlockSpec((TOKEN_BLOCK, D_MODEL), lambda i: (i, 0)),
        out_shape=jax.ShapeDtypeStruct((m, D_MODEL), F32),
        compiler_params=_compiler_params(("arbitrary",)),
    )(x2d, g, b)


def _matmul_kernel(x_ref, w_ref, o_ref):
    o_ref[...] = _dot(x_ref[...].astype(BF16), w_ref[...])


def _matmul_call(x2d, w):
    m, k = x2d.shape
    n = w.shape[1]
    return pl.pallas_call(
        _matmul_kernel,
        name="mem_proj",
        grid=(m // TOKEN_BLOCK,),
        in_specs=[pl.BlockSpec((TOKEN_BLOCK, k), lambda i: (i, 0)), _full_spec((k, n))],
        out_specs=pl.BlockSpec((TOKEN_BLOCK, n), lambda i: (i, 0)),
        out_shape=jax.ShapeDtypeStruct((m, n), F32),
        compiler_params=_compiler_params(("arbitrary",)),
    )(x2d, w)


def _ffn_kernel(x_ref, wg_ref, wu_ref, wd_ref, g_ref, b_ref, o_ref):
    x = x_ref[...]
    xb = x.astype(BF16)
    h = _silu(_dot(xb, wg_ref[...])) * _dot(xb, wu_ref[...])
    y = _dot(h.astype(BF16), wd_ref[...])
    o_ref[...] = _layer_norm(ALPHA * x + y, g_ref[...], b_ref[...])


def _ffn_call(l, x2d, w):
    m = x2d.shape[0]
    return pl.pallas_call(
        _ffn_kernel,
        name="ffn",
        grid=(m // TOKEN_BLOCK,),
        in_specs=[pl.BlockSpec((TOKEN_BLOCK, D_MODEL), lambda i: (i, 0)),
                  _layer_spec((D_MODEL, D_FF), l, single=True),
                  _layer_spec((D_MODEL, D_FF), l, single=True),
                  _layer_spec((D_FF, D_MODEL), l, single=True),
                  _layer_spec((1, D_MODEL), l), _layer_spec((1, D_MODEL), l)],
        out_specs=pl.BlockSpec((TOKEN_BLOCK, D_MODEL), lambda i: (i, 0)),
        out_shape=jax.ShapeDtypeStruct((m, D_MODEL), F32),
        compiler_params=_compiler_params(("arbitrary",)),
    )(x2d, w["w_gate"], w["w_up"], w["w_down"], w["ln3_g"], w["ln3_b"])


def _xattn_prompt_kernel(x_ref, mk_ref, mv_ref, wq_ref, wo_ref, g_ref, b_ref, o_ref, q_scr, att_scr, yo_scr):
    tb = x_ref.shape[0]
    hb = tb // SUB_BLOCKS
    mk = mk_ref[...].astype(BF16)
    mv = mv_ref[...].astype(BF16)
    lane_head = lax.broadcasted_iota(jnp.int32, (hb, D_X), 1) >> 6
    sub = lambda r: slice(r * hb, (r + 1) * hb)

    def project(r):
        q_scr[sub(r), :] = _dot(x_ref[sub(r), :].astype(BF16), wq_ref[...]) * (HEAD_DIM ** -0.5)

    def finish(r, c0, c1):
        yo_scr[sub(r), c0:c1] = _dot(att_scr[sub(r), :], wo_ref[:, c0:c1])

    def finish_ln(r):
        o_ref[sub(r), :] = _layer_norm(ALPHA * x_ref[sub(r), :] + yo_scr[sub(r), :], g_ref[...], b_ref[...])

    out_cols = ((0, D_MODEL // 2), (D_MODEL // 2, D_MODEL))
    project(0)
    queue = _MatmulQueue()
    for r in range(SUB_BLOCKS):
        if r >= 1:
            for cols in out_cols:
                queue.add(finish, r - 1, *cols)
        if r + 1 < SUB_BLOCKS:
            queue.add(project, r + 1)
        q = q_scr[sub(r), :]
        o = jnp.zeros_like(q)
        for h in range(N_XH):
            qh = jnp.where(lane_head == h, q, 0.0).astype(BF16)
            s = _nt_dot(mk, qh)
            m = jnp.max(s, axis=0, keepdims=True)
            p = jnp.exp(s - m)
            den = jnp.sum(p, axis=0, keepdims=True)
            p = (p * (1.0 / den)).astype(BF16)
            o = jnp.where(lane_head == h, _tn_dot(p, mv), o)
            queue.issue()
        att_scr[sub(r), :] = o.astype(BF16)
        queue.flush()
        if r >= 1:
            finish_ln(r - 1)
    for cols in out_cols:
        finish(SUB_BLOCKS - 1, *cols)
    finish_ln(SUB_BLOCKS - 1)


def _xattn_prompt_call(l, x, mkv, w):
    bsz, t, _ = x.shape
    tb = min(TOKEN_BLOCK, t)
    return pl.pallas_call(
        _xattn_prompt_kernel,
        name="xattn_prompt",
        grid=(bsz, t // tb),
        in_specs=[pl.BlockSpec((None, tb, D_MODEL), lambda i, j: (i, j, 0)),
                  pl.BlockSpec((None, N_MEM, D_X), lambda i, j: (i, 0, 2 * l)),
                  pl.BlockSpec((None, N_MEM, D_X), lambda i, j: (i, 0, 2 * l + 1)),
                  _layer_spec((D_MODEL, D_X), l), _layer_spec((D_X, D_MODEL), l),
                  _layer_spec((1, D_MODEL), l), _layer_spec((1, D_MODEL), l)],
        out_specs=pl.BlockSpec((None, tb, D_MODEL), lambda i, j: (i, j, 0)),
        out_shape=jax.ShapeDtypeStruct((bsz, t, D_MODEL), F32),
        scratch_shapes=[pltpu.VMEM((tb, D_X), F32), pltpu.VMEM((tb, D_X), BF16), pltpu.VMEM((tb, D_MODEL), F32)],
        compiler_params=_compiler_params(("arbitrary", "arbitrary")),
    )(x, mkv, mkv, w["w_xq"], w["w_xo"], w["ln2_g"], w["ln2_b"])


def _swa_bias_table():
    c = lax.broadcasted_iota(jnp.int32, (2 * WINDOW, 4 * WINDOW), 0)
    r = lax.broadcasted_iota(jnp.int32, (2 * WINDOW, 4 * WINDOW), 1)
    head = r >> 7
    rel = (r & (WINDOW - 1)) + WINDOW - c
    slope = jnp.exp2(-2.0 * (head.astype(F32) + 1.0))
    valid = (rel >= 0) & (rel <= WINDOW)
    return jnp.where(valid, -slope * rel.astype(F32), MASK_VALUE)


def _sink_select(sink_ref, l, head):
    return jnp.where(head == 0, sink_ref[l, 0],
                     jnp.where(head == 1, sink_ref[l, 1], jnp.where(head == 2, sink_ref[l, 2], sink_ref[l, 3])))


def _prompt_mixer_kernel(l, x_ref, w_in_ref, sink_ref, lb_ref, ng_ref, cw_ref, pw_ref, ps_ref, w_o_ref,
                         g_ref, b_ref,
                         y_ref, knew_ref, vnew_ref, snew_ref, cnew_ref, pnew_ref,
                         proj_scr, kext_scr, vext_scr, st_scr, u_scr, p_scr, bias_scr, mix_scr, hg_scr,
                         hq_scr, hk_scr, hv_scr, ghl_scr, cum_scr, qp_scr, kp_scr, qs_scr, ks_scr, dec_scr, inc_scr,
                         stb_scr, a_scr, yo_scr, st0_scr):
    tb = x_ref.shape[0]
    hb = tb // SUB_BLOCKS
    n_qb = hb // WINDOW
    n_ch = hb // HG_CHUNK
    bi = pl.program_id(0)
    ti = pl.program_id(1)
    last = ti == pl.num_programs(1) - 1

    @pl.when((bi == 0) & (ti == 0))
    def _():
        bias_scr[...] = _swa_bias_table()

    @pl.when(ti == 0)
    def _():
        kext_scr[0:WINDOW, :] = jnp.zeros((WINDOW, LANES), BF16)
        vext_scr[0:WINDOW, :] = jnp.zeros((WINDOW, LANES), BF16)
        st_scr[...] = jnp.zeros(st_scr.shape, F32)
        u_scr[0:SUBLANES, :] = jnp.zeros((SUBLANES, W_GRP), F32)
        p_scr[0:16, :] = jnp.zeros((16, W_GRP), F32)

    def project(r, c0, c1):
        rows = slice(r * hb, (r + 1) * hb)
        proj_scr[rows, c0:c1] = _dot(x_ref[rows, :].astype(BF16), w_in_ref[:, c0:c1])

    def finish(r, c0, c1):
        rows = slice(r * hb, (r + 1) * hb)
        yo_scr[rows, c0:c1] = _dot(mix_scr[rows, :], w_o_ref[:, c0:c1])

    def finish_ln(r):
        rows = slice(r * hb, (r + 1) * hb)
        y_ref[rows, :] = _layer_norm(ALPHA * x_ref[rows, :] + yo_scr[rows, :], g_ref[...], b_ref[...])

    proj_cols = ((0, C_BQ), (C_BQ, C_BI), (C_BI, C_CB), (C_CB, C_CH), (C_CH, D_IN))
    out_cols = ((0, D_MODEL // 2), (D_MODEL // 2, D_MODEL))
    for cols in proj_cols:
        project(0, *cols)
    queue = _MatmulQueue()

    lane = lax.broadcasted_iota(jnp.int32, (WINDOW, LANES), 1)
    lo = lane < HEAD_DIM
    key_row = lax.broadcasted_iota(jnp.int32, (2 * WINDOW, 4 * WINDOW), 0)
    sink_row = _sink_select(sink_ref, l, lax.broadcasted_iota(jnp.int32, (1, 4 * WINDOW), 1) >> 7)
    lb = lb_ref[...]
    lbf = jnp.maximum(lb, LB_FLOOR)
    one_m_lb = 1.0 - lb
    ng = ng_ref[...]
    r256 = lax.broadcasted_iota(jnp.int32, (W_GRP, W_GRP), 0)
    c256 = lax.broadcasted_iota(jnp.int32, (W_GRP, W_GRP), 1)
    same_head = (r256 >> 6) == (c256 >> 6)
    head_ones = jnp.where(same_head, 1.0, 0.0).astype(BF16)
    same_head_b = head_ones > 0
    zero_b = jnp.zeros((W_GRP, W_GRP), BF16)
    rc = lax.broadcasted_iota(jnp.int32, (HG_CHUNK, W_GRP), 0)
    cc = lax.broadcasted_iota(jnp.int32, (HG_CHUNK, W_GRP), 1)
    causal = (cc & (HG_CHUNK - 1)) <= rc
    r64 = lax.broadcasted_iota(jnp.int32, (HG_CHUNK, HG_CHUNK), 0)
    c64 = lax.broadcasted_iota(jnp.int32, (HG_CHUNK, HG_CHUNK), 1)
    tril = jnp.where(c64 <= r64, 1.0, 0.0).astype(BF16)
    cw = cw_ref[...]
    rp = lax.broadcasted_iota(jnp.int32, (hb, W_GRP), 0)
    grp = lax.broadcasted_iota(jnp.int32, (hb, W_GRP), 1) >> 6
    width = jnp.left_shift(2, grp)
    chunk = lambda c: slice(c * HG_CHUNK, (c + 1) * HG_CHUNK)

    def hgrn_output(r):
        rows = slice(r * hb, (r + 1) * hb)
        o = hg_scr[...]
        ms = _dot((o * o).astype(BF16), head_ones) * (1.0 / HEAD_DIM)
        o = o * lax.rsqrt(ms + RMS_EPS) * ng
        mix_scr[rows, W_GRP:2 * W_GRP] = (o * _silu_tanh(proj_scr[rows, C_BG:C_BG + W_GRP])).astype(BF16)

    st0_scr[...] = st_scr[...]
    span = jnp.zeros((1, W_GRP), F32)
    for r in range(SUB_BLOCKS):
        r0 = r * hb
        sub = slice(r0, r0 + hb)
        if r >= 1:
            for cols in out_cols:
                queue.add(finish, r - 1, *cols)
        if r + 1 < SUB_BLOCKS:
            for cols in proj_cols:
                queue.add(project, r + 1, *cols)

        kext_scr[WINDOW:WINDOW + hb, :] = proj_scr[sub, C_AK:C_AK + LANES].astype(BF16)
        vext_scr[WINDOW:WINDOW + hb, :] = proj_scr[sub, C_AV:C_AV + LANES].astype(BF16)
        for j in range(n_qb):
            rows = slice(r0 + j * WINDOW, r0 + (j + 1) * WINDOW)
            q0 = proj_scr[rows, 0:LANES] * (HEAD_DIM ** -0.5)
            q1 = proj_scr[rows, LANES:2 * LANES] * (HEAD_DIM ** -0.5)
            q0r = pltpu.roll(q0, HEAD_DIM, axis=1)
            q1r = pltpu.roll(q1, HEAD_DIM, axis=1)
            zero = jnp.zeros_like(q0)
            q4 = jnp.concatenate([jnp.where(lo, q0, zero), jnp.where(lo, q0r, zero),
                                  jnp.where(lo, zero, q1r), jnp.where(lo, zero, q1)], axis=0).astype(BF16)
            kj = kext_scr[j * WINDOW:(j + 2) * WINDOW, :]
            vj = vext_scr[j * WINDOW:(j + 2) * WINDOW, :]
            s = _nt_dot(kj, q4) + bias_scr[...]
            if r == 0 and j == 0:
                s = jnp.where((ti == 0) & (key_row < WINDOW), MASK_VALUE, s)
            m = jnp.maximum(jnp.max(s, axis=0, keepdims=True), sink_row)
            p = jnp.exp(s - m)
            den = jnp.sum(p, axis=0, keepdims=True) + jnp.exp(sink_row - m)
            p = (p * (1.0 / den)).astype(BF16)
            o_all = _tn_dot(p, vj)
            o = [o_all[h * WINDOW:(h + 1) * WINDOW, :] for h in range(4)]
            mix_scr[rows, 0:LANES] = jnp.where(lo, o[0], pltpu.roll(o[1], HEAD_DIM, axis=1)).astype(BF16)
            mix_scr[rows, LANES:2 * LANES] = jnp.where(lo, pltpu.roll(o[2], HEAD_DIM, axis=1), o[3]).astype(BF16)
            queue.issue()
        kext_scr[0:WINDOW, :] = kext_scr[hb:hb + WINDOW, :]
        vext_scr[0:WINDOW, :] = vext_scr[hb:hb + WINDOW, :]

        hq_scr[...] = _silu_tanh(proj_scr[sub, C_BQ:C_BQ + W_GRP])
        queue.issue()
        sig_pos, sig_neg = _sigmoid_pair(proj_scr[sub, C_BF:C_BF + W_GRP])
        g = jnp.log(sig_pos + lbf * sig_neg)
        hk_scr[...] = one_m_lb * sig_neg
        g_hi = g.astype(BF16)
        ghl_scr[:, 0:W_GRP] = g_hi
        ghl_scr[:, W_GRP:2 * W_GRP] = (g - g_hi.astype(F32)).astype(BF16)
        hv_scr[...] = proj_scr[sub, C_BI:C_BI + W_GRP].astype(BF16)
        for c in range(n_ch):
            cum2 = _dot(tril, ghl_scr[chunk(c), :])
            cum_scr[chunk(c), :] = cum2[:, 0:W_GRP] + cum2[:, W_GRP:2 * W_GRP]
        for c in range(n_ch):
            cum = cum_scr[chunk(c), :]
            ref = cum[HG_MID:HG_MID + 1, :]
            tot = cum[HG_CHUNK - 1:HG_CHUNK, :]
            span = jnp.maximum(span, jnp.maximum(cum[0:1, :] - ref, ref - tot))
            qp = hq_scr[chunk(c), :] * jnp.exp(cum - ref)
            kp = hk_scr[chunk(c), :] * jnp.exp(ref - cum)
            qp_scr[chunk(c), :] = qp.astype(BF16)
            kp_scr[chunk(c), :] = kp.astype(BF16)
            qs_scr[chunk(c), :] = (qp * jnp.exp(ref)).astype(BF16)
            ks_scr[chunk(c), :] = (kp * jnp.exp(tot - ref)).astype(BF16)
            dec_scr[c] = jnp.broadcast_to(jnp.exp(tot), (SUBLANES, W_GRP))
        queue.issue()
        for c in range(n_ch):
            bk = jnp.where(same_head_b, jnp.concatenate([kp_scr[chunk(c), :]] * N_HG, axis=0), zero_b)
            a = jnp.where(causal, _nt_dot(qp_scr[chunk(c), :], bk), 0.0)
            a_scr[chunk(c), :] = a.astype(BF16)
        for c in range(n_ch):
            inc_scr[c] = jnp.where(same_head, _tn_dot(hv_scr[chunk(c), :], ks_scr[chunk(c), :]), 0.0)
        for c in range(n_ch):
            bv = jnp.where(same_head_b, jnp.concatenate([hv_scr[chunk(c), :]] * N_HG, axis=0), zero_b)
            hg_scr[chunk(c), :] = _dot(a_scr[chunk(c), :], bv)
        st = st_scr[...]
        for c in range(n_ch):
            stb_scr[c] = st.astype(BF16)
            st = st * dec_scr[c, 0:1, :] + inc_scr[c]
        st_scr[...] = st
        queue.issue()
        for c in range(n_ch):
            hg_scr[chunk(c), :] = hg_scr[chunk(c), :] + _nt_dot(qs_scr[chunk(c), :], stb_scr[c])

        hgrn_output(r)
        queue.issue()

        u_scr[SUBLANES:SUBLANES + hb, :] = proj_scr[sub, C_CC:C_CC + W_GRP] * proj_scr[sub, C_CH:C_CH + W_GRP]
        yc = (u_scr[SUBLANES - 2:SUBLANES - 2 + hb, :] * cw[0:1, :]
              + u_scr[SUBLANES - 1:SUBLANES - 1 + hb, :] * cw[1:2, :]
              + u_scr[SUBLANES:SUBLANES + hb, :] * cw[2:3, :])
        mix_scr[sub, 2 * W_GRP:3 * W_GRP] = (proj_scr[sub, C_CB:C_CB + W_GRP] * yc).astype(BF16)
        u_scr[0:SUBLANES, :] = u_scr[hb:hb + SUBLANES, :]
        queue.issue()

        dv = proj_scr[sub, C_DV:C_DV + W_GRP]
        p_scr[16:16 + hb, :] = dv
        ext = p_scr[...]
        s2 = ext + pltpu.roll(ext, 1, axis=0)
        s4 = s2 + pltpu.roll(s2, 2, axis=0)
        s8 = s4 + pltpu.roll(s4, 4, axis=0)
        s16 = s8 + pltpu.roll(s8, 8, axis=0)
        win = jnp.where(grp == 0, s2[16:], jnp.where(grp == 1, s4[16:], jnp.where(grp == 2, s8[16:], s16[16:])))
        cnt = jnp.minimum(ti * tb + r0 + rp + 1, width).astype(F32)
        pooled = win / cnt - dv
        yd = _dot(pooled.astype(BF16), pw_ref[...]) * ps_ref[...]
        mix_scr[sub, 3 * W_GRP:4 * W_GRP] = yd.astype(BF16)
        p_scr[0:16, :] = p_scr[hb:hb + 16, :]
        queue.flush()
        if r >= 1:
            finish_ln(r - 1)

    for cols in out_cols:
        finish(SUB_BLOCKS - 1, *cols)
    finish_ln(SUB_BLOCKS - 1)

    @pl.when(jnp.max(span) > HG_MAX_EXPONENT)
    def _():
        st_scr[...] = st0_scr[...]
        sub8 = lax.broadcasted_iota(jnp.int32, (SUBLANES, W_GRP), 0)
        for r in range(SUB_BLOCKS):
            def tile_body(i, carry, r=r):
                rows8 = pl.ds(pl.multiple_of(i * SUBLANES, SUBLANES), SUBLANES)
                prow = pl.ds(pl.multiple_of(r * hb + i * SUBLANES, SUBLANES), SUBLANES)
                sp8, sn8 = _sigmoid_pair(proj_scr[prow, C_BF:C_BF + W_GRP])
                f8 = sp8 + lbf * sn8
                k8 = (one_m_lb * sn8).astype(BF16)
                v8 = proj_scr[prow, C_BI:C_BI + W_GRP]
                q8 = _silu_tanh(proj_scr[prow, C_BQ:C_BQ + W_GRP]).astype(BF16)
                o8 = jnp.zeros((SUBLANES, W_GRP), F32)
                for j in range(SUBLANES):
                    vj = jnp.where(sub8 == j, v8, 0.0).astype(BF16)
                    stj = st_scr[...] * f8[j:j + 1, :] + jnp.where(same_head, _tn_dot(vj, k8), 0.0)
                    st_scr[...] = stj
                    o8 = jnp.where(sub8 == j, _nt_dot(q8, stj.astype(BF16)), o8)
                hg_scr[rows8, :] = o8
                return carry

            lax.fori_loop(0, hb // SUBLANES, tile_body, 0)
            hgrn_output(r)
            for cols in out_cols:
                finish(r, *cols)
            finish_ln(r)

    @pl.when(last)
    def _():
        knew_ref[...] = proj_scr[tb - WINDOW:tb, C_AK:C_AK + LANES]
        vnew_ref[...] = proj_scr[tb - WINDOW:tb, C_AV:C_AV + LANES]
        s_t = st_scr[...].T
        for h in range(N_HG):
            snew_ref[h * HEAD_DIM:(h + 1) * HEAD_DIM, :] = (
                s_t[h * HEAD_DIM:(h + 1) * HEAD_DIM, h * HEAD_DIM:(h + 1) * HEAD_DIM])
        cnew_ref[...] = u_scr[SUBLANES - 2:SUBLANES, :]
        pnew_ref[...] = p_scr[1:16, :]


def _prompt_mixer_call(l, x, w):
    bsz, t, _ = x.shape
    tb = min(TOKEN_BLOCK, t)
    hb = tb // SUB_BLOCKS
    n_ch = hb // HG_CHUNK
    row = lambda i, j: (i, 0, 0)
    out_shape = (
        jax.ShapeDtypeStruct((bsz, t, D_MODEL), F32),
        jax.ShapeDtypeStruct((bsz, WINDOW, LANES), F32),
        jax.ShapeDtypeStruct((bsz, WINDOW, LANES), F32),
        jax.ShapeDtypeStruct((bsz, W_GRP, HEAD_DIM), F32),
        jax.ShapeDtypeStruct((bsz, CONV_W - 1, W_GRP), F32),
        jax.ShapeDtypeStruct((bsz, POOL_BUF, W_GRP), F32),
    )
    f32 = lambda *shape: pltpu.VMEM(shape, F32)
    bf16 = lambda *shape: pltpu.VMEM(shape, BF16)
    return pl.pallas_call(
        functools.partial(_prompt_mixer_kernel, l),
        name="mixer_prompt",
        grid=(bsz, t // tb),
        in_specs=[pl.BlockSpec((None, tb, D_MODEL), lambda i, j: (i, j, 0)),
                  _layer_spec((D_MODEL, D_IN), l, single=True),
                  pl.BlockSpec(memory_space=pltpu.SMEM),
                  _layer_spec((1, W_GRP), l), _layer_spec((1, W_GRP), l), _layer_spec((CONV_W, W_GRP), l),
                  _layer_spec((W_GRP, W_GRP), l), _layer_spec((1, W_GRP), l),
                  _layer_spec((D_MODEL, D_MODEL), l, single=True),
                  _layer_spec((1, D_MODEL), l), _layer_spec((1, D_MODEL), l)],
        out_specs=(pl.BlockSpec((None, tb, D_MODEL), lambda i, j: (i, j, 0)),
                   pl.BlockSpec((None, WINDOW, LANES), row),
                   pl.BlockSpec((None, WINDOW, LANES), row),
                   pl.BlockSpec((None, W_GRP, HEAD_DIM), row),
                   pl.BlockSpec((None, CONV_W - 1, W_GRP), row),
                   pl.BlockSpec((None, POOL_BUF, W_GRP), row)),
        out_shape=out_shape,
        scratch_shapes=[
            f32(tb, D_IN),
            bf16(WINDOW + hb, LANES),
            bf16(WINDOW + hb, LANES),
            f32(W_GRP, W_GRP),
            f32(SUBLANES + hb, W_GRP),
            f32(16 + hb, W_GRP),
            f32(2 * WINDOW, 4 * WINDOW),
            bf16(tb, D_MODEL),
            f32(hb, W_GRP),
            f32(hb, W_GRP), f32(hb, W_GRP),
            bf16(hb, W_GRP),
            bf16(hb, 2 * W_GRP),
            f32(hb, W_GRP),
            bf16(hb, W_GRP), bf16(hb, W_GRP),
            bf16(hb, W_GRP), bf16(hb, W_GRP),
            f32(n_ch, SUBLANES, W_GRP),
            f32(n_ch, W_GRP, W_GRP),
            bf16(n_ch, W_GRP, W_GRP),
            bf16(hb, W_GRP),
            f32(tb, D_MODEL),
            f32(W_GRP, W_GRP),
        ],
        compiler_params=_compiler_params(("arbitrary", "arbitrary")),
    )(x, w["w_in"], w["sink"], w["lb"], w["ng"], w["cw"], w["pw"], w["ps"], w["w_o"], w["ln1_g"], w["ln1_b"])


def _gather_block(dst_scr, src_scr, n_tiles, gi, sb, nb):
    for t in range(DEC_SEQ):
        start = pl.multiple_of(t * nb + gi * sb, sb)
        for j in range(n_tiles):
            dst_scr[j, t * sb:(t + 1) * sb, :] = src_scr[j, pl.ds(start, sb), :]


def _scatter_block(dst_scr, src_scr, n_tiles, gi, sb, nb):
    for t in range(DEC_SEQ):
        start = pl.multiple_of(t * nb + gi * sb, sb)
        for j in range(n_tiles):
            dst_scr[pl.ds(start, sb), j * LANES:(j + 1) * LANES] = src_scr[j, t * sb:(t + 1) * sb, :]


def _sample_mixer_kernel(l, xt_ref, ck_ref, cv_ref, s_ref, cs_ref, pb_ref,
                         w_in_ref, w_hgt_ref, w_kvt_ref, sink_ref, lbc_ref, ngc_ref, cw_ref, pw_ref, ps_ref,
                         w_o_ref, g_ref, b_ref, *rest):
    (y_ref, knew_ref, vnew_ref, snew_ref, cnew_ref, pnew_ref,
     proj_scr, a_scr, kvt_scr, q_scr, f_scr, k_scr, v_scr, gate_scr, o_scr, mix_scr,
     ab_scr, oa_scr, sc_scr, sn_scr, pc_scr, pn_scr) = rest[-22:]
    gi = pl.program_id(0)
    sb = ck_ref.shape[0]
    nb = xt_ref.shape[0] // DEC_SEQ
    half = sb // 2

    @pl.when(gi == 0)
    def _():
        xt = xt_ref[...].astype(BF16)
        pa = _dot(xt, w_in_ref[:, 0:C_BQ])
        for j in range(4):
            a_scr[j] = pa[:, j * LANES:(j + 1) * LANES]
        kvt_scr[...] = _nt_dot(w_kvt_ref[...], xt)
        hgt = _nt_dot(w_hgt_ref[...], xt)
        lbc = lbc_ref[...]
        q_scr[...] = _silu(hgt[0:W_GRP, :])
        sig_pos, sig_neg = _sigmoid_pair(hgt[W_GRP:2 * W_GRP, :])
        f_scr[...] = sig_pos + jnp.maximum(lbc, LB_FLOOR) * sig_neg
        k_scr[...] = (1.0 - lbc) * sig_neg
        v_scr[...] = hgt[2 * W_GRP:3 * W_GRP, :]
        gate_scr[...] = _silu(hgt[3 * W_GRP:4 * W_GRP, :])
        o_scr[...] = jnp.zeros(o_scr.shape, F32)
        proj_scr[...] = _dot(xt, w_in_ref[:, C_CB:D_IN])
        cw = cw_ref[...]
        ps = ps_ref[...]
        cp = lax.broadcasted_iota(jnp.int32, (nb, W_GRP), 1) >> 6
        width = jnp.left_shift(2, cp)
        u = [cs_ref[0], cs_ref[1]]
        ext = [pb_ref[i] for i in range(POOL_BUF)]
        for t in range(DEC_SEQ):
            rows = slice(t * nb, (t + 1) * nb)
            u.append(proj_scr[rows, W_GRP:2 * W_GRP] * proj_scr[rows, 2 * W_GRP:3 * W_GRP])
            ext.append(proj_scr[rows, 3 * W_GRP:4 * W_GRP])
        cnew_ref[0] = u[DEC_SEQ]
        cnew_ref[1] = u[DEC_SEQ + 1]
        for i in range(POOL_BUF):
            pnew_ref[i] = ext[DEC_SEQ + i]
        for t in range(DEC_SEQ):
            rows = slice(t * nb, (t + 1) * nb)
            yc = u[t] * cw[0:1, :] + u[t + 1] * cw[1:2, :] + u[t + 2] * cw[2:3, :]
            mix_scr[rows, 2 * W_GRP:3 * W_GRP] = proj_scr[rows, 0:W_GRP] * yc
            top = POOL_BUF + t
            acc = ext[top]
            sums = {}
            for jj in range(1, 16):
                acc = acc + ext[top - jj]
                if jj + 1 in POOL_WINDOWS:
                    sums[jj + 1] = acc
            win = jnp.where(cp == 0, sums[2], jnp.where(cp == 1, sums[4], jnp.where(cp == 2, sums[8], sums[16])))
            cnt = jnp.minimum(PAST_LEN + t + 1, width).astype(F32)
            pooled = win / cnt - ext[top]
            mix_scr[rows, 3 * W_GRP:4 * W_GRP] = _dot(pooled.astype(BF16), pw_ref[...]) * ps

    _gather_block(ab_scr, a_scr, 4, gi, sb, nb)
    lane = lax.broadcasted_iota(jnp.int32, (PAIR_ROWS, LANES), 1)
    lo = lane < HEAD_DIM
    n_rows = 4 * PAIR_ROWS
    r_c = lax.broadcasted_iota(jnp.int32, (n_rows, 2 * WINDOW), 0)
    c_c = lax.broadcasted_iota(jnp.int32, (n_rows, 2 * WINDOW), 1)
    rel_c = ((r_c >> 1) & 3) + WINDOW - (c_c & (WINDOW - 1))
    ok_c = ((r_c & 1) == (c_c >> 7)) & (rel_c <= WINDOW)
    bias_c = jnp.where(ok_c, -jnp.exp2(-2.0 * ((r_c >> 3).astype(F32) + 1.0)) * rel_c.astype(F32), MASK_VALUE)
    r_n = lax.broadcasted_iota(jnp.int32, (n_rows, PAIR_ROWS), 0)
    c_n = lax.broadcasted_iota(jnp.int32, (n_rows, PAIR_ROWS), 1)
    rel_n = ((r_n >> 1) & 3) - (c_n >> 1)
    ok_n = ((r_n & 1) == (c_n & 1)) & (rel_n >= 0)
    bias_n = jnp.where(ok_n, -jnp.exp2(-2.0 * ((r_n >> 3).astype(F32) + 1.0)) * rel_n.astype(F32), MASK_VALUE)
    pair_rows = lambda p: pl.ds(p, PAIR_ROWS, stride=half)
    for p in range(half):
        q0 = ab_scr[0, pair_rows(p), :] * (HEAD_DIM ** -0.5)
        q1 = ab_scr[1, pair_rows(p), :] * (HEAD_DIM ** -0.5)
        q0r = pltpu.roll(q0, HEAD_DIM, axis=1)
        q1r = pltpu.roll(q1, HEAD_DIM, axis=1)
        zero = jnp.zeros_like(q0)
        q4 = jnp.concatenate([jnp.where(lo, q0, zero), jnp.where(lo, q0r, zero),
                              jnp.where(lo, zero, q1r), jnp.where(lo, zero, q1)], axis=0).astype(BF16)
        kt2 = jnp.concatenate([ck_ref[p], ck_ref[p + half]], axis=1).astype(BF16)
        kn8 = ab_scr[2, pair_rows(p), :].astype(BF16)
        sc_scr[n_rows * p:n_rows * (p + 1), :] = _dot(q4, kt2) + bias_c
        sn_scr[n_rows * p:n_rows * (p + 1), :] = _nt_dot(q4, kn8) + bias_n
    sink_col = _sink_select(sink_ref, l, (lax.broadcasted_iota(jnp.int32, (n_rows * half, 1), 0) >> 3) & 3)
    s_c = sc_scr[...]
    s_n = sn_scr[...]
    m = jnp.maximum(jnp.maximum(jnp.max(s_c, axis=-1, keepdims=True), jnp.max(s_n, axis=-1, keepdims=True)),
                    sink_col)
    p_c = jnp.exp(s_c - m)
    p_n = jnp.exp(s_n - m)
    inv = 1.0 / (jnp.sum(p_c, axis=-1, keepdims=True) + jnp.sum(p_n, axis=-1, keepdims=True)
                 + jnp.exp(sink_col - m))
    pc_scr[...] = (p_c * inv).astype(BF16)
    pn_scr[...] = (p_n * inv).astype(BF16)
    for p in range(half):
        vt2 = jnp.concatenate([cv_ref[p], cv_ref[p + half]], axis=1).astype(BF16)
        vn8 = ab_scr[3, pair_rows(p), :].astype(BF16)
        o_all = (_nt_dot(pc_scr[n_rows * p:n_rows * (p + 1), :], vt2)
                 + _dot(pn_scr[n_rows * p:n_rows * (p + 1), :], vn8))
        o = [o_all[h * PAIR_ROWS:(h + 1) * PAIR_ROWS, :] for h in range(4)]
        oa_scr[0, pair_rows(p), :] = jnp.where(lo, o[0], pltpu.roll(o[1], HEAD_DIM, axis=1))
        oa_scr[1, pair_rows(p), :] = jnp.where(lo, pltpu.roll(o[2], HEAD_DIM, axis=1), o[3])
    _scatter_block(mix_scr, oa_scr, 2, gi, sb, nb)
    lane_w = lax.broadcasted_iota(jnp.int32, (LANES, LANES), 1)
    for b in range(sb):
        seq = gi * sb + b
        for old_ref, out_ref, r0 in ((ck_ref, knew_ref, 0), (cv_ref, vnew_ref, LANES)):
            cols = pltpu.roll(old_ref[b], WINDOW - DEC_SEQ, axis=1)
            for t in range(DEC_SEQ):
                shift = jnp.bitwise_and(WINDOW - DEC_SEQ + t - seq, LANES - 1)
                new_t = pltpu.roll(kvt_scr[r0:r0 + LANES, t * nb:(t + 1) * nb], shift, axis=1)
                cols = jnp.where(lane_w == WINDOW - DEC_SEQ + t, new_t, cols)
            out_ref[b] = cols

    steps_per_head = pl.num_programs(0) // N_HG
    head = gi // steps_per_head
    k_base = head * HEAD_DIM + (gi % steps_per_head) * K_PER_STEP
    v_rows = pl.ds(pl.multiple_of(head * HEAD_DIM, HEAD_DIM), HEAD_DIM)

    def k_body(k8, accs):
        rows8 = pl.ds(pl.multiple_of(k_base + k8 * SUBLANES, SUBLANES), SUBLANES)
        accs = list(accs)
        for j in range(SUBLANES):
            st_rows = pl.ds(pl.multiple_of((k8 * SUBLANES + j) * HEAD_DIM, HEAD_DIM), HEAD_DIM)
            st = s_ref[st_rows, :]
            for t in range(DEC_SEQ):
                cols = slice(t * nb, (t + 1) * nb)
                f8, k8v, q8 = f_scr[rows8, cols], k_scr[rows8, cols], q_scr[rows8, cols]
                st = f8[j:j + 1, :] * st + k8v[j:j + 1, :] * v_scr[v_rows, cols]
                accs[t] = accs[t] + q8[j:j + 1, :] * st
            snew_ref[st_rows, :] = st
        return tuple(accs)

    zero_acc = jnp.zeros((HEAD_DIM, nb), F32)
    accs = lax.fori_loop(0, K_PER_STEP // SUBLANES, k_body, (zero_acc,) * DEC_SEQ)
    for t in range(DEC_SEQ):
        cols = slice(t * nb, (t + 1) * nb)
        o_scr[v_rows, cols] = o_scr[v_rows, cols] + accs[t]

    @pl.when(gi == pl.num_programs(0) - 1)
    def _():
        o = o_scr[...]
        parts = []
        for h in range(N_HG):
            oh = o[h * HEAD_DIM:(h + 1) * HEAD_DIM, :]
            parts.append(oh * lax.rsqrt(jnp.mean(oh * oh, axis=0, keepdims=True) + RMS_EPS))
        ob = jnp.concatenate(parts, axis=0) * ngc_ref[...] * gate_scr[...]
        y = (_dot(mix_scr[:, 0:W_GRP].astype(BF16), w_o_ref[0:W_GRP, :])
             + _tn_dot(ob.astype(BF16), w_o_ref[W_GRP:2 * W_GRP, :])
             + _dot(mix_scr[:, 2 * W_GRP:4 * W_GRP].astype(BF16), w_o_ref[2 * W_GRP:4 * W_GRP, :]))
        y_ref[...] = _layer_norm(ALPHA * xt_ref[...] + y, g_ref[...], b_ref[...])


def _sample_mixer_call(l, xt, ck, cv, s, cs_tm, pb_tm, w, prev):
    m = xt.shape[0]
    nb = m // DEC_SEQ
    sb = SEQ_BLOCK
    n_steps = nb // sb
    n_sc = 4 * PAIR_ROWS * (sb // 2)
    s_rows = HG_ROWS // n_steps
    single = pl.Buffered(1)
    const2 = lambda i: (0, 0)
    out_shape = (
        jax.ShapeDtypeStruct((m, D_MODEL), F32),
        jax.ShapeDtypeStruct((DEPTH, nb, LANES, WINDOW), F32),
        jax.ShapeDtypeStruct((DEPTH, nb, LANES, WINDOW), F32),
        jax.ShapeDtypeStruct((DEPTH, HG_ROWS, nb), F32),
        jax.ShapeDtypeStruct((CONV_W - 1, nb, W_GRP), F32),
        jax.ShapeDtypeStruct((POOL_BUF, nb, W_GRP), F32),
    )
    seq_blk = lambda i: (l, i, 0, 0)
    in_specs = [pl.BlockSpec((m, D_MODEL), const2, pipeline_mode=single),
                pl.BlockSpec((None, sb, LANES, WINDOW), seq_blk),
                pl.BlockSpec((None, sb, LANES, WINDOW), seq_blk),
                pl.BlockSpec((None, s_rows, nb), lambda i: (l, i, 0)),
                _layer_spec((CONV_W - 1, nb, W_GRP), l, single=True),
                _layer_spec((POOL_BUF, nb, W_GRP), l, single=True),
                _layer_spec((D_MODEL, D_IN), l, single=True),
                _layer_spec((4 * W_GRP, D_MODEL), l, single=True),
                _layer_spec((2 * LANES, D_MODEL), l, single=True),
                pl.BlockSpec(memory_space=pltpu.SMEM),
                _layer_spec((W_GRP, 1), l), _layer_spec((W_GRP, 1), l),
                _layer_spec((CONV_W, W_GRP), l), _layer_spec((W_GRP, W_GRP), l), _layer_spec((1, W_GRP), l),
                _layer_spec((D_MODEL, D_MODEL), l, single=True),
                _layer_spec((1, D_MODEL), l), _layer_spec((1, D_MODEL), l)]
    args = [xt, ck, cv, s, cs_tm, pb_tm, w["w_in"], w["w_hgt"], w["w_kvt"], w["sink"], w["lb_col"], w["ng_col"],
            w["cw"], w["pw"], w["ps"], w["w_o"], w["ln1_g"], w["ln1_b"]]
    aliases = {}
    if prev is not None:
        for j, buf in enumerate(prev):
            aliases[len(args)] = 1 + j
            args.append(buf)
            in_specs.append(pl.BlockSpec(memory_space=pl.ANY))
    f32 = lambda *shape: pltpu.VMEM(shape, F32)
    return pl.pallas_call(
        functools.partial(_sample_mixer_kernel, l),
        name="mixer_sample",
        grid=(n_steps,),
        in_specs=in_specs,
        out_specs=(pl.BlockSpec((m, D_MODEL), const2),
                   pl.BlockSpec((None, sb, LANES, WINDOW), seq_blk),
                   pl.BlockSpec((None, sb, LANES, WINDOW), seq_blk),
                   pl.BlockSpec((None, s_rows, nb), lambda i: (l, i, 0)),
                   pl.BlockSpec((CONV_W - 1, nb, W_GRP), lambda i: (0, 0, 0)),
                   pl.BlockSpec((POOL_BUF, nb, W_GRP), lambda i: (0, 0, 0))),
        out_shape=out_shape,
        input_output_aliases=aliases,
        scratch_shapes=[f32(m, 4 * W_GRP),
                        f32(4, m, LANES),
                        f32(2 * LANES, m),
                        f32(W_GRP, m), f32(W_GRP, m), f32(W_GRP, m), f32(W_GRP, m), f32(W_GRP, m),
                        f32(W_GRP, m),
                        f32(m, D_MODEL),
                        f32(4, sb * DEC_SEQ, LANES), f32(2, sb * DEC_SEQ, LANES),
                        f32(n_sc, 2 * WINDOW), f32(n_sc, PAIR_ROWS),
                        pltpu.VMEM((n_sc, 2 * WINDOW), BF16), pltpu.VMEM((n_sc, PAIR_ROWS), BF16)],
        compiler_params=_compiler_params(("arbitrary",)),
    )(*args)


def _xattn_sample_kernel(xt_ref, mk_ref, mv_ref, wq_ref, wo_ref, g_ref, b_ref, o_ref,
                         q_scr, qb_scr, ob_scr, att_scr, s_scr, p_scr):
    gi = pl.program_id(0)
    sb = mk_ref.shape[0]
    nb = xt_ref.shape[0] // DEC_SEQ
    half = sb // 2
    n_rows = N_XH * PAIR_ROWS

    @pl.when(gi == 0)
    def _():
        q = _dot(xt_ref[...].astype(BF16), wq_ref[...]) * (HEAD_DIM ** -0.5)
        q_scr[0] = q[:, 0:LANES]
        q_scr[1] = q[:, LANES:2 * LANES]

    _gather_block(qb_scr, q_scr, 2, gi, sb, nb)
    lane_head = lax.broadcasted_iota(jnp.int32, (PAIR_ROWS, D_X), 1) >> 6
    r_s = lax.broadcasted_iota(jnp.int32, (n_rows, 2 * N_MEM), 0)
    c_s = lax.broadcasted_iota(jnp.int32, (n_rows, 2 * N_MEM), 1)
    own = (r_s & 1) == (c_s >> 8)
    pair_rows = lambda p: pl.ds(p, PAIR_ROWS, stride=half)
    for p in range(half):
        q8 = jnp.concatenate([qb_scr[0, pair_rows(p), :], qb_scr[1, pair_rows(p), :]], axis=1)
        zero = jnp.zeros_like(q8)
        q4 = jnp.concatenate([jnp.where(lane_head == h, q8, zero) for h in range(N_XH)], axis=0).astype(BF16)
        kt2 = jnp.concatenate([mk_ref[p], mk_ref[p + half]], axis=1).astype(BF16)
        s_scr[n_rows * p:n_rows * (p + 1), :] = jnp.where(own, _dot(q4, kt2), MASK_VALUE)
    s_all = s_scr[...]
    m = jnp.max(s_all, axis=-1, keepdims=True)
    pr = jnp.exp(s_all - m)
    p_scr[...] = (pr * (1.0 / jnp.sum(pr, axis=-1, keepdims=True))).astype(BF16)
    for p in range(half):
        vt2 = jnp.concatenate([mv_ref[p], mv_ref[p + half]], axis=1).astype(BF16)
        o_all = _nt_dot(p_scr[n_rows * p:n_rows * (p + 1), :], vt2)
        o8 = jnp.zeros((PAIR_ROWS, D_X), F32)
        for h in range(N_XH):
            o8 = jnp.where(lane_head == h, o_all[h * PAIR_ROWS:(h + 1) * PAIR_ROWS, :], o8)
        ob_scr[0, pair_rows(p), :] = o8[:, 0:LANES]
        ob_scr[1, pair_rows(p), :] = o8[:, LANES:2 * LANES]
    _scatter_block(att_scr, ob_scr, 2, gi, sb, nb)

    @pl.when(gi == pl.num_programs(0) - 1)
    def _():
        y = _dot(att_scr[...].astype(BF16), wo_ref[...])
        o_ref[...] = _layer_norm(ALPHA * xt_ref[...] + y, g_ref[...], b_ref[...])


def _xattn_sample_call(l, xt, mk, mv, w):
    m = xt.shape[0]
    nb = m // DEC_SEQ
    sb = SEQ_BLOCK
    n_sc = N_XH * PAIR_ROWS * (sb // 2)
    const2 = lambda i: (0, 0)
    return pl.pallas_call(
        _xattn_sample_kernel,
        name="xattn_sample",
        grid=(nb // sb,),
        in_specs=[pl.BlockSpec((m, D_MODEL), const2, pipeline_mode=pl.Buffered(1)),
                  pl.BlockSpec((None, sb, D_X, N_MEM), lambda i: (l, i, 0, 0)),
                  pl.BlockSpec((None, sb, D_X, N_MEM), lambda i: (l, i, 0, 0)),
                  _layer_spec((D_MODEL, D_X), l), _layer_spec((D_X, D_MODEL), l),
                  _layer_spec((1, D_MODEL), l), _layer_spec((1, D_MODEL), l)],
        out_specs=pl.BlockSpec((m, D_MODEL), const2),
        out_shape=jax.ShapeDtypeStruct((m, D_MODEL), F32),
        scratch_shapes=[pltpu.VMEM((2, m, LANES), F32),
                        pltpu.VMEM((2, sb * DEC_SEQ, LANES), F32),
                        pltpu.VMEM((2, sb * DEC_SEQ, LANES), F32),
                        pltpu.VMEM((m, D_X), F32),
                        pltpu.VMEM((n_sc, 2 * N_MEM), F32),
                        pltpu.VMEM((n_sc, 2 * N_MEM), BF16)],
        compiler_params=_compiler_params(("arbitrary",)),
    )(xt, mk, mv, w["w_xq"], w["w_xo"], w["ln2_g"], w["ln2_b"])


def _hgrn_lower_bounds(lb_param):
    p = jax.nn.softmax(lb_param.astype(F32), axis=0)
    return jnp.cumsum(p, axis=0) - p[0:1]


def _prepare_weights(w_in, attn_sink, hgrn_lb, hgrn_norm_g, conv_w, pool_w, pool_scale, w_o, ln1_g, ln1_b,
                     w_xq, w_xo, ln2_g, ln2_b, w_gate, w_up, w_down, ln3_g, ln3_b):
    bf = lambda a: a.astype(BF16)
    rows = lambda a: a.astype(F32)[:, None, :]
    cols = lambda a: a.astype(F32)[:, :, None]
    lb_all = _hgrn_lower_bounds(hgrn_lb)
    eye = jnp.eye(len(POOL_WINDOWS), dtype=pool_w.dtype)
    pw = (pool_w[:, :, :, None, :] * eye[None, :, None, :, None]).reshape(DEPTH, W_GRP, W_GRP)
    return dict(
        w_in=bf(w_in),
        w_hgt=bf(w_in[:, :, C_BQ:C_CB].transpose(0, 2, 1)),
        w_kvt=bf(w_in[:, :, C_AK:C_BQ].transpose(0, 2, 1)),
        sink=attn_sink.astype(F32),
        lb=rows(lb_all), lb_col=cols(lb_all), ng=rows(hgrn_norm_g), ng_col=cols(hgrn_norm_g),
        cw=conv_w.astype(F32), pw=bf(pw), ps=rows(pool_scale),
        w_o=bf(w_o), ln1_g=rows(ln1_g), ln1_b=rows(ln1_b),
        w_xq=bf(w_xq), w_xo=bf(w_xo), ln2_g=rows(ln2_g), ln2_b=rows(ln2_b),
        w_gate=bf(w_gate), w_up=bf(w_up), w_down=bf(w_down), ln3_g=rows(ln3_g), ln3_b=rows(ln3_b))


def _prompt_layer(l, x, mkv, w):
    bsz, t, _ = x.shape
    x, kn, vn, sn, cn, pn = _prompt_mixer_call(l, x, w)
    x = _xattn_prompt_call(l, x, mkv, w)
    x = _ffn_call(l, x.reshape(bsz * t, D_MODEL), w).reshape(bsz, t, D_MODEL)
    return x, kn, vn, sn, cn, pn


def _sample_layer(l, xt, ck, cv, s, cs_tm, pb_tm, mk, mv, w, prev):
    xt, kn, vn, sn, cn, pn = _sample_mixer_call(l, xt, ck, cv, s, cs_tm, pb_tm, w, prev)
    xt = _xattn_sample_call(l, xt, mk, mv, w)
    xt = _ffn_call(l, xt, w)
    return xt, kn, vn, sn, cn, pn


def kernel(x_prompt, x_sample, cache_swa_k, cache_swa_v, state_hgrn, state_conv, state_pool, cache_mem_k,
           cache_mem_v, mem_prompt, emb_ln_g, emb_ln_b, w_in, attn_sink, hgrn_lb, hgrn_norm_g, conv_w, pool_w,
           pool_scale, w_o, ln1_g, ln1_b, w_xq, w_xk, w_xv, w_xo, ln2_g, ln2_b, w_gate, w_up, w_down, ln3_g,
           ln3_b):
    bp, t, _ = x_prompt.shape
    bs, ts, _ = x_sample.shape
    w = _prepare_weights(w_in, attn_sink, hgrn_lb, hgrn_norm_g, conv_w, pool_w, pool_scale, w_o, ln1_g, ln1_b,
                         w_xq, w_xo, ln2_g, ln2_b, w_gate, w_up, w_down, ln3_g, ln3_b)
    emb_g = emb_ln_g.reshape(1, D_MODEL).astype(F32)
    emb_b = emb_ln_b.reshape(1, D_MODEL).astype(F32)
    hp = _ln_call(x_prompt.reshape(bp * t, D_MODEL), emb_g, emb_b).reshape(bp, t, D_MODEL)
    w_kv = jnp.concatenate([w_xk, w_xv], axis=2).transpose(1, 0, 2).reshape(D_MODEL, DEPTH * 2 * D_X).astype(BF16)
    mkv = _matmul_call(mem_prompt.reshape(bp * N_MEM, D_MODEL), w_kv).reshape(bp, N_MEM, DEPTH * 2 * D_X)
    hs = _ln_call(x_sample.transpose(1, 0, 2).reshape(ts * bs, D_MODEL), emb_g, emb_b)
    ck = cache_swa_k.transpose(0, 1, 3, 4, 2).reshape(DEPTH, bs, LANES, WINDOW)
    cv = cache_swa_v.transpose(0, 1, 3, 4, 2).reshape(DEPTH, bs, LANES, WINDOW)
    st = state_hgrn.transpose(0, 2, 3, 4, 1).reshape(DEPTH, HG_ROWS, bs)
    cs_tm = state_conv.transpose(0, 2, 1, 3)
    pb_tm = state_pool.transpose(0, 2, 1, 3)
    mk_s = cache_mem_k.transpose(0, 1, 3, 4, 2).reshape(DEPTH, bs, D_X, N_MEM)
    mv_s = cache_mem_v.transpose(0, 1, 3, 4, 2).reshape(DEPTH, bs, D_X, N_MEM)
    outs = [[] for _ in range(5)]
    souts = [[] for _ in range(2)]
    prev = None
    for l in range(DEPTH):
        res = _prompt_layer(l, hp, mkv, w)
        hp = res[0]
        for acc, r in zip(outs, res[1:]):
            acc.append(r)
        sres = _sample_layer(l, hs, ck, cv, st, cs_tm, pb_tm, mk_s, mv_s, w, prev)
        hs = sres[0]
        prev = sres[1:4]
        for acc, r in zip(souts, sres[4:]):
            acc.append(r)
    pk, pv, ps, pc, pp = [jnp.stack(o) for o in outs]
    sk, sv, ss = prev
    sc, sp = [jnp.stack(o) for o in souts]
    mem_out = mkv.reshape(bp, N_MEM, DEPTH, 2, N_XH, HEAD_DIM).transpose(3, 2, 0, 1, 4, 5)
    swa_out = lambda a: a.reshape(DEPTH, bs, N_KV, HEAD_DIM, WINDOW).transpose(0, 1, 4, 2, 3)
    return (hp, hs.reshape(ts, bs, D_MODEL).transpose(1, 0, 2),
            pk.reshape(DEPTH, bp, WINDOW, N_KV, HEAD_DIM), pv.reshape(DEPTH, bp, WINDOW, N_KV, HEAD_DIM),
            ps.reshape(DEPTH, bp, N_HG, HEAD_DIM, HEAD_DIM), pc, pp,
            mem_out[0], mem_out[1],
            swa_out(sk), swa_out(sv),
            ss.reshape(DEPTH, N_HG, HEAD_DIM, HEAD_DIM, bs).transpose(0, 4, 1, 2, 3),
            sc.transpose(0, 2, 1, 3), sp.transpose(0, 2, 1, 3))
```

```python
import functools

import jax
import jax.numpy as jnp
from jax import lax
from jax.experimental import pallas as pl
from jax.experimental.pallas import tpu as pltpu

F32 = jnp.float32
BF16 = jnp.bfloat16

D_MODEL = 1024
DEPTH = 4
HEAD_DIM = 64
W_GRP = 256
N_KV = 2
WINDOW = 128
N_HG = 4
CONV_W = 3
POOL_WINDOWS = (2, 4, 8, 16)
POOL_BUF = 15
N_MEM = 256
N_XH = 4
D_X = 256
D_FF = 2816
D_IN = 2560
DEC_SEQ = 4
ALPHA = (2 * DEPTH) ** 0.25
LN_EPS = 1e-5
RMS_EPS = 1e-6
MASK_VALUE = -1e30
LB_FLOOR = 1e-30
PAST_LEN = 8192

C_AQ, C_AK, C_AV = 0, 256, 384
C_BQ, C_BF, C_BI, C_BG = 512, 768, 1024, 1280
C_CB, C_CC, C_CH = 1536, 1792, 2048
C_DV = 2304

LANES = 128
SUBLANES = 8
VMEM_LIMIT_BYTES = 56 * 1024 * 1024

TOKEN_BLOCK = 512
SUB_BLOCKS = 2
HG_CHUNK = 64
HG_MID = HG_CHUNK // 2 - 1
HG_MAX_EXPONENT = 80.0
HG_ROWS = N_HG * HEAD_DIM * HEAD_DIM
SEQ_BLOCK = 16
PAIR_ROWS = 2 * DEC_SEQ
K_PER_STEP = HG_ROWS // HEAD_DIM * SEQ_BLOCK // 128


def _nt_dot(a, b):
    return lax.dot_general(a, b, (((1,), (1,)), ((), ())), preferred_element_type=F32)


def _tn_dot(a, b):
    return lax.dot_general(a, b, (((0,), (0,)), ((), ())), preferred_element_type=F32)


def _dot(a, b):
    return jnp.dot(a, b, preferred_element_type=F32)


def _layer_norm(x, g, b):
    mu = jnp.mean(x, axis=-1, keepdims=True)
    xc = x - mu
    var = jnp.mean(xc * xc, axis=-1, keepdims=True)
    return xc * lax.rsqrt(var + LN_EPS) * g + b


def _sigmoid_pair(z):
    e = jnp.exp(-jnp.abs(z))
    inv = 1.0 / (1.0 + e)
    small = e * inv
    pos = z >= 0
    return jnp.where(pos, inv, small), jnp.where(pos, small, inv)


def _silu(z):
    s, _ = _sigmoid_pair(z)
    return z * s


def _silu_tanh(z):
    return z * (0.5 + 0.5 * jnp.tanh(0.5 * z))


def _full_spec(shape):
    nd = len(shape)
    return pl.BlockSpec(shape, lambda *_: (0,) * nd)


def _layer_spec(shape, l, single=False):
    nd = len(shape)
    index = lambda *_: (l,) + (0,) * nd
    if single:
        return pl.BlockSpec((None,) + tuple(shape), index, pipeline_mode=pl.Buffered(1))
    return pl.BlockSpec((None,) + tuple(shape), index)


def _compiler_params(sem):
    return pltpu.CompilerParams(dimension_semantics=sem, vmem_limit_bytes=VMEM_LIMIT_BYTES)


class _MatmulQueue:
    def __init__(self):
        self.items = []

    def add(self, fn, *args):
        self.items.append((fn, args))

    def issue(self, n=1):
        for _ in range(min(n, len(self.items))):
            fn, args = self.items.pop(0)
            fn(*args)

    def flush(self):
        self.issue(len(self.items))


def _ln_kernel(x_ref, g_ref, b_ref, o_ref):
    o_ref[...] = _layer_norm(x_ref[...], g_ref[...], b_ref[...])


def _ln_call(x2d, g, b):
    m = x2d.shape[0]
    return pl.pallas_call(
        _ln_kernel,
        name="input_ln",
        grid=(m // TOKEN_BLOCK,),
        in_specs=[pl.BlockSpec((TOKEN_BLOCK, D_MODEL), lambda i: (i, 0)),
                  _full_spec((1, D_MODEL)), _full_spec((1, D_MODEL))],
        out_specs=pl.BlockSpec((TOKEN_BLOCK, D_MODEL), lambda i: (i, 0)),
        out_shape=jax.ShapeDtypeStruct((m, D_MODEL), F32),
        compiler_params=_compiler_params(("arbitrary",)),
    )(x2d, g, b)


def _matmul_kernel(x_ref, w_ref, o_ref):
    o_ref[...] = _dot(x_ref[...].astype(BF16), w_ref[...])


def _matmul_call(x2d, w):
    m, k = x2d.shape
    n = w.shape[1]
    return pl.pallas_call(
        _matmul_kernel,
        name="mem_proj",
        grid=(m // TOKEN_BLOCK,),
        in_specs=[pl.BlockSpec((TOKEN_BLOCK, k), lambda i: (i, 0)), _full_spec((k, n))],
        out_specs=pl.BlockSpec((TOKEN_BLOCK, n), lambda i: (i, 0)),
        out_shape=jax.ShapeDtypeStruct((m, n), F32),
        compiler_params=_compiler_params(("arbitrary",)),
    )(x2d, w)


def _ffn_kernel(x_ref, wg_ref, wu_ref, wd_ref, g_ref, b_ref, o_ref):
    x = x_ref[...]
    xb = x.astype(BF16)
    h = _silu(_dot(xb, wg_ref[...])) * _dot(xb, wu_ref[...])
    y = _dot(h.astype(BF16), wd_ref[...])
    o_ref[...] = _layer_norm(ALPHA * x + y, g_ref[...], b_ref[...])


def _ffn_call(l, x2d, w):
    m = x2d.shape[0]
    tb = min(2 * TOKEN_BLOCK, m)
    return pl.pallas_call(
        _ffn_kernel,
        name="ffn",
        grid=(m // tb,),
        in_specs=[pl.BlockSpec((tb, D_MODEL), lambda i: (i, 0)),
                  _layer_spec((D_MODEL, D_FF), l, single=True),
                  _layer_spec((D_MODEL, D_FF), l, single=True),
                  _layer_spec((D_FF, D_MODEL), l, single=True),
                  _layer_spec((1, D_MODEL), l), _layer_spec((1, D_MODEL), l)],
        out_specs=pl.BlockSpec((tb, D_MODEL), lambda i: (i, 0)),
        out_shape=jax.ShapeDtypeStruct((m, D_MODEL), F32),
        compiler_params=_compiler_params(("arbitrary",)),
    )(x2d, w["w_gate"], w["w_up"], w["w_down"], w["ln3_g"], w["ln3_b"])


def _xattn_prompt_kernel(x_ref, mk_ref, mv_ref, wq_ref, wo_ref, g_ref, b_ref, o_ref, q_scr, att_scr, yo_scr):
    tb = x_ref.shape[0]
    hb = tb // SUB_BLOCKS
    mk = mk_ref[...].astype(BF16)
    mv = mv_ref[...].astype(BF16)
    lane_head = lax.broadcasted_iota(jnp.int32, (hb, D_X), 1) >> 6
    sub = lambda r: slice(r * hb, (r + 1) * hb)

    def project(r):
        q_scr[sub(r), :] = _dot(x_ref[sub(r), :].astype(BF16), wq_ref[...]) * (HEAD_DIM ** -0.5)

    def finish(r, c0, c1):
        yo_scr[sub(r), c0:c1] = _dot(att_scr[sub(r), :], wo_ref[:, c0:c1])

    def finish_ln(r):
        o_ref[sub(r), :] = _layer_norm(ALPHA * x_ref[sub(r), :] + yo_scr[sub(r), :], g_ref[...], b_ref[...])

    out_cols = ((0, D_MODEL // 2), (D_MODEL // 2, D_MODEL))
    project(0)
    queue = _MatmulQueue()
    for r in range(SUB_BLOCKS):
        if r >= 1:
            for cols in out_cols:
                queue.add(finish, r - 1, *cols)
        if r + 1 < SUB_BLOCKS:
            queue.add(project, r + 1)
        q = q_scr[sub(r), :]
        o = jnp.zeros_like(q)
        for h in range(N_XH):
            qh = jnp.where(lane_head == h, q, 0.0).astype(BF16)
            s = _nt_dot(mk, qh)
            m = jnp.max(s, axis=0, keepdims=True)
            p = jnp.exp(s - m)
            den = jnp.sum(p, axis=0, keepdims=True)
            p = (p * (1.0 / den)).astype(BF16)
            o = jnp.where(lane_head == h, _tn_dot(p, mv), o)
            queue.issue()
        att_scr[sub(r), :] = o.astype(BF16)
        queue.flush()
        if r >= 1:
            finish_ln(r - 1)
    for cols in out_cols:
        finish(SUB_BLOCKS - 1, *cols)
    finish_ln(SUB_BLOCKS - 1)


def _xattn_prompt_call(l, x, mkv, w):
    bsz, t, _ = x.shape
    tb = min(2 * TOKEN_BLOCK, t)
    return pl.pallas_call(
        _xattn_prompt_kernel,
        name="xattn_prompt",
        grid=(bsz, t // tb),
        in_specs=[pl.BlockSpec((None, tb, D_MODEL), lambda i, j: (i, j, 0)),
                  pl.BlockSpec((None, N_MEM, D_X), lambda i, j: (i, 0, 2 * l)),
                  pl.BlockSpec((None, N_MEM, D_X), lambda i, j: (i, 0, 2 * l + 1)),
                  _layer_spec((D_MODEL, D_X), l), _layer_spec((D_X, D_MODEL), l),
                  _layer_spec((1, D_MODEL), l), _layer_spec((1, D_MODEL), l)],
        out_specs=pl.BlockSpec((None, tb, D_MODEL), lambda i, j: (i, j, 0)),
        out_shape=jax.ShapeDtypeStruct((bsz, t, D_MODEL), F32),
        scratch_shapes=[pltpu.VMEM((tb, D_X), F32), pltpu.VMEM((tb, D_X), BF16), pltpu.VMEM((tb, D_MODEL), F32)],
        compiler_params=_compiler_params(("arbitrary", "arbitrary")),
    )(x, mkv, mkv, w["w_xq"], w["w_xo"], w["ln2_g"], w["ln2_b"])


def _swa_bias_table():
    c = lax.broadcasted_iota(jnp.int32, (2 * WINDOW, 4 * WINDOW), 0)
    r = lax.broadcasted_iota(jnp.int32, (2 * WINDOW, 4 * WINDOW), 1)
    head = r >> 7
    rel = (r & (WINDOW - 1)) + WINDOW - c
    slope = jnp.exp2(-2.0 * (head.astype(F32) + 1.0))
    valid = (rel >= 0) & (rel <= WINDOW)
    return jnp.where(valid, -slope * rel.astype(F32), MASK_VALUE)


def _sink_select(sink_ref, l, head):
    return jnp.where(head == 0, sink_ref[l, 0],
                     jnp.where(head == 1, sink_ref[l, 1], jnp.where(head == 2, sink_ref[l, 2], sink_ref[l, 3])))


def _prompt_mixer_kernel(l, x_ref, w_in_ref, sink_ref, lb_ref, ng_ref, cw_ref, pw_ref, ps_ref, w_o_ref,
                         g_ref, b_ref,
                         y_ref, knew_ref, vnew_ref, snew_ref, cnew_ref, pnew_ref,
                         proj_scr, kext_scr, vext_scr, st_scr, u_scr, p_scr, bias_scr, mix_scr, hg_scr,
                         hq_scr, hk_scr, hv_scr, ghl_scr, cum_scr, qp_scr, kp_scr, qs_scr, ks_scr, dec_scr, inc_scr,
                         stb_scr, a_scr, yo_scr, st0_scr):
    tb = x_ref.shape[0]
    hb = tb // SUB_BLOCKS
    n_qb = hb // WINDOW
    n_ch = hb // HG_CHUNK
    bi = pl.program_id(0)
    ti = pl.program_id(1)
    last = ti == pl.num_programs(1) - 1

    @pl.when((bi == 0) & (ti == 0))
    def _():
        bias_scr[...] = _swa_bias_table()

    @pl.when(ti == 0)
    def _():
        kext_scr[0:WINDOW, :] = jnp.zeros((WINDOW, LANES), BF16)
        vext_scr[0:WINDOW, :] = jnp.zeros((WINDOW, LANES), BF16)
        st_scr[...] = jnp.zeros(st_scr.shape, F32)
        u_scr[0:SUBLANES, :] = jnp.zeros((SUBLANES, W_GRP), F32)
        p_scr[0:16, :] = jnp.zeros((16, W_GRP), F32)

    def project(r, c0, c1):
        rows = slice(r * hb, (r + 1) * hb)
        proj_scr[rows, c0:c1] = _dot(x_ref[rows, :].astype(BF16), w_in_ref[:, c0:c1])

    def finish(r, c0, c1):
        rows = slice(r * hb, (r + 1) * hb)
        yo_scr[rows, c0:c1] = _dot(mix_scr[rows, :], w_o_ref[:, c0:c1])

    def finish_ln(r):
        rows = slice(r * hb, (r + 1) * hb)
        y_ref[rows, :] = _layer_norm(ALPHA * x_ref[rows, :] + yo_scr[rows, :], g_ref[...], b_ref[...])

    proj_cols = ((0, C_BQ), (C_BQ, C_BI), (C_BI, C_CB), (C_CB, C_CH), (C_CH, D_IN))
    out_cols = ((0, D_MODEL // 2), (D_MODEL // 2, D_MODEL))
    for cols in proj_cols:
        project(0, *cols)
    queue = _MatmulQueue()

    lane = lax.broadcasted_iota(jnp.int32, (WINDOW, LANES), 1)
    lo = lane < HEAD_DIM
    key_row = lax.broadcasted_iota(jnp.int32, (2 * WINDOW, 4 * WINDOW), 0)
    sink_row = _sink_select(sink_ref, l, lax.broadcasted_iota(jnp.int32, (1, 4 * WINDOW), 1) >> 7)
    lb = lb_ref[...]
    lbf = jnp.maximum(lb, LB_FLOOR)
    one_m_lb = 1.0 - lb
    ng = ng_ref[...]
    r256 = lax.broadcasted_iota(jnp.int32, (W_GRP, W_GRP), 0)
    c256 = lax.broadcasted_iota(jnp.int32, (W_GRP, W_GRP), 1)
    same_head = (r256 >> 6) == (c256 >> 6)
    head_ones = jnp.where(same_head, 1.0, 0.0).astype(BF16)
    same_head_b = head_ones > 0
    zero_b = jnp.zeros((W_GRP, W_GRP), BF16)
    rc = lax.broadcasted_iota(jnp.int32, (HG_CHUNK, W_GRP), 0)
    cc = lax.broadcasted_iota(jnp.int32, (HG_CHUNK, W_GRP), 1)
    causal = (cc & (HG_CHUNK - 1)) <= rc
    r64 = lax.broadcasted_iota(jnp.int32, (HG_CHUNK, HG_CHUNK), 0)
    c64 = lax.broadcasted_iota(jnp.int32, (HG_CHUNK, HG_CHUNK), 1)
    tril = jnp.where(c64 <= r64, 1.0, 0.0).astype(BF16)
    cw = cw_ref[...]
    rp = lax.broadcasted_iota(jnp.int32, (hb, W_GRP), 0)
    grp = lax.broadcasted_iota(jnp.int32, (hb, W_GRP), 1) >> 6
    width = jnp.left_shift(2, grp)
    chunk = lambda c: slice(c * HG_CHUNK, (c + 1) * HG_CHUNK)

    def hgrn_output(r):
        rows = slice(r * hb, (r + 1) * hb)
        o = hg_scr[...]
        ms = _dot((o * o).astype(BF16), head_ones) * (1.0 / HEAD_DIM)
        o = o * lax.rsqrt(ms + RMS_EPS) * ng
        mix_scr[rows, W_GRP:2 * W_GRP] = (o * _silu_tanh(proj_scr[rows, C_BG:C_BG + W_GRP])).astype(BF16)

    st0_scr[...] = st_scr[...]
    span = jnp.zeros((1, W_GRP), F32)
    for r in range(SUB_BLOCKS):
        r0 = r * hb
        sub = slice(r0, r0 + hb)
        if r >= 1:
            for cols in out_cols:
                queue.add(finish, r - 1, *cols)
        if r + 1 < SUB_BLOCKS:
            for cols in proj_cols:
                queue.add(project, r + 1, *cols)

        kext_scr[WINDOW:WINDOW + hb, :] = proj_scr[sub, C_AK:C_AK + LANES].astype(BF16)
        vext_scr[WINDOW:WINDOW + hb, :] = proj_scr[sub, C_AV:C_AV + LANES].astype(BF16)
        for j in range(n_qb):
            rows = slice(r0 + j * WINDOW, r0 + (j + 1) * WINDOW)
            q0 = proj_scr[rows, 0:LANES] * (HEAD_DIM ** -0.5)
            q1 = proj_scr[rows, LANES:2 * LANES] * (HEAD_DIM ** -0.5)
            q0r = pltpu.roll(q0, HEAD_DIM, axis=1)
            q1r = pltpu.roll(q1, HEAD_DIM, axis=1)
            zero = jnp.zeros_like(q0)
            q4 = jnp.concatenate([jnp.where(lo, q0, zero), jnp.where(lo, q0r, zero),
                                  jnp.where(lo, zero, q1r), jnp.where(lo, zero, q1)], axis=0).astype(BF16)
            kj = kext_scr[j * WINDOW:(j + 2) * WINDOW, :]
            vj = vext_scr[j * WINDOW:(j + 2) * WINDOW, :]
            s = _nt_dot(kj, q4) + bias_scr[...]
            if r == 0 and j == 0:
                s = jnp.where((ti == 0) & (key_row < WINDOW), MASK_VALUE, s)
            m = jnp.maximum(jnp.max(s, axis=0, keepdims=True), sink_row)
            p = jnp.exp(s - m)
            den = jnp.sum(p, axis=0, keepdims=True) + jnp.exp(sink_row - m)
            p = (p * (1.0 / den)).astype(BF16)
            o_all = _tn_dot(p, vj)
            o = [o_all[h * WINDOW:(h + 1) * WINDOW, :] for h in range(4)]
            mix_scr[rows, 0:LANES] = jnp.where(lo, o[0], pltpu.roll(o[1], HEAD_DIM, axis=1)).astype(BF16)
            mix_scr[rows, LANES:2 * LANES] = jnp.where(lo, pltpu.roll(o[2], HEAD_DIM, axis=1), o[3]).astype(BF16)
            queue.issue()
        kext_scr[0:WINDOW, :] = kext_scr[hb:hb + WINDOW, :]
        vext_scr[0:WINDOW, :] = vext_scr[hb:hb + WINDOW, :]

        hq_scr[...] = _silu_tanh(proj_scr[sub, C_BQ:C_BQ + W_GRP])
        queue.issue()
        sig_pos, sig_neg = _sigmoid_pair(proj_scr[sub, C_BF:C_BF + W_GRP])
        g = jnp.log(sig_pos + lbf * sig_neg)
        hk_scr[...] = one_m_lb * sig_neg
        g_hi = g.astype(BF16)
        ghl_scr[:, 0:W_GRP] = g_hi
        ghl_scr[:, W_GRP:2 * W_GRP] = (g - g_hi.astype(F32)).astype(BF16)
        hv_scr[...] = proj_scr[sub, C_BI:C_BI + W_GRP].astype(BF16)
        for c in range(n_ch):
            cum2 = _dot(tril, ghl_scr[chunk(c), :])
            cum_scr[chunk(c), :] = cum2[:, 0:W_GRP] + cum2[:, W_GRP:2 * W_GRP]
        for c in range(n_ch):
            cum = cum_scr[chunk(c), :]
            ref = cum[HG_MID:HG_MID + 1, :]
            tot = cum[HG_CHUNK - 1:HG_CHUNK, :]
            span = jnp.maximum(span, jnp.maximum(cum[0:1, :] - ref, ref - tot))
            qp = hq_scr[chunk(c), :] * jnp.exp(cum - ref)
            kp = hk_scr[chunk(c), :] * jnp.exp(ref - cum)
            qp_scr[chunk(c), :] = qp.astype(BF16)
            kp_scr[chunk(c), :] = kp.astype(BF16)
            qs_scr[chunk(c), :] = (qp * jnp.exp(ref)).astype(BF16)
            ks_scr[chunk(c), :] = (kp * jnp.exp(tot - ref)).astype(BF16)
            dec_scr[c] = jnp.broadcast_to(jnp.exp(tot), (SUBLANES, W_GRP))
        queue.issue()
        for c in range(n_ch):
            bk = jnp.where(same_head_b, jnp.concatenate([kp_scr[chunk(c), :]] * N_HG, axis=0), zero_b)
            a = jnp.where(causal, _nt_dot(qp_scr[chunk(c), :], bk), 0.0)
            a_scr[chunk(c), :] = a.astype(BF16)
        for c in range(n_ch):
            inc_scr[c] = jnp.where(same_head, _tn_dot(hv_scr[chunk(c), :], ks_scr[chunk(c), :]), 0.0)
        for c in range(n_ch):
            bv = jnp.where(same_head_b, jnp.concatenate([hv_scr[chunk(c), :]] * N_HG, axis=0), zero_b)
            hg_scr[chunk(c), :] = _dot(a_scr[chunk(c), :], bv)
        st = st_scr[...]
        for c in range(n_ch):
            stb_scr[c] = st.astype(BF16)
            st = st * dec_scr[c, 0:1, :] + inc_scr[c]
        st_scr[...] = st
        queue.issue()
        for c in range(n_ch):
            hg_scr[chunk(c), :] = hg_scr[chunk(c), :] + _nt_dot(qs_scr[chunk(c), :], stb_scr[c])

        hgrn_output(r)
        queue.issue()

        u_scr[SUBLANES:SUBLANES + hb, :] = proj_scr[sub, C_CC:C_CC + W_GRP] * proj_scr[sub, C_CH:C_CH + W_GRP]
        yc = (u_scr[SUBLANES - 2:SUBLANES - 2 + hb, :] * cw[0:1, :]
              + u_scr[SUBLANES - 1:SUBLANES - 1 + hb, :] * cw[1:2, :]
              + u_scr[SUBLANES:SUBLANES + hb, :] * cw[2:3, :])
        mix_scr[sub, 2 * W_GRP:3 * W_GRP] = (proj_scr[sub, C_CB:C_CB + W_GRP] * yc).astype(BF16)
        u_scr[0:SUBLANES, :] = u_scr[hb:hb + SUBLANES, :]
        queue.issue()

        dv = proj_scr[sub, C_DV:C_DV + W_GRP]
        p_scr[16:16 + hb, :] = dv
        ext = p_scr[...]
        s2 = ext + pltpu.roll(ext, 1, axis=0)
        s4 = s2 + pltpu.roll(s2, 2, axis=0)
        s8 = s4 + pltpu.roll(s4, 4, axis=0)
        s16 = s8 + pltpu.roll(s8, 8, axis=0)
        win = jnp.where(grp == 0, s2[16:], jnp.where(grp == 1, s4[16:], jnp.where(grp == 2, s8[16:], s16[16:])))
        cnt = jnp.minimum(ti * tb + r0 + rp + 1, width).astype(F32)
        pooled = win / cnt - dv
        yd = _dot(pooled.astype(BF16), pw_ref[...]) * ps_ref[...]
        mix_scr[sub, 3 * W_GRP:4 * W_GRP] = yd.astype(BF16)
        p_scr[0:16, :] = p_scr[hb:hb + 16, :]
        queue.flush()
        if r >= 1:
            finish_ln(r - 1)

    for cols in out_cols:
        finish(SUB_BLOCKS - 1, *cols)
    finish_ln(SUB_BLOCKS - 1)

    @pl.when(jnp.max(span) > HG_MAX_EXPONENT)
    def _():
        st_scr[...] = st0_scr[...]
        sub8 = lax.broadcasted_iota(jnp.int32, (SUBLANES, W_GRP), 0)
        for r in range(SUB_BLOCKS):
            def tile_body(i, carry, r=r):
                rows8 = pl.ds(pl.multiple_of(i * SUBLANES, SUBLANES), SUBLANES)
                prow = pl.ds(pl.multiple_of(r * hb + i * SUBLANES, SUBLANES), SUBLANES)
                sp8, sn8 = _sigmoid_pair(proj_scr[prow, C_BF:C_BF + W_GRP])
                f8 = sp8 + lbf * sn8
                k8 = (one_m_lb * sn8).astype(BF16)
                v8 = proj_scr[prow, C_BI:C_BI + W_GRP]
                q8 = _silu_tanh(proj_scr[prow, C_BQ:C_BQ + W_GRP]).astype(BF16)
                o8 = jnp.zeros((SUBLANES, W_GRP), F32)
                for j in range(SUBLANES):
                    vj = jnp.where(sub8 == j, v8, 0.0).astype(BF16)
                    stj = st_scr[...] * f8[j:j + 1, :] + jnp.where(same_head, _tn_dot(vj, k8), 0.0)
                    st_scr[...] = stj
                    o8 = jnp.where(sub8 == j, _nt_dot(q8, stj.astype(BF16)), o8)
                hg_scr[rows8, :] = o8
                return carry

            lax.fori_loop(0, hb // SUBLANES, tile_body, 0)
            hgrn_output(r)
            for cols in out_cols:
                finish(r, *cols)
            finish_ln(r)

    @pl.when(last)
    def _():
        knew_ref[...] = proj_scr[tb - WINDOW:tb, C_AK:C_AK + LANES]
        vnew_ref[...] = proj_scr[tb - WINDOW:tb, C_AV:C_AV + LANES]
        s_t = st_scr[...].T
        for h in range(N_HG):
            snew_ref[h * HEAD_DIM:(h + 1) * HEAD_DIM, :] = (
                s_t[h * HEAD_DIM:(h + 1) * HEAD_DIM, h * HEAD_DIM:(h + 1) * HEAD_DIM])
        cnew_ref[...] = u_scr[SUBLANES - 2:SUBLANES, :]
        pnew_ref[...] = p_scr[1:16, :]


def _prompt_mixer_call(l, x, w):
    bsz, t, _ = x.shape
    tb = min(2 * TOKEN_BLOCK, t)
    hb = tb // SUB_BLOCKS
    n_ch = hb // HG_CHUNK
    row = lambda i, j: (i, 0, 0)
    out_shape = (
        jax.ShapeDtypeStruct((bsz, t, D_MODEL), F32),
        jax.ShapeDtypeStruct((bsz, WINDOW, LANES), F32),
        jax.ShapeDtypeStruct((bsz, WINDOW, LANES), F32),
        jax.ShapeDtypeStruct((bsz, W_GRP, HEAD_DIM), F32),
        jax.ShapeDtypeStruct((bsz, CONV_W - 1, W_GRP), F32),
        jax.ShapeDtypeStruct((bsz, POOL_BUF, W_GRP), F32),
    )
    f32 = lambda *shape: pltpu.VMEM(shape, F32)
    bf16 = lambda *shape: pltpu.VMEM(shape, BF16)
    return pl.pallas_call(
        functools.partial(_prompt_mixer_kernel, l),
        name="mixer_prompt",
        grid=(bsz, t // tb),
        in_specs=[pl.BlockSpec((None, tb, D_MODEL), lambda i, j: (i, j, 0)),
                  _layer_spec((D_MODEL, D_IN), l, single=True),
                  pl.BlockSpec(memory_space=pltpu.SMEM),
                  _layer_spec((1, W_GRP), l), _layer_spec((1, W_GRP), l), _layer_spec((CONV_W, W_GRP), l),
                  _layer_spec((W_GRP, W_GRP), l), _layer_spec((1, W_GRP), l),
                  _layer_spec((D_MODEL, D_MODEL), l, single=True),
                  _layer_spec((1, D_MODEL), l), _layer_spec((1, D_MODEL), l)],
        out_specs=(pl.BlockSpec((None, tb, D_MODEL), lambda i, j: (i, j, 0)),
                   pl.BlockSpec((None, WINDOW, LANES), row),
                   pl.BlockSpec((None, WINDOW, LANES), row),
                   pl.BlockSpec((None, W_GRP, HEAD_DIM), row),
                   pl.BlockSpec((None, CONV_W - 1, W_GRP), row),
                   pl.BlockSpec((None, POOL_BUF, W_GRP), row)),
        out_shape=out_shape,
        scratch_shapes=[
            f32(tb, D_IN),
            bf16(WINDOW + hb, LANES),
            bf16(WINDOW + hb, LANES),
            f32(W_GRP, W_GRP),
            f32(SUBLANES + hb, W_GRP),
            f32(16 + hb, W_GRP),
            f32(2 * WINDOW, 4 * WINDOW),
            bf16(tb, D_MODEL),
            f32(hb, W_GRP),
            f32(hb, W_GRP), f32(hb, W_GRP),
            bf16(hb, W_GRP),
            bf16(hb, 2 * W_GRP),
            f32(hb, W_GRP),
            bf16(hb, W_GRP), bf16(hb, W_GRP),
            bf16(hb, W_GRP), bf16(hb, W_GRP),
            f32(n_ch, SUBLANES, W_GRP),
            f32(n_ch, W_GRP, W_GRP),
            bf16(n_ch, W_GRP, W_GRP),
            bf16(hb, W_GRP),
            f32(tb, D_MODEL),
            f32(W_GRP, W_GRP),
        ],
        compiler_params=_compiler_params(("arbitrary", "arbitrary")),
    )(x, w["w_in"], w["sink"], w["lb"], w["ng"], w["cw"], w["pw"], w["ps"], w["w_o"], w["ln1_g"], w["ln1_b"])


def _gather_block(dst_scr, src_scr, n_tiles, gi, sb, nb):
    for t in range(DEC_SEQ):
        start = pl.multiple_of(t * nb + gi * sb, sb)
        for j in range(n_tiles):
            dst_scr[j, t * sb:(t + 1) * sb, :] = src_scr[j, pl.ds(start, sb), :]


def _scatter_block(dst_scr, src_scr, n_tiles, gi, sb, nb):
    for t in range(DEC_SEQ):
        start = pl.multiple_of(t * nb + gi * sb, sb)
        for j in range(n_tiles):
            dst_scr[pl.ds(start, sb), j * LANES:(j + 1) * LANES] = src_scr[j, t * sb:(t + 1) * sb, :]


def _sample_mixer_kernel(l, xt_ref, ck_ref, cv_ref, s_ref, cs_ref, pb_ref,
                         w_in_ref, sink_ref, lbc_ref, ngc_ref, cw_ref, pw_ref, ps_ref,
                         w_o_ref, g_ref, b_ref, *rest):
    (y_ref, knew_ref, vnew_ref, snew_ref, cnew_ref, pnew_ref,
     proj_scr, a_scr, kvt_scr, q_scr, f_scr, k_scr, v_scr, gate_scr, o_scr, mix_scr,
     ab_scr, oa_scr, sc_scr, sn_scr, pc_scr, pn_scr) = rest[-22:]
    gi = pl.program_id(0)
    sb = ck_ref.shape[0]
    nb = xt_ref.shape[0] // DEC_SEQ
    half = sb // 2

    @pl.when(gi == 0)
    def _():
        xt = xt_ref[...].astype(BF16)
        pa = _dot(xt, w_in_ref[:, 0:C_BQ])
        for j in range(4):
            a_scr[j] = pa[:, j * LANES:(j + 1) * LANES]
        kvt_scr[...] = pa[:, C_AK:C_BQ].T
        hgt = _dot(xt, w_in_ref[:, C_BQ:C_CB]).T
        lbc = lbc_ref[...]
        q_scr[...] = _silu(hgt[0:W_GRP, :])
        sig_pos, sig_neg = _sigmoid_pair(hgt[W_GRP:2 * W_GRP, :])
        f_scr[...] = sig_pos + jnp.maximum(lbc, LB_FLOOR) * sig_neg
        k_scr[...] = (1.0 - lbc) * sig_neg
        v_scr[...] = hgt[2 * W_GRP:3 * W_GRP, :]
        gate_scr[...] = _silu(hgt[3 * W_GRP:4 * W_GRP, :])
        o_scr[...] = jnp.zeros(o_scr.shape, F32)
        proj_scr[...] = _dot(xt, w_in_ref[:, C_CB:D_IN])
        cw = cw_ref[...]
        ps = ps_ref[...]
        cp = lax.broadcasted_iota(jnp.int32, (nb, W_GRP), 1) >> 6
        width = jnp.left_shift(2, cp)
        u = [cs_ref[0], cs_ref[1]]
        ext = [pb_ref[i] for i in range(POOL_BUF)]
        for t in range(DEC_SEQ):
            rows = slice(t * nb, (t + 1) * nb)
            u.append(proj_scr[rows, W_GRP:2 * W_GRP] * proj_scr[rows, 2 * W_GRP:3 * W_GRP])
            ext.append(proj_scr[rows, 3 * W_GRP:4 * W_GRP])
        cnew_ref[0] = u[DEC_SEQ]
        cnew_ref[1] = u[DEC_SEQ + 1]
        for i in range(POOL_BUF):
            pnew_ref[i] = ext[DEC_SEQ + i]
        for t in range(DEC_SEQ):
            rows = slice(t * nb, (t + 1) * nb)
            yc = u[t] * cw[0:1, :] + u[t + 1] * cw[1:2, :] + u[t + 2] * cw[2:3, :]
            mix_scr[rows, 2 * W_GRP:3 * W_GRP] = proj_scr[rows, 0:W_GRP] * yc
            top = POOL_BUF + t
            acc = ext[top]
            sums = {}
            for jj in range(1, 16):
                acc = acc + ext[top - jj]
                if jj + 1 in POOL_WINDOWS:
                    sums[jj + 1] = acc
            win = jnp.where(cp == 0, sums[2], jnp.where(cp == 1, sums[4], jnp.where(cp == 2, sums[8], sums[16])))
            cnt = jnp.minimum(PAST_LEN + t + 1, width).astype(F32)
            pooled = win / cnt - ext[top]
            mix_scr[rows, 3 * W_GRP:4 * W_GRP] = _dot(pooled.astype(BF16), pw_ref[...]) * ps

    _gather_block(ab_scr, a_scr, 4, gi, sb, nb)
    lane = lax.broadcasted_iota(jnp.int32, (PAIR_ROWS, LANES), 1)
    lo = lane < HEAD_DIM
    n_rows = 4 * PAIR_ROWS
    r_c = lax.broadcasted_iota(jnp.int32, (n_rows, 2 * WINDOW), 0)
    c_c = lax.broadcasted_iota(jnp.int32, (n_rows, 2 * WINDOW), 1)
    rel_c = ((r_c >> 1) & 3) + WINDOW - (c_c & (WINDOW - 1))
    ok_c = ((r_c & 1) == (c_c >> 7)) & (rel_c <= WINDOW)
    bias_c = jnp.where(ok_c, -jnp.exp2(-2.0 * ((r_c >> 3).astype(F32) + 1.0)) * rel_c.astype(F32), MASK_VALUE)
    r_n = lax.broadcasted_iota(jnp.int32, (n_rows, PAIR_ROWS), 0)
    c_n = lax.broadcasted_iota(jnp.int32, (n_rows, PAIR_ROWS), 1)
    rel_n = ((r_n >> 1) & 3) - (c_n >> 1)
    ok_n = ((r_n & 1) == (c_n & 1)) & (rel_n >= 0)
    bias_n = jnp.where(ok_n, -jnp.exp2(-2.0 * ((r_n >> 3).astype(F32) + 1.0)) * rel_n.astype(F32), MASK_VALUE)
    pair_rows = lambda p: pl.ds(p, PAIR_ROWS, stride=half)
    for p in range(half):
        q0 = ab_scr[0, pair_rows(p), :] * (HEAD_DIM ** -0.5)
        q1 = ab_scr[1, pair_rows(p), :] * (HEAD_DIM ** -0.5)
        q0r = pltpu.roll(q0, HEAD_DIM, axis=1)
        q1r = pltpu.roll(q1, HEAD_DIM, axis=1)
        zero = jnp.zeros_like(q0)
        q4 = jnp.concatenate([jnp.where(lo, q0, zero), jnp.where(lo, q0r, zero),
                              jnp.where(lo, zero, q1r), jnp.where(lo, zero, q1)], axis=0).astype(BF16)
        kt2 = jnp.concatenate([ck_ref[p], ck_ref[p + half]], axis=1).astype(BF16)
        kn8 = ab_scr[2, pair_rows(p), :].astype(BF16)
        sc_scr[n_rows * p:n_rows * (p + 1), :] = _dot(q4, kt2) + bias_c
        sn_scr[n_rows * p:n_rows * (p + 1), :] = _nt_dot(q4, kn8) + bias_n
    sink_col = _sink_select(sink_ref, l, (lax.broadcasted_iota(jnp.int32, (n_rows * half, 1), 0) >> 3) & 3)
    s_c = sc_scr[...]
    s_n = sn_scr[...]
    m = jnp.maximum(jnp.maximum(jnp.max(s_c, axis=-1, keepdims=True), jnp.max(s_n, axis=-1, keepdims=True)),
                    sink_col)
    p_c = jnp.exp(s_c - m)
    p_n = jnp.exp(s_n - m)
    inv = 1.0 / (jnp.sum(p_c, axis=-1, keepdims=True) + jnp.sum(p_n, axis=-1, keepdims=True)
                 + jnp.exp(sink_col - m))
    pc_scr[...] = (p_c * inv).astype(BF16)
    pn_scr[...] = (p_n * inv).astype(BF16)
    for p in range(half):
        vt2 = jnp.concatenate([cv_ref[p], cv_ref[p + half]], axis=1).astype(BF16)
        vn8 = ab_scr[3, pair_rows(p), :].astype(BF16)
        o_all = (_nt_dot(pc_scr[n_rows * p:n_rows * (p + 1), :], vt2)
                 + _dot(pn_scr[n_rows * p:n_rows * (p + 1), :], vn8))
        o = [o_all[h * PAIR_ROWS:(h + 1) * PAIR_ROWS, :] for h in range(4)]
        oa_scr[0, pair_rows(p), :] = jnp.where(lo, o[0], pltpu.roll(o[1], HEAD_DIM, axis=1))
        oa_scr[1, pair_rows(p), :] = jnp.where(lo, pltpu.roll(o[2], HEAD_DIM, axis=1), o[3])
    _scatter_block(mix_scr, oa_scr, 2, gi, sb, nb)
    lane_w = lax.broadcasted_iota(jnp.int32, (LANES, LANES), 1)
    for b in range(sb):
        seq = gi * sb + b
        for old_ref, out_ref, r0 in ((ck_ref, knew_ref, 0), (cv_ref, vnew_ref, LANES)):
            cols = pltpu.roll(old_ref[b], WINDOW - DEC_SEQ, axis=1)
            for t in range(DEC_SEQ):
                shift = jnp.bitwise_and(WINDOW - DEC_SEQ + t - seq, LANES - 1)
                new_t = pltpu.roll(kvt_scr[r0:r0 + LANES, t * nb:(t + 1) * nb], shift, axis=1)
                cols = jnp.where(lane_w == WINDOW - DEC_SEQ + t, new_t, cols)
            out_ref[b] = cols

    steps_per_head = pl.num_programs(0) // N_HG
    head = gi // steps_per_head
    k_base = head * HEAD_DIM + (gi % steps_per_head) * K_PER_STEP
    v_rows = pl.ds(pl.multiple_of(head * HEAD_DIM, HEAD_DIM), HEAD_DIM)

    def k_body(k8, accs):
        rows8 = pl.ds(pl.multiple_of(k_base + k8 * SUBLANES, SUBLANES), SUBLANES)
        accs = list(accs)
        for j in range(SUBLANES):
            st_rows = pl.ds(pl.multiple_of((k8 * SUBLANES + j) * HEAD_DIM, HEAD_DIM), HEAD_DIM)
            st = s_ref[st_rows, :]
            for t in range(DEC_SEQ):
                cols = slice(t * nb, (t + 1) * nb)
                f8, k8v, q8 = f_scr[rows8, cols], k_scr[rows8, cols], q_scr[rows8, cols]
                st = f8[j:j + 1, :] * st + k8v[j:j + 1, :] * v_scr[v_rows, cols]
                accs[t] = accs[t] + q8[j:j + 1, :] * st
            snew_ref[st_rows, :] = st
        return tuple(accs)

    zero_acc = jnp.zeros((HEAD_DIM, nb), F32)
    accs = lax.fori_loop(0, K_PER_STEP // SUBLANES, k_body, (zero_acc,) * DEC_SEQ)
    for t in range(DEC_SEQ):
        cols = slice(t * nb, (t + 1) * nb)
        o_scr[v_rows, cols] = o_scr[v_rows, cols] + accs[t]

    @pl.when(gi == pl.num_programs(0) - 1)
    def _():
        o = o_scr[...]
        parts = []
        for h in range(N_HG):
            oh = o[h * HEAD_DIM:(h + 1) * HEAD_DIM, :]
            parts.append(oh * lax.rsqrt(jnp.mean(oh * oh, axis=0, keepdims=True) + RMS_EPS))
        ob = jnp.concatenate(parts, axis=0) * ngc_ref[...] * gate_scr[...]
        y = (_dot(mix_scr[:, 0:W_GRP].astype(BF16), w_o_ref[0:W_GRP, :])
             + _tn_dot(ob.astype(BF16), w_o_ref[W_GRP:2 * W_GRP, :])
             + _dot(mix_scr[:, 2 * W_GRP:4 * W_GRP].astype(BF16), w_o_ref[2 * W_GRP:4 * W_GRP, :]))
        y_ref[...] = _layer_norm(ALPHA * xt_ref[...] + y, g_ref[...], b_ref[...])


def _sample_mixer_call(l, xt, ck, cv, s, cs_tm, pb_tm, w, prev):
    m = xt.shape[0]
    nb = m // DEC_SEQ
    sb = SEQ_BLOCK
    n_steps = nb // sb
    n_sc = 4 * PAIR_ROWS * (sb // 2)
    s_rows = HG_ROWS // n_steps
    single = pl.Buffered(1)
    const2 = lambda i: (0, 0)
    out_shape = (
        jax.ShapeDtypeStruct((m, D_MODEL), F32),
        jax.ShapeDtypeStruct((DEPTH, nb, LANES, WINDOW), F32),
        jax.ShapeDtypeStruct((DEPTH, nb, LANES, WINDOW), F32),
        jax.ShapeDtypeStruct((DEPTH, HG_ROWS, nb), F32),
        jax.ShapeDtypeStruct((CONV_W - 1, nb, W_GRP), F32),
        jax.ShapeDtypeStruct((POOL_BUF, nb, W_GRP), F32),
    )
    seq_blk = lambda i: (l, i, 0, 0)
    in_specs = [pl.BlockSpec((m, D_MODEL), const2, pipeline_mode=single),
                pl.BlockSpec((None, sb, LANES, WINDOW), seq_blk),
                pl.BlockSpec((None, sb, LANES, WINDOW), seq_blk),
                pl.BlockSpec((None, s_rows, nb), lambda i: (l, i, 0)),
                _layer_spec((CONV_W - 1, nb, W_GRP), l, single=True),
                _layer_spec((POOL_BUF, nb, W_GRP), l, single=True),
                _layer_spec((D_MODEL, D_IN), l, single=True),
                pl.BlockSpec(memory_space=pltpu.SMEM),
                _layer_spec((W_GRP, 1), l), _layer_spec((W_GRP, 1), l),
                _layer_spec((CONV_W, W_GRP), l), _layer_spec((W_GRP, W_GRP), l), _layer_spec((1, W_GRP), l),
                _layer_spec((D_MODEL, D_MODEL), l, single=True),
                _layer_spec((1, D_MODEL), l), _layer_spec((1, D_MODEL), l)]
    args = [xt, ck, cv, s, cs_tm, pb_tm, w["w_in"], w["sink"], w["lb_col"], w["ng_col"],
            w["cw"], w["pw"], w["ps"], w["w_o"], w["ln1_g"], w["ln1_b"]]
    aliases = {}
    if prev is not None:
        for j, buf in enumerate(prev):
            aliases[len(args)] = 1 + j
            args.append(buf)
            in_specs.append(pl.BlockSpec(memory_space=pl.ANY))
    f32 = lambda *shape: pltpu.VMEM(shape, F32)
    return pl.pallas_call(
        functools.partial(_sample_mixer_kernel, l),
        name="mixer_sample",
        grid=(n_steps,),
        in_specs=in_specs,
        out_specs=(pl.BlockSpec((m, D_MODEL), const2),
                   pl.BlockSpec((None, sb, LANES, WINDOW), seq_blk),
                   pl.BlockSpec((None, sb, LANES, WINDOW), seq_blk),
                   pl.BlockSpec((None, s_rows, nb), lambda i: (l, i, 0)),
                   pl.BlockSpec((CONV_W - 1, nb, W_GRP), lambda i: (0, 0, 0)),
                   pl.BlockSpec((POOL_BUF, nb, W_GRP), lambda i: (0, 0, 0))),
        out_shape=out_shape,
        input_output_aliases=aliases,
        scratch_shapes=[f32(m, 4 * W_GRP),
                        f32(4, m, LANES),
                        f32(2 * LANES, m),
                        f32(W_GRP, m), f32(W_GRP, m), f32(W_GRP, m), f32(W_GRP, m), f32(W_GRP, m),
                        f32(W_GRP, m),
                        f32(m, D_MODEL),
                        f32(4, sb * DEC_SEQ, LANES), f32(2, sb * DEC_SEQ, LANES),
                        f32(n_sc, 2 * WINDOW), f32(n_sc, PAIR_ROWS),
                        pltpu.VMEM((n_sc, 2 * WINDOW), BF16), pltpu.VMEM((n_sc, PAIR_ROWS), BF16)],
        compiler_params=_compiler_params(("arbitrary",)),
    )(*args)


def _xattn_sample_kernel(xt_ref, mk_ref, mv_ref, wq_ref, wo_ref, g_ref, b_ref, o_ref,
                         q_scr, qb_scr, ob_scr, att_scr, s_scr, p_scr):
    gi = pl.program_id(0)
    sb = mk_ref.shape[0]
    nb = xt_ref.shape[0] // DEC_SEQ
    half = sb // 2
    n_rows = N_XH * PAIR_ROWS

    @pl.when(gi == 0)
    def _():
        q = _dot(xt_ref[...].astype(BF16), wq_ref[...]) * (HEAD_DIM ** -0.5)
        q_scr[0] = q[:, 0:LANES]
        q_scr[1] = q[:, LANES:2 * LANES]

    _gather_block(qb_scr, q_scr, 2, gi, sb, nb)
    lane_head = lax.broadcasted_iota(jnp.int32, (PAIR_ROWS, D_X), 1) >> 6
    r_s = lax.broadcasted_iota(jnp.int32, (n_rows, 2 * N_MEM), 0)
    c_s = lax.broadcasted_iota(jnp.int32, (n_rows, 2 * N_MEM), 1)
    own = (r_s & 1) == (c_s >> 8)
    pair_rows = lambda p: pl.ds(p, PAIR_ROWS, stride=half)
    for p in range(half):
        q8 = jnp.concatenate([qb_scr[0, pair_rows(p), :], qb_scr[1, pair_rows(p), :]], axis=1)
        zero = jnp.zeros_like(q8)
        q4 = jnp.concatenate([jnp.where(lane_head == h, q8, zero) for h in range(N_XH)], axis=0).astype(BF16)
        kt2 = jnp.concatenate([mk_ref[p], mk_ref[p + half]], axis=1).astype(BF16)
        s_scr[n_rows * p:n_rows * (p + 1), :] = jnp.where(own, _dot(q4, kt2), MASK_VALUE)
    s_all = s_scr[...]
    m = jnp.max(s_all, axis=-1, keepdims=True)
    pr = jnp.exp(s_all - m)
    p_scr[...] = (pr * (1.0 / jnp.sum(pr, axis=-1, keepdims=True))).astype(BF16)
    for p in range(half):
        vt2 = jnp.concatenate([mv_ref[p], mv_ref[p + half]], axis=1).astype(BF16)
        o_all = _nt_dot(p_scr[n_rows * p:n_rows * (p + 1), :], vt2)
        o8 = jnp.zeros((PAIR_ROWS, D_X), F32)
        for h in range(N_XH):
            o8 = jnp.where(lane_head == h, o_all[h * PAIR_ROWS:(h + 1) * PAIR_ROWS, :], o8)
        ob_scr[0, pair_rows(p), :] = o8[:, 0:LANES]
        ob_scr[1, pair_rows(p), :] = o8[:, LANES:2 * LANES]
    _scatter_block(att_scr, ob_scr, 2, gi, sb, nb)

    @pl.when(gi == pl.num_programs(0) - 1)
    def _():
        y = _dot(att_scr[...].astype(BF16), wo_ref[...])
        o_ref[...] = _layer_norm(ALPHA * xt_ref[...] + y, g_ref[...], b_ref[...])


def _xattn_sample_call(l, xt, mk, mv, w):
    m = xt.shape[0]
    nb = m // DEC_SEQ
    sb = SEQ_BLOCK
    n_sc = N_XH * PAIR_ROWS * (sb // 2)
    const2 = lambda i: (0, 0)
    return pl.pallas_call(
        _xattn_sample_kernel,
        name="xattn_sample",
        grid=(nb // sb,),
        in_specs=[pl.BlockSpec((m, D_MODEL), const2, pipeline_mode=pl.Buffered(1)),
                  pl.BlockSpec((None, sb, D_X, N_MEM), lambda i: (l, i, 0, 0)),
                  pl.BlockSpec((None, sb, D_X, N_MEM), lambda i: (l, i, 0, 0)),
                  _layer_spec((D_MODEL, D_X), l), _layer_spec((D_X, D_MODEL), l),
                  _layer_spec((1, D_MODEL), l), _layer_spec((1, D_MODEL), l)],
        out_specs=pl.BlockSpec((m, D_MODEL), const2),
        out_shape=jax.ShapeDtypeStruct((m, D_MODEL), F32),
        scratch_shapes=[pltpu.VMEM((2, m, LANES), F32),
                        pltpu.VMEM((2, sb * DEC_SEQ, LANES), F32),
                        pltpu.VMEM((2, sb * DEC_SEQ, LANES), F32),
                        pltpu.VMEM((m, D_X), F32),
                        pltpu.VMEM((n_sc, 2 * N_MEM), F32),
                        pltpu.VMEM((n_sc, 2 * N_MEM), BF16)],
        compiler_params=_compiler_params(("arbitrary",)),
    )(xt, mk, mv, w["w_xq"], w["w_xo"], w["ln2_g"], w["ln2_b"])


def _hgrn_lower_bounds(lb_param):
    p = jax.nn.softmax(lb_param.astype(F32), axis=0)
    return jnp.cumsum(p, axis=0) - p[0:1]


def _prepare_weights(w_in, attn_sink, hgrn_lb, hgrn_norm_g, conv_w, pool_w, pool_scale, w_o, ln1_g, ln1_b,
                     w_xq, w_xo, ln2_g, ln2_b, w_gate, w_up, w_down, ln3_g, ln3_b):
    bf = lambda a: a.astype(BF16)
    rows = lambda a: a.astype(F32)[:, None, :]
    cols = lambda a: a.astype(F32)[:, :, None]
    lb_all = _hgrn_lower_bounds(hgrn_lb)
    eye = jnp.eye(len(POOL_WINDOWS), dtype=pool_w.dtype)
    pw = (pool_w[:, :, :, None, :] * eye[None, :, None, :, None]).reshape(DEPTH, W_GRP, W_GRP)
    return dict(
        w_in=bf(w_in),
        sink=attn_sink.astype(F32),
        lb=rows(lb_all), lb_col=cols(lb_all), ng=rows(hgrn_norm_g), ng_col=cols(hgrn_norm_g),
        cw=conv_w.astype(F32), pw=bf(pw), ps=rows(pool_scale),
        w_o=bf(w_o), ln1_g=rows(ln1_g), ln1_b=rows(ln1_b),
        w_xq=bf(w_xq), w_xo=bf(w_xo), ln2_g=rows(ln2_g), ln2_b=rows(ln2_b),
        w_gate=bf(w_gate), w_up=bf(w_up), w_down=bf(w_down), ln3_g=rows(ln3_g), ln3_b=rows(ln3_b))


def _prompt_layer(l, x, mkv, w):
    bsz, t, _ = x.shape
    x, kn, vn, sn, cn, pn = _prompt_mixer_call(l, x, w)
    x = _xattn_prompt_call(l, x, mkv, w)
    x = _ffn_call(l, x.reshape(bsz * t, D_MODEL), w).reshape(bsz, t, D_MODEL)
    return x, kn, vn, sn, cn, pn


def _sample_layer(l, xt, ck, cv, s, cs_tm, pb_tm, mk, mv, w, prev):
    xt, kn, vn, sn, cn, pn = _sample_mixer_call(l, xt, ck, cv, s, cs_tm, pb_tm, w, prev)
    xt = _xattn_sample_call(l, xt, mk, mv, w)
    xt = _ffn_call(l, xt, w)
    return xt, kn, vn, sn, cn, pn


def kernel(x_prompt, x_sample, cache_swa_k, cache_swa_v, state_hgrn, state_conv, state_pool, cache_mem_k,
           cache_mem_v, mem_prompt, emb_ln_g, emb_ln_b, w_in, attn_sink, hgrn_lb, hgrn_norm_g, conv_w, pool_w,
           pool_scale, w_o, ln1_g, ln1_b, w_xq, w_xk, w_xv, w_xo, ln2_g, ln2_b, w_gate, w_up, w_down, ln3_g,
           ln3_b):
    bp, t, _ = x_prompt.shape
    bs, ts, _ = x_sample.shape
    w = _prepare_weights(w_in, attn_sink, hgrn_lb, hgrn_norm_g, conv_w, pool_w, pool_scale, w_o, ln1_g, ln1_b,
                         w_xq, w_xo, ln2_g, ln2_b, w_gate, w_up, w_down, ln3_g, ln3_b)
    emb_g = emb_ln_g.reshape(1, D_MODEL).astype(F32)
    emb_b = emb_ln_b.reshape(1, D_MODEL).astype(F32)
    hp = _ln_call(x_prompt.reshape(bp * t, D_MODEL), emb_g, emb_b).reshape(bp, t, D_MODEL)
    w_kv = jnp.concatenate([w_xk, w_xv], axis=2).transpose(1, 0, 2).reshape(D_MODEL, DEPTH * 2 * D_X).astype(BF16)
    mkv = _matmul_call(mem_prompt.reshape(bp * N_MEM, D_MODEL), w_kv).reshape(bp, N_MEM, DEPTH * 2 * D_X)
    hs = _ln_call(x_sample.transpose(1, 0, 2).reshape(ts * bs, D_MODEL), emb_g, emb_b)
    ck = cache_swa_k.transpose(0, 1, 3, 4, 2).reshape(DEPTH, bs, LANES, WINDOW)
    cv = cache_swa_v.transpose(0, 1, 3, 4, 2).reshape(DEPTH, bs, LANES, WINDOW)
    st = state_hgrn.transpose(0, 2, 3, 4, 1).reshape(DEPTH, HG_ROWS, bs)
    cs_tm = state_conv.transpose(0, 2, 1, 3)
    pb_tm = state_pool.transpose(0, 2, 1, 3)
    mk_s = cache_mem_k.transpose(0, 1, 3, 4, 2).reshape(DEPTH, bs, D_X, N_MEM)
    mv_s = cache_mem_v.transpose(0, 1, 3, 4, 2).reshape(DEPTH, bs, D_X, N_MEM)
    outs = [[] for _ in range(5)]
    souts = [[] for _ in range(2)]
    prev = None
    for l in range(DEPTH):
        res = _prompt_layer(l, hp, mkv, w)
        hp = res[0]
        for acc, r in zip(outs, res[1:]):
            acc.append(r)
        sres = _sample_layer(l, hs, ck, cv, st, cs_tm, pb_tm, mk_s, mv_s, w, prev)
        hs = sres[0]
        prev = sres[1:4]
        for acc, r in zip(souts, sres[4:]):
            acc.append(r)
    pk, pv, ps, pc, pp = [jnp.stack(o) for o in outs]
    sk, sv, ss = prev
    sc, sp = [jnp.stack(o) for o in souts]
    mem_out = mkv.reshape(bp, N_MEM, DEPTH, 2, N_XH, HEAD_DIM).transpose(3, 2, 0, 1, 4, 5)
    swa_out = lambda a: a.reshape(DEPTH, bs, N_KV, HEAD_DIM, WINDOW).transpose(0, 1, 4, 2, 3)
    return (hp, hs.reshape(ts, bs, D_MODEL).transpose(1, 0, 2),
            pk.reshape(DEPTH, bp, WINDOW, N_KV, HEAD_DIM), pv.reshape(DEPTH, bp, WINDOW, N_KV, HEAD_DIM),
            ps.reshape(DEPTH, bp, N_HG, HEAD_DIM, HEAD_DIM), pc, pp,
            mem_out[0], mem_out[1],
            swa_out(sk), swa_out(sv),
            ss.reshape(DEPTH, N_HG, HEAD_DIM, HEAD_DIM, bs).transpose(0, 4, 1, 2, 3),
            sc.transpose(0, 2, 1, 3), sp.transpose(0, 2, 1, 3))
```

```python
import functools

import jax
import jax.numpy as jnp
from jax import lax
from jax.experimental import pallas as pl
from jax.experimental.pallas import tpu as pltpu

F32 = jnp.float32
BF16 = jnp.bfloat16

D_MODEL = 1024
DEPTH = 4
HEAD_DIM = 64
W_GRP = 256
N_KV = 2
WINDOW = 128
N_HG = 4
CONV_W = 3
POOL_WINDOWS = (2, 4, 8, 16)
POOL_BUF = 15
N_MEM = 256
N_XH = 4
D_X = 256
D_FF = 2816
D_IN = 2560
DEC_SEQ = 4
ALPHA = (2 * DEPTH) ** 0.25
LN_EPS = 1e-5
RMS_EPS = 1e-6
MASK_VALUE = -1e30
LB_FLOOR = 1e-30
PAST_LEN = 8192

C_AQ, C_AK, C_AV = 0, 256, 384
C_BQ, C_BF, C_BI, C_BG = 512, 768, 1024, 1280
C_CB, C_CC, C_CH = 1536, 1792, 2048
C_DV = 2304

LANES = 128
SUBLANES = 8
VMEM_LIMIT_BYTES = 56 * 1024 * 1024

TOKEN_BLOCK = 512
SUB_BLOCKS = 2
HG_CHUNK = 64
HG_MID = HG_CHUNK // 2 - 1
HG_MAX_EXPONENT = 80.0
HG_ROWS = N_HG * HEAD_DIM * HEAD_DIM
SEQ_BLOCK = 16
PAIR_ROWS = 2 * DEC_SEQ
K_PER_STEP = HG_ROWS // HEAD_DIM * SEQ_BLOCK // 128


def _nt_dot(a, b):
    return lax.dot_general(a, b, (((1,), (1,)), ((), ())), preferred_element_type=F32)


def _tn_dot(a, b):
    return lax.dot_general(a, b, (((0,), (0,)), ((), ())), preferred_element_type=F32)


def _dot(a, b):
    return jnp.dot(a, b, preferred_element_type=F32)


def _layer_norm(x, g, b):
    mu = jnp.mean(x, axis=-1, keepdims=True)
    xc = x - mu
    var = jnp.mean(xc * xc, axis=-1, keepdims=True)
    return xc * lax.rsqrt(var + LN_EPS) * g + b


def _sigmoid_pair(z):
    e = jnp.exp(-jnp.abs(z))
    inv = 1.0 / (1.0 + e)
    small = e * inv
    pos = z >= 0
    return jnp.where(pos, inv, small), jnp.where(pos, small, inv)


def _silu(z):
    s, _ = _sigmoid_pair(z)
    return z * s


def _silu_tanh(z):
    return z * (0.5 + 0.5 * jnp.tanh(0.5 * z))


def _full_spec(shape):
    nd = len(shape)
    return pl.BlockSpec(shape, lambda *_: (0,) * nd)


def _layer_spec(shape, l, single=False):
    nd = len(shape)
    index = lambda *_: (l,) + (0,) * nd
    if single:
        return pl.BlockSpec((None,) + tuple(shape), index, pipeline_mode=pl.Buffered(1))
    return pl.BlockSpec((None,) + tuple(shape), index)


def _compiler_params(sem):
    return pltpu.CompilerParams(dimension_semantics=sem, vmem_limit_bytes=VMEM_LIMIT_BYTES)


class _MatmulQueue:
    def __init__(self):
        self.items = []

    def add(self, fn, *args):
        self.items.append((fn, args))

    def issue(self, n=1):
        for _ in range(min(n, len(self.items))):
            fn, args = self.items.pop(0)
            fn(*args)

    def flush(self):
        self.issue(len(self.items))


def _ln_kernel(x_ref, g_ref, b_ref, o_ref):
    o_ref[...] = _layer_norm(x_ref[...], g_ref[...], b_ref[...])


def _ln_call(x2d, g, b):
    m = x2d.shape[0]
    return pl.pallas_call(
        _ln_kernel,
        name="input_ln",
        grid=(m // TOKEN_BLOCK,),
        in_specs=[pl.BlockSpec((TOKEN_BLOCK, D_MODEL), lambda i: (i, 0)),
                  _full_spec((1, D_MODEL)), _full_spec((1, D_MODEL))],
        out_specs=pl.BlockSpec((TOKEN_BLOCK, D_MODEL), lambda i: (i, 0)),
        out_shape=jax.ShapeDtypeStruct((m, D_MODEL), F32),
        compiler_params=_compiler_params(("arbitrary",)),
    )(x2d, g, b)


def _matmul_kernel(x_ref, w_ref, o_ref):
    o_ref[...] = _dot(x_ref[...].astype(BF16), w_ref[...])


def _matmul_call(x2d, w):
    m, k = x2d.shape
    n = w.shape[1]
    return pl.pallas_call(
        _matmul_kernel,
        name="mem_proj",
        grid=(m // TOKEN_BLOCK,),
        in_specs=[pl.BlockSpec((TOKEN_BLOCK, k), lambda i: (i, 0)), _full_spec((k, n))],
        out_specs=pl.BlockSpec((TOKEN_BLOCK, n), lambda i: (i, 0)),
        out_shape=jax.ShapeDtypeStruct((m, n), F32),
        compiler_params=_compiler_params(("arbitrary",)),
    )(x2d, w)


def _ffn_kernel(x_ref, wg_ref, wu_ref, wd_ref, g_ref, b_ref, o_ref):
    x = x_ref[...]
    xb = x.astype(BF16)
    h = _silu(_dot(xb, wg_ref[...])) * _dot(xb, wu_ref[...])
    y = _dot(h.astype(BF16), wd_ref[...])
    o_ref[...] = _layer_norm(ALPHA * x + y, g_ref[...], b_ref[...])


def _ffn_call(l, x2d, w):
    m = x2d.shape[0]
    tb = min(2 * TOKEN_BLOCK, m)
    return pl.pallas_call(
        _ffn_kernel,
        name="ffn",
        grid=(m // tb,),
        in_specs=[pl.BlockSpec((tb, D_MODEL), lambda i: (i, 0)),
                  _layer_spec((D_MODEL, D_FF), l, single=True),
                  _layer_spec((D_MODEL, D_FF), l, single=True),
                  _layer_spec((D_FF, D_MODEL), l, single=True),
                  _layer_spec((1, D_MODEL), l), _layer_spec((1, D_MODEL), l)],
        out_specs=pl.BlockSpec((tb, D_MODEL), lambda i: (i, 0)),
        out_shape=jax.ShapeDtypeStruct((m, D_MODEL), F32),
        compiler_params=_compiler_params(("arbitrary",)),
    )(x2d, w["w_gate"], w["w_up"], w["w_down"], w["ln3_g"], w["ln3_b"])


def _xattn_prompt_kernel(x_ref, mk_ref, mv_ref, wq_ref, wo_ref, g_ref, b_ref, o_ref, q_scr, att_scr, yo_scr):
    tb = x_ref.shape[0]
    hb = tb // SUB_BLOCKS
    mk = mk_ref[...].astype(BF16)
    mv = mv_ref[...].astype(BF16)
    lane_head = lax.broadcasted_iota(jnp.int32, (hb, D_X), 1) >> 6
    sub = lambda r: slice(r * hb, (r + 1) * hb)

    def project(r):
        q_scr[sub(r), :] = _dot(x_ref[sub(r), :].astype(BF16), wq_ref[...]) * (HEAD_DIM ** -0.5)

    def finish(r, c0, c1):
        yo_scr[sub(r), c0:c1] = _dot(att_scr[sub(r), :], wo_ref[:, c0:c1])

    def finish_ln(r):
        o_ref[sub(r), :] = _layer_norm(ALPHA * x_ref[sub(r), :] + yo_scr[sub(r), :], g_ref[...], b_ref[...])

    out_cols = ((0, D_MODEL // 2), (D_MODEL // 2, D_MODEL))
    project(0)
    queue = _MatmulQueue()
    for r in range(SUB_BLOCKS):
        if r >= 1:
            for cols in out_cols:
                queue.add(finish, r - 1, *cols)
        if r + 1 < SUB_BLOCKS:
            queue.add(project, r + 1)
        q = q_scr[sub(r), :]
        o = jnp.zeros_like(q)
        for h in range(N_XH):
            qh = jnp.where(lane_head == h, q, 0.0).astype(BF16)
            s = _nt_dot(mk, qh)
            m = jnp.max(s, axis=0, keepdims=True)
            p = jnp.exp(s - m)
            den = jnp.sum(p, axis=0, keepdims=True)
            p = (p * (1.0 / den)).astype(BF16)
            o = jnp.where(lane_head == h, _tn_dot(p, mv), o)
            queue.issue()
        att_scr[sub(r), :] = o.astype(BF16)
        queue.flush()
        if r >= 1:
            finish_ln(r - 1)
    for cols in out_cols:
        finish(SUB_BLOCKS - 1, *cols)
    finish_ln(SUB_BLOCKS - 1)


def _xattn_prompt_call(l, x, mkv, w):
    bsz, t, _ = x.shape
    tb = min(2 * TOKEN_BLOCK, t)
    return pl.pallas_call(
        _xattn_prompt_kernel,
        name="xattn_prompt",
        grid=(bsz, t // tb),
        in_specs=[pl.BlockSpec((None, tb, D_MODEL), lambda i, j: (i, j, 0)),
                  pl.BlockSpec((None, N_MEM, D_X), lambda i, j: (i, 0, 2 * l)),
                  pl.BlockSpec((None, N_MEM, D_X), lambda i, j: (i, 0, 2 * l + 1)),
                  _layer_spec((D_MODEL, D_X), l), _layer_spec((D_X, D_MODEL), l),
                  _layer_spec((1, D_MODEL), l), _layer_spec((1, D_MODEL), l)],
        out_specs=pl.BlockSpec((None, tb, D_MODEL), lambda i, j: (i, j, 0)),
        out_shape=jax.ShapeDtypeStruct((bsz, t, D_MODEL), F32),
        scratch_shapes=[pltpu.VMEM((tb, D_X), F32), pltpu.VMEM((tb, D_X), BF16), pltpu.VMEM((tb, D_MODEL), F32)],
        compiler_params=_compiler_params(("arbitrary", "arbitrary")),
    )(x, mkv, mkv, w["w_xq"], w["w_xo"], w["ln2_g"], w["ln2_b"])


def _swa_bias_table():
    c = lax.broadcasted_iota(jnp.int32, (2 * WINDOW, 4 * WINDOW), 0)
    r = lax.broadcasted_iota(jnp.int32, (2 * WINDOW, 4 * WINDOW), 1)
    head = r >> 7
    rel = (r & (WINDOW - 1)) + WINDOW - c
    slope = jnp.exp2(-2.0 * (head.astype(F32) + 1.0))
    valid = (rel >= 0) & (rel <= WINDOW)
    return jnp.where(valid, -slope * rel.astype(F32), MASK_VALUE)


def _sink_select(sink_ref, l, head):
    return jnp.where(head == 0, sink_ref[l, 0],
                     jnp.where(head == 1, sink_ref[l, 1], jnp.where(head == 2, sink_ref[l, 2], sink_ref[l, 3])))


def _prompt_mixer_kernel(l, x_ref, w_in_ref, sink_ref, lb_ref, ng_ref, cw_ref, pw_ref, ps_ref, w_o_ref,
                         g_ref, b_ref,
                         y_ref, knew_ref, vnew_ref, snew_ref, cnew_ref, pnew_ref,
                         proj_scr, kext_scr, vext_scr, st_scr, u_scr, p_scr, bias_scr, mix_scr, hg_scr,
                         hq_scr, hk_scr, hv_scr, ghl_scr, cum_scr, qp_scr, kp_scr, qs_scr, ks_scr, dec_scr, inc_scr,
                         stb_scr, a_scr, yo_scr, st0_scr):
    tb = x_ref.shape[0]
    hb = tb // SUB_BLOCKS
    n_qb = hb // WINDOW
    n_ch = hb // HG_CHUNK
    bi = pl.program_id(0)
    ti = pl.program_id(1)
    last = ti == pl.num_programs(1) - 1

    @pl.when((bi == 0) & (ti == 0))
    def _():
        bias_scr[...] = _swa_bias_table()

    @pl.when(ti == 0)
    def _():
        kext_scr[0:WINDOW, :] = jnp.zeros((WINDOW, LANES), BF16)
        vext_scr[0:WINDOW, :] = jnp.zeros((WINDOW, LANES), BF16)
        st_scr[...] = jnp.zeros(st_scr.shape, F32)
        u_scr[0:SUBLANES, :] = jnp.zeros((SUBLANES, W_GRP), F32)
        p_scr[0:16, :] = jnp.zeros((16, W_GRP), F32)

    def project(r, c0, c1):
        rows = slice(r * hb, (r + 1) * hb)
        proj_scr[rows, c0:c1] = _dot(x_ref[rows, :].astype(BF16), w_in_ref[:, c0:c1])

    def finish(r, c0, c1):
        rows = slice(r * hb, (r + 1) * hb)
        yo_scr[rows, c0:c1] = _dot(mix_scr[rows, :], w_o_ref[:, c0:c1])

    def finish_ln(r):
        rows = slice(r * hb, (r + 1) * hb)
        y_ref[rows, :] = _layer_norm(ALPHA * x_ref[rows, :] + yo_scr[rows, :], g_ref[...], b_ref[...])

    proj_cols = ((0, C_BQ), (C_BQ, C_BI), (C_BI, C_CB), (C_CB, C_CH), (C_CH, D_IN))
    out_cols = ((0, D_MODEL // 2), (D_MODEL // 2, D_MODEL))
    for cols in proj_cols:
        project(0, *cols)
    queue = _MatmulQueue()

    lane = lax.broadcasted_iota(jnp.int32, (WINDOW, LANES), 1)
    lo = lane < HEAD_DIM
    key_row = lax.broadcasted_iota(jnp.int32, (2 * WINDOW, 4 * WINDOW), 0)
    sink_row = _sink_select(sink_ref, l, lax.broadcasted_iota(jnp.int32, (1, 4 * WINDOW), 1) >> 7)
    lb = lb_ref[...]
    lbf = jnp.maximum(lb, LB_FLOOR)
    one_m_lb = 1.0 - lb
    ng = ng_ref[...]
    r256 = lax.broadcasted_iota(jnp.int32, (W_GRP, W_GRP), 0)
    c256 = lax.broadcasted_iota(jnp.int32, (W_GRP, W_GRP), 1)
    same_head = (r256 >> 6) == (c256 >> 6)
    head_ones = jnp.where(same_head, 1.0, 0.0).astype(BF16)
    same_head_b = head_ones > 0
    zero_b = jnp.zeros((W_GRP, W_GRP), BF16)
    rc = lax.broadcasted_iota(jnp.int32, (HG_CHUNK, W_GRP), 0)
    cc = lax.broadcasted_iota(jnp.int32, (HG_CHUNK, W_GRP), 1)
    causal = (cc & (HG_CHUNK - 1)) <= rc
    r64 = lax.broadcasted_iota(jnp.int32, (HG_CHUNK, HG_CHUNK), 0)
    c64 = lax.broadcasted_iota(jnp.int32, (HG_CHUNK, HG_CHUNK), 1)
    tril = jnp.where(c64 <= r64, 1.0, 0.0).astype(BF16)
    cw = cw_ref[...]
    rp = lax.broadcasted_iota(jnp.int32, (hb, W_GRP), 0)
    grp = lax.broadcasted_iota(jnp.int32, (hb, W_GRP), 1) >> 6
    width = jnp.left_shift(2, grp)
    chunk = lambda c: slice(c * HG_CHUNK, (c + 1) * HG_CHUNK)

    def hgrn_output(r):
        rows = slice(r * hb, (r + 1) * hb)
        o = hg_scr[...]
        ms = _dot((o * o).astype(BF16), head_ones) * (1.0 / HEAD_DIM)
        o = o * lax.rsqrt(ms + RMS_EPS) * ng
        mix_scr[rows, W_GRP:2 * W_GRP] = (o * _silu_tanh(proj_scr[rows, C_BG:C_BG + W_GRP])).astype(BF16)

    st0_scr[...] = st_scr[...]
    span = jnp.zeros((1, W_GRP), F32)
    for r in range(SUB_BLOCKS):
        r0 = r * hb
        sub = slice(r0, r0 + hb)
        if r >= 1:
            for cols in out_cols:
                queue.add(finish, r - 1, *cols)
        if r + 1 < SUB_BLOCKS:
            for cols in proj_cols:
                queue.add(project, r + 1, *cols)

        kext_scr[WINDOW:WINDOW + hb, :] = proj_scr[sub, C_AK:C_AK + LANES].astype(BF16)
        vext_scr[WINDOW:WINDOW + hb, :] = proj_scr[sub, C_AV:C_AV + LANES].astype(BF16)
        for j in range(n_qb):
            rows = slice(r0 + j * WINDOW, r0 + (j + 1) * WINDOW)
            q0 = proj_scr[rows, 0:LANES] * (HEAD_DIM ** -0.5)
            q1 = proj_scr[rows, LANES:2 * LANES] * (HEAD_DIM ** -0.5)
            q0r = pltpu.roll(q0, HEAD_DIM, axis=1)
            q1r = pltpu.roll(q1, HEAD_DIM, axis=1)
            zero = jnp.zeros_like(q0)
            q4 = jnp.concatenate([jnp.where(lo, q0, zero), jnp.where(lo, q0r, zero),
                                  jnp.where(lo, zero, q1r), jnp.where(lo, zero, q1)], axis=0).astype(BF16)
            kj = kext_scr[j * WINDOW:(j + 2) * WINDOW, :]
            vj = vext_scr[j * WINDOW:(j + 2) * WINDOW, :]
            s = _nt_dot(kj, q4) + bias_scr[...]
            if r == 0 and j == 0:
                s = jnp.where((ti == 0) & (key_row < WINDOW), MASK_VALUE, s)
            m = jnp.maximum(jnp.max(s, axis=0, keepdims=True), sink_row)
            p = jnp.exp(s - m)
            den = jnp.sum(p, axis=0, keepdims=True) + jnp.exp(sink_row - m)
            p = (p * (1.0 / den)).astype(BF16)
            o_all = _tn_dot(p, vj)
            o = [o_all[h * WINDOW:(h + 1) * WINDOW, :] for h in range(4)]
            mix_scr[rows, 0:LANES] = jnp.where(lo, o[0], pltpu.roll(o[1], HEAD_DIM, axis=1)).astype(BF16)
            mix_scr[rows, LANES:2 * LANES] = jnp.where(lo, pltpu.roll(o[2], HEAD_DIM, axis=1), o[3]).astype(BF16)
            queue.issue()
        kext_scr[0:WINDOW, :] = kext_scr[hb:hb + WINDOW, :]
        vext_scr[0:WINDOW, :] = vext_scr[hb:hb + WINDOW, :]

        hq_scr[...] = _silu_tanh(proj_scr[sub, C_BQ:C_BQ + W_GRP])
        queue.issue()
        sig_pos, sig_neg = _sigmoid_pair(proj_scr[sub, C_BF:C_BF + W_GRP])
        g = jnp.log(sig_pos + lbf * sig_neg)
        hk_scr[...] = one_m_lb * sig_neg
        g_hi = g.astype(BF16)
        ghl_scr[:, 0:W_GRP] = g_hi
        ghl_scr[:, W_GRP:2 * W_GRP] = (g - g_hi.astype(F32)).astype(BF16)
        hv_scr[...] = proj_scr[sub, C_BI:C_BI + W_GRP].astype(BF16)
        for c in range(n_ch):
            cum2 = _dot(tril, ghl_scr[chunk(c), :])
            cum_scr[chunk(c), :] = cum2[:, 0:W_GRP] + cum2[:, W_GRP:2 * W_GRP]
        for c in range(n_ch):
            cum = cum_scr[chunk(c), :]
            ref = cum[HG_MID:HG_MID + 1, :]
            tot = cum[HG_CHUNK - 1:HG_CHUNK, :]
            span = jnp.maximum(span, jnp.maximum(cum[0:1, :] - ref, ref - tot))
            qp = hq_scr[chunk(c), :] * jnp.exp(cum - ref)
            kp = hk_scr[chunk(c), :] * jnp.exp(ref - cum)
            qp_scr[chunk(c), :] = qp.astype(BF16)
            kp_scr[chunk(c), :] = kp.astype(BF16)
            qs_scr[chunk(c), :] = (qp * jnp.exp(ref)).astype(BF16)
            ks_scr[chunk(c), :] = (kp * jnp.exp(tot - ref)).astype(BF16)
            dec_scr[c] = jnp.broadcast_to(jnp.exp(tot), (SUBLANES, W_GRP))
        queue.issue()
        for c in range(n_ch):
            bk = jnp.where(same_head_b, jnp.concatenate([kp_scr[chunk(c), :]] * N_HG, axis=0), zero_b)
            a = jnp.where(causal, _nt_dot(qp_scr[chunk(c), :], bk), 0.0)
            a_scr[chunk(c), :] = a.astype(BF16)
        for c in range(n_ch):
            inc_scr[c] = jnp.where(same_head, _tn_dot(hv_scr[chunk(c), :], ks_scr[chunk(c), :]), 0.0)
        for c in range(n_ch):
            bv = jnp.where(same_head_b, jnp.concatenate([hv_scr[chunk(c), :]] * N_HG, axis=0), zero_b)
            hg_scr[chunk(c), :] = _dot(a_scr[chunk(c), :], bv)
        st = st_scr[...]
        for c in range(n_ch):
            stb_scr[c] = st.astype(BF16)
            st = st * dec_scr[c, 0:1, :] + inc_scr[c]
        st_scr[...] = st
        queue.issue()
        for c in range(n_ch):
            hg_scr[chunk(c), :] = hg_scr[chunk(c), :] + _nt_dot(qs_scr[chunk(c), :], stb_scr[c])

        hgrn_output(r)
        queue.issue()

        u_scr[SUBLANES:SUBLANES + hb, :] = proj_scr[sub, C_CC:C_CC + W_GRP] * proj_scr[sub, C_CH:C_CH + W_GRP]
        yc = (u_scr[SUBLANES - 2:SUBLANES - 2 + hb, :] * cw[0:1, :]
              + u_scr[SUBLANES - 1:SUBLANES - 1 + hb, :] * cw[1:2, :]
              + u_scr[SUBLANES:SUBLANES + hb, :] * cw[2:3, :])
        mix_scr[sub, 2 * W_GRP:3 * W_GRP] = (proj_scr[sub, C_CB:C_CB + W_GRP] * yc).astype(BF16)
        u_scr[0:SUBLANES, :] = u_scr[hb:hb + SUBLANES, :]
        queue.issue()

        dv = proj_scr[sub, C_DV:C_DV + W_GRP]
        p_scr[16:16 + hb, :] = dv
        ext = p_scr[...]
        s2 = ext + pltpu.roll(ext, 1, axis=0)
        s4 = s2 + pltpu.roll(s2, 2, axis=0)
        s8 = s4 + pltpu.roll(s4, 4, axis=0)
        s16 = s8 + pltpu.roll(s8, 8, axis=0)
        win = jnp.where(grp == 0, s2[16:], jnp.where(grp == 1, s4[16:], jnp.where(grp == 2, s8[16:], s16[16:])))
        cnt = jnp.minimum(ti * tb + r0 + rp + 1, width).astype(F32)
        pooled = win / cnt - dv
        yd = _dot(pooled.astype(BF16), pw_ref[...]) * ps_ref[...]
        mix_scr[sub, 3 * W_GRP:4 * W_GRP] = yd.astype(BF16)
        p_scr[0:16, :] = p_scr[hb:hb + 16, :]
        queue.flush()
        if r >= 1:
            finish_ln(r - 1)

    for cols in out_cols:
        finish(SUB_BLOCKS - 1, *cols)
    finish_ln(SUB_BLOCKS - 1)

    @pl.when(jnp.max(span) > HG_MAX_EXPONENT)
    def _():
        st_scr[...] = st0_scr[...]
        sub8 = lax.broadcasted_iota(jnp.int32, (SUBLANES, W_GRP), 0)
        for r in range(SUB_BLOCKS):
            def tile_body(i, carry, r=r):
                rows8 = pl.ds(pl.multiple_of(i * SUBLANES, SUBLANES), SUBLANES)
                prow = pl.ds(pl.multiple_of(r * hb + i * SUBLANES, SUBLANES), SUBLANES)
                sp8, sn8 = _sigmoid_pair(proj_scr[prow, C_BF:C_BF + W_GRP])
                f8 = sp8 + lbf * sn8
                k8 = (one_m_lb * sn8).astype(BF16)
                v8 = proj_scr[prow, C_BI:C_BI + W_GRP]
                q8 = _silu_tanh(proj_scr[prow, C_BQ:C_BQ + W_GRP]).astype(BF16)
                o8 = jnp.zeros((SUBLANES, W_GRP), F32)
                for j in range(SUBLANES):
                    vj = jnp.where(sub8 == j, v8, 0.0).astype(BF16)
                    stj = st_scr[...] * f8[j:j + 1, :] + jnp.where(same_head, _tn_dot(vj, k8), 0.0)
                    st_scr[...] = stj
                    o8 = jnp.where(sub8 == j, _nt_dot(q8, stj.astype(BF16)), o8)
                hg_scr[rows8, :] = o8
                return carry

            lax.fori_loop(0, hb // SUBLANES, tile_body, 0)
            hgrn_output(r)
            for cols in out_cols:
                finish(r, *cols)
            finish_ln(r)

    @pl.when(last)
    def _():
        knew_ref[...] = proj_scr[tb - WINDOW:tb, C_AK:C_AK + LANES]
        vnew_ref[...] = proj_scr[tb - WINDOW:tb, C_AV:C_AV + LANES]
        s_t = st_scr[...].T
        for h in range(N_HG):
            snew_ref[h * HEAD_DIM:(h + 1) * HEAD_DIM, :] = (
                s_t[h * HEAD_DIM:(h + 1) * HEAD_DIM, h * HEAD_DIM:(h + 1) * HEAD_DIM])
        cnew_ref[...] = u_scr[SUBLANES - 2:SUBLANES, :]
        pnew_ref[...] = p_scr[1:16, :]


def _prompt_mixer_call(l, x, w):
    bsz, t, _ = x.shape
    tb = min(2 * TOKEN_BLOCK, t)
    hb = tb // SUB_BLOCKS
    n_ch = hb // HG_CHUNK
    row = lambda i, j: (i, 0, 0)
    out_shape = (
        jax.ShapeDtypeStruct((bsz, t, D_MODEL), F32),
        jax.ShapeDtypeStruct((bsz, WINDOW, LANES), F32),
        jax.ShapeDtypeStruct((bsz, WINDOW, LANES), F32),
        jax.ShapeDtypeStruct((bsz, W_GRP, HEAD_DIM), F32),
        jax.ShapeDtypeStruct((bsz, CONV_W - 1, W_GRP), F32),
        jax.ShapeDtypeStruct((bsz, POOL_BUF, W_GRP), F32),
    )
    f32 = lambda *shape: pltpu.VMEM(shape, F32)
    bf16 = lambda *shape: pltpu.VMEM(shape, BF16)
    return pl.pallas_call(
        functools.partial(_prompt_mixer_kernel, l),
        name="mixer_prompt",
        grid=(bsz, t // tb),
        in_specs=[pl.BlockSpec((None, tb, D_MODEL), lambda i, j: (i, j, 0)),
                  _layer_spec((D_MODEL, D_IN), l, single=True),
                  pl.BlockSpec(memory_space=pltpu.SMEM),
                  _layer_spec((1, W_GRP), l), _layer_spec((1, W_GRP), l), _layer_spec((CONV_W, W_GRP), l),
                  _layer_spec((W_GRP, W_GRP), l), _layer_spec((1, W_GRP), l),
                  _layer_spec((D_MODEL, D_MODEL), l, single=True),
                  _layer_spec((1, D_MODEL), l), _layer_spec((1, D_MODEL), l)],
        out_specs=(pl.BlockSpec((None, tb, D_MODEL), lambda i, j: (i, j, 0)),
                   pl.BlockSpec((None, WINDOW, LANES), row),
                   pl.BlockSpec((None, WINDOW, LANES), row),
                   pl.BlockSpec((None, W_GRP, HEAD_DIM), row),
                   pl.BlockSpec((None, CONV_W - 1, W_GRP), row),
                   pl.BlockSpec((None, POOL_BUF, W_GRP), row)),
        out_shape=out_shape,
        scratch_shapes=[
            f32(tb, D_IN),
            bf16(WINDOW + hb, LANES),
            bf16(WINDOW + hb, LANES),
            f32(W_GRP, W_GRP),
            f32(SUBLANES + hb, W_GRP),
            f32(16 + hb, W_GRP),
            f32(2 * WINDOW, 4 * WINDOW),
            bf16(tb, D_MODEL),
            f32(hb, W_GRP),
            f32(hb, W_GRP), f32(hb, W_GRP),
            bf16(hb, W_GRP),
            bf16(hb, 2 * W_GRP),
            f32(hb, W_GRP),
            bf16(hb, W_GRP), bf16(hb, W_GRP),
            bf16(hb, W_GRP), bf16(hb, W_GRP),
            f32(n_ch, SUBLANES, W_GRP),
            f32(n_ch, W_GRP, W_GRP),
            bf16(n_ch, W_GRP, W_GRP),
            bf16(hb, W_GRP),
            f32(tb, D_MODEL),
            f32(W_GRP, W_GRP),
        ],
        compiler_params=_compiler_params(("arbitrary", "arbitrary")),
    )(x, w["w_in"], w["sink"], w["lb"], w["ng"], w["cw"], w["pw"], w["ps"], w["w_o"], w["ln1_g"], w["ln1_b"])


def _gather_block(dst_scr, src_scr, n_tiles, gi, sb, nb):
    for t in range(DEC_SEQ):
        start = pl.multiple_of(t * nb + gi * sb, sb)
        for j in range(n_tiles):
            dst_scr[j, t * sb:(t + 1) * sb, :] = src_scr[j, pl.ds(start, sb), :]


def _scatter_block(dst_scr, src_scr, n_tiles, gi, sb, nb):
    for t in range(DEC_SEQ):
        start = pl.multiple_of(t * nb + gi * sb, sb)
        for j in range(n_tiles):
            dst_scr[pl.ds(start, sb), j * LANES:(j + 1) * LANES] = src_scr[j, t * sb:(t + 1) * sb, :]


def _sample_mixer_kernel(l, xt_ref, ck_ref, cv_ref, s_ref, cs_ref, pb_ref,
                         w_in_ref, sink_ref, lbc_ref, ngc_ref, cw_ref, pw_ref, ps_ref,
                         w_o_ref, g_ref, b_ref, *rest):
    (y_ref, knew_ref, vnew_ref, snew_ref, cnew_ref, pnew_ref,
     proj_scr, a_scr, r_scr, q_scr, f_scr, k_scr, v_scr, gate_scr, o_scr, mix_scr,
     ab_scr, oa_scr, sc_scr, sn_scr, pc_scr, pn_scr) = rest[-22:]
    gi = pl.program_id(0)
    sb = ck_ref.shape[0]
    nb = xt_ref.shape[0] // DEC_SEQ
    half = sb // 2

    @pl.when(gi == 0)
    def _():
        xt = xt_ref[...].astype(BF16)
        pa = _dot(xt, w_in_ref[:, 0:C_BQ])
        for j in range(4):
            a_scr[j] = pa[:, j * LANES:(j + 1) * LANES]
        r_scr[...] = jnp.zeros(r_scr.shape, F32)
        hgt =_dot(xt, w_in_ref[:, C_BQ:C_CB]).T
        lbc = lbc_ref[...]
        q_scr[...] = _silu(hgt[0:W_GRP, :])
        sig_pos, sig_neg = _sigmoid_pair(hgt[W_GRP:2 * W_GRP, :])
        f_scr[...] = sig_pos + jnp.maximum(lbc, LB_FLOOR) * sig_neg
        k_scr[...] = (1.0 - lbc) * sig_neg
        v_scr[...] = hgt[2 * W_GRP:3 * W_GRP, :]
        gate_scr[...] = _silu(hgt[3 * W_GRP:4 * W_GRP, :])
        o_scr[...] = jnp.zeros(o_scr.shape, F32)
        proj_scr[...] = _dot(xt, w_in_ref[:, C_CB:D_IN])
        cw = cw_ref[...]
        ps = ps_ref[...]
        cp = lax.broadcasted_iota(jnp.int32, (nb, W_GRP), 1) >> 6
        width = jnp.left_shift(2, cp)
        u = [cs_ref[0], cs_ref[1]]
        ext = [pb_ref[i] for i in range(POOL_BUF)]
        for t in range(DEC_SEQ):
            rows = slice(t * nb, (t + 1) * nb)
            u.append(proj_scr[rows, W_GRP:2 * W_GRP] * proj_scr[rows, 2 * W_GRP:3 * W_GRP])
            ext.append(proj_scr[rows, 3 * W_GRP:4 * W_GRP])
        cnew_ref[0] = u[DEC_SEQ]
        cnew_ref[1] = u[DEC_SEQ + 1]
        for i in range(POOL_BUF):
            pnew_ref[i] = ext[DEC_SEQ + i]
        for t in range(DEC_SEQ):
            rows = slice(t * nb, (t + 1) * nb)
            yc = u[t] * cw[0:1, :] + u[t + 1] * cw[1:2, :] + u[t + 2] * cw[2:3, :]
            mix_scr[rows, 2 * W_GRP:3 * W_GRP] = proj_scr[rows, 0:W_GRP] * yc
            top = POOL_BUF + t
            acc = ext[top]
            sums = {}
            for jj in range(1, 16):
                acc = acc + ext[top - jj]
                if jj + 1 in POOL_WINDOWS:
                    sums[jj + 1] = acc
            win = jnp.where(cp == 0, sums[2], jnp.where(cp == 1, sums[4], jnp.where(cp == 2, sums[8], sums[16])))
            cnt = jnp.minimum(PAST_LEN + t + 1, width).astype(F32)
            pooled = win / cnt - ext[top]
            mix_scr[rows, 3 * W_GRP:4 * W_GRP] = _dot(pooled.astype(BF16), pw_ref[...]) * ps

    _gather_block(ab_scr, a_scr, 4, gi, sb, nb)
    lane = lax.broadcasted_iota(jnp.int32, (PAIR_ROWS, LANES), 1)
    lo = lane < HEAD_DIM
    n_rows = 4 * PAIR_ROWS
    r_c = lax.broadcasted_iota(jnp.int32, (n_rows, 2 * WINDOW), 0)
    c_c = lax.broadcasted_iota(jnp.int32, (n_rows, 2 * WINDOW), 1)
    rel_c = ((r_c >> 1) & 3) + WINDOW - (c_c & (WINDOW - 1))
    ok_c = ((r_c & 1) == (c_c >> 7)) & (rel_c <= WINDOW)
    bias_c = jnp.where(ok_c, -jnp.exp2(-2.0 * ((r_c >> 3).astype(F32) + 1.0)) * rel_c.astype(F32), MASK_VALUE)
    r_n = lax.broadcasted_iota(jnp.int32, (n_rows, PAIR_ROWS), 0)
    c_n = lax.broadcasted_iota(jnp.int32, (n_rows, PAIR_ROWS), 1)
    rel_n = ((r_n >> 1) & 3) - (c_n >> 1)
    ok_n = ((r_n & 1) == (c_n & 1)) & (rel_n >= 0)
    bias_n = jnp.where(ok_n, -jnp.exp2(-2.0 * ((r_n >> 3).astype(F32) + 1.0)) * rel_n.astype(F32), MASK_VALUE)
    pair_rows = lambda p: pl.ds(p, PAIR_ROWS, stride=half)
    for p in range(half):
        q0 = ab_scr[0, pair_rows(p), :] * (HEAD_DIM ** -0.5)
        q1 = ab_scr[1, pair_rows(p), :] * (HEAD_DIM ** -0.5)
        q0r = pltpu.roll(q0, HEAD_DIM, axis=1)
        q1r = pltpu.roll(q1, HEAD_DIM, axis=1)
        zero = jnp.zeros_like(q0)
        q4 = jnp.concatenate([jnp.where(lo, q0, zero), jnp.where(lo, q0r, zero),
                              jnp.where(lo, zero, q1r), jnp.where(lo, zero, q1)], axis=0).astype(BF16)
        kt2 = jnp.concatenate([ck_ref[p], ck_ref[p + half]], axis=1).astype(BF16)
        kn8 = ab_scr[2, pair_rows(p), :].astype(BF16)
        sc_scr[n_rows * p:n_rows * (p + 1), :] = _dot(q4, kt2) + bias_c
        sn_scr[n_rows * p:n_rows * (p + 1), :] = _nt_dot(q4, kn8) + bias_n
    sink_col = _sink_select(sink_ref, l, (lax.broadcasted_iota(jnp.int32, (n_rows * half, 1), 0) >> 3) & 3)
    s_c = sc_scr[...]
    s_n = sn_scr[...]
    m = jnp.maximum(jnp.maximum(jnp.max(s_c, axis=-1, keepdims=True), jnp.max(s_n, axis=-1, keepdims=True)),
                    sink_col)
    p_c = jnp.exp(s_c - m)
    p_n = jnp.exp(s_n - m)
    inv = 1.0 / (jnp.sum(p_c, axis=-1, keepdims=True) + jnp.sum(p_n, axis=-1, keepdims=True)
                 + jnp.exp(sink_col - m))
    pc_scr[...] = (p_c * inv).astype(BF16)
    pn_scr[...] = (p_n * inv).astype(BF16)
    for p in range(half):
        vt2 = jnp.concatenate([cv_ref[p], cv_ref[p + half]], axis=1).astype(BF16)
        vn8 = ab_scr[3, pair_rows(p), :].astype(BF16)
        o_all = (_nt_dot(pc_scr[n_rows * p:n_rows * (p + 1), :], vt2)
                 + _dot(pn_scr[n_rows * p:n_rows * (p + 1), :], vn8))
        o = [o_all[h * PAIR_ROWS:(h + 1) * PAIR_ROWS, :] for h in range(4)]
        oa_scr[0, pair_rows(p), :] = jnp.where(lo, o[0], pltpu.roll(o[1], HEAD_DIM, axis=1))
        oa_scr[1, pair_rows(p), :] = jnp.where(lo, pltpu.roll(o[2], HEAD_DIM, axis=1), o[3])
    _scatter_block(mix_scr, oa_scr, 2, gi, sb, nb)
    lane_w = lax.broadcasted_iota(jnp.int32, (LANES, LANES), 1)
    for tile, old_ref, out_ref in ((2, ck_ref, knew_ref), (3, cv_ref, vnew_ref)):
        for t in range(DEC_SEQ):
            r_scr[pl.ds(t, sb, stride=SUBLANES), :] = ab_scr[tile, t * sb:(t + 1) * sb, :]
        new_t = r_scr[...].T
        for b in range(sb):
            new_cols = pltpu.roll(new_t, (WINDOW - DEC_SEQ - SUBLANES * b) % LANES, axis=1)
            old_cols = pltpu.roll(old_ref[b], WINDOW - DEC_SEQ, axis=1)
            out_ref[b] = jnp.where(lane_w >= WINDOW - DEC_SEQ, new_cols, old_cols)

    steps_per_head = pl.num_programs(0) // N_HG
    head = gi // steps_per_head
    k_base = head * HEAD_DIM + (gi % steps_per_head) * K_PER_STEP
    v_rows = pl.ds(pl.multiple_of(head * HEAD_DIM, HEAD_DIM), HEAD_DIM)

    def k_body(k8, accs):
        rows8 = pl.ds(pl.multiple_of(k_base + k8 * SUBLANES, SUBLANES), SUBLANES)
        accs = list(accs)
        for j in range(SUBLANES):
            st_rows = pl.ds(pl.multiple_of((k8 * SUBLANES + j) * HEAD_DIM, HEAD_DIM), HEAD_DIM)
            st = s_ref[st_rows, :]
            for t in range(DEC_SEQ):
                cols = slice(t * nb, (t + 1) * nb)
                f8, k8v, q8 = f_scr[rows8, cols], k_scr[rows8, cols], q_scr[rows8, cols]
                st = f8[j:j + 1, :] * st + k8v[j:j + 1, :] * v_scr[v_rows, cols]
                accs[t] = accs[t] + q8[j:j + 1, :] * st
            snew_ref[st_rows, :] = st
        return tuple(accs)

    zero_acc = jnp.zeros((HEAD_DIM, nb), F32)
    accs = lax.fori_loop(0, K_PER_STEP // SUBLANES, k_body, (zero_acc,) * DEC_SEQ)
    for t in range(DEC_SEQ):
        cols = slice(t * nb, (t + 1) * nb)
        o_scr[v_rows, cols] = o_scr[v_rows, cols] + accs[t]

    @pl.when(gi == pl.num_programs(0) - 1)
    def _():
        o = o_scr[...]
        parts = []
        for h in range(N_HG):
            oh = o[h * HEAD_DIM:(h + 1) * HEAD_DIM, :]
            parts.append(oh * lax.rsqrt(jnp.mean(oh * oh, axis=0, keepdims=True) + RMS_EPS))
        ob = jnp.concatenate(parts, axis=0) * ngc_ref[...] * gate_scr[...]
        y = (_dot(mix_scr[:, 0:W_GRP].astype(BF16), w_o_ref[0:W_GRP, :])
             + _tn_dot(ob.astype(BF16), w_o_ref[W_GRP:2 * W_GRP, :])
             + _dot(mix_scr[:, 2 * W_GRP:4 * W_GRP].astype(BF16), w_o_ref[2 * W_GRP:4 * W_GRP, :]))
        y_ref[...] = _layer_norm(ALPHA * xt_ref[...] + y, g_ref[...], b_ref[...])


def _sample_mixer_call(l, xt, ck, cv, s, cs_tm, pb_tm, w, prev):
    m = xt.shape[0]
    nb = m // DEC_SEQ
    sb = SEQ_BLOCK
    n_steps = nb // sb
    n_sc = 4 * PAIR_ROWS * (sb // 2)
    s_rows = HG_ROWS // n_steps
    single = pl.Buffered(1)
    const2 = lambda i: (0, 0)
    out_shape = (
        jax.ShapeDtypeStruct((m, D_MODEL), F32),
        jax.ShapeDtypeStruct((DEPTH, nb, LANES, WINDOW), F32),
        jax.ShapeDtypeStruct((DEPTH, nb, LANES, WINDOW), F32),
        jax.ShapeDtypeStruct((DEPTH, HG_ROWS, nb), F32),
        jax.ShapeDtypeStruct((CONV_W - 1, nb, W_GRP), F32),
        jax.ShapeDtypeStruct((POOL_BUF, nb, W_GRP), F32),
    )
    seq_blk = lambda i: (l, i, 0, 0)
    in_specs = [pl.BlockSpec((m, D_MODEL), const2, pipeline_mode=single),
                pl.BlockSpec((None, sb, LANES, WINDOW), seq_blk),
                pl.BlockSpec((None, sb, LANES, WINDOW), seq_blk),
                pl.BlockSpec((None, s_rows, nb), lambda i: (l, i, 0)),
                _layer_spec((CONV_W - 1, nb, W_GRP), l, single=True),
                _layer_spec((POOL_BUF, nb, W_GRP), l, single=True),
                _layer_spec((D_MODEL, D_IN), l, single=True),
                pl.BlockSpec(memory_space=pltpu.SMEM),
                _layer_spec((W_GRP, 1), l), _layer_spec((W_GRP, 1), l),
                _layer_spec((CONV_W, W_GRP), l), _layer_spec((W_GRP, W_GRP), l), _layer_spec((1, W_GRP), l),
                _layer_spec((D_MODEL, D_MODEL), l, single=True),
                _layer_spec((1, D_MODEL), l), _layer_spec((1, D_MODEL), l)]
    args = [xt, ck, cv, s, cs_tm, pb_tm, w["w_in"], w["sink"], w["lb_col"], w["ng_col"],
            w["cw"], w["pw"], w["ps"], w["w_o"], w["ln1_g"], w["ln1_b"]]
    aliases = {}
    if prev is not None:
        for j, buf in enumerate(prev):
            aliases[len(args)] = 1 + j
            args.append(buf)
            in_specs.append(pl.BlockSpec(memory_space=pl.ANY))
    f32 = lambda *shape: pltpu.VMEM(shape, F32)
    return pl.pallas_call(
        functools.partial(_sample_mixer_kernel, l),
        name="mixer_sample",
        grid=(n_steps,),
        in_specs=in_specs,
        out_specs=(pl.BlockSpec((m, D_MODEL), const2),
                   pl.BlockSpec((None, sb, LANES, WINDOW), seq_blk),
                   pl.BlockSpec((None, sb, LANES, WINDOW), seq_blk),
                   pl.BlockSpec((None, s_rows, nb), lambda i: (l, i, 0)),
                   pl.BlockSpec((CONV_W - 1, nb, W_GRP), lambda i: (0, 0, 0)),
                   pl.BlockSpec((POOL_BUF, nb, W_GRP), lambda i: (0, 0, 0))),
        out_shape=out_shape,
        input_output_aliases=aliases,
        scratch_shapes=[f32(m, 4 * W_GRP),
                        f32(4, m, LANES),
                        f32(SUBLANES * sb, LANES),
                        f32(W_GRP, m), f32(W_GRP, m), f32(W_GRP, m), f32(W_GRP, m), f32(W_GRP, m),
                        f32(W_GRP, m),
                        f32(m, D_MODEL),
                        f32(4, sb * DEC_SEQ, LANES), f32(2, sb * DEC_SEQ, LANES),
                        f32(n_sc, 2 * WINDOW), f32(n_sc, PAIR_ROWS),
                        pltpu.VMEM((n_sc, 2 * WINDOW), BF16), pltpu.VMEM((n_sc, PAIR_ROWS), BF16)],
        compiler_params=_compiler_params(("arbitrary",)),
    )(*args)


def _xattn_sample_kernel(xt_ref, mk_ref, mv_ref, wq_ref, wo_ref, g_ref, b_ref, o_ref,
                         q_scr, qb_scr, ob_scr, att_scr, s_scr, p_scr):
    gi = pl.program_id(0)
    sb = mk_ref.shape[0]
    nb = xt_ref.shape[0] // DEC_SEQ
    half = sb // 2
    n_rows = N_XH * PAIR_ROWS

    @pl.when(gi == 0)
    def _():
        q = _dot(xt_ref[...].astype(BF16), wq_ref[...]) * (HEAD_DIM ** -0.5)
        q_scr[0] = q[:, 0:LANES]
        q_scr[1] = q[:, LANES:2 * LANES]

    _gather_block(qb_scr, q_scr, 2, gi, sb, nb)
    lane_head = lax.broadcasted_iota(jnp.int32, (PAIR_ROWS, D_X), 1) >> 6
    r_s = lax.broadcasted_iota(jnp.int32, (n_rows, 2 * N_MEM), 0)
    c_s = lax.broadcasted_iota(jnp.int32, (n_rows, 2 * N_MEM), 1)
    own = (r_s & 1) == (c_s >> 8)
    pair_rows = lambda p: pl.ds(p, PAIR_ROWS, stride=half)
    for p in range(half):
        q8 = jnp.concatenate([qb_scr[0, pair_rows(p), :], qb_scr[1, pair_rows(p), :]], axis=1)
        zero = jnp.zeros_like(q8)
        q4 = jnp.concatenate([jnp.where(lane_head == h, q8, zero) for h in range(N_XH)], axis=0).astype(BF16)
        kt2 = jnp.concatenate([mk_ref[p], mk_ref[p + half]], axis=1).astype(BF16)
        s_scr[n_rows * p:n_rows * (p + 1), :] = jnp.where(own, _dot(q4, kt2), MASK_VALUE)
    s_all = s_scr[...]
    m = jnp.max(s_all, axis=-1, keepdims=True)
    pr = jnp.exp(s_all - m)
    p_scr[...] = (pr * (1.0 / jnp.sum(pr, axis=-1, keepdims=True))).astype(BF16)
    for p in range(half):
        vt2 = jnp.concatenate([mv_ref[p], mv_ref[p + half]], axis=1).astype(BF16)
        o_all = _nt_dot(p_scr[n_rows * p:n_rows * (p + 1), :], vt2)
        o8 = jnp.zeros((PAIR_ROWS, D_X), F32)
        for h in range(N_XH):
            o8 = jnp.where(lane_head == h, o_all[h * PAIR_ROWS:(h + 1) * PAIR_ROWS, :], o8)
        ob_scr[0, pair_rows(p), :] = o8[:, 0:LANES]
        ob_scr[1, pair_rows(p), :] = o8[:, LANES:2 * LANES]
    _scatter_block(att_scr, ob_scr, 2, gi, sb, nb)

    @pl.when(gi == pl.num_programs(0) - 1)
    def _():
        y = _dot(att_scr[...].astype(BF16), wo_ref[...])
        o_ref[...] = _layer_norm(ALPHA * xt_ref[...] + y, g_ref[...], b_ref[...])


def _xattn_sample_call(l, xt, mk, mv, w):
    m = xt.shape[0]
    nb = m // DEC_SEQ
    sb = SEQ_BLOCK
    n_sc = N_XH * PAIR_ROWS * (sb // 2)
    const2 = lambda i: (0, 0)
    return pl.pallas_call(
        _xattn_sample_kernel,
        name="xattn_sample",
        grid=(nb // sb,),
        in_specs=[pl.BlockSpec((m, D_MODEL), const2, pipeline_mode=pl.Buffered(1)),
                  pl.BlockSpec((None, sb, D_X, N_MEM), lambda i: (l, i, 0, 0)),
                  pl.BlockSpec((None, sb, D_X, N_MEM), lambda i: (l, i, 0, 0)),
                  _layer_spec((D_MODEL, D_X), l), _layer_spec((D_X, D_MODEL), l),
                  _layer_spec((1, D_MODEL), l), _layer_spec((1, D_MODEL), l)],
        out_specs=pl.BlockSpec((m, D_MODEL), const2),
        out_shape=jax.ShapeDtypeStruct((m, D_MODEL), F32),
        scratch_shapes=[pltpu.VMEM((2, m, LANES), F32),
                        pltpu.VMEM((2, sb * DEC_SEQ, LANES), F32),
                        pltpu.VMEM((2, sb * DEC_SEQ, LANES), F32),
                        pltpu.VMEM((m, D_X), F32),
                        pltpu.VMEM((n_sc, 2 * N_MEM), F32),
                        pltpu.VMEM((n_sc, 2 * N_MEM), BF16)],
        compiler_params=_compiler_params(("arbitrary",)),
    )(xt, mk, mv, w["w_xq"], w["w_xo"], w["ln2_g"], w["ln2_b"])


def _hgrn_lower_bounds(lb_param):
    p = jax.nn.softmax(lb_param.astype(F32), axis=0)
    return jnp.cumsum(p, axis=0) - p[0:1]


def _prepare_weights(w_in, attn_sink, hgrn_lb, hgrn_norm_g, conv_w, pool_w, pool_scale, w_o, ln1_g, ln1_b,
                     w_xq, w_xo, ln2_g, ln2_b, w_gate, w_up, w_down, ln3_g, ln3_b):
    bf = lambda a: a.astype(BF16)
    rows = lambda a: a.astype(F32)[:, None, :]
    cols = lambda a: a.astype(F32)[:, :, None]
    lb_all = _hgrn_lower_bounds(hgrn_lb)
    eye = jnp.eye(len(POOL_WINDOWS), dtype=pool_w.dtype)
    pw = (pool_w[:, :, :, None, :] * eye[None, :, None, :, None]).reshape(DEPTH, W_GRP, W_GRP)
    return dict(
        w_in=bf(w_in),
        sink=attn_sink.astype(F32),
        lb=rows(lb_all), lb_col=cols(lb_all), ng=rows(hgrn_norm_g), ng_col=cols(hgrn_norm_g),
        cw=conv_w.astype(F32), pw=bf(pw), ps=rows(pool_scale),
        w_o=bf(w_o), ln1_g=rows(ln1_g), ln1_b=rows(ln1_b),
        w_xq=bf(w_xq), w_xo=bf(w_xo), ln2_g=rows(ln2_g), ln2_b=rows(ln2_b),
        w_gate=bf(w_gate), w_up=bf(w_up), w_down=bf(w_down), ln3_g=rows(ln3_g), ln3_b=rows(ln3_b))


def _prompt_layer(l, x, mkv, w):
    bsz, t, _ = x.shape
    x, kn, vn, sn, cn, pn = _prompt_mixer_call(l, x, w)
    x = _xattn_prompt_call(l, x, mkv, w)
    x = _ffn_call(l, x.reshape(bsz * t, D_MODEL), w).reshape(bsz, t, D_MODEL)
    return x, kn, vn, sn, cn, pn


def _sample_layer(l, xt, ck, cv, s, cs_tm, pb_tm, mk, mv, w, prev):
    xt, kn, vn, sn, cn, pn = _sample_mixer_call(l, xt, ck, cv, s, cs_tm, pb_tm, w, prev)
    xt = _xattn_sample_call(l, xt, mk, mv, w)
    xt = _ffn_call(l, xt, w)
    return xt, kn, vn, sn, cn, pn


def kernel(x_prompt, x_sample, cache_swa_k, cache_swa_v, state_hgrn, state_conv, state_pool, cache_mem_k,
           cache_mem_v, mem_prompt, emb_ln_g, emb_ln_b, w_in, attn_sink, hgrn_lb, hgrn_norm_g, conv_w, pool_w,
           pool_scale, w_o, ln1_g, ln1_b, w_xq, w_xk, w_xv, w_xo, ln2_g, ln2_b, w_gate, w_up, w_down, ln3_g,
           ln3_b):
    bp, t, _ = x_prompt.shape
    bs, ts, _ = x_sample.shape
    w = _prepare_weights(w_in, attn_sink, hgrn_lb, hgrn_norm_g, conv_w, pool_w, pool_scale, w_o, ln1_g, ln1_b,
                         w_xq, w_xo, ln2_g, ln2_b, w_gate, w_up, w_down, ln3_g, ln3_b)
    emb_g = emb_ln_g.reshape(1, D_MODEL).astype(F32)
    emb_b = emb_ln_b.reshape(1, D_MODEL).astype(F32)
    hp = _ln_call(x_prompt.reshape(bp * t, D_MODEL), emb_g, emb_b).reshape(bp, t, D_MODEL)
    w_kv = jnp.concatenate([w_xk, w_xv], axis=2).transpose(1, 0, 2).reshape(D_MODEL, DEPTH * 2 * D_X).astype(BF16)
    mkv = _matmul_call(mem_prompt.reshape(bp * N_MEM, D_MODEL), w_kv).reshape(bp, N_MEM, DEPTH * 2 * D_X)
    hs = _ln_call(x_sample.transpose(1, 0, 2).reshape(ts * bs, D_MODEL), emb_g, emb_b)
    ck = cache_swa_k.transpose(0, 1, 3, 4, 2).reshape(DEPTH, bs, LANES, WINDOW)
    cv = cache_swa_v.transpose(0, 1, 3, 4, 2).reshape(DEPTH, bs, LANES, WINDOW)
    st = state_hgrn.transpose(0, 2, 3, 4, 1).reshape(DEPTH, HG_ROWS, bs)
    cs_tm = state_conv.transpose(0, 2, 1, 3)
    pb_tm = state_pool.transpose(0, 2, 1, 3)
    mk_s = cache_mem_k.transpose(0, 1, 3, 4, 2).reshape(DEPTH, bs, D_X, N_MEM)
    mv_s = cache_mem_v.transpose(0, 1, 3, 4, 2).reshape(DEPTH, bs, D_X, N_MEM)
    outs = [[] for _ in range(5)]
    souts = [[] for _ in range(2)]
    prev = None
    for l in range(DEPTH):
        res = _prompt_layer(l, hp, mkv, w)
        hp = res[0]
        for acc, r in zip(outs, res[1:]):
            acc.append(r)
        sres = _sample_layer(l, hs, ck, cv, st, cs_tm, pb_tm, mk_s, mv_s, w, prev)
        hs = sres[0]
        prev = sres[1:4]
        for acc, r in zip(souts, sres[4:]):
            acc.append(r)
    pk, pv, ps, pc, pp = [jnp.stack(o) for o in outs]
    sk, sv, ss = prev
    sc, sp = [jnp.stack(o) for o in souts]
    mem_out = mkv.reshape(bp, N_MEM, DEPTH, 2, N_XH, HEAD_DIM).transpose(3, 2, 0, 1, 4, 5)
    swa_out = lambda a: a.reshape(DEPTH, bs, N_KV, HEAD_DIM, WINDOW).transpose(0, 1, 4, 2, 3)
    return (hp, hs.reshape(ts, bs, D_MODEL).transpose(1, 0, 2),
            pk.reshape(DEPTH, bp, WINDOW, N_KV, HEAD_DIM), pv.reshape(DEPTH, bp, WINDOW, N_KV, HEAD_DIM),
            ps.reshape(DEPTH, bp, N_HG, HEAD_DIM, HEAD_DIM), pc, pp,
            mem_out[0], mem_out[1],
            swa_out(sk), swa_out(sv),
            ss.reshape(DEPTH, N_HG, HEAD_DIM, HEAD_DIM, bs).transpose(0, 4, 1, 2, 3),
            sc.transpose(0, 2, 1, 3), sp.transpose(0, 2, 1, 3))
```

```python
import functools

import jax
import jax.numpy as jnp
from jax import lax
from jax.experimental import pallas as pl
from jax.experimental.pallas import tpu as pltpu

F32 = jnp.float32
BF16 = jnp.bfloat16

D_MODEL = 1024
DEPTH = 4
HEAD_DIM = 64
W_GRP = 256
N_KV = 2
WINDOW = 128
N_HG = 4
CONV_W = 3
POOL_WINDOWS = (2, 4, 8, 16)
POOL_BUF = 15
N_MEM = 256
N_XH = 4
D_X = 256
D_FF = 2816
D_IN = 2560
DEC_SEQ = 4
ALPHA = (2 * DEPTH) ** 0.25
LN_EPS = 1e-5
RMS_EPS = 1e-6
MASK_VALUE = -1e30
LB_FLOOR = 1e-30
PAST_LEN = 8192

C_AQ, C_AK, C_AV = 0, 256, 384
C_BQ, C_BF, C_BI, C_BG = 512, 768, 1024, 1280
C_CB, C_CC, C_CH = 1536, 1792, 2048
C_DV = 2304

LANES = 128
SUBLANES = 8
VMEM_LIMIT_BYTES = 56 * 1024 * 1024

TOKEN_BLOCK = 512
SUB_BLOCKS = 2
HG_CHUNK = 64
HG_MID = HG_CHUNK // 2 - 1
HG_MAX_EXPONENT = 80.0
HG_ROWS = N_HG * HEAD_DIM * HEAD_DIM
SEQ_BLOCK = 16
PAIR_ROWS = 2 * DEC_SEQ
K_PER_STEP = HG_ROWS // HEAD_DIM * SEQ_BLOCK // 128


def _nt_dot(a, b):
    return lax.dot_general(a, b, (((1,), (1,)), ((), ())), preferred_element_type=F32)


def _tn_dot(a, b):
    return lax.dot_general(a, b, (((0,), (0,)), ((), ())), preferred_element_type=F32)


def _dot(a, b):
    return jnp.dot(a, b, preferred_element_type=F32)


def _layer_norm(x, g, b):
    mu = jnp.mean(x, axis=-1, keepdims=True)
    xc = x - mu
    var = jnp.mean(xc * xc, axis=-1, keepdims=True)
    return xc * lax.rsqrt(var + LN_EPS) * g + b


def _sigmoid_pair(z):
    e = jnp.exp(-jnp.abs(z))
    inv = 1.0 / (1.0 + e)
    small = e * inv
    pos = z >= 0
    return jnp.where(pos, inv, small), jnp.where(pos, small, inv)


def _silu(z):
    s, _ = _sigmoid_pair(z)
    return z * s


def _silu_tanh(z):
    return z * (0.5 + 0.5 * jnp.tanh(0.5 * z))


def _full_spec(shape):
    nd = len(shape)
    return pl.BlockSpec(shape, lambda *_: (0,) * nd)


def _layer_spec(shape, l, single=False):
    nd = len(shape)
    index = lambda *_: (l,) + (0,) * nd
    if single:
        return pl.BlockSpec((None,) + tuple(shape), index, pipeline_mode=pl.Buffered(1))
    return pl.BlockSpec((None,) + tuple(shape), index)


def _compiler_params(sem):
    return pltpu.CompilerParams(dimension_semantics=sem, vmem_limit_bytes=VMEM_LIMIT_BYTES)


class _MatmulQueue:
    def __init__(self):
        self.items = []

    def add(self, fn, *args):
        self.items.append((fn, args))

    def issue(self, n=1):
        for _ in range(min(n, len(self.items))):
            fn, args = self.items.pop(0)
            fn(*args)

    def flush(self):
        self.issue(len(self.items))


def _ln_kernel(x_ref, g_ref, b_ref, o_ref):
    o_ref[...] = _layer_norm(x_ref[...], g_ref[...], b_ref[...])


def _ln_call(x2d, g, b):
    m = x2d.shape[0]
    return pl.pallas_call(
        _ln_kernel,
        name="input_ln",
        grid=(m // TOKEN_BLOCK,),
        in_specs=[pl.BlockSpec((TOKEN_BLOCK, D_MODEL), lambda i: (i, 0)),
                  _full_spec((1, D_MODEL)), _full_spec((1, D_MODEL))],
        out_specs=pl.BlockSpec((TOKEN_BLOCK, D_MODEL), lambda i: (i, 0)),
        out_shape=jax.ShapeDtypeStruct((m, D_MODEL), F32),
        compiler_params=_compiler_params(("arbitrary",)),
    )(x2d, g, b)


def _matmul_kernel(x_ref, w_ref, o_ref):
    o_ref[...] = _dot(x_ref[...].astype(BF16), w_ref[...])


def _matmul_call(x2d, w):
    m, k = x2d.shape
    n = w.shape[1]
    return pl.pallas_call(
        _matmul_kernel,
        name="mem_proj",
        grid=(m // TOKEN_BLOCK,),
        in_specs=[pl.BlockSpec((TOKEN_BLOCK, k), lambda i: (i, 0)), _full_spec((k, n))],
        out_specs=pl.BlockSpec((TOKEN_BLOCK, n), lambda i: (i, 0)),
        out_shape=jax.ShapeDtypeStruct((m, n), F32),
        compiler_params=_compiler_params(("arbitrary",)),
    )(x2d, w)


def _ffn_kernel(x_ref, wg_ref, wu_ref, wd_ref, g_ref, b_ref, o_ref):
    x = x_ref[...]
    xb = x.astype(BF16)
    h = _silu(_dot(xb, wg_ref[...])) * _dot(xb, wu_ref[...])
    y = _dot(h.astype(BF16), wd_ref[...])
    o_ref[...] = _layer_norm(ALPHA * x + y, g_ref[...], b_ref[...])


def _ffn_call(l, x2d, w):
    m = x2d.shape[0]
    tb = min(2 * TOKEN_BLOCK, m)
    return pl.pallas_call(
        _ffn_kernel,
        name="ffn",
        grid=(m // tb,),
        in_specs=[pl.BlockSpec((tb, D_MODEL), lambda i: (i, 0)),
                  _layer_spec((D_MODEL, D_FF), l, single=True),
                  _layer_spec((D_MODEL, D_FF), l, single=True),
                  _layer_spec((D_FF, D_MODEL), l, single=True),
                  _layer_spec((1, D_MODEL), l), _layer_spec((1, D_MODEL), l)],
        out_specs=pl.BlockSpec((tb, D_MODEL), lambda i: (i, 0)),
        out_shape=jax.ShapeDtypeStruct((m, D_MODEL), F32),
        compiler_params=_compiler_params(("arbitrary",)),
    )(x2d, w["w_gate"], w["w_up"], w["w_down"], w["ln3_g"], w["ln3_b"])


def _xattn_prompt_kernel(x_ref, mk_ref, mv_ref, wq_ref, wo_ref, g_ref, b_ref, o_ref, q_scr, att_scr, yo_scr):
    tb = x_ref.shape[0]
    hb = tb // SUB_BLOCKS
    mk = mk_ref[...].astype(BF16)
    mv = mv_ref[...].astype(BF16)
    lane_head = lax.broadcasted_iota(jnp.int32, (hb, D_X), 1) >> 6
    sub = lambda r: slice(r * hb, (r + 1) * hb)

    def project(r):
        q_scr[sub(r), :] = _dot(x_ref[sub(r), :].astype(BF16), wq_ref[...]) * (HEAD_DIM ** -0.5)

    def finish(r, c0, c1):
        yo_scr[sub(r), c0:c1] = _dot(att_scr[sub(r), :], wo_ref[:, c0:c1])

    def finish_ln(r):
        o_ref[sub(r), :] = _layer_norm(ALPHA * x_ref[sub(r), :] + yo_scr[sub(r), :], g_ref[...], b_ref[...])

    out_cols = ((0, D_MODEL // 2), (D_MODEL // 2, D_MODEL))
    project(0)
    queue = _MatmulQueue()
    for r in range(SUB_BLOCKS):
        if r >= 1:
            for cols in out_cols:
                queue.add(finish, r - 1, *cols)
        if r + 1 < SUB_BLOCKS:
            queue.add(project, r + 1)
        q = q_scr[sub(r), :]
        o = jnp.zeros_like(q)
        for h in range(N_XH):
            qh = jnp.where(lane_head == h, q, 0.0).astype(BF16)
            s = _nt_dot(mk, qh)
            m = jnp.max(s, axis=0, keepdims=True)
            p = jnp.exp(s - m)
            den = jnp.sum(p, axis=0, keepdims=True)
            p = (p * (1.0 / den)).astype(BF16)
            o = jnp.where(lane_head == h, _tn_dot(p, mv), o)
            queue.issue()
        att_scr[sub(r), :] = o.astype(BF16)
        queue.flush()
        if r >= 1:
            finish_ln(r - 1)
    for cols in out_cols:
        finish(SUB_BLOCKS - 1, *cols)
    finish_ln(SUB_BLOCKS - 1)


def _xattn_prompt_call(l, x, mkv, w):
    bsz, t, _ = x.shape
    tb = min(2 * TOKEN_BLOCK, t)
    return pl.pallas_call(
        _xattn_prompt_kernel,
        name="xattn_prompt",
        grid=(bsz, t // tb),
        in_specs=[pl.BlockSpec((None, tb, D_MODEL), lambda i, j: (i, j, 0)),
                  pl.BlockSpec((None, N_MEM, D_X), lambda i, j: (i, 0, 2 * l)),
                  pl.BlockSpec((None, N_MEM, D_X), lambda i, j: (i, 0, 2 * l + 1)),
                  _layer_spec((D_MODEL, D_X), l), _layer_spec((D_X, D_MODEL), l),
                  _layer_spec((1, D_MODEL), l), _layer_spec((1, D_MODEL), l)],
        out_specs=pl.BlockSpec((None, tb, D_MODEL), lambda i, j: (i, j, 0)),
        out_shape=jax.ShapeDtypeStruct((bsz, t, D_MODEL), F32),
        scratch_shapes=[pltpu.VMEM((tb, D_X), F32), pltpu.VMEM((tb, D_X), BF16), pltpu.VMEM((tb, D_MODEL), F32)],
        compiler_params=_compiler_params(("arbitrary", "arbitrary")),
    )(x, mkv, mkv, w["w_xq"], w["w_xo"], w["ln2_g"], w["ln2_b"])


def _swa_bias_table():
    c = lax.broadcasted_iota(jnp.int32, (2 * WINDOW, 4 * WINDOW), 0)
    r = lax.broadcasted_iota(jnp.int32, (2 * WINDOW, 4 * WINDOW), 1)
    head = r >> 7
    rel = (r & (WINDOW - 1)) + WINDOW - c
    slope = jnp.exp2(-2.0 * (head.astype(F32) + 1.0))
    valid = (rel >= 0) & (rel <= WINDOW)
    return jnp.where(valid, -slope * rel.astype(F32), MASK_VALUE)


def _sink_select(sink_ref, l, head):
    return jnp.where(head == 0, sink_ref[l, 0],
                     jnp.where(head == 1, sink_ref[l, 1], jnp.where(head == 2, sink_ref[l, 2], sink_ref[l, 3])))


def _prompt_mixer_kernel(l, x_ref, eg_ref, eb_ref, w_in_ref, sink_ref, lb_ref, ng_ref, cw_ref, pw_ref, ps_ref,
                         w_o_ref, g_ref, b_ref,
                         y_ref, knew_ref, vnew_ref, snew_ref, cnew_ref, pnew_ref,
                         proj_scr, kext_scr, vext_scr, st_scr, u_scr, p_scr, bias_scr, mix_scr, hg_scr,
                         hq_scr, hk_scr, hv_scr, ghl_scr, cum_scr, qp_scr, kp_scr, qs_scr, ks_scr, dec_scr, inc_scr,
                         stb_scr, a_scr, yo_scr, st0_scr):
    tb = x_ref.shape[0]
    hb = tb // SUB_BLOCKS
    n_qb = hb // WINDOW
    n_ch = hb // HG_CHUNK
    bi = pl.program_id(0)
    ti = pl.program_id(1)
    last = ti == pl.num_programs(1) - 1

    @pl.when((bi == 0) & (ti == 0))
    def _():
        bias_scr[...] = _swa_bias_table()

    @pl.when(ti == 0)
    def _():
        kext_scr[0:WINDOW, :] = jnp.zeros((WINDOW, LANES), BF16)
        vext_scr[0:WINDOW, :] = jnp.zeros((WINDOW, LANES), BF16)
        st_scr[...] = jnp.zeros(st_scr.shape, F32)
        u_scr[0:SUBLANES, :] = jnp.zeros((SUBLANES, W_GRP), F32)
        p_scr[0:16, :] = jnp.zeros((16, W_GRP), F32)

    embed = l == 0

    def normalize(r):
        rows = slice(r * hb, (r + 1) * hb)
        y_ref[rows, :] = _layer_norm(x_ref[rows, :], eg_ref[...], eb_ref[...])

    def stream(rows):
        return y_ref[rows, :] if embed else x_ref[rows, :]

    def project(r, c0, c1):
        rows = slice(r * hb, (r + 1) * hb)
        proj_scr[rows, c0:c1] = _dot(stream(rows).astype(BF16), w_in_ref[:, c0:c1])

    def finish(r, c0, c1):
        rows = slice(r * hb, (r + 1) * hb)
        yo_scr[rows, c0:c1] = _dot(mix_scr[rows, :], w_o_ref[:, c0:c1])

    def finish_ln(r, redo=False):
        rows = slice(r * hb, (r + 1) * hb)
        if embed and redo:
            base = _layer_norm(x_ref[rows, :], eg_ref[...], eb_ref[...])
        else:
            base = stream(rows)
        y_ref[rows, :] = _layer_norm(ALPHA * base + yo_scr[rows, :], g_ref[...], b_ref[...])

    proj_cols = ((0, C_BQ), (C_BQ, C_BI), (C_BI, C_CB), (C_CB, C_CH), (C_CH, D_IN))
    out_cols = ((0, D_MODEL // 2), (D_MODEL // 2, D_MODEL))
    if embed:
        normalize(0)
    for cols in proj_cols:
        project(0, *cols)
    queue = _MatmulQueue()

    lane = lax.broadcasted_iota(jnp.int32, (WINDOW, LANES), 1)
    lo = lane < HEAD_DIM
    key_row = lax.broadcasted_iota(jnp.int32, (2 * WINDOW, 4 * WINDOW), 0)
    sink_row = _sink_select(sink_ref, l, lax.broadcasted_iota(jnp.int32, (1, 4 * WINDOW), 1) >> 7)
    lb = lb_ref[...]
    lbf = jnp.maximum(lb, LB_FLOOR)
    one_m_lb = 1.0 - lb
    ng = ng_ref[...]
    r256 = lax.broadcasted_iota(jnp.int32, (W_GRP, W_GRP), 0)
    c256 = lax.broadcasted_iota(jnp.int32, (W_GRP, W_GRP), 1)
    same_head = (r256 >> 6) == (c256 >> 6)
    head_ones = jnp.where(same_head, 1.0, 0.0).astype(BF16)
    same_head_b = head_ones > 0
    zero_b = jnp.zeros((W_GRP, W_GRP), BF16)
    rc = lax.broadcasted_iota(jnp.int32, (HG_CHUNK, W_GRP), 0)
    cc = lax.broadcasted_iota(jnp.int32, (HG_CHUNK, W_GRP), 1)
    causal = (cc & (HG_CHUNK - 1)) <= rc
    r64 = lax.broadcasted_iota(jnp.int32, (HG_CHUNK, HG_CHUNK), 0)
    c64 = lax.broadcasted_iota(jnp.int32, (HG_CHUNK, HG_CHUNK), 1)
    tril = jnp.where(c64 <= r64, 1.0, 0.0).astype(BF16)
    cw = cw_ref[...]
    rp = lax.broadcasted_iota(jnp.int32, (hb, W_GRP), 0)
    grp = lax.broadcasted_iota(jnp.int32, (hb, W_GRP), 1) >> 6
    width = jnp.left_shift(2, grp)
    chunk = lambda c: slice(c * HG_CHUNK, (c + 1) * HG_CHUNK)

    def hgrn_output(r):
        rows = slice(r * hb, (r + 1) * hb)
        o = hg_scr[...]
        ms = _dot((o * o).astype(BF16), head_ones) * (1.0 / HEAD_DIM)
        o = o * lax.rsqrt(ms + RMS_EPS) * ng
        mix_scr[rows, W_GRP:2 * W_GRP] = (o * _silu_tanh(proj_scr[rows, C_BG:C_BG + W_GRP])).astype(BF16)

    st0_scr[...] = st_scr[...]
    span = jnp.zeros((1, W_GRP), F32)
    for r in range(SUB_BLOCKS):
        r0 = r * hb
        sub = slice(r0, r0 + hb)
        if r >= 1:
            for cols in out_cols:
                queue.add(finish, r - 1, *cols)
        if r + 1 < SUB_BLOCKS:
            if embed:
                queue.add(normalize, r + 1)
            for cols in proj_cols:
                queue.add(project, r + 1, *cols)

        kext_scr[WINDOW:WINDOW + hb, :] = proj_scr[sub, C_AK:C_AK + LANES].astype(BF16)
        vext_scr[WINDOW:WINDOW + hb, :] = proj_scr[sub, C_AV:C_AV + LANES].astype(BF16)
        for j in range(n_qb):
            rows = slice(r0 + j * WINDOW, r0 + (j + 1) * WINDOW)
            q0 = proj_scr[rows, 0:LANES] * (HEAD_DIM ** -0.5)
            q1 = proj_scr[rows, LANES:2 * LANES] * (HEAD_DIM ** -0.5)
            q0r = pltpu.roll(q0, HEAD_DIM, axis=1)
            q1r = pltpu.roll(q1, HEAD_DIM, axis=1)
            zero = jnp.zeros_like(q0)
            q4 = jnp.concatenate([jnp.where(lo, q0, zero), jnp.where(lo, q0r, zero),
                                  jnp.where(lo, zero, q1r), jnp.where(lo, zero, q1)], axis=0).astype(BF16)
            kj = kext_scr[j * WINDOW:(j + 2) * WINDOW, :]
            vj = vext_scr[j * WINDOW:(j + 2) * WINDOW, :]
            s = _nt_dot(kj, q4) + bias_scr[...]
            if r == 0 and j == 0:
                s = jnp.where((ti == 0) & (key_row < WINDOW), MASK_VALUE, s)
            m = jnp.maximum(jnp.max(s, axis=0, keepdims=True), sink_row)
            p = jnp.exp(s - m)
            den = jnp.sum(p, axis=0, keepdims=True) + jnp.exp(sink_row - m)
            p = (p * (1.0 / den)).astype(BF16)
            o_all = _tn_dot(p, vj)
            o = [o_all[h * WINDOW:(h + 1) * WINDOW, :] for h in range(4)]
            mix_scr[rows, 0:LANES] = jnp.where(lo, o[0], pltpu.roll(o[1], HEAD_DIM, axis=1)).astype(BF16)
            mix_scr[rows, LANES:2 * LANES] = jnp.where(lo, pltpu.roll(o[2], HEAD_DIM, axis=1), o[3]).astype(BF16)
            queue.issue()
        kext_scr[0:WINDOW, :] = kext_scr[hb:hb + WINDOW, :]
        vext_scr[0:WINDOW, :] = vext_scr[hb:hb + WINDOW, :]

        hq_scr[...] = _silu_tanh(proj_scr[sub, C_BQ:C_BQ + W_GRP])
        queue.issue()
        sig_pos, sig_neg = _sigmoid_pair(proj_scr[sub, C_BF:C_BF + W_GRP])
        g = jnp.log(sig_pos + lbf * sig_neg)
        hk_scr[...] = one_m_lb * sig_neg
        g_hi = g.astype(BF16)
        ghl_scr[:, 0:W_GRP] = g_hi
        ghl_scr[:, W_GRP:2 * W_GRP] = (g - g_hi.astype(F32)).astype(BF16)
        hv_scr[...] = proj_scr[sub, C_BI:C_BI + W_GRP].astype(BF16)
        for c in range(n_ch):
            cum2 = _dot(tril, ghl_scr[chunk(c), :])
            cum_scr[chunk(c), :] = cum2[:, 0:W_GRP] + cum2[:, W_GRP:2 * W_GRP]
        for c in range(n_ch):
            cum = cum_scr[chunk(c), :]
            ref = cum[HG_MID:HG_MID + 1, :]
            tot = cum[HG_CHUNK - 1:HG_CHUNK, :]
            span = jnp.maximum(span, jnp.maximum(cum[0:1, :] - ref, ref - tot))
            qp = hq_scr[chunk(c), :] * jnp.exp(cum - ref)
            kp = hk_scr[chunk(c), :] * jnp.exp(ref - cum)
            qp_scr[chunk(c), :] = qp.astype(BF16)
            kp_scr[chunk(c), :] = kp.astype(BF16)
            qs_scr[chunk(c), :] = (qp * jnp.exp(ref)).astype(BF16)
            ks_scr[chunk(c), :] = (kp * jnp.exp(tot - ref)).astype(BF16)
            dec_scr[c] = jnp.broadcast_to(jnp.exp(tot), (SUBLANES, W_GRP))
        queue.issue()
        for c in range(n_ch):
            bk = jnp.where(same_head_b, jnp.concatenate([kp_scr[chunk(c), :]] * N_HG, axis=0), zero_b)
            a = jnp.where(causal, _nt_dot(qp_scr[chunk(c), :], bk), 0.0)
            a_scr[chunk(c), :] = a.astype(BF16)
        for c in range(n_ch):
            inc_scr[c] = jnp.where(same_head, _tn_dot(hv_scr[chunk(c), :], ks_scr[chunk(c), :]), 0.0)
        for c in range(n_ch):
            bv = jnp.where(same_head_b, jnp.concatenate([hv_scr[chunk(c), :]] * N_HG, axis=0), zero_b)
            hg_scr[chunk(c), :] = _dot(a_scr[chunk(c), :], bv)
        st = st_scr[...]
        for c in range(n_ch):
            stb_scr[c] = st.astype(BF16)
            st = st * dec_scr[c, 0:1, :] + inc_scr[c]
        st_scr[...] = st
        queue.issue()
        for c in range(n_ch):
            hg_scr[chunk(c), :] = hg_scr[chunk(c), :] + _nt_dot(qs_scr[chunk(c), :], stb_scr[c])

        hgrn_output(r)
        queue.issue()

        u_scr[SUBLANES:SUBLANES + hb, :] = proj_scr[sub, C_CC:C_CC + W_GRP] * proj_scr[sub, C_CH:C_CH + W_GRP]
        yc = (u_scr[SUBLANES - 2:SUBLANES - 2 + hb, :] * cw[0:1, :]
              + u_scr[SUBLANES - 1:SUBLANES - 1 + hb, :] * cw[1:2, :]
              + u_scr[SUBLANES:SUBLANES + hb, :] * cw[2:3, :])
        mix_scr[sub, 2 * W_GRP:3 * W_GRP] = (proj_scr[sub, C_CB:C_CB + W_GRP] * yc).astype(BF16)
        u_scr[0:SUBLANES, :] = u_scr[hb:hb + SUBLANES, :]
        queue.issue()

        dv = proj_scr[sub, C_DV:C_DV + W_GRP]
        p_scr[16:16 + hb, :] = dv
        ext = p_scr[...]
        s2 = ext + pltpu.roll(ext, 1, axis=0)
        s4 = s2 + pltpu.roll(s2, 2, axis=0)
        s8 = s4 + pltpu.roll(s4, 4, axis=0)
        s16 = s8 + pltpu.roll(s8, 8, axis=0)
        win = jnp.where(grp == 0, s2[16:], jnp.where(grp == 1, s4[16:], jnp.where(grp == 2, s8[16:], s16[16:])))
        cnt = jnp.minimum(ti * tb + r0 + rp + 1, width).astype(F32)
        pooled = win / cnt - dv
        yd = _dot(pooled.astype(BF16), pw_ref[...]) * ps_ref[...]
        mix_scr[sub, 3 * W_GRP:4 * W_GRP] = yd.astype(BF16)
        p_scr[0:16, :] = p_scr[hb:hb + 16, :]
        queue.flush()
        if r >= 1:
            finish_ln(r - 1)

    for cols in out_cols:
        finish(SUB_BLOCKS - 1, *cols)
    finish_ln(SUB_BLOCKS - 1)

    @pl.when(jnp.max(span) > HG_MAX_EXPONENT)
    def _():
        st_scr[...] = st0_scr[...]
        sub8 = lax.broadcasted_iota(jnp.int32, (SUBLANES, W_GRP), 0)
        for r in range(SUB_BLOCKS):
            def tile_body(i, carry, r=r):
                rows8 = pl.ds(pl.multiple_of(i * SUBLANES, SUBLANES), SUBLANES)
                prow = pl.ds(pl.multiple_of(r * hb + i * SUBLANES, SUBLANES), SUBLANES)
                sp8, sn8 = _sigmoid_pair(proj_scr[prow, C_BF:C_BF + W_GRP])
                f8 = sp8 + lbf * sn8
                k8 = (one_m_lb * sn8).astype(BF16)
                v8 = proj_scr[prow, C_BI:C_BI + W_GRP]
                q8 = _silu_tanh(proj_scr[prow, C_BQ:C_BQ + W_GRP]).astype(BF16)
                o8 = jnp.zeros((SUBLANES, W_GRP), F32)
                for j in range(SUBLANES):
                    vj = jnp.where(sub8 == j, v8, 0.0).astype(BF16)
                    stj = st_scr[...] * f8[j:j + 1, :] + jnp.where(same_head, _tn_dot(vj, k8), 0.0)
                    st_scr[...] = stj
                    o8 = jnp.where(sub8 == j, _nt_dot(q8, stj.astype(BF16)), o8)
                hg_scr[rows8, :] = o8
                return carry

            lax.fori_loop(0, hb // SUBLANES, tile_body, 0)
            hgrn_output(r)
            for cols in out_cols:
                finish(r, *cols)
            finish_ln(r, redo=True)

    @pl.when(last)
    def _():
        knew_ref[...] = proj_scr[tb - WINDOW:tb, C_AK:C_AK + LANES]
        vnew_ref[...] = proj_scr[tb - WINDOW:tb, C_AV:C_AV + LANES]
        s_t = st_scr[...].T
        for h in range(N_HG):
            snew_ref[h * HEAD_DIM:(h + 1) * HEAD_DIM, :] = (
                s_t[h * HEAD_DIM:(h + 1) * HEAD_DIM, h * HEAD_DIM:(h + 1) * HEAD_DIM])
        cnew_ref[...] = u_scr[SUBLANES - 2:SUBLANES, :]
        pnew_ref[...] = p_scr[1:16, :]


def _prompt_mixer_call(l, x, w):
    bsz, t, _ = x.shape
    tb = min(2 * TOKEN_BLOCK, t)
    hb = tb // SUB_BLOCKS
    n_ch = hb // HG_CHUNK
    row = lambda i, j: (i, 0, 0)
    out_shape = (
        jax.ShapeDtypeStruct((bsz, t, D_MODEL), F32),
        jax.ShapeDtypeStruct((bsz, WINDOW, LANES), F32),
        jax.ShapeDtypeStruct((bsz, WINDOW, LANES), F32),
        jax.ShapeDtypeStruct((bsz, W_GRP, HEAD_DIM), F32),
        jax.ShapeDtypeStruct((bsz, CONV_W - 1, W_GRP), F32),
        jax.ShapeDtypeStruct((bsz, POOL_BUF, W_GRP), F32),
    )
    f32 = lambda *shape: pltpu.VMEM(shape, F32)
    bf16 = lambda *shape: pltpu.VMEM(shape, BF16)
    return pl.pallas_call(
        functools.partial(_prompt_mixer_kernel, l),
        name="mixer_prompt",
        grid=(bsz, t // tb),
        in_specs=[pl.BlockSpec((None, tb, D_MODEL), lambda i, j: (i, j, 0)),
                  _full_spec((1, D_MODEL)), _full_spec((1, D_MODEL)),
                  _layer_spec((D_MODEL, D_IN), l, single=True),
                  pl.BlockSpec(memory_space=pltpu.SMEM),
                  _layer_spec((1, W_GRP), l), _layer_spec((1, W_GRP), l), _layer_spec((CONV_W, W_GRP), l),
                  _layer_spec((W_GRP, W_GRP), l), _layer_spec((1, W_GRP), l),
                  _layer_spec((D_MODEL, D_MODEL), l, single=True),
                  _layer_spec((1, D_MODEL), l), _layer_spec((1, D_MODEL), l)],
        out_specs=(pl.BlockSpec((None, tb, D_MODEL), lambda i, j: (i, j, 0)),
                   pl.BlockSpec((None, WINDOW, LANES), row),
                   pl.BlockSpec((None, WINDOW, LANES), row),
                   pl.BlockSpec((None, W_GRP, HEAD_DIM), row),
                   pl.BlockSpec((None, CONV_W - 1, W_GRP), row),
                   pl.BlockSpec((None, POOL_BUF, W_GRP), row)),
        out_shape=out_shape,
        scratch_shapes=[
            f32(tb, D_IN),
            bf16(WINDOW + hb, LANES),
            bf16(WINDOW + hb, LANES),
            f32(W_GRP, W_GRP),
            f32(SUBLANES + hb, W_GRP),
            f32(16 + hb, W_GRP),
            f32(2 * WINDOW, 4 * WINDOW),
            bf16(tb, D_MODEL),
            f32(hb, W_GRP),
            f32(hb, W_GRP), f32(hb, W_GRP),
            bf16(hb, W_GRP),
            bf16(hb, 2 * W_GRP),
            f32(hb, W_GRP),
            bf16(hb, W_GRP), bf16(hb, W_GRP),
            bf16(hb, W_GRP), bf16(hb, W_GRP),
            f32(n_ch, SUBLANES, W_GRP),
            f32(n_ch, W_GRP, W_GRP),
            bf16(n_ch, W_GRP, W_GRP),
            bf16(hb, W_GRP),
            f32(tb, D_MODEL),
            f32(W_GRP, W_GRP),
        ],
        compiler_params=_compiler_params(("arbitrary", "arbitrary")),
    )(x, w["emb_g"], w["emb_b"], w["w_in"], w["sink"], w["lb"], w["ng"], w["cw"], w["pw"], w["ps"], w["w_o"],
      w["ln1_g"], w["ln1_b"])


def _gather_block(dst_scr, src_scr, n_tiles, gi, sb, nb):
    for t in range(DEC_SEQ):
        start = pl.multiple_of(t * nb + gi * sb, sb)
        for j in range(n_tiles):
            dst_scr[j, t * sb:(t + 1) * sb, :] = src_scr[j, pl.ds(start, sb), :]


def _scatter_block(dst_scr, src_scr, n_tiles, gi, sb, nb):
    for t in range(DEC_SEQ):
        start = pl.multiple_of(t * nb + gi * sb, sb)
        for j in range(n_tiles):
            dst_scr[pl.ds(start, sb), j * LANES:(j + 1) * LANES] = src_scr[j, t * sb:(t + 1) * sb, :]


def _sample_mixer_kernel(l, xt_ref, ck_ref, cv_ref, s_ref, cs_ref, pb_ref,
                         w_in_ref, sink_ref, lbc_ref, ngc_ref, cw_ref, pw_ref, ps_ref,
                         w_o_ref, g_ref, b_ref, *rest):
    (y_ref, knew_ref, vnew_ref, snew_ref, cnew_ref, pnew_ref,
     proj_scr, a_scr, r_scr, q_scr, f_scr, k_scr, v_scr, gate_scr, o_scr, mix_scr,
     ab_scr, oa_scr, sc_scr, sn_scr, pc_scr, pn_scr) = rest[-22:]
    gi = pl.program_id(0)
    sb = ck_ref.shape[0]
    nb = xt_ref.shape[0] // DEC_SEQ
    half = sb // 2

    @pl.when(gi == 0)
    def _():
        xt = xt_ref[...].astype(BF16)
        pa = _dot(xt, w_in_ref[:, 0:C_BQ])
        for j in range(4):
            a_scr[j] = pa[:, j * LANES:(j + 1) * LANES]
        r_scr[...] = jnp.zeros(r_scr.shape, F32)
        hgt =_dot(xt, w_in_ref[:, C_BQ:C_CB]).T
        lbc = lbc_ref[...]
        q_scr[...] = _silu(hgt[0:W_GRP, :])
        sig_pos, sig_neg = _sigmoid_pair(hgt[W_GRP:2 * W_GRP, :])
        f_scr[...] = sig_pos + jnp.maximum(lbc, LB_FLOOR) * sig_neg
        k_scr[...] = (1.0 - lbc) * sig_neg
        v_scr[...] = hgt[2 * W_GRP:3 * W_GRP, :]
        gate_scr[...] = _silu(hgt[3 * W_GRP:4 * W_GRP, :])
        o_scr[...] = jnp.zeros(o_scr.shape, F32)
        proj_scr[...] = _dot(xt, w_in_ref[:, C_CB:D_IN])
        cw = cw_ref[...]
        ps = ps_ref[...]
        cp = lax.broadcasted_iota(jnp.int32, (nb, W_GRP), 1) >> 6
        width = jnp.left_shift(2, cp)
        u = [cs_ref[0], cs_ref[1]]
        ext = [pb_ref[i] for i in range(POOL_BUF)]
        for t in range(DEC_SEQ):
            rows = slice(t * nb, (t + 1) * nb)
            u.append(proj_scr[rows, W_GRP:2 * W_GRP] * proj_scr[rows, 2 * W_GRP:3 * W_GRP])
            ext.append(proj_scr[rows, 3 * W_GRP:4 * W_GRP])
        cnew_ref[0] = u[DEC_SEQ]
        cnew_ref[1] = u[DEC_SEQ + 1]
        for i in range(POOL_BUF):
            pnew_ref[i] = ext[DEC_SEQ + i]
        for t in range(DEC_SEQ):
            rows = slice(t * nb, (t + 1) * nb)
            yc = u[t] * cw[0:1, :] + u[t + 1] * cw[1:2, :] + u[t + 2] * cw[2:3, :]
            mix_scr[rows, 2 * W_GRP:3 * W_GRP] = proj_scr[rows, 0:W_GRP] * yc
            top = POOL_BUF + t
            acc = ext[top]
            sums = {}
            for jj in range(1, 16):
                acc = acc + ext[top - jj]
                if jj + 1 in POOL_WINDOWS:
                    sums[jj + 1] = acc
            win = jnp.where(cp == 0, sums[2], jnp.where(cp == 1, sums[4], jnp.where(cp == 2, sums[8], sums[16])))
            cnt = jnp.minimum(PAST_LEN + t + 1, width).astype(F32)
            pooled = win / cnt - ext[top]
            mix_scr[rows, 3 * W_GRP:4 * W_GRP] = _dot(pooled.astype(BF16), pw_ref[...]) * ps

    _gather_block(ab_scr, a_scr, 4, gi, sb, nb)
    lane = lax.broadcasted_iota(jnp.int32, (PAIR_ROWS, LANES), 1)
    lo = lane < HEAD_DIM
    n_rows = 4 * PAIR_ROWS
    r_c = lax.broadcasted_iota(jnp.int32, (n_rows, 2 * WINDOW), 0)
    c_c = lax.broadcasted_iota(jnp.int32, (n_rows, 2 * WINDOW), 1)
    rel_c = ((r_c >> 1) & 3) + WINDOW - (c_c & (WINDOW - 1))
    ok_c = ((r_c & 1) == (c_c >> 7)) & (rel_c <= WINDOW)
    bias_c = jnp.where(ok_c, -jnp.exp2(-2.0 * ((r_c >> 3).astype(F32) + 1.0)) * rel_c.astype(F32), MASK_VALUE)
    r_n = lax.broadcasted_iota(jnp.int32, (n_rows, PAIR_ROWS), 0)
    c_n = lax.broadcasted_iota(jnp.int32, (n_rows, PAIR_ROWS), 1)
    rel_n = ((r_n >> 1) & 3) - (c_n >> 1)
    ok_n = ((r_n & 1) == (c_n & 1)) & (rel_n >= 0)
    bias_n = jnp.where(ok_n, -jnp.exp2(-2.0 * ((r_n >> 3).astype(F32) + 1.0)) * rel_n.astype(F32), MASK_VALUE)
    pair_rows = lambda p: pl.ds(p, PAIR_ROWS, stride=half)
    for p in range(half):
        q0 = ab_scr[0, pair_rows(p), :] * (HEAD_DIM ** -0.5)
        q1 = ab_scr[1, pair_rows(p), :] * (HEAD_DIM ** -0.5)
        q0r = pltpu.roll(q0, HEAD_DIM, axis=1)
        q1r = pltpu.roll(q1, HEAD_DIM, axis=1)
        zero = jnp.zeros_like(q0)
        q4 = jnp.concatenate([jnp.where(lo, q0, zero), jnp.where(lo, q0r, zero),
                              jnp.where(lo, zero, q1r), jnp.where(lo, zero, q1)], axis=0).astype(BF16)
        kt2 = jnp.concatenate([ck_ref[p], ck_ref[p + half]], axis=1).astype(BF16)
        kn8 = ab_scr[2, pair_rows(p), :].astype(BF16)
        sc_scr[n_rows * p:n_rows * (p + 1), :] = _dot(q4, kt2) + bias_c
        sn_scr[n_rows * p:n_rows * (p + 1), :] = _nt_dot(q4, kn8) + bias_n
    sink_col = _sink_select(sink_ref, l, (lax.broadcasted_iota(jnp.int32, (n_rows * half, 1), 0) >> 3) & 3)
    s_c = sc_scr[...]
    s_n = sn_scr[...]
    m = jnp.maximum(jnp.maximum(jnp.max(s_c, axis=-1, keepdims=True), jnp.max(s_n, axis=-1, keepdims=True)),
                    sink_col)
    p_c = jnp.exp(s_c - m)
    p_n = jnp.exp(s_n - m)
    inv = 1.0 / (jnp.sum(p_c, axis=-1, keepdims=True) + jnp.sum(p_n, axis=-1, keepdims=True)
                 + jnp.exp(sink_col - m))
    pc_scr[...] = (p_c * inv).astype(BF16)
    pn_scr[...] = (p_n * inv).astype(BF16)
    for p in range(half):
        vt2 = jnp.concatenate([cv_ref[p], cv_ref[p + half]], axis=1).astype(BF16)
        vn8 = ab_scr[3, pair_rows(p), :].astype(BF16)
        o_all = (_nt_dot(pc_scr[n_rows * p:n_rows * (p + 1), :], vt2)
                 + _dot(pn_scr[n_rows * p:n_rows * (p + 1), :], vn8))
        o = [o_all[h * PAIR_ROWS:(h + 1) * PAIR_ROWS, :] for h in range(4)]
        oa_scr[0, pair_rows(p), :] = jnp.where(lo, o[0], pltpu.roll(o[1], HEAD_DIM, axis=1))
        oa_scr[1, pair_rows(p), :] = jnp.where(lo, pltpu.roll(o[2], HEAD_DIM, axis=1), o[3])
    _scatter_block(mix_scr, oa_scr, 2, gi, sb, nb)
    lane_w = lax.broadcasted_iota(jnp.int32, (LANES, LANES), 1)
    for tile, old_ref, out_ref in ((2, ck_ref, knew_ref), (3, cv_ref, vnew_ref)):
        for t in range(DEC_SEQ):
            r_scr[pl.ds(t, sb, stride=SUBLANES), :] = ab_scr[tile, t * sb:(t + 1) * sb, :]
        new_t = r_scr[...].T
        for b in range(sb):
            new_cols = pltpu.roll(new_t, (WINDOW - DEC_SEQ - SUBLANES * b) % LANES, axis=1)
            old_cols = pltpu.roll(old_ref[b], WINDOW - DEC_SEQ, axis=1)
            out_ref[b] = jnp.where(lane_w >= WINDOW - DEC_SEQ, new_cols, old_cols)

    steps_per_head = pl.num_programs(0) // N_HG
    head = gi // steps_per_head
    k_base = head * HEAD_DIM + (gi % steps_per_head) * K_PER_STEP
    v_rows = pl.ds(pl.multiple_of(head * HEAD_DIM, HEAD_DIM), HEAD_DIM)

    def k_body(k8, accs):
        rows8 = pl.ds(pl.multiple_of(k_base + k8 * SUBLANES, SUBLANES), SUBLANES)
        accs = list(accs)
        for j in range(SUBLANES):
            st_rows = pl.ds(pl.multiple_of((k8 * SUBLANES + j) * HEAD_DIM, HEAD_DIM), HEAD_DIM)
            st = s_ref[st_rows, :]
            for t in range(DEC_SEQ):
                cols = slice(t * nb, (t + 1) * nb)
                f8, k8v, q8 = f_scr[rows8, cols], k_scr[rows8, cols], q_scr[rows8, cols]
                st = f8[j:j + 1, :] * st + k8v[j:j + 1, :] * v_scr[v_rows, cols]
                accs[t] = accs[t] + q8[j:j + 1, :] * st
            snew_ref[st_rows, :] = st
        return tuple(accs)

    zero_acc = jnp.zeros((HEAD_DIM, nb), F32)
    accs = lax.fori_loop(0, K_PER_STEP // SUBLANES, k_body, (zero_acc,) * DEC_SEQ)
    for t in range(DEC_SEQ):
        cols = slice(t * nb, (t + 1) * nb)
        o_scr[v_rows, cols] = o_scr[v_rows, cols] + accs[t]

    @pl.when(gi == pl.num_programs(0) - 1)
    def _():
        o = o_scr[...]
        parts = []
        for h in range(N_HG):
            oh = o[h * HEAD_DIM:(h + 1) * HEAD_DIM, :]
            parts.append(oh * lax.rsqrt(jnp.mean(oh * oh, axis=0, keepdims=True) + RMS_EPS))
        ob = jnp.concatenate(parts, axis=0) * ngc_ref[...] * gate_scr[...]
        y = (_dot(mix_scr[:, 0:W_GRP].astype(BF16), w_o_ref[0:W_GRP, :])
             + _tn_dot(ob.astype(BF16), w_o_ref[W_GRP:2 * W_GRP, :])
             + _dot(mix_scr[:, 2 * W_GRP:4 * W_GRP].astype(BF16), w_o_ref[2 * W_GRP:4 * W_GRP, :]))
        y_ref[...] = _layer_norm(ALPHA * xt_ref[...] + y, g_ref[...], b_ref[...])


def _sample_mixer_call(l, xt, ck, cv, s, cs_tm, pb_tm, w, prev):
    m = xt.shape[0]
    nb = m // DEC_SEQ
    sb = SEQ_BLOCK
    n_steps = nb // sb
    n_sc = 4 * PAIR_ROWS * (sb // 2)
    s_rows = HG_ROWS // n_steps
    single = pl.Buffered(1)
    const2 = lambda i: (0, 0)
    out_shape = (
        jax.ShapeDtypeStruct((m, D_MODEL), F32),
        jax.ShapeDtypeStruct((DEPTH, nb, LANES, WINDOW), F32),
        jax.ShapeDtypeStruct((DEPTH, nb, LANES, WINDOW), F32),
        jax.ShapeDtypeStruct((DEPTH, HG_ROWS, nb), F32),
        jax.ShapeDtypeStruct((CONV_W - 1, nb, W_GRP), F32),
        jax.ShapeDtypeStruct((POOL_BUF, nb, W_GRP), F32),
    )
    seq_blk = lambda i: (l, i, 0, 0)
    in_specs = [pl.BlockSpec((m, D_MODEL), const2, pipeline_mode=single),
                pl.BlockSpec((None, sb, LANES, WINDOW), seq_blk),
                pl.BlockSpec((None, sb, LANES, WINDOW), seq_blk),
                pl.BlockSpec((None, s_rows, nb), lambda i: (l, i, 0)),
                _layer_spec((CONV_W - 1, nb, W_GRP), l, single=True),
                _layer_spec((POOL_BUF, nb, W_GRP), l, single=True),
                _layer_spec((D_MODEL, D_IN), l, single=True),
                pl.BlockSpec(memory_space=pltpu.SMEM),
                _layer_spec((W_GRP, 1), l), _layer_spec((W_GRP, 1), l),
                _layer_spec((CONV_W, W_GRP), l), _layer_spec((W_GRP, W_GRP), l), _layer_spec((1, W_GRP), l),
                _layer_spec((D_MODEL, D_MODEL), l, single=True),
                _layer_spec((1, D_MODEL), l), _layer_spec((1, D_MODEL), l)]
    args = [xt, ck, cv, s, cs_tm, pb_tm, w["w_in"], w["sink"], w["lb_col"], w["ng_col"],
            w["cw"], w["pw"], w["ps"], w["w_o"], w["ln1_g"], w["ln1_b"]]
    aliases = {}
    if prev is not None:
        for j, buf in enumerate(prev):
            aliases[len(args)] = 1 + j
            args.append(buf)
            in_specs.append(pl.BlockSpec(memory_space=pl.ANY))
    f32 = lambda *shape: pltpu.VMEM(shape, F32)
    return pl.pallas_call(
        functools.partial(_sample_mixer_kernel, l),
        name="mixer_sample",
        grid=(n_steps,),
        in_specs=in_specs,
        out_specs=(pl.BlockSpec((m, D_MODEL), const2),
                   pl.BlockSpec((None, sb, LANES, WINDOW), seq_blk),
                   pl.BlockSpec((None, sb, LANES, WINDOW), seq_blk),
                   pl.BlockSpec((None, s_rows, nb), lambda i: (l, i, 0)),
                   pl.BlockSpec((CONV_W - 1, nb, W_GRP), lambda i: (0, 0, 0)),
                   pl.BlockSpec((POOL_BUF, nb, W_GRP), lambda i: (0, 0, 0))),
        out_shape=out_shape,
        input_output_aliases=aliases,
        scratch_shapes=[f32(m, 4 * W_GRP),
                        f32(4, m, LANES),
                        f32(SUBLANES * sb, LANES),
                        f32(W_GRP, m), f32(W_GRP, m), f32(W_GRP, m), f32(W_GRP, m), f32(W_GRP, m),
                        f32(W_GRP, m),
                        f32(m, D_MODEL),
                        f32(4, sb * DEC_SEQ, LANES), f32(2, sb * DEC_SEQ, LANES),
                        f32(n_sc, 2 * WINDOW), f32(n_sc, PAIR_ROWS),
                        pltpu.VMEM((n_sc, 2 * WINDOW), BF16), pltpu.VMEM((n_sc, PAIR_ROWS), BF16)],
        compiler_params=_compiler_params(("arbitrary",)),
    )(*args)


def _xattn_sample_kernel(xt_ref, mk_ref, mv_ref, wq_ref, wo_ref, g_ref, b_ref, o_ref,
                         q_scr, qb_scr, ob_scr, att_scr, s_scr, p_scr):
    gi = pl.program_id(0)
    sb = mk_ref.shape[0]
    nb = xt_ref.shape[0] // DEC_SEQ
    half = sb // 2
    n_rows = N_XH * PAIR_ROWS

    @pl.when(gi == 0)
    def _():
        q = _dot(xt_ref[...].astype(BF16), wq_ref[...]) * (HEAD_DIM ** -0.5)
        q_scr[0] = q[:, 0:LANES]
        q_scr[1] = q[:, LANES:2 * LANES]

    _gather_block(qb_scr, q_scr, 2, gi, sb, nb)
    lane_head = lax.broadcasted_iota(jnp.int32, (PAIR_ROWS, D_X), 1) >> 6
    r_s = lax.broadcasted_iota(jnp.int32, (n_rows, 2 * N_MEM), 0)
    c_s = lax.broadcasted_iota(jnp.int32, (n_rows, 2 * N_MEM), 1)
    own = (r_s & 1) == (c_s >> 8)
    pair_rows = lambda p: pl.ds(p, PAIR_ROWS, stride=half)
    for p in range(half):
        q8 = jnp.concatenate([qb_scr[0, pair_rows(p), :], qb_scr[1, pair_rows(p), :]], axis=1)
        zero = jnp.zeros_like(q8)
        q4 = jnp.concatenate([jnp.where(lane_head == h, q8, zero) for h in range(N_XH)], axis=0).astype(BF16)
        kt2 = jnp.concatenate([mk_ref[p], mk_ref[p + half]], axis=1).astype(BF16)
        s_scr[n_rows * p:n_rows * (p + 1), :] = jnp.where(own, _dot(q4, kt2), MASK_VALUE)
    s_all = s_scr[...]
    m = jnp.max(s_all, axis=-1, keepdims=True)
    pr = jnp.exp(s_all - m)
    p_scr[...] = (pr * (1.0 / jnp.sum(pr, axis=-1, keepdims=True))).astype(BF16)
    for p in range(half):
        vt2 = jnp.concatenate([mv_ref[p], mv_ref[p + half]], axis=1).astype(BF16)
        o_all = _nt_dot(p_scr[n_rows * p:n_rows * (p + 1), :], vt2)
        o8 = jnp.zeros((PAIR_ROWS, D_X), F32)
        for h in range(N_XH):
            o8 = jnp.where(lane_head == h, o_all[h * PAIR_ROWS:(h + 1) * PAIR_ROWS, :], o8)
        ob_scr[0, pair_rows(p), :] = o8[:, 0:LANES]
        ob_scr[1, pair_rows(p), :] = o8[:, LANES:2 * LANES]
    _scatter_block(att_scr, ob_scr, 2, gi, sb, nb)

    @pl.when(gi == pl.num_programs(0) - 1)
    def _():
        y = _dot(att_scr[...].astype(BF16), wo_ref[...])
        o_ref[...] = _layer_norm(ALPHA * xt_ref[...] + y, g_ref[...], b_ref[...])


def _xattn_sample_call(l, xt, mk, mv, w):
    m = xt.shape[0]
    nb = m // DEC_SEQ
    sb = SEQ_BLOCK
    n_sc = N_XH * PAIR_ROWS * (sb // 2)
    const2 = lambda i: (0, 0)
    return pl.pallas_call(
        _xattn_sample_kernel,
        name="xattn_sample",
        grid=(nb // sb,),
        in_specs=[pl.BlockSpec((m, D_MODEL), const2, pipeline_mode=pl.Buffered(1)),
                  pl.BlockSpec((None, sb, D_X, N_MEM), lambda i: (l, i, 0, 0)),
                  pl.BlockSpec((None, sb, D_X, N_MEM), lambda i: (l, i, 0, 0)),
                  _layer_spec((D_MODEL, D_X), l), _layer_spec((D_X, D_MODEL), l),
                  _layer_spec((1, D_MODEL), l), _layer_spec((1, D_MODEL), l)],
        out_specs=pl.BlockSpec((m, D_MODEL), const2),
        out_shape=jax.ShapeDtypeStruct((m, D_MODEL), F32),
        scratch_shapes=[pltpu.VMEM((2, m, LANES), F32),
                        pltpu.VMEM((2, sb * DEC_SEQ, LANES), F32),
                        pltpu.VMEM((2, sb * DEC_SEQ, LANES), F32),
                        pltpu.VMEM((m, D_X), F32),
                        pltpu.VMEM((n_sc, 2 * N_MEM), F32),
                        pltpu.VMEM((n_sc, 2 * N_MEM), BF16)],
        compiler_params=_compiler_params(("arbitrary",)),
    )(xt, mk, mv, w["w_xq"], w["w_xo"], w["ln2_g"], w["ln2_b"])


def _hgrn_lower_bounds(lb_param):
    p = jax.nn.softmax(lb_param.astype(F32), axis=0)
    return jnp.cumsum(p, axis=0) - p[0:1]


def _prepare_weights(w_in, attn_sink, hgrn_lb, hgrn_norm_g, conv_w, pool_w, pool_scale, w_o, ln1_g, ln1_b,
                     w_xq, w_xo, ln2_g, ln2_b, w_gate, w_up, w_down, ln3_g, ln3_b):
    bf = lambda a: a.astype(BF16)
    rows = lambda a: a.astype(F32)[:, None, :]
    cols = lambda a: a.astype(F32)[:, :, None]
    lb_all = _hgrn_lower_bounds(hgrn_lb)
    eye = jnp.eye(len(POOL_WINDOWS), dtype=pool_w.dtype)
    pw = (pool_w[:, :, :, None, :] * eye[None, :, None, :, None]).reshape(DEPTH, W_GRP, W_GRP)
    return dict(
        w_in=bf(w_in),
        sink=attn_sink.astype(F32),
        lb=rows(lb_all), lb_col=cols(lb_all), ng=rows(hgrn_norm_g), ng_col=cols(hgrn_norm_g),
        cw=conv_w.astype(F32), pw=bf(pw), ps=rows(pool_scale),
        w_o=bf(w_o), ln1_g=rows(ln1_g), ln1_b=rows(ln1_b),
        w_xq=bf(w_xq), w_xo=bf(w_xo), ln2_g=rows(ln2_g), ln2_b=rows(ln2_b),
        w_gate=bf(w_gate), w_up=bf(w_up), w_down=bf(w_down), ln3_g=rows(ln3_g), ln3_b=rows(ln3_b))


def _prompt_layer(l, x, mkv, w):
    bsz, t, _ = x.shape
    x, kn, vn, sn, cn, pn = _prompt_mixer_call(l, x, w)
    x = _xattn_prompt_call(l, x, mkv, w)
    x = _ffn_call(l, x.reshape(bsz * t, D_MODEL), w).reshape(bsz, t, D_MODEL)
    return x, kn, vn, sn, cn, pn


def _sample_layer(l, xt, ck, cv, s, cs_tm, pb_tm, mk, mv, w, prev):
    xt, kn, vn, sn, cn, pn = _sample_mixer_call(l, xt, ck, cv, s, cs_tm, pb_tm, w, prev)
    xt = _xattn_sample_call(l, xt, mk, mv, w)
    xt = _ffn_call(l, xt, w)
    return xt, kn, vn, sn, cn, pn


def kernel(x_prompt, x_sample, cache_swa_k, cache_swa_v, state_hgrn, state_conv, state_pool, cache_mem_k,
           cache_mem_v, mem_prompt, emb_ln_g, emb_ln_b, w_in, attn_sink, hgrn_lb, hgrn_norm_g, conv_w, pool_w,
           pool_scale, w_o, ln1_g, ln1_b, w_xq, w_xk, w_xv, w_xo, ln2_g, ln2_b, w_gate, w_up, w_down, ln3_g,
           ln3_b):
    bp, t, _ = x_prompt.shape
    bs, ts, _ = x_sample.shape
    w = _prepare_weights(w_in, attn_sink, hgrn_lb, hgrn_norm_g, conv_w, pool_w, pool_scale, w_o, ln1_g, ln1_b,
                         w_xq, w_xo, ln2_g, ln2_b, w_gate, w_up, w_down, ln3_g, ln3_b)
    emb_g = emb_ln_g.reshape(1, D_MODEL).astype(F32)
    emb_b = emb_ln_b.reshape(1, D_MODEL).astype(F32)
    w = dict(w, emb_g=emb_g, emb_b=emb_b)
    hp = x_prompt
    w_kv = jnp.concatenate([w_xk, w_xv], axis=2).transpose(1, 0, 2).reshape(D_MODEL, DEPTH * 2 * D_X).astype(BF16)
    mkv = _matmul_call(mem_prompt.reshape(bp * N_MEM, D_MODEL), w_kv).reshape(bp, N_MEM, DEPTH * 2 * D_X)
    hs = _ln_call(x_sample.transpose(1, 0, 2).reshape(ts * bs, D_MODEL), emb_g, emb_b)
    ck = cache_swa_k.transpose(0, 1, 3, 4, 2).reshape(DEPTH, bs, LANES, WINDOW)
    cv = cache_swa_v.transpose(0, 1, 3, 4, 2).reshape(DEPTH, bs, LANES, WINDOW)
    st = state_hgrn.transpose(0, 2, 3, 4, 1).reshape(DEPTH, HG_ROWS, bs)
    cs_tm = state_conv.transpose(0, 2, 1, 3)
    pb_tm = state_pool.transpose(0, 2, 1, 3)
    mk_s = cache_mem_k.transpose(0, 1, 3, 4, 2).reshape(DEPTH, bs, D_X, N_MEM)
    mv_s = cache_mem_v.transpose(0, 1, 3, 4, 2).reshape(DEPTH, bs, D_X, N_MEM)
    outs = [[] for _ in range(5)]
    souts = [[] for _ in range(2)]
    prev = None
    for l in range(DEPTH):
        res = _prompt_layer(l, hp, mkv, w)
        hp = res[0]
        for acc, r in zip(outs, res[1:]):
            acc.append(r)
        sres = _sample_layer(l, hs, ck, cv, st, cs_tm, pb_tm, mk_s, mv_s, w, prev)
        hs = sres[0]
        prev = sres[1:4]
        for acc, r in zip(souts, sres[4:]):
            acc.append(r)
    pk, pv, ps, pc, pp = [jnp.stack(o) for o in outs]
    sk, sv, ss = prev
    sc, sp = [jnp.stack(o) for o in souts]
    mem_out = mkv.reshape(bp, N_MEM, DEPTH, 2, N_XH, HEAD_DIM).transpose(3, 2, 0, 1, 4, 5)
    swa_out = lambda a: a.reshape(DEPTH, bs, N_KV, HEAD_DIM, WINDOW).transpose(0, 1, 4, 2, 3)
    return (hp, hs.reshape(ts, bs, D_MODEL).transpose(1, 0, 2),
            pk.reshape(DEPTH, bp, WINDOW, N_KV, HEAD_DIM), pv.reshape(DEPTH, bp, WINDOW, N_KV, HEAD_DIM),
            ps.reshape(DEPTH, bp, N_HG, HEAD_DIM, HEAD_DIM), pc, pp,
            mem_out[0], mem_out[1],
            swa_out(sk), swa_out(sv),
            ss.reshape(DEPTH, N_HG, HEAD_DIM, HEAD_DIM, bs).transpose(0, 4, 1, 2, 3),
            sc.transpose(0, 2, 1, 3), sp.transpose(0, 2, 1, 3))
```

```python
import functools

import jax
import jax.numpy as jnp
from jax import lax
from jax.experimental import pallas as pl
from jax.experimental.pallas import tpu as pltpu

F32 = jnp.float32
BF16 = jnp.bfloat16

D_MODEL = 1024
DEPTH = 4
HEAD_DIM = 64
W_GRP = 256
N_KV = 2
WINDOW = 128
N_HG = 4
CONV_W = 3
POOL_WINDOWS = (2, 4, 8, 16)
POOL_BUF = 15
N_MEM = 256
N_XH = 4
D_X = 256
D_FF = 2816
D_IN = 2560
DEC_SEQ = 4
ALPHA = (2 * DEPTH) ** 0.25
LN_EPS = 1e-5
RMS_EPS = 1e-6
MASK_VALUE = -1e30
LB_FLOOR = 1e-30
PAST_LEN = 8192

C_AQ, C_AK, C_AV = 0, 256, 384
C_BQ, C_BF, C_BI, C_BG = 512, 768, 1024, 1280
C_CB, C_CC, C_CH = 1536, 1792, 2048
C_DV = 2304

LANES = 128
SUBLANES = 8
VMEM_LIMIT_BYTES = 56 * 1024 * 1024

TOKEN_BLOCK = 512
SUB_BLOCKS = 2
HG_CHUNK = 64
HG_MID = HG_CHUNK // 2 - 1
HG_MAX_EXPONENT = 80.0
HG_ROWS = N_HG * HEAD_DIM * HEAD_DIM
SEQ_BLOCK = 16
PAIR_ROWS = 2 * DEC_SEQ


def _nt_dot(a, b):
    return lax.dot_general(a, b, (((1,), (1,)), ((), ())), preferred_element_type=F32)


def _tn_dot(a, b):
    return lax.dot_general(a, b, (((0,), (0,)), ((), ())), preferred_element_type=F32)


def _dot(a, b):
    return jnp.dot(a, b, preferred_element_type=F32)


def _layer_norm(x, g, b):
    mu = jnp.mean(x, axis=-1, keepdims=True)
    xc = x - mu
    var = jnp.mean(xc * xc, axis=-1, keepdims=True)
    return xc * lax.rsqrt(var + LN_EPS) * g + b


def _sigmoid_pair(z):
    e = jnp.exp(-jnp.abs(z))
    inv = 1.0 / (1.0 + e)
    small = e * inv
    pos = z >= 0
    return jnp.where(pos, inv, small), jnp.where(pos, small, inv)


def _silu_tanh(z):
    return z * (0.5 + 0.5 * jnp.tanh(0.5 * z))


def _full_spec(shape):
    nd = len(shape)
    return pl.BlockSpec(shape, lambda *_: (0,) * nd)


def _layer_spec(shape, l, single=False):
    nd = len(shape)
    index = lambda *_: (l,) + (0,) * nd
    if single:
        return pl.BlockSpec((None,) + tuple(shape), index, pipeline_mode=pl.Buffered(1))
    return pl.BlockSpec((None,) + tuple(shape), index)


def _compiler_params(sem):
    return pltpu.CompilerParams(dimension_semantics=sem, vmem_limit_bytes=VMEM_LIMIT_BYTES)


class _MatmulQueue:
    def __init__(self):
        self.items = []

    def add(self, fn, *args):
        self.items.append((fn, args))

    def issue(self, n=1):
        for _ in range(min(n, len(self.items))):
            fn, args = self.items.pop(0)
            fn(*args)

    def flush(self):
        self.issue(len(self.items))


def _ln_kernel(x_ref, g_ref, b_ref, o_ref):
    o_ref[...] = _layer_norm(x_ref[...], g_ref[...], b_ref[...])


def _ln_call(x2d, g, b):
    m = x2d.shape[0]
    return pl.pallas_call(
        _ln_kernel,
        name="input_ln",
        grid=(m // TOKEN_BLOCK,),
        in_specs=[pl.BlockSpec((TOKEN_BLOCK, D_MODEL), lambda i: (i, 0)),
                  _full_spec((1, D_MODEL)), _full_spec((1, D_MODEL))],
        out_specs=pl.BlockSpec((TOKEN_BLOCK, D_MODEL), lambda i: (i, 0)),
        out_shape=jax.ShapeDtypeStruct((m, D_MODEL), F32),
        compiler_params=_compiler_params(("arbitrary",)),
    )(x2d, g, b)


def _matmul_kernel(x_ref, w_ref, o_ref):
    o_ref[...] = _dot(x_ref[...].astype(BF16), w_ref[...])


def _matmul_call(x2d, w):
    m, k = x2d.shape
    n = w.shape[1]
    return pl.pallas_call(
        _matmul_kernel,
        name="mem_proj",
        grid=(m // TOKEN_BLOCK,),
        in_specs=[pl.BlockSpec((TOKEN_BLOCK, k), lambda i: (i, 0)), _full_spec((k, n))],
        out_specs=pl.BlockSpec((TOKEN_BLOCK, n), lambda i: (i, 0)),
        out_shape=jax.ShapeDtypeStruct((m, n), F32),
        compiler_params=_compiler_params(("arbitrary",)),
    )(x2d, w)


def _ffn_kernel(x_ref, wg_ref, wu_ref, wd_ref, g_ref, b_ref, o_ref):
    x = x_ref[...]
    xb = x.astype(BF16)
    h = _silu_tanh(_dot(xb, wg_ref[...])) * _dot(xb, wu_ref[...])
    y = _dot(h.astype(BF16), wd_ref[...])
    o_ref[...] = _layer_norm(ALPHA * x + y, g_ref[...], b_ref[...])


def _ffn_call(l, x2d, w):
    m = x2d.shape[0]
    tb = min(2 * TOKEN_BLOCK, m)
    return pl.pallas_call(
        _ffn_kernel,
        name="ffn",
        grid=(m // tb,),
        in_specs=[pl.BlockSpec((tb, D_MODEL), lambda i: (i, 0)),
                  _layer_spec((D_MODEL, D_FF), l, single=True),
                  _layer_spec((D_MODEL, D_FF), l, single=True),
                  _layer_spec((D_FF, D_MODEL), l, single=True),
                  _layer_spec((1, D_MODEL), l), _layer_spec((1, D_MODEL), l)],
        out_specs=pl.BlockSpec((tb, D_MODEL), lambda i: (i, 0)),
        out_shape=jax.ShapeDtypeStruct((m, D_MODEL), F32),
        compiler_params=_compiler_params(("arbitrary",)),
    )(x2d, w["w_gate"], w["w_up"], w["w_down"], w["ln3_g"], w["ln3_b"])


def _xattn_prompt_kernel(x_ref, mk_ref, mv_ref, wq_ref, wo_ref, g_ref, b_ref, o_ref, q_scr, att_scr, yo_scr):
    tb = x_ref.shape[0]
    hb = tb // SUB_BLOCKS
    mk = mk_ref[...].astype(BF16)
    mv = mv_ref[...].astype(BF16)
    lane_head = lax.broadcasted_iota(jnp.int32, (hb, D_X), 1) >> 6
    sub = lambda r: slice(r * hb, (r + 1) * hb)

    def project(r):
        q_scr[sub(r), :] = _dot(x_ref[sub(r), :].astype(BF16), wq_ref[...]) * (HEAD_DIM ** -0.5)

    def finish(r, c0, c1):
        yo_scr[sub(r), c0:c1] = _dot(att_scr[sub(r), :], wo_ref[:, c0:c1])

    def finish_ln(r):
        o_ref[sub(r), :] = _layer_norm(ALPHA * x_ref[sub(r), :] + yo_scr[sub(r), :], g_ref[...], b_ref[...])

    out_cols = ((0, D_MODEL // 2), (D_MODEL // 2, D_MODEL))
    project(0)
    queue = _MatmulQueue()
    for r in range(SUB_BLOCKS):
        if r >= 1:
            for cols in out_cols:
                queue.add(finish, r - 1, *cols)
        if r + 1 < SUB_BLOCKS:
            queue.add(project, r + 1)
        q = q_scr[sub(r), :]
        o = jnp.zeros_like(q)
        for h in range(N_XH):
            qh = jnp.where(lane_head == h, q, 0.0).astype(BF16)
            s = _nt_dot(mk, qh)
            m = jnp.max(s, axis=0, keepdims=True)
            p = jnp.exp(s - m)
            den = jnp.sum(p, axis=0, keepdims=True)
            p = (p * (1.0 / den)).astype(BF16)
            o = jnp.where(lane_head == h, _tn_dot(p, mv), o)
            queue.issue()
        att_scr[sub(r), :] = o.astype(BF16)
        queue.flush()
        if r >= 1:
            finish_ln(r - 1)
    for cols in out_cols:
        finish(SUB_BLOCKS - 1, *cols)
    finish_ln(SUB_BLOCKS - 1)


def _xattn_prompt_call(l, x, mkv, w):
    bsz, t, _ = x.shape
    tb = min(2 * TOKEN_BLOCK, t)
    return pl.pallas_call(
        _xattn_prompt_kernel,
        name="xattn_prompt",
        grid=(bsz, t // tb),
        in_specs=[pl.BlockSpec((None, tb, D_MODEL), lambda i, j: (i, j, 0)),
                  pl.BlockSpec((None, N_MEM, D_X), lambda i, j: (i, 0, 2 * l)),
                  pl.BlockSpec((None, N_MEM, D_X), lambda i, j: (i, 0, 2 * l + 1)),
                  _layer_spec((D_MODEL, D_X), l), _layer_spec((D_X, D_MODEL), l),
                  _layer_spec((1, D_MODEL), l), _layer_spec((1, D_MODEL), l)],
        out_specs=pl.BlockSpec((None, tb, D_MODEL), lambda i, j: (i, j, 0)),
        out_shape=jax.ShapeDtypeStruct((bsz, t, D_MODEL), F32),
        scratch_shapes=[pltpu.VMEM((tb, D_X), F32), pltpu.VMEM((tb, D_X), BF16), pltpu.VMEM((tb, D_MODEL), F32)],
        compiler_params=_compiler_params(("arbitrary", "arbitrary")),
    )(x, mkv, mkv, w["w_xq"], w["w_xo"], w["ln2_g"], w["ln2_b"])


def _swa_bias_table():
    c = lax.broadcasted_iota(jnp.int32, (2 * WINDOW, 4 * WINDOW), 0)
    r = lax.broadcasted_iota(jnp.int32, (2 * WINDOW, 4 * WINDOW), 1)
    head = r >> 7
    rel = (r & (WINDOW - 1)) + WINDOW - c
    slope = jnp.exp2(-2.0 * (head.astype(F32) + 1.0))
    valid = (rel >= 0) & (rel <= WINDOW)
    return jnp.where(valid, -slope * rel.astype(F32), MASK_VALUE)


def _sink_select(sink_ref, l, head):
    return jnp.where(head == 0, sink_ref[l, 0],
                     jnp.where(head == 1, sink_ref[l, 1], jnp.where(head == 2, sink_ref[l, 2], sink_ref[l, 3])))


def _prompt_mixer_kernel(l, x_ref, eg_ref, eb_ref, w_in_ref, sink_ref, lb_ref, ng_ref, cw_ref, pw_ref, ps_ref,
                         w_o_ref, g_ref, b_ref,
                         y_ref, knew_ref, vnew_ref, snew_ref, cnew_ref, pnew_ref,
                         proj_scr, kext_scr, vext_scr, st_scr, u_scr, p_scr, bias_scr, mix_scr, hg_scr,
                         hq_scr, hk_scr, hv_scr, ghl_scr, cum_scr, qp_scr, kp_scr, qs_scr, ks_scr, dec_scr, inc_scr,
                         stb_scr, a_scr, yo_scr, st0_scr):
    tb = x_ref.shape[0]
    hb = tb // SUB_BLOCKS
    n_qb = hb // WINDOW
    n_ch = hb // HG_CHUNK
    bi = pl.program_id(0)
    ti = pl.program_id(1)
    last = ti == pl.num_programs(1) - 1

    @pl.when((bi == 0) & (ti == 0))
    def _():
        bias_scr[...] = _swa_bias_table()

    @pl.when(ti == 0)
    def _():
        kext_scr[0:WINDOW, :] = jnp.zeros((WINDOW, LANES), BF16)
        vext_scr[0:WINDOW, :] = jnp.zeros((WINDOW, LANES), BF16)
        st_scr[...] = jnp.zeros(st_scr.shape, F32)
        u_scr[0:SUBLANES, :] = jnp.zeros((SUBLANES, W_GRP), F32)
        p_scr[0:16, :] = jnp.zeros((16, W_GRP), F32)

    embed = l == 0

    def normalize(r):
        rows = slice(r * hb, (r + 1) * hb)
        y_ref[rows, :] = _layer_norm(x_ref[rows, :], eg_ref[...], eb_ref[...])

    def stream(rows):
        return y_ref[rows, :] if embed else x_ref[rows, :]

    def project(r, c0, c1):
        rows = slice(r * hb, (r + 1) * hb)
        proj_scr[rows, c0:c1] = _dot(stream(rows).astype(BF16), w_in_ref[:, c0:c1])

    def finish(r, c0, c1):
        rows = slice(r * hb, (r + 1) * hb)
        yo_scr[rows, c0:c1] = _dot(mix_scr[rows, :], w_o_ref[:, c0:c1])

    def finish_ln(r, redo=False):
        rows = slice(r * hb, (r + 1) * hb)
        if embed and redo:
            base = _layer_norm(x_ref[rows, :], eg_ref[...], eb_ref[...])
        else:
            base = stream(rows)
        y_ref[rows, :] = _layer_norm(ALPHA * base + yo_scr[rows, :], g_ref[...], b_ref[...])

    proj_cols = ((0, C_BQ), (C_BQ, C_BI), (C_BI, C_CB), (C_CB, C_CH), (C_CH, D_IN))
    out_cols = ((0, D_MODEL // 2), (D_MODEL // 2, D_MODEL))
    if embed:
        normalize(0)
    for cols in proj_cols:
        project(0, *cols)
    queue = _MatmulQueue()

    lane = lax.broadcasted_iota(jnp.int32, (WINDOW, LANES), 1)
    lo = lane < HEAD_DIM
    key_row = lax.broadcasted_iota(jnp.int32, (2 * WINDOW, 4 * WINDOW), 0)
    sink_row = _sink_select(sink_ref, l, lax.broadcasted_iota(jnp.int32, (1, 4 * WINDOW), 1) >> 7)
    lb = lb_ref[...]
    lbf = jnp.maximum(lb, LB_FLOOR)
    one_m_lb = 1.0 - lb
    ng = ng_ref[...]
    r256 = lax.broadcasted_iota(jnp.int32, (W_GRP, W_GRP), 0)
    c256 = lax.broadcasted_iota(jnp.int32, (W_GRP, W_GRP), 1)
    same_head = (r256 >> 6) == (c256 >> 6)
    head_ones = jnp.where(same_head, 1.0, 0.0).astype(BF16)
    same_head_b = head_ones > 0
    zero_b = jnp.zeros((W_GRP, W_GRP), BF16)
    rc = lax.broadcasted_iota(jnp.int32, (HG_CHUNK, W_GRP), 0)
    cc = lax.broadcasted_iota(jnp.int32, (HG_CHUNK, W_GRP), 1)
    causal = (cc & (HG_CHUNK - 1)) <= rc
    r64 = lax.broadcasted_iota(jnp.int32, (HG_CHUNK, HG_CHUNK), 0)
    c64 = lax.broadcasted_iota(jnp.int32, (HG_CHUNK, HG_CHUNK), 1)
    tril = jnp.where(c64 <= r64, 1.0, 0.0).astype(BF16)
    cw = cw_ref[...]
    rp = lax.broadcasted_iota(jnp.int32, (hb, W_GRP), 0)
    grp = lax.broadcasted_iota(jnp.int32, (hb, W_GRP), 1) >> 6
    width = jnp.left_shift(2, grp)
    chunk = lambda c: slice(c * HG_CHUNK, (c + 1) * HG_CHUNK)

    def hgrn_output(r):
        rows = slice(r * hb, (r + 1) * hb)
        o = hg_scr[...]
        ms = _dot((o * o).astype(BF16), head_ones) * (1.0 / HEAD_DIM)
        o = o * lax.rsqrt(ms + RMS_EPS) * ng
        mix_scr[rows, W_GRP:2 * W_GRP] = (o * _silu_tanh(proj_scr[rows, C_BG:C_BG + W_GRP])).astype(BF16)

    st0_scr[...] = st_scr[...]
    span = jnp.zeros((1, W_GRP), F32)
    for r in range(SUB_BLOCKS):
        r0 = r * hb
        sub = slice(r0, r0 + hb)
        if r >= 1:
            for cols in out_cols:
                queue.add(finish, r - 1, *cols)
        if r + 1 < SUB_BLOCKS:
            if embed:
                queue.add(normalize, r + 1)
            for cols in proj_cols:
                queue.add(project, r + 1, *cols)

        kext_scr[WINDOW:WINDOW + hb, :] = proj_scr[sub, C_AK:C_AK + LANES].astype(BF16)
        vext_scr[WINDOW:WINDOW + hb, :] = proj_scr[sub, C_AV:C_AV + LANES].astype(BF16)
        for j in range(n_qb):
            rows = slice(r0 + j * WINDOW, r0 + (j + 1) * WINDOW)
            q0 = proj_scr[rows, 0:LANES] * (HEAD_DIM ** -0.5)
            q1 = proj_scr[rows, LANES:2 * LANES] * (HEAD_DIM ** -0.5)
            q0r = pltpu.roll(q0, HEAD_DIM, axis=1)
            q1r = pltpu.roll(q1, HEAD_DIM, axis=1)
            zero = jnp.zeros_like(q0)
            q4 = jnp.concatenate([jnp.where(lo, q0, zero), jnp.where(lo, q0r, zero),
                                  jnp.where(lo, zero, q1r), jnp.where(lo, zero, q1)], axis=0).astype(BF16)
            kj = kext_scr[j * WINDOW:(j + 2) * WINDOW, :]
            vj = vext_scr[j * WINDOW:(j + 2) * WINDOW, :]
            s = _nt_dot(kj, q4) + bias_scr[...]
            if r == 0 and j == 0:
                s = jnp.where((ti == 0) & (key_row < WINDOW), MASK_VALUE, s)
            m = jnp.maximum(jnp.max(s, axis=0, keepdims=True), sink_row)
            p = jnp.exp(s - m)
            den = jnp.sum(p, axis=0, keepdims=True) + jnp.exp(sink_row - m)
            p = (p * (1.0 / den)).astype(BF16)
            o_all = _tn_dot(p, vj)
            o = [o_all[h * WINDOW:(h + 1) * WINDOW, :] for h in range(4)]
            mix_scr[rows, 0:LANES] = jnp.where(lo, o[0], pltpu.roll(o[1], HEAD_DIM, axis=1)).astype(BF16)
            mix_scr[rows, LANES:2 * LANES] = jnp.where(lo, pltpu.roll(o[2], HEAD_DIM, axis=1), o[3]).astype(BF16)
            queue.issue()
        kext_scr[0:WINDOW, :] = kext_scr[hb:hb + WINDOW, :]
        vext_scr[0:WINDOW, :] = vext_scr[hb:hb + WINDOW, :]

        hq_scr[...] = _silu_tanh(proj_scr[sub, C_BQ:C_BQ + W_GRP])
        queue.issue()
        sig_pos, sig_neg = _sigmoid_pair(proj_scr[sub, C_BF:C_BF + W_GRP])
        g = jnp.log(sig_pos + lbf * sig_neg)
        hk_scr[...] = one_m_lb * sig_neg
        g_hi = g.astype(BF16)
        ghl_scr[:, 0:W_GRP] = g_hi
        ghl_scr[:, W_GRP:2 * W_GRP] = (g - g_hi.astype(F32)).astype(BF16)
        hv_scr[...] = proj_scr[sub, C_BI:C_BI + W_GRP].astype(BF16)
        for c in range(n_ch):
            cum2 = _dot(tril, ghl_scr[chunk(c), :])
            cum_scr[chunk(c), :] = cum2[:, 0:W_GRP] + cum2[:, W_GRP:2 * W_GRP]
        for c in range(n_ch):
            cum = cum_scr[chunk(c), :]
            ref = cum[HG_MID:HG_MID + 1, :]
            tot = cum[HG_CHUNK - 1:HG_CHUNK, :]
            span = jnp.maximum(span, jnp.maximum(cum[0:1, :] - ref, ref - tot))
            qp = hq_scr[chunk(c), :] * jnp.exp(cum - ref)
            kp = hk_scr[chunk(c), :] * jnp.exp(ref - cum)
            qp_scr[chunk(c), :] = qp.astype(BF16)
            kp_scr[chunk(c), :] = kp.astype(BF16)
            qs_scr[chunk(c), :] = (qp * jnp.exp(ref)).astype(BF16)
            ks_scr[chunk(c), :] = (kp * jnp.exp(tot - ref)).astype(BF16)
            dec_scr[c] = jnp.broadcast_to(jnp.exp(tot), (SUBLANES, W_GRP))
        queue.issue()
        for c in range(n_ch):
            bk = jnp.where(same_head_b, jnp.concatenate([kp_scr[chunk(c), :]] * N_HG, axis=0), zero_b)
            a = jnp.where(causal, _nt_dot(qp_scr[chunk(c), :], bk), 0.0)
            a_scr[chunk(c), :] = a.astype(BF16)
        for c in range(n_ch):
            inc_scr[c] = jnp.where(same_head, _tn_dot(hv_scr[chunk(c), :], ks_scr[chunk(c), :]), 0.0)
        for c in range(n_ch):
            bv = jnp.where(same_head_b, jnp.concatenate([hv_scr[chunk(c), :]] * N_HG, axis=0), zero_b)
            hg_scr[chunk(c), :] = _dot(a_scr[chunk(c), :], bv)
        st = st_scr[...]
        for c in range(n_ch):
            stb_scr[c] = st.astype(BF16)
            st = st * dec_scr[c, 0:1, :] + inc_scr[c]
        st_scr[...] = st
        queue.issue()
        for c in range(n_ch):
            hg_scr[chunk(c), :] = hg_scr[chunk(c), :] + _nt_dot(qs_scr[chunk(c), :], stb_scr[c])

        hgrn_output(r)
        queue.issue()

        u_scr[SUBLANES:SUBLANES + hb, :] = proj_scr[sub, C_CC:C_CC + W_GRP] * proj_scr[sub, C_CH:C_CH + W_GRP]
        yc = (u_scr[SUBLANES - 2:SUBLANES - 2 + hb, :] * cw[0:1, :]
              + u_scr[SUBLANES - 1:SUBLANES - 1 + hb, :] * cw[1:2, :]
              + u_scr[SUBLANES:SUBLANES + hb, :] * cw[2:3, :])
        mix_scr[sub, 2 * W_GRP:3 * W_GRP] = (proj_scr[sub, C_CB:C_CB + W_GRP] * yc).astype(BF16)
        u_scr[0:SUBLANES, :] = u_scr[hb:hb + SUBLANES, :]
        queue.issue()

        dv = proj_scr[sub, C_DV:C_DV + W_GRP]
        p_scr[16:16 + hb, :] = dv
        ext = p_scr[...]
        s2 = ext + pltpu.roll(ext, 1, axis=0)
        s4 = s2 + pltpu.roll(s2, 2, axis=0)
        s8 = s4 + pltpu.roll(s4, 4, axis=0)
        s16 = s8 + pltpu.roll(s8, 8, axis=0)
        win = jnp.where(grp == 0, s2[16:], jnp.where(grp == 1, s4[16:], jnp.where(grp == 2, s8[16:], s16[16:])))
        cnt = jnp.minimum(ti * tb + r0 + rp + 1, width).astype(F32)
        pooled = win / cnt - dv
        yd = _dot(pooled.astype(BF16), pw_ref[...]) * ps_ref[...]
        mix_scr[sub, 3 * W_GRP:4 * W_GRP] = yd.astype(BF16)
        p_scr[0:16, :] = p_scr[hb:hb + 16, :]
        queue.flush()
        if r >= 1:
            finish_ln(r - 1)

    for cols in out_cols:
        finish(SUB_BLOCKS - 1, *cols)
    finish_ln(SUB_BLOCKS - 1)

    @pl.when(jnp.max(span) > HG_MAX_EXPONENT)
    def _():
        st_scr[...] = st0_scr[...]
        sub8 = lax.broadcasted_iota(jnp.int32, (SUBLANES, W_GRP), 0)
        for r in range(SUB_BLOCKS):
            def tile_body(i, carry, r=r):
                rows8 = pl.ds(pl.multiple_of(i * SUBLANES, SUBLANES), SUBLANES)
                prow = pl.ds(pl.multiple_of(r * hb + i * SUBLANES, SUBLANES), SUBLANES)
                sp8, sn8 = _sigmoid_pair(proj_scr[prow, C_BF:C_BF + W_GRP])
                f8 = sp8 + lbf * sn8
                k8 = (one_m_lb * sn8).astype(BF16)
                v8 = proj_scr[prow, C_BI:C_BI + W_GRP]
                q8 = _silu_tanh(proj_scr[prow, C_BQ:C_BQ + W_GRP]).astype(BF16)
                o8 = jnp.zeros((SUBLANES, W_GRP), F32)
                for j in range(SUBLANES):
                    vj = jnp.where(sub8 == j, v8, 0.0).astype(BF16)
                    stj = st_scr[...] * f8[j:j + 1, :] + jnp.where(same_head, _tn_dot(vj, k8), 0.0)
                    st_scr[...] = stj
                    o8 = jnp.where(sub8 == j, _nt_dot(q8, stj.astype(BF16)), o8)
                hg_scr[rows8, :] = o8
                return carry

            lax.fori_loop(0, hb // SUBLANES, tile_body, 0)
            hgrn_output(r)
            for cols in out_cols:
                finish(r, *cols)
            finish_ln(r, redo=True)

    @pl.when(last)
    def _():
        knew_ref[...] = proj_scr[tb - WINDOW:tb, C_AK:C_AK + LANES]
        vnew_ref[...] = proj_scr[tb - WINDOW:tb, C_AV:C_AV + LANES]
        s_t = st_scr[...].T
        for h in range(N_HG):
            snew_ref[h * HEAD_DIM:(h + 1) * HEAD_DIM, :] = (
                s_t[h * HEAD_DIM:(h + 1) * HEAD_DIM, h * HEAD_DIM:(h + 1) * HEAD_DIM])
        cnew_ref[...] = u_scr[SUBLANES - 2:SUBLANES, :]
        pnew_ref[...] = p_scr[1:16, :]


def _prompt_mixer_call(l, x, w):
    bsz, t, _ = x.shape
    tb = min(2 * TOKEN_BLOCK, t)
    hb = tb // SUB_BLOCKS
    n_ch = hb // HG_CHUNK
    row = lambda i, j: (i, 0, 0)
    out_shape = (
        jax.ShapeDtypeStruct((bsz, t, D_MODEL), F32),
        jax.ShapeDtypeStruct((bsz, WINDOW, LANES), F32),
        jax.ShapeDtypeStruct((bsz, WINDOW, LANES), F32),
        jax.ShapeDtypeStruct((bsz, W_GRP, HEAD_DIM), F32),
        jax.ShapeDtypeStruct((bsz, CONV_W - 1, W_GRP), F32),
        jax.ShapeDtypeStruct((bsz, POOL_BUF, W_GRP), F32),
    )
    f32 = lambda *shape: pltpu.VMEM(shape, F32)
    bf16 = lambda *shape: pltpu.VMEM(shape, BF16)
    return pl.pallas_call(
        functools.partial(_prompt_mixer_kernel, l),
        name="mixer_prompt",
        grid=(bsz, t // tb),
        in_specs=[pl.BlockSpec((None, tb, D_MODEL), lambda i, j: (i, j, 0)),
                  _full_spec((1, D_MODEL)), _full_spec((1, D_MODEL)),
                  _layer_spec((D_MODEL, D_IN), l, single=True),
                  pl.BlockSpec(memory_space=pltpu.SMEM),
                  _layer_spec((1, W_GRP), l), _layer_spec((1, W_GRP), l), _layer_spec((CONV_W, W_GRP), l),
                  _layer_spec((W_GRP, W_GRP), l), _layer_spec((1, W_GRP), l),
                  _layer_spec((D_MODEL, D_MODEL), l, single=True),
                  _layer_spec((1, D_MODEL), l), _layer_spec((1, D_MODEL), l)],
        out_specs=(pl.BlockSpec((None, tb, D_MODEL), lambda i, j: (i, j, 0)),
                   pl.BlockSpec((None, WINDOW, LANES), row),
                   pl.BlockSpec((None, WINDOW, LANES), row),
                   pl.BlockSpec((None, W_GRP, HEAD_DIM), row),
                   pl.BlockSpec((None, CONV_W - 1, W_GRP), row),
                   pl.BlockSpec((None, POOL_BUF, W_GRP), row)),
        out_shape=out_shape,
        scratch_shapes=[
            f32(tb, D_IN),
            bf16(WINDOW + hb, LANES),
            bf16(WINDOW + hb, LANES),
            f32(W_GRP, W_GRP),
            f32(SUBLANES + hb, W_GRP),
            f32(16 + hb, W_GRP),
            f32(2 * WINDOW, 4 * WINDOW),
            bf16(tb, D_MODEL),
            f32(hb, W_GRP),
            f32(hb, W_GRP), f32(hb, W_GRP),
            bf16(hb, W_GRP),
            bf16(hb, 2 * W_GRP),
            f32(hb, W_GRP),
            bf16(hb, W_GRP), bf16(hb, W_GRP),
            bf16(hb, W_GRP), bf16(hb, W_GRP),
            f32(n_ch, SUBLANES, W_GRP),
            f32(n_ch, W_GRP, W_GRP),
            bf16(n_ch, W_GRP, W_GRP),
            bf16(hb, W_GRP),
            f32(tb, D_MODEL),
            f32(W_GRP, W_GRP),
        ],
        compiler_params=_compiler_params(("arbitrary", "arbitrary")),
    )(x, w["emb_g"], w["emb_b"], w["w_in"], w["sink"], w["lb"], w["ng"], w["cw"], w["pw"], w["ps"], w["w_o"],
      w["ln1_g"], w["ln1_b"])


def _gather_block(dst_scr, src_scr, n_tiles, gi, sb, nb):
    for t in range(DEC_SEQ):
        start = pl.multiple_of(t * nb + gi * sb, sb)
        for j in range(n_tiles):
            dst_scr[j, t * sb:(t + 1) * sb, :] = src_scr[j, pl.ds(start, sb), :]


def _scatter_block(dst_scr, src_scr, n_tiles, gi, sb, nb):
    for t in range(DEC_SEQ):
        start = pl.multiple_of(t * nb + gi * sb, sb)
        for j in range(n_tiles):
            dst_scr[pl.ds(start, sb), j * LANES:(j + 1) * LANES] = src_scr[j, t * sb:(t + 1) * sb, :]


def _sample_mixer_kernel(l, xt_ref, ck_ref, cv_ref, s_ref, cs_ref, pb_ref,
                         w_in_ref, sink_ref, lbc_ref, ngc_ref, cw_ref, pw_ref, ps_ref,
                         w_o_ref, g_ref, b_ref, *rest):
    (y_ref, knew_ref, vnew_ref, snew_ref, cnew_ref, pnew_ref,
     proj_scr, a_scr, r_scr, q_scr, f_scr, k_scr, v_scr, gate_scr, o_scr, mix_scr,
     ab_scr, oa_scr, sc_scr, sn_scr, pc_scr, pn_scr) = rest[-22:]
    gi = pl.program_id(0)
    sb = ck_ref.shape[0]
    nb = xt_ref.shape[0] // DEC_SEQ
    half = sb // 2

    @pl.when(gi == 0)
    def _():
        xt = xt_ref[...].astype(BF16)
        pa = _dot(xt, w_in_ref[:, 0:C_BQ])
        for j in range(4):
            a_scr[j] = pa[:, j * LANES:(j + 1) * LANES]
        r_scr[...] = jnp.zeros(r_scr.shape, F32)
        hgt =_dot(xt, w_in_ref[:, C_BQ:C_CB]).T
        lbc = lbc_ref[...]
        q_scr[...] = _silu_tanh(hgt[0:W_GRP, :])
        sig_pos, sig_neg = _sigmoid_pair(hgt[W_GRP:2 * W_GRP, :])
        f_scr[...] = sig_pos + jnp.maximum(lbc, LB_FLOOR) * sig_neg
        k_scr[...] = (1.0 - lbc) * sig_neg
        v_scr[...] = hgt[2 * W_GRP:3 * W_GRP, :]
        gate_scr[...] = _silu_tanh(hgt[3 * W_GRP:4 * W_GRP, :])
        o_scr[...] = jnp.zeros(o_scr.shape, F32)
        proj_scr[...] = _dot(xt, w_in_ref[:, C_CB:D_IN])
        cw = cw_ref[...]
        ps = ps_ref[...]
        cp = lax.broadcasted_iota(jnp.int32, (nb, W_GRP), 1) >> 6
        width = jnp.left_shift(2, cp)
        u = [cs_ref[0], cs_ref[1]]
        ext = [pb_ref[i] for i in range(POOL_BUF)]
        for t in range(DEC_SEQ):
            rows = slice(t * nb, (t + 1) * nb)
            u.append(proj_scr[rows, W_GRP:2 * W_GRP] * proj_scr[rows, 2 * W_GRP:3 * W_GRP])
            ext.append(proj_scr[rows, 3 * W_GRP:4 * W_GRP])
        cnew_ref[0] = u[DEC_SEQ]
        cnew_ref[1] = u[DEC_SEQ + 1]
        for i in range(POOL_BUF):
            pnew_ref[i] = ext[DEC_SEQ + i]
        for t in range(DEC_SEQ):
            rows = slice(t * nb, (t + 1) * nb)
            yc = u[t] * cw[0:1, :] + u[t + 1] * cw[1:2, :] + u[t + 2] * cw[2:3, :]
            mix_scr[rows, 2 * W_GRP:3 * W_GRP] = proj_scr[rows, 0:W_GRP] * yc
            top = POOL_BUF + t
            acc = ext[top]
            sums = {}
            for jj in range(1, 16):
                acc = acc + ext[top - jj]
                if jj + 1 in POOL_WINDOWS:
                    sums[jj + 1] = acc
            win = jnp.where(cp == 0, sums[2], jnp.where(cp == 1, sums[4], jnp.where(cp == 2, sums[8], sums[16])))
            cnt = jnp.minimum(PAST_LEN + t + 1, width).astype(F32)
            pooled = win / cnt - ext[top]
            mix_scr[rows, 3 * W_GRP:4 * W_GRP] = _dot(pooled.astype(BF16), pw_ref[...]) * ps

    _gather_block(ab_scr, a_scr, 4, gi, sb, nb)
    lane = lax.broadcasted_iota(jnp.int32, (PAIR_ROWS, LANES), 1)
    lo = lane < HEAD_DIM
    n_rows = 4 * PAIR_ROWS
    r_c = lax.broadcasted_iota(jnp.int32, (n_rows, 2 * WINDOW), 0)
    c_c = lax.broadcasted_iota(jnp.int32, (n_rows, 2 * WINDOW), 1)
    rel_c = ((r_c >> 1) & 3) + WINDOW - (c_c & (WINDOW - 1))
    ok_c = ((r_c & 1) == (c_c >> 7)) & (rel_c <= WINDOW)
    bias_c = jnp.where(ok_c, -jnp.exp2(-2.0 * ((r_c >> 3).astype(F32) + 1.0)) * rel_c.astype(F32), MASK_VALUE)
    r_n = lax.broadcasted_iota(jnp.int32, (n_rows, PAIR_ROWS), 0)
    c_n = lax.broadcasted_iota(jnp.int32, (n_rows, PAIR_ROWS), 1)
    rel_n = ((r_n >> 1) & 3) - (c_n >> 1)
    ok_n = ((r_n & 1) == (c_n & 1)) & (rel_n >= 0)
    bias_n = jnp.where(ok_n, -jnp.exp2(-2.0 * ((r_n >> 3).astype(F32) + 1.0)) * rel_n.astype(F32), MASK_VALUE)
    pair_rows = lambda p: pl.ds(p, PAIR_ROWS, stride=half)
    for p in range(half):
        q0 = ab_scr[0, pair_rows(p), :] * (HEAD_DIM ** -0.5)
        q1 = ab_scr[1, pair_rows(p), :] * (HEAD_DIM ** -0.5)
        q0r = pltpu.roll(q0, HEAD_DIM, axis=1)
        q1r = pltpu.roll(q1, HEAD_DIM, axis=1)
        zero = jnp.zeros_like(q0)
        q4 = jnp.concatenate([jnp.where(lo, q0, zero), jnp.where(lo, q0r, zero),
                              jnp.where(lo, zero, q1r), jnp.where(lo, zero, q1)], axis=0).astype(BF16)
        kt2 = jnp.concatenate([ck_ref[p], ck_ref[p + half]], axis=1).astype(BF16)
        kn8 = ab_scr[2, pair_rows(p), :].astype(BF16)
        sc_scr[n_rows * p:n_rows * (p + 1), :] = _dot(q4, kt2) + bias_c
        sn_scr[n_rows * p:n_rows * (p + 1), :] = _nt_dot(q4, kn8) + bias_n
    sink_col = _sink_select(sink_ref, l, (lax.broadcasted_iota(jnp.int32, (n_rows * half, 1), 0) >> 3) & 3)
    s_c = sc_scr[...]
    s_n = sn_scr[...]
    m = jnp.maximum(jnp.maximum(jnp.max(s_c, axis=-1, keepdims=True), jnp.max(s_n, axis=-1, keepdims=True)),
                    sink_col)
    p_c = jnp.exp(s_c - m)
    p_n = jnp.exp(s_n - m)
    inv = 1.0 / (jnp.sum(p_c, axis=-1, keepdims=True) + jnp.sum(p_n, axis=-1, keepdims=True)
                 + jnp.exp(sink_col - m))
    pc_scr[...] = (p_c * inv).astype(BF16)
    pn_scr[...] = (p_n * inv).astype(BF16)
    for p in range(half):
        vt2 = jnp.concatenate([cv_ref[p], cv_ref[p + half]], axis=1).astype(BF16)
        vn8 = ab_scr[3, pair_rows(p), :].astype(BF16)
        o_all = (_nt_dot(pc_scr[n_rows * p:n_rows * (p + 1), :], vt2)
                 + _dot(pn_scr[n_rows * p:n_rows * (p + 1), :], vn8))
        o = [o_all[h * PAIR_ROWS:(h + 1) * PAIR_ROWS, :] for h in range(4)]
        oa_scr[0, pair_rows(p), :] = jnp.where(lo, o[0], pltpu.roll(o[1], HEAD_DIM, axis=1))
        oa_scr[1, pair_rows(p), :] = jnp.where(lo, pltpu.roll(o[2], HEAD_DIM, axis=1), o[3])
    _scatter_block(mix_scr, oa_scr, 2, gi, sb, nb)
    lane_w = lax.broadcasted_iota(jnp.int32, (LANES, LANES), 1)
    for tile, old_ref, out_ref in ((2, ck_ref, knew_ref), (3, cv_ref, vnew_ref)):
        for t in range(DEC_SEQ):
            r_scr[pl.ds(t, sb, stride=SUBLANES), :] = ab_scr[tile, t * sb:(t + 1) * sb, :]
        new_t = r_scr[...].T
        for b in range(sb):
            new_cols = pltpu.roll(new_t, (WINDOW - DEC_SEQ - SUBLANES * b) % LANES, axis=1)
            old_cols = pltpu.roll(old_ref[b], WINDOW - DEC_SEQ, axis=1)
            out_ref[b] = jnp.where(lane_w >= WINDOW - DEC_SEQ, new_cols, old_cols)

    k_per_step = s_ref.shape[0] // HEAD_DIM
    steps_per_head = HEAD_DIM // k_per_step
    head = gi // steps_per_head
    k_base = head * HEAD_DIM + (gi % steps_per_head) * k_per_step
    v_rows = pl.ds(pl.multiple_of(head * HEAD_DIM, HEAD_DIM), HEAD_DIM)

    def k_body(k8, accs):
        rows8 = pl.ds(pl.multiple_of(k_base + k8 * SUBLANES, SUBLANES), SUBLANES)
        accs = list(accs)
        for j in range(SUBLANES):
            st_rows = pl.ds(pl.multiple_of((k8 * SUBLANES + j) * HEAD_DIM, HEAD_DIM), HEAD_DIM)
            st = s_ref[st_rows, :]
            for t in range(DEC_SEQ):
                cols = slice(t * nb, (t + 1) * nb)
                f8, k8v, q8 = f_scr[rows8, cols], k_scr[rows8, cols], q_scr[rows8, cols]
                st = f8[j:j + 1, :] * st + k8v[j:j + 1, :] * v_scr[v_rows, cols]
                accs[t] = accs[t] + q8[j:j + 1, :] * st
            snew_ref[st_rows, :] = st
        return tuple(accs)

    zero_acc = jnp.zeros((HEAD_DIM, nb), F32)
    accs = lax.fori_loop(0, k_per_step // SUBLANES, k_body, (zero_acc,) * DEC_SEQ)
    for t in range(DEC_SEQ):
        cols = slice(t * nb, (t + 1) * nb)
        o_scr[v_rows, cols] = o_scr[v_rows, cols] + accs[t]

    @pl.when(gi == pl.num_programs(0) - 1)
    def _():
        o = o_scr[...]
        parts = []
        for h in range(N_HG):
            oh = o[h * HEAD_DIM:(h + 1) * HEAD_DIM, :]
            parts.append(oh * lax.rsqrt(jnp.mean(oh * oh, axis=0, keepdims=True) + RMS_EPS))
        ob = jnp.concatenate(parts, axis=0) * ngc_ref[...] * gate_scr[...]
        y = (_dot(mix_scr[:, 0:W_GRP].astype(BF16), w_o_ref[0:W_GRP, :])
             + _tn_dot(ob.astype(BF16), w_o_ref[W_GRP:2 * W_GRP, :])
             + _dot(mix_scr[:, 2 * W_GRP:4 * W_GRP].astype(BF16), w_o_ref[2 * W_GRP:4 * W_GRP, :]))
        y_ref[...] = _layer_norm(ALPHA * xt_ref[...] + y, g_ref[...], b_ref[...])


def _sample_mixer_call(l, xt, ck, cv, s, cs_tm, pb_tm, w, prev):
    m = xt.shape[0]
    nb = m // DEC_SEQ
    sb = SEQ_BLOCK
    n_steps = nb // sb
    n_sc = 4 * PAIR_ROWS * (sb // 2)
    s_rows = HG_ROWS // n_steps
    single = pl.Buffered(1)
    const2 = lambda i: (0, 0)
    out_shape = (
        jax.ShapeDtypeStruct((m, D_MODEL), F32),
        jax.ShapeDtypeStruct((DEPTH, nb, LANES, WINDOW), F32),
        jax.ShapeDtypeStruct((DEPTH, nb, LANES, WINDOW), F32),
        jax.ShapeDtypeStruct((DEPTH, HG_ROWS, nb), F32),
        jax.ShapeDtypeStruct((CONV_W - 1, nb, W_GRP), F32),
        jax.ShapeDtypeStruct((POOL_BUF, nb, W_GRP), F32),
    )
    seq_blk = lambda i: (l, i, 0, 0)
    in_specs = [pl.BlockSpec((m, D_MODEL), const2, pipeline_mode=single),
                pl.BlockSpec((None, sb, LANES, WINDOW), seq_blk),
                pl.BlockSpec((None, sb, LANES, WINDOW), seq_blk),
                pl.BlockSpec((None, s_rows, nb), lambda i: (l, i, 0)),
                _layer_spec((CONV_W - 1, nb, W_GRP), l, single=True),
                _layer_spec((POOL_BUF, nb, W_GRP), l, single=True),
                _layer_spec((D_MODEL, D_IN), l, single=True),
                pl.BlockSpec(memory_space=pltpu.SMEM),
                _layer_spec((W_GRP, 1), l), _layer_spec((W_GRP, 1), l),
                _layer_spec((CONV_W, W_GRP), l), _layer_spec((W_GRP, W_GRP), l), _layer_spec((1, W_GRP), l),
                _layer_spec((D_MODEL, D_MODEL), l, single=True),
                _layer_spec((1, D_MODEL), l), _layer_spec((1, D_MODEL), l)]
    args = [xt, ck, cv, s, cs_tm, pb_tm, w["w_in"], w["sink"], w["lb_col"], w["ng_col"],
            w["cw"], w["pw"], w["ps"], w["w_o"], w["ln1_g"], w["ln1_b"]]
    aliases = {}
    if prev is not None:
        for j, buf in enumerate(prev):
            aliases[len(args)] = 1 + j
            args.append(buf)
            in_specs.append(pl.BlockSpec(memory_space=pl.ANY))
    f32 = lambda *shape: pltpu.VMEM(shape, F32)
    return pl.pallas_call(
        functools.partial(_sample_mixer_kernel, l),
        name="mixer_sample",
        grid=(n_steps,),
        in_specs=in_specs,
        out_specs=(pl.BlockSpec((m, D_MODEL), const2),
                   pl.BlockSpec((None, sb, LANES, WINDOW), seq_blk),
                   pl.BlockSpec((None, sb, LANES, WINDOW), seq_blk),
                   pl.BlockSpec((None, s_rows, nb), lambda i: (l, i, 0)),
                   pl.BlockSpec((CONV_W - 1, nb, W_GRP), lambda i: (0, 0, 0)),
                   pl.BlockSpec((POOL_BUF, nb, W_GRP), lambda i: (0, 0, 0))),
        out_shape=out_shape,
        input_output_aliases=aliases,
        scratch_shapes=[f32(m, 4 * W_GRP),
                        f32(4, m, LANES),
                        f32(SUBLANES * sb, LANES),
                        f32(W_GRP, m), f32(W_GRP, m), f32(W_GRP, m), f32(W_GRP, m), f32(W_GRP, m),
                        f32(W_GRP, m),
                        f32(m, D_MODEL),
                        f32(4, sb * DEC_SEQ, LANES), f32(2, sb * DEC_SEQ, LANES),
                        f32(n_sc, 2 * WINDOW), f32(n_sc, PAIR_ROWS),
                        pltpu.VMEM((n_sc, 2 * WINDOW), BF16), pltpu.VMEM((n_sc, PAIR_ROWS), BF16)],
        compiler_params=_compiler_params(("arbitrary",)),
    )(*args)


def _xattn_sample_kernel(xt_ref, mk_ref, mv_ref, wq_ref, wo_ref, g_ref, b_ref, o_ref,
                         q_scr, qb_scr, ob_scr, att_scr, s_scr, p_scr):
    gi = pl.program_id(0)
    sb = mk_ref.shape[0]
    nb = xt_ref.shape[0] // DEC_SEQ
    half = sb // 2
    n_rows = N_XH * PAIR_ROWS

    @pl.when(gi == 0)
    def _():
        q = _dot(xt_ref[...].astype(BF16), wq_ref[...]) * (HEAD_DIM ** -0.5)
        q_scr[0] = q[:, 0:LANES]
        q_scr[1] = q[:, LANES:2 * LANES]

    _gather_block(qb_scr, q_scr, 2, gi, sb, nb)
    lane_head = lax.broadcasted_iota(jnp.int32, (PAIR_ROWS, D_X), 1) >> 6
    r_s = lax.broadcasted_iota(jnp.int32, (n_rows, 2 * N_MEM), 0)
    c_s = lax.broadcasted_iota(jnp.int32, (n_rows, 2 * N_MEM), 1)
    own = (r_s & 1) == (c_s >> 8)
    pair_rows = lambda p: pl.ds(p, PAIR_ROWS, stride=half)
    for p in range(half):
        q8 = jnp.concatenate([qb_scr[0, pair_rows(p), :], qb_scr[1, pair_rows(p), :]], axis=1)
        zero = jnp.zeros_like(q8)
        q4 = jnp.concatenate([jnp.where(lane_head == h, q8, zero) for h in range(N_XH)], axis=0).astype(BF16)
        kt2 = jnp.concatenate([mk_ref[p], mk_ref[p + half]], axis=1).astype(BF16)
        s_scr[n_rows * p:n_rows * (p + 1), :] = jnp.where(own, _dot(q4, kt2), MASK_VALUE)
    s_all = s_scr[...]
    m = jnp.max(s_all, axis=-1, keepdims=True)
    pr = jnp.exp(s_all - m)
    p_scr[...] = (pr * (1.0 / jnp.sum(pr, axis=-1, keepdims=True))).astype(BF16)
    for p in range(half):
        vt2 = jnp.concatenate([mv_ref[p], mv_ref[p + half]], axis=1).astype(BF16)
        o_all = _nt_dot(p_scr[n_rows * p:n_rows * (p + 1), :], vt2)
        o8 = jnp.zeros((PAIR_ROWS, D_X), F32)
        for h in range(N_XH):
            o8 = jnp.where(lane_head == h, o_all[h * PAIR_ROWS:(h + 1) * PAIR_ROWS, :], o8)
        ob_scr[0, pair_rows(p), :] = o8[:, 0:LANES]
        ob_scr[1, pair_rows(p), :] = o8[:, LANES:2 * LANES]
    _scatter_block(att_scr, ob_scr, 2, gi, sb, nb)

    @pl.when(gi == pl.num_programs(0) - 1)
    def _():
        y = _dot(att_scr[...].astype(BF16), wo_ref[...])
        o_ref[...] = _layer_norm(ALPHA * xt_ref[...] + y, g_ref[...], b_ref[...])


def _xattn_sample_call(l, xt, mk, mv, w):
    m = xt.shape[0]
    nb = m // DEC_SEQ
    sb = SEQ_BLOCK
    n_sc = N_XH * PAIR_ROWS * (sb // 2)
    const2 = lambda i: (0, 0)
    return pl.pallas_call(
        _xattn_sample_kernel,
        name="xattn_sample",
        grid=(nb // sb,),
        in_specs=[pl.BlockSpec((m, D_MODEL), const2, pipeline_mode=pl.Buffered(1)),
                  pl.BlockSpec((None, sb, D_X, N_MEM), lambda i: (l, i, 0, 0)),
                  pl.BlockSpec((None, sb, D_X, N_MEM), lambda i: (l, i, 0, 0)),
                  _layer_spec((D_MODEL, D_X), l), _layer_spec((D_X, D_MODEL), l),
                  _layer_spec((1, D_MODEL), l), _layer_spec((1, D_MODEL), l)],
        out_specs=pl.BlockSpec((m, D_MODEL), const2),
        out_shape=jax.ShapeDtypeStruct((m, D_MODEL), F32),
        scratch_shapes=[pltpu.VMEM((2, m, LANES), F32),
                        pltpu.VMEM((2, sb * DEC_SEQ, LANES), F32),
                        pltpu.VMEM((2, sb * DEC_SEQ, LANES), F32),
                        pltpu.VMEM((m, D_X), F32),
                        pltpu.VMEM((n_sc, 2 * N_MEM), F32),
                        pltpu.VMEM((n_sc, 2 * N_MEM), BF16)],
        compiler_params=_compiler_params(("arbitrary",)),
    )(xt, mk, mv, w["w_xq"], w["w_xo"], w["ln2_g"], w["ln2_b"])


def _hgrn_lower_bounds(lb_param):
    p = jax.nn.softmax(lb_param.astype(F32), axis=0)
    return jnp.cumsum(p, axis=0) - p[0:1]


def _prepare_weights(w_in, attn_sink, hgrn_lb, hgrn_norm_g, conv_w, pool_w, pool_scale, w_o, ln1_g, ln1_b,
                     w_xq, w_xo, ln2_g, ln2_b, w_gate, w_up, w_down, ln3_g, ln3_b):
    bf = lambda a: a.astype(BF16)
    rows = lambda a: a.astype(F32)[:, None, :]
    cols = lambda a: a.astype(F32)[:, :, None]
    lb_all = _hgrn_lower_bounds(hgrn_lb)
    eye = jnp.eye(len(POOL_WINDOWS), dtype=pool_w.dtype)
    pw = (pool_w[:, :, :, None, :] * eye[None, :, None, :, None]).reshape(DEPTH, W_GRP, W_GRP)
    return dict(
        w_in=bf(w_in),
        sink=attn_sink.astype(F32),
        lb=rows(lb_all), lb_col=cols(lb_all), ng=rows(hgrn_norm_g), ng_col=cols(hgrn_norm_g),
        cw=conv_w.astype(F32), pw=bf(pw), ps=rows(pool_scale),
        w_o=bf(w_o), ln1_g=rows(ln1_g), ln1_b=rows(ln1_b),
        w_xq=bf(w_xq), w_xo=bf(w_xo), ln2_g=rows(ln2_g), ln2_b=rows(ln2_b),
        w_gate=bf(w_gate), w_up=bf(w_up), w_down=bf(w_down), ln3_g=rows(ln3_g), ln3_b=rows(ln3_b))


def _prompt_layer(l, x, mkv, w):
    bsz, t, _ = x.shape
    x, kn, vn, sn, cn, pn = _prompt_mixer_call(l, x, w)
    x = _xattn_prompt_call(l, x, mkv, w)
    x = _ffn_call(l, x.reshape(bsz * t, D_MODEL), w).reshape(bsz, t, D_MODEL)
    return x, kn, vn, sn, cn, pn


def _sample_layer(l, xt, ck, cv, s, cs_tm, pb_tm, mk, mv, w, prev):
    xt, kn, vn, sn, cn, pn = _sample_mixer_call(l, xt, ck, cv, s, cs_tm, pb_tm, w, prev)
    xt = _xattn_sample_call(l, xt, mk, mv, w)
    xt = _ffn_call(l, xt, w)
    return xt, kn, vn, sn, cn, pn


def kernel(x_prompt, x_sample, cache_swa_k, cache_swa_v, state_hgrn, state_conv, state_pool, cache_mem_k,
           cache_mem_v, mem_prompt, emb_ln_g, emb_ln_b, w_in, attn_sink, hgrn_lb, hgrn_norm_g, conv_w, pool_w,
           pool_scale, w_o, ln1_g, ln1_b, w_xq, w_xk, w_xv, w_xo, ln2_g, ln2_b, w_gate, w_up, w_down, ln3_g,
           ln3_b):
    bp, t, _ = x_prompt.shape
    bs, ts, _ = x_sample.shape
    w = _prepare_weights(w_in, attn_sink, hgrn_lb, hgrn_norm_g, conv_w, pool_w, pool_scale, w_o, ln1_g, ln1_b,
                         w_xq, w_xo, ln2_g, ln2_b, w_gate, w_up, w_down, ln3_g, ln3_b)
    emb_g = emb_ln_g.reshape(1, D_MODEL).astype(F32)
    emb_b = emb_ln_b.reshape(1, D_MODEL).astype(F32)
    w = dict(w, emb_g=emb_g, emb_b=emb_b)
    hp = x_prompt
    w_kv = jnp.concatenate([w_xk, w_xv], axis=2).transpose(1, 0, 2).reshape(D_MODEL, DEPTH * 2 * D_X).astype(BF16)
    mkv = _matmul_call(mem_prompt.reshape(bp * N_MEM, D_MODEL), w_kv).reshape(bp, N_MEM, DEPTH * 2 * D_X)
    hs = _ln_call(x_sample.transpose(1, 0, 2).reshape(ts * bs, D_MODEL), emb_g, emb_b)
    ck = cache_swa_k.transpose(0, 1, 3, 4, 2).reshape(DEPTH, bs, LANES, WINDOW)
    cv = cache_swa_v.transpose(0, 1, 3, 4, 2).reshape(DEPTH, bs, LANES, WINDOW)
    st = state_hgrn.transpose(0, 2, 3, 4, 1).reshape(DEPTH, HG_ROWS, bs)
    cs_tm = state_conv.transpose(0, 2, 1, 3)
    pb_tm = state_pool.transpose(0, 2, 1, 3)
    mk_s = cache_mem_k.transpose(0, 1, 3, 4, 2).reshape(DEPTH, bs, D_X, N_MEM)
    mv_s = cache_mem_v.transpose(0, 1, 3, 4, 2).reshape(DEPTH, bs, D_X, N_MEM)
    outs = [[] for _ in range(5)]
    souts = [[] for _ in range(2)]
    prev = None
    for l in range(DEPTH):
        res = _prompt_layer(l, hp, mkv, w)
        hp = res[0]
        for acc, r in zip(outs, res[1:]):
            acc.append(r)
        sres = _sample_layer(l, hs, ck, cv, st, cs_tm, pb_tm, mk_s, mv_s, w, prev)
        hs = sres[0]
        prev = sres[1:4]
        for acc, r in zip(souts, sres[4:]):
            acc.append(r)
    pk, pv, ps, pc, pp = [jnp.stack(o) for o in outs]
    sk, sv, ss = prev
    sc, sp = [jnp.stack(o) for o in souts]
    mem_out = mkv.reshape(bp, N_MEM, DEPTH, 2, N_XH, HEAD_DIM).transpose(3, 2, 0, 1, 4, 5)
    swa_out = lambda a: a.reshape(DEPTH, bs, N_KV, HEAD_DIM, WINDOW).transpose(0, 1, 4, 2, 3)
    return (hp, hs.reshape(ts, bs, D_MODEL).transpose(1, 0, 2),
            pk.reshape(DEPTH, bp, WINDOW, N_KV, HEAD_DIM), pv.reshape(DEPTH, bp, WINDOW, N_KV, HEAD_DIM),
            ps.reshape(DEPTH, bp, N_HG, HEAD_DIM, HEAD_DIM), pc, pp,
            mem_out[0], mem_out[1],
            swa_out(sk), swa_out(sv),
            ss.reshape(DEPTH, N_HG, HEAD_DIM, HEAD_DIM, bs).transpose(0, 4, 1, 2, 3),
            sc.transpose(0, 2, 1, 3), sp.transpose(0, 2, 1, 3))
```

```python
import functools

import jax
import jax.numpy as jnp
from jax import lax
from jax.experimental import pallas as pl
from jax.experimental.pallas import tpu as pltpu

F32 = jnp.float32
BF16 = jnp.bfloat16

D_MODEL = 1024
DEPTH = 4
HEAD_DIM = 64
W_GRP = 256
N_KV = 2
WINDOW = 128
N_HG = 4
CONV_W = 3
POOL_WINDOWS = (2, 4, 8, 16)
POOL_BUF = 15
N_MEM = 256
N_XH = 4
D_X = 256
D_FF = 2816
D_IN = 2560
DEC_SEQ = 4
ALPHA = (2 * DEPTH) ** 0.25
LN_EPS = 1e-5
RMS_EPS = 1e-6
MASK_VALUE = -1e30
LB_FLOOR = 1e-30
PAST_LEN = 8192

C_AQ, C_AK, C_AV = 0, 256, 384
C_BQ, C_BF, C_BI, C_BG = 512, 768, 1024, 1280
C_CB, C_CC, C_CH = 1536, 1792, 2048
C_DV = 2304

LANES = 128
SUBLANES = 8
VMEM_LIMIT_BYTES = 56 * 1024 * 1024

TOKEN_BLOCK = 512
SUB_BLOCKS = 2
HG_CHUNK = 64
HG_MID = HG_CHUNK // 2 - 1
HG_MAX_EXPONENT = 80.0
HG_ROWS = N_HG * HEAD_DIM * HEAD_DIM
SEQ_BLOCK = 16
PAIR_ROWS = 2 * DEC_SEQ


def _nt_dot(a, b):
    return lax.dot_general(a, b, (((1,), (1,)), ((), ())), preferred_element_type=F32)


def _tn_dot(a, b):
    return lax.dot_general(a, b, (((0,), (0,)), ((), ())), preferred_element_type=F32)


def _dot(a, b):
    return jnp.dot(a, b, preferred_element_type=F32)


def _layer_norm(x, g, b):
    mu = jnp.mean(x, axis=-1, keepdims=True)
    xc = x - mu
    var = jnp.mean(xc * xc, axis=-1, keepdims=True)
    return xc * lax.rsqrt(var + LN_EPS) * g + b


def _sigmoid_pair(z):
    e = jnp.exp(-jnp.abs(z))
    inv = 1.0 / (1.0 + e)
    small = e * inv
    pos = z >= 0
    return jnp.where(pos, inv, small), jnp.where(pos, small, inv)


def _silu_tanh(z):
    return z * (0.5 + 0.5 * jnp.tanh(0.5 * z))


def _full_spec(shape):
    nd = len(shape)
    return pl.BlockSpec(shape, lambda *_: (0,) * nd)


def _layer_spec(shape, l, single=False):
    nd = len(shape)
    index = lambda *_: (l,) + (0,) * nd
    if single:
        return pl.BlockSpec((None,) + tuple(shape), index, pipeline_mode=pl.Buffered(1))
    return pl.BlockSpec((None,) + tuple(shape), index)


def _compiler_params(sem):
    return pltpu.CompilerParams(dimension_semantics=sem, vmem_limit_bytes=VMEM_LIMIT_BYTES)


class _MatmulQueue:
    def __init__(self):
        self.items = []

    def add(self, fn, *args):
        self.items.append((fn, args))

    def issue(self, n=1):
        for _ in range(min(n, len(self.items))):
            fn, args = self.items.pop(0)
            fn(*args)

    def flush(self):
        self.issue(len(self.items))


def _ln_kernel(x_ref, g_ref, b_ref, o_ref):
    o_ref[...] = _layer_norm(x_ref[...], g_ref[...], b_ref[...])


def _ln_call(x2d, g, b):
    m = x2d.shape[0]
    return pl.pallas_call(
        _ln_kernel,
        name="input_ln",
        grid=(m // TOKEN_BLOCK,),
        in_specs=[pl.BlockSpec((TOKEN_BLOCK, D_MODEL), lambda i: (i, 0)),
                  _full_spec((1, D_MODEL)), _full_spec((1, D_MODEL))],
        out_specs=pl.BlockSpec((TOKEN_BLOCK, D_MODEL), lambda i: (i, 0)),
        out_shape=jax.ShapeDtypeStruct((m, D_MODEL), F32),
        compiler_params=_compiler_params(("arbitrary",)),
    )(x2d, g, b)


def _matmul_kernel(x_ref, w_ref, o_ref):
    o_ref[...] = _dot(x_ref[...].astype(BF16), w_ref[...])


def _matmul_call(x2d, w):
    m, k = x2d.shape
    n = w.shape[1]
    return pl.pallas_call(
        _matmul_kernel,
        name="mem_proj",
        grid=(m // TOKEN_BLOCK,),
        in_specs=[pl.BlockSpec((TOKEN_BLOCK, k), lambda i: (i, 0)), _full_spec((k, n))],
        out_specs=pl.BlockSpec((TOKEN_BLOCK, n), lambda i: (i, 0)),
        out_shape=jax.ShapeDtypeStruct((m, n), F32),
        compiler_params=_compiler_params(("arbitrary",)),
    )(x2d, w)


def _ffn_kernel(x_ref, wg_ref, wu_ref, wd_ref, g_ref, b_ref, o_ref):
    x = x_ref[...]
    xb = x.astype(BF16)
    h = _silu_tanh(_dot(xb, wg_ref[...])) * _dot(xb, wu_ref[...])
    y = _dot(h.astype(BF16), wd_ref[...])
    o_ref[...] = _layer_norm(ALPHA * x + y, g_ref[...], b_ref[...])


def _ffn_call(l, x2d, w):
    m = x2d.shape[0]
    tb = min(2 * TOKEN_BLOCK, m)
    return pl.pallas_call(
        _ffn_kernel,
        name="ffn",
        grid=(m // tb,),
        in_specs=[pl.BlockSpec((tb, D_MODEL), lambda i: (i, 0)),
                  _layer_spec((D_MODEL, D_FF), l, single=True),
                  _layer_spec((D_MODEL, D_FF), l, single=True),
                  _layer_spec((D_FF, D_MODEL), l, single=True),
                  _layer_spec((1, D_MODEL), l), _layer_spec((1, D_MODEL), l)],
        out_specs=pl.BlockSpec((tb, D_MODEL), lambda i: (i, 0)),
        out_shape=jax.ShapeDtypeStruct((m, D_MODEL), F32),
        compiler_params=_compiler_params(("arbitrary",)),
    )(x2d, w["w_gate"], w["w_up"], w["w_down"], w["ln3_g"], w["ln3_b"])


def _xattn_prompt_kernel(x_ref, mk_ref, mv_ref, wq_ref, wo_ref, g_ref, b_ref, o_ref, q_scr, att_scr, yo_scr):
    tb = x_ref.shape[0]
    hb = tb // SUB_BLOCKS
    mk = mk_ref[...].astype(BF16)
    mv = mv_ref[...].astype(BF16)
    lane_head = lax.broadcasted_iota(jnp.int32, (hb, D_X), 1) >> 6
    sub = lambda r: slice(r * hb, (r + 1) * hb)

    def project(r):
        q_scr[sub(r), :] = _dot(x_ref[sub(r), :].astype(BF16), wq_ref[...]) * (HEAD_DIM ** -0.5)

    def finish(r, c0, c1):
        yo_scr[sub(r), c0:c1] = _dot(att_scr[sub(r), :], wo_ref[:, c0:c1])

    def finish_ln(r):
        o_ref[sub(r), :] = _layer_norm(ALPHA * x_ref[sub(r), :] + yo_scr[sub(r), :], g_ref[...], b_ref[...])

    out_cols = ((0, D_MODEL // 2), (D_MODEL // 2, D_MODEL))
    project(0)
    queue = _MatmulQueue()
    for r in range(SUB_BLOCKS):
        if r >= 1:
            for cols in out_cols:
                queue.add(finish, r - 1, *cols)
        if r + 1 < SUB_BLOCKS:
            queue.add(project, r + 1)
        q = q_scr[sub(r), :]
        o = jnp.zeros_like(q)
        for h in range(N_XH):
            qh = jnp.where(lane_head == h, q, 0.0).astype(BF16)
            s = _nt_dot(mk, qh)
            m = jnp.max(s, axis=0, keepdims=True)
            p = jnp.exp(s - m)
            den = jnp.sum(p, axis=0, keepdims=True)
            p = (p * (1.0 / den)).astype(BF16)
            o = jnp.where(lane_head == h, _tn_dot(p, mv), o)
            queue.issue()
        att_scr[sub(r), :] = o.astype(BF16)
        queue.flush()
        if r >= 1:
            finish_ln(r - 1)
    for cols in out_cols:
        finish(SUB_BLOCKS - 1, *cols)
    finish_ln(SUB_BLOCKS - 1)


def _xattn_prompt_call(l, x, mkv, w):
    bsz, t, _ = x.shape
    tb = min(4 * TOKEN_BLOCK, t)
    return pl.pallas_call(
        _xattn_prompt_kernel,
        name="xattn_prompt",
        grid=(bsz, t // tb),
        in_specs=[pl.BlockSpec((None, tb, D_MODEL), lambda i, j: (i, j, 0)),
                  pl.BlockSpec((None, N_MEM, D_X), lambda i, j: (i, 0, 2 * l)),
                  pl.BlockSpec((None, N_MEM, D_X), lambda i, j: (i, 0, 2 * l + 1)),
                  _layer_spec((D_MODEL, D_X), l), _layer_spec((D_X, D_MODEL), l),
                  _layer_spec((1, D_MODEL), l), _layer_spec((1, D_MODEL), l)],
        out_specs=pl.BlockSpec((None, tb, D_MODEL), lambda i, j: (i, j, 0)),
        out_shape=jax.ShapeDtypeStruct((bsz, t, D_MODEL), F32),
        scratch_shapes=[pltpu.VMEM((tb, D_X), F32), pltpu.VMEM((tb, D_X), BF16), pltpu.VMEM((tb, D_MODEL), F32)],
        compiler_params=_compiler_params(("arbitrary", "arbitrary")),
    )(x, mkv, mkv, w["w_xq"], w["w_xo"], w["ln2_g"], w["ln2_b"])


def _swa_bias_table():
    c = lax.broadcasted_iota(jnp.int32, (2 * WINDOW, 4 * WINDOW), 0)
    r = lax.broadcasted_iota(jnp.int32, (2 * WINDOW, 4 * WINDOW), 1)
    head = r >> 7
    rel = (r & (WINDOW - 1)) + WINDOW - c
    slope = jnp.exp2(-2.0 * (head.astype(F32) + 1.0))
    valid = (rel >= 0) & (rel <= WINDOW)
    return jnp.where(valid, -slope * rel.astype(F32), MASK_VALUE)


def _sink_select(sink_ref, l, head):
    return jnp.where(head == 0, sink_ref[l, 0],
                     jnp.where(head == 1, sink_ref[l, 1], jnp.where(head == 2, sink_ref[l, 2], sink_ref[l, 3])))


def _prompt_mixer_kernel(l, x_ref, eg_ref, eb_ref, w_in_ref, sink_ref, lb_ref, ng_ref, cw_ref, pw_ref, ps_ref,
                         w_o_ref, g_ref, b_ref,
                         y_ref, knew_ref, vnew_ref, snew_ref, cnew_ref, pnew_ref,
                         proj_scr, kext_scr, vext_scr, st_scr, u_scr, p_scr, bias_scr, mix_scr, hg_scr,
                         hq_scr, hk_scr, hv_scr, ghl_scr, cum_scr, qp_scr, kp_scr, qs_scr, ks_scr, dec_scr, inc_scr,
                         stb_scr, a_scr, yo_scr, st0_scr):
    tb = x_ref.shape[0]
    hb = tb // SUB_BLOCKS
    n_qb = hb // WINDOW
    n_ch = hb // HG_CHUNK
    bi = pl.program_id(0)
    ti = pl.program_id(1)
    last = ti == pl.num_programs(1) - 1

    @pl.when((bi == 0) & (ti == 0))
    def _():
        bias_scr[...] = _swa_bias_table()

    @pl.when(ti == 0)
    def _():
        kext_scr[0:WINDOW, :] = jnp.zeros((WINDOW, LANES), BF16)
        vext_scr[0:WINDOW, :] = jnp.zeros((WINDOW, LANES), BF16)
        st_scr[...] = jnp.zeros(st_scr.shape, F32)
        u_scr[0:SUBLANES, :] = jnp.zeros((SUBLANES, W_GRP), F32)
        p_scr[0:16, :] = jnp.zeros((16, W_GRP), F32)

    embed = l == 0

    def normalize(r):
        rows = slice(r * hb, (r + 1) * hb)
        y_ref[rows, :] = _layer_norm(x_ref[rows, :], eg_ref[...], eb_ref[...])

    def stream(rows):
        return y_ref[rows, :] if embed else x_ref[rows, :]

    def project(r, c0, c1):
        rows = slice(r * hb, (r + 1) * hb)
        proj_scr[rows, c0:c1] = _dot(stream(rows).astype(BF16), w_in_ref[:, c0:c1])

    def finish(r, c0, c1):
        rows = slice(r * hb, (r + 1) * hb)
        yo_scr[rows, c0:c1] = _dot(mix_scr[rows, :], w_o_ref[:, c0:c1])

    def finish_ln(r, redo=False):
        rows = slice(r * hb, (r + 1) * hb)
        if embed and redo:
            base = _layer_norm(x_ref[rows, :], eg_ref[...], eb_ref[...])
        else:
            base = stream(rows)
        y_ref[rows, :] = _layer_norm(ALPHA * base + yo_scr[rows, :], g_ref[...], b_ref[...])

    proj_cols = ((0, C_BQ), (C_BQ, C_BI), (C_BI, C_CB), (C_CB, C_CH), (C_CH, D_IN))
    out_cols = ((0, D_MODEL // 2), (D_MODEL // 2, D_MODEL))
    if embed:
        normalize(0)
    for cols in proj_cols:
        project(0, *cols)
    queue = _MatmulQueue()

    lane = lax.broadcasted_iota(jnp.int32, (WINDOW, LANES), 1)
    lo = lane < HEAD_DIM
    key_row = lax.broadcasted_iota(jnp.int32, (2 * WINDOW, 4 * WINDOW), 0)
    sink_row = _sink_select(sink_ref, l, lax.broadcasted_iota(jnp.int32, (1, 4 * WINDOW), 1) >> 7)
    lb = lb_ref[...]
    lbf = jnp.maximum(lb, LB_FLOOR)
    one_m_lb = 1.0 - lb
    ng = ng_ref[...]
    r256 = lax.broadcasted_iota(jnp.int32, (W_GRP, W_GRP), 0)
    c256 = lax.broadcasted_iota(jnp.int32, (W_GRP, W_GRP), 1)
    same_head = (r256 >> 6) == (c256 >> 6)
    head_ones = jnp.where(same_head, 1.0, 0.0).astype(BF16)
    same_head_b = head_ones > 0
    zero_b = jnp.zeros((W_GRP, W_GRP), BF16)
    rc = lax.broadcasted_iota(jnp.int32, (HG_CHUNK, W_GRP), 0)
    cc = lax.broadcasted_iota(jnp.int32, (HG_CHUNK, W_GRP), 1)
    causal = (cc & (HG_CHUNK - 1)) <= rc
    r64 = lax.broadcasted_iota(jnp.int32, (HG_CHUNK, HG_CHUNK), 0)
    c64 = lax.broadcasted_iota(jnp.int32, (HG_CHUNK, HG_CHUNK), 1)
    tril = jnp.where(c64 <= r64, 1.0, 0.0).astype(BF16)
    cw = cw_ref[...]
    rp = lax.broadcasted_iota(jnp.int32, (hb, W_GRP), 0)
    grp = lax.broadcasted_iota(jnp.int32, (hb, W_GRP), 1) >> 6
    width = jnp.left_shift(2, grp)
    chunk = lambda c: slice(c * HG_CHUNK, (c + 1) * HG_CHUNK)

    def hgrn_output(r):
        rows = slice(r * hb, (r + 1) * hb)
        o = hg_scr[...]
        ms = _dot((o * o).astype(BF16), head_ones) * (1.0 / HEAD_DIM)
        o = o * lax.rsqrt(ms + RMS_EPS) * ng
        mix_scr[rows, W_GRP:2 * W_GRP] = (o * _silu_tanh(proj_scr[rows, C_BG:C_BG + W_GRP])).astype(BF16)

    st0_scr[...] = st_scr[...]
    span = jnp.zeros((1, W_GRP), F32)
    for r in range(SUB_BLOCKS):
        r0 = r * hb
        sub = slice(r0, r0 + hb)
        if r >= 1:
            for cols in out_cols:
                queue.add(finish, r - 1, *cols)
        if r + 1 < SUB_BLOCKS:
            if embed:
                queue.add(normalize, r + 1)
            for cols in proj_cols:
                queue.add(project, r + 1, *cols)

        kext_scr[WINDOW:WINDOW + hb, :] = proj_scr[sub, C_AK:C_AK + LANES].astype(BF16)
        vext_scr[WINDOW:WINDOW + hb, :] = proj_scr[sub, C_AV:C_AV + LANES].astype(BF16)
        for j in range(n_qb):
            rows = slice(r0 + j * WINDOW, r0 + (j + 1) * WINDOW)
            q0 = proj_scr[rows, 0:LANES] * (HEAD_DIM ** -0.5)
            q1 = proj_scr[rows, LANES:2 * LANES] * (HEAD_DIM ** -0.5)
            q0r = pltpu.roll(q0, HEAD_DIM, axis=1)
            q1r = pltpu.roll(q1, HEAD_DIM, axis=1)
            zero = jnp.zeros_like(q0)
            q4 = jnp.concatenate([jnp.where(lo, q0, zero), jnp.where(lo, q0r, zero),
                                  jnp.where(lo, zero, q1r), jnp.where(lo, zero, q1)], axis=0).astype(BF16)
            kj = kext_scr[j * WINDOW:(j + 2) * WINDOW, :]
            vj = vext_scr[j * WINDOW:(j + 2) * WINDOW, :]
            s = _nt_dot(kj, q4) + bias_scr[...]
            if r == 0 and j == 0:
                s = jnp.where((ti == 0) & (key_row < WINDOW), MASK_VALUE, s)
            m = jnp.maximum(jnp.max(s, axis=0, keepdims=True), sink_row)
            p = jnp.exp(s - m)
            den = jnp.sum(p, axis=0, keepdims=True) + jnp.exp(sink_row - m)
            p = (p * (1.0 / den)).astype(BF16)
            o_all = _tn_dot(p, vj)
            o = [o_all[h * WINDOW:(h + 1) * WINDOW, :] for h in range(4)]
            mix_scr[rows, 0:LANES] = jnp.where(lo, o[0], pltpu.roll(o[1], HEAD_DIM, axis=1)).astype(BF16)
            mix_scr[rows, LANES:2 * LANES] = jnp.where(lo, pltpu.roll(o[2], HEAD_DIM, axis=1), o[3]).astype(BF16)
            queue.issue()
        kext_scr[0:WINDOW, :] = kext_scr[hb:hb + WINDOW, :]
        vext_scr[0:WINDOW, :] = vext_scr[hb:hb + WINDOW, :]

        hq_scr[...] = _silu_tanh(proj_scr[sub, C_BQ:C_BQ + W_GRP])
        queue.issue()
        sig_pos, sig_neg = _sigmoid_pair(proj_scr[sub, C_BF:C_BF + W_GRP])
        g = jnp.log(sig_pos + lbf * sig_neg)
        hk_scr[...] = one_m_lb * sig_neg
        g_hi = g.astype(BF16)
        ghl_scr[:, 0:W_GRP] = g_hi
        ghl_scr[:, W_GRP:2 * W_GRP] = (g - g_hi.astype(F32)).astype(BF16)
        hv_scr[...] = proj_scr[sub, C_BI:C_BI + W_GRP].astype(BF16)
        for c in range(n_ch):
            cum2 = _dot(tril, ghl_scr[chunk(c), :])
            cum_scr[chunk(c), :] = cum2[:, 0:W_GRP] + cum2[:, W_GRP:2 * W_GRP]
        for c in range(n_ch):
            cum = cum_scr[chunk(c), :]
            ref = cum[HG_MID:HG_MID + 1, :]
            tot = cum[HG_CHUNK - 1:HG_CHUNK, :]
            span = jnp.maximum(span, jnp.maximum(cum[0:1, :] - ref, ref - tot))
            qp = hq_scr[chunk(c), :] * jnp.exp(cum - ref)
            kp = hk_scr[chunk(c), :] * jnp.exp(ref - cum)
            qp_scr[chunk(c), :] = qp.astype(BF16)
            kp_scr[chunk(c), :] = kp.astype(BF16)
            qs_scr[chunk(c), :] = (qp * jnp.exp(ref)).astype(BF16)
            ks_scr[chunk(c), :] = (kp * jnp.exp(tot - ref)).astype(BF16)
            dec_scr[c] = jnp.broadcast_to(jnp.exp(tot), (SUBLANES, W_GRP))
        queue.issue()
        for c in range(n_ch):
            bk = jnp.where(same_head_b, jnp.concatenate([kp_scr[chunk(c), :]] * N_HG, axis=0), zero_b)
            a = jnp.where(causal, _nt_dot(qp_scr[chunk(c), :], bk), 0.0)
            a_scr[chunk(c), :] = a.astype(BF16)
        for c in range(n_ch):
            inc_scr[c] = jnp.where(same_head, _tn_dot(hv_scr[chunk(c), :], ks_scr[chunk(c), :]), 0.0)
        for c in range(n_ch):
            bv = jnp.where(same_head_b, jnp.concatenate([hv_scr[chunk(c), :]] * N_HG, axis=0), zero_b)
            hg_scr[chunk(c), :] = _dot(a_scr[chunk(c), :], bv)
        st = st_scr[...]
        for c in range(n_ch):
            stb_scr[c] = st.astype(BF16)
            st = st * dec_scr[c, 0:1, :] + inc_scr[c]
        st_scr[...] = st
        queue.issue()
        for c in range(n_ch):
            hg_scr[chunk(c), :] = hg_scr[chunk(c), :] + _nt_dot(qs_scr[chunk(c), :], stb_scr[c])

        hgrn_output(r)
        queue.issue()

        u_scr[SUBLANES:SUBLANES + hb, :] = proj_scr[sub, C_CC:C_CC + W_GRP] * proj_scr[sub, C_CH:C_CH + W_GRP]
        yc = (u_scr[SUBLANES - 2:SUBLANES - 2 + hb, :] * cw[0:1, :]
              + u_scr[SUBLANES - 1:SUBLANES - 1 + hb, :] * cw[1:2, :]
              + u_scr[SUBLANES:SUBLANES + hb, :] * cw[2:3, :])
        mix_scr[sub, 2 * W_GRP:3 * W_GRP] = (proj_scr[sub, C_CB:C_CB + W_GRP] * yc).astype(BF16)
        u_scr[0:SUBLANES, :] = u_scr[hb:hb + SUBLANES, :]
        queue.issue()

        dv = proj_scr[sub, C_DV:C_DV + W_GRP]
        p_scr[16:16 + hb, :] = dv
        ext = p_scr[...]
        s2 = ext + pltpu.roll(ext, 1, axis=0)
        s4 = s2 + pltpu.roll(s2, 2, axis=0)
        s8 = s4 + pltpu.roll(s4, 4, axis=0)
        s16 = s8 + pltpu.roll(s8, 8, axis=0)
        win = jnp.where(grp == 0, s2[16:], jnp.where(grp == 1, s4[16:], jnp.where(grp == 2, s8[16:], s16[16:])))
        cnt = jnp.minimum(ti * tb + r0 + rp + 1, width).astype(F32)
        pooled = win / cnt - dv
        yd = _dot(pooled.astype(BF16), pw_ref[...]) * ps_ref[...]
        mix_scr[sub, 3 * W_GRP:4 * W_GRP] = yd.astype(BF16)
        p_scr[0:16, :] = p_scr[hb:hb + 16, :]
        queue.flush()
        if r >= 1:
            finish_ln(r - 1)

    for cols in out_cols:
        finish(SUB_BLOCKS - 1, *cols)
    finish_ln(SUB_BLOCKS - 1)

    @pl.when(jnp.max(span) > HG_MAX_EXPONENT)
    def _():
        st_scr[...] = st0_scr[...]
        sub8 = lax.broadcasted_iota(jnp.int32, (SUBLANES, W_GRP), 0)
        for r in range(SUB_BLOCKS):
            def tile_body(i, carry, r=r):
                rows8 = pl.ds(pl.multiple_of(i * SUBLANES, SUBLANES), SUBLANES)
                prow = pl.ds(pl.multiple_of(r * hb + i * SUBLANES, SUBLANES), SUBLANES)
                sp8, sn8 = _sigmoid_pair(proj_scr[prow, C_BF:C_BF + W_GRP])
                f8 = sp8 + lbf * sn8
                k8 = (one_m_lb * sn8).astype(BF16)
                v8 = proj_scr[prow, C_BI:C_BI + W_GRP]
                q8 = _silu_tanh(proj_scr[prow, C_BQ:C_BQ + W_GRP]).astype(BF16)
                o8 = jnp.zeros((SUBLANES, W_GRP), F32)
                for j in range(SUBLANES):
                    vj = jnp.where(sub8 == j, v8, 0.0).astype(BF16)
                    stj = st_scr[...] * f8[j:j + 1, :] + jnp.where(same_head, _tn_dot(vj, k8), 0.0)
                    st_scr[...] = stj
                    o8 = jnp.where(sub8 == j, _nt_dot(q8, stj.astype(BF16)), o8)
                hg_scr[rows8, :] = o8
                return carry

            lax.fori_loop(0, hb // SUBLANES, tile_body, 0)
            hgrn_output(r)
            for cols in out_cols:
                finish(r, *cols)
            finish_ln(r, redo=True)

    @pl.when(last)
    def _():
        knew_ref[...] = proj_scr[tb - WINDOW:tb, C_AK:C_AK + LANES]
        vnew_ref[...] = proj_scr[tb - WINDOW:tb, C_AV:C_AV + LANES]
        s_t = st_scr[...].T
        for h in range(N_HG):
            snew_ref[h * HEAD_DIM:(h + 1) * HEAD_DIM, :] = (
                s_t[h * HEAD_DIM:(h + 1) * HEAD_DIM, h * HEAD_DIM:(h + 1) * HEAD_DIM])
        cnew_ref[...] = u_scr[SUBLANES - 2:SUBLANES, :]
        pnew_ref[...] = p_scr[1:16, :]


def _prompt_mixer_call(l, x, w):
    bsz, t, _ = x.shape
    tb = min(2 * TOKEN_BLOCK, t)
    hb = tb // SUB_BLOCKS
    n_ch = hb // HG_CHUNK
    row = lambda i, j: (i, 0, 0)
    out_shape = (
        jax.ShapeDtypeStruct((bsz, t, D_MODEL), F32),
        jax.ShapeDtypeStruct((bsz, WINDOW, LANES), F32),
        jax.ShapeDtypeStruct((bsz, WINDOW, LANES), F32),
        jax.ShapeDtypeStruct((bsz, W_GRP, HEAD_DIM), F32),
        jax.ShapeDtypeStruct((bsz, CONV_W - 1, W_GRP), F32),
        jax.ShapeDtypeStruct((bsz, POOL_BUF, W_GRP), F32),
    )
    f32 = lambda *shape: pltpu.VMEM(shape, F32)
    bf16 = lambda *shape: pltpu.VMEM(shape, BF16)
    return pl.pallas_call(
        functools.partial(_prompt_mixer_kernel, l),
        name="mixer_prompt",
        grid=(bsz, t // tb),
        in_specs=[pl.BlockSpec((None, tb, D_MODEL), lambda i, j: (i, j, 0)),
                  _full_spec((1, D_MODEL)), _full_spec((1, D_MODEL)),
                  _layer_spec((D_MODEL, D_IN), l, single=True),
                  pl.BlockSpec(memory_space=pltpu.SMEM),
                  _layer_spec((1, W_GRP), l), _layer_spec((1, W_GRP), l), _layer_spec((CONV_W, W_GRP), l),
                  _layer_spec((W_GRP, W_GRP), l), _layer_spec((1, W_GRP), l),
                  _layer_spec((D_MODEL, D_MODEL), l, single=True),
                  _layer_spec((1, D_MODEL), l), _layer_spec((1, D_MODEL), l)],
        out_specs=(pl.BlockSpec((None, tb, D_MODEL), lambda i, j: (i, j, 0)),
                   pl.BlockSpec((None, WINDOW, LANES), row),
                   pl.BlockSpec((None, WINDOW, LANES), row),
                   pl.BlockSpec((None, W_GRP, HEAD_DIM), row),
                   pl.BlockSpec((None, CONV_W - 1, W_GRP), row),
                   pl.BlockSpec((None, POOL_BUF, W_GRP), row)),
        out_shape=out_shape,
        scratch_shapes=[
            f32(tb, D_IN),
            bf16(WINDOW + hb, LANES),
            bf16(WINDOW + hb, LANES),
            f32(W_GRP, W_GRP),
            f32(SUBLANES + hb, W_GRP),
            f32(16 + hb, W_GRP),
            f32(2 * WINDOW, 4 * WINDOW),
            bf16(tb, D_MODEL),
            f32(hb, W_GRP),
            f32(hb, W_GRP), f32(hb, W_GRP),
            bf16(hb, W_GRP),
            bf16(hb, 2 * W_GRP),
            f32(hb, W_GRP),
            bf16(hb, W_GRP), bf16(hb, W_GRP),
            bf16(hb, W_GRP), bf16(hb, W_GRP),
            f32(n_ch, SUBLANES, W_GRP),
            f32(n_ch, W_GRP, W_GRP),
            bf16(n_ch, W_GRP, W_GRP),
            bf16(hb, W_GRP),
            f32(tb, D_MODEL),
            f32(W_GRP, W_GRP),
        ],
        compiler_params=_compiler_params(("arbitrary", "arbitrary")),
    )(x, w["emb_g"], w["emb_b"], w["w_in"], w["sink"], w["lb"], w["ng"], w["cw"], w["pw"], w["ps"], w["w_o"],
      w["ln1_g"], w["ln1_b"])


def _gather_block(dst_scr, src_scr, n_tiles, gi, sb, nb):
    for t in range(DEC_SEQ):
        start = pl.multiple_of(t * nb + gi * sb, sb)
        for j in range(n_tiles):
            dst_scr[j, t * sb:(t + 1) * sb, :] = src_scr[j, pl.ds(start, sb), :]


def _scatter_block(dst_scr, src_scr, n_tiles, gi, sb, nb):
    for t in range(DEC_SEQ):
        start = pl.multiple_of(t * nb + gi * sb, sb)
        for j in range(n_tiles):
            dst_scr[pl.ds(start, sb), j * LANES:(j + 1) * LANES] = src_scr[j, t * sb:(t + 1) * sb, :]


def _sample_mixer_kernel(l, xt_ref, ck_ref, cv_ref, s_ref, cs_ref, pb_ref,
                         w_in_ref, sink_ref, lbc_ref, ngc_ref, cw_ref, pw_ref, ps_ref,
                         w_o_ref, g_ref, b_ref, *rest):
    (y_ref, knew_ref, vnew_ref, snew_ref, cnew_ref, pnew_ref,
     proj_scr, a_scr, r_scr, q_scr, f_scr, k_scr, v_scr, gate_scr, o_scr, mix_scr,
     ab_scr, oa_scr, sc_scr, sn_scr, pc_scr, pn_scr) = rest[-22:]
    gi = pl.program_id(0)
    sb = ck_ref.shape[0]
    nb = xt_ref.shape[0] // DEC_SEQ
    half = sb // 2

    @pl.when(gi == 0)
    def _():
        xt = xt_ref[...].astype(BF16)
        pa = _dot(xt, w_in_ref[:, 0:C_BQ])
        for j in range(4):
            a_scr[j] = pa[:, j * LANES:(j + 1) * LANES]
        r_scr[...] = jnp.zeros(r_scr.shape, F32)
        hgt =_dot(xt, w_in_ref[:, C_BQ:C_CB]).T
        lbc = lbc_ref[...]
        q_scr[...] = _silu_tanh(hgt[0:W_GRP, :])
        sig_pos, sig_neg = _sigmoid_pair(hgt[W_GRP:2 * W_GRP, :])
        f_scr[...] = sig_pos + jnp.maximum(lbc, LB_FLOOR) * sig_neg
        k_scr[...] = (1.0 - lbc) * sig_neg
        v_scr[...] = hgt[2 * W_GRP:3 * W_GRP, :]
        gate_scr[...] = _silu_tanh(hgt[3 * W_GRP:4 * W_GRP, :])
        o_scr[...] = jnp.zeros(o_scr.shape, F32)
        proj_scr[...] = _dot(xt, w_in_ref[:, C_CB:D_IN])
        cw = cw_ref[...]
        ps = ps_ref[...]
        cp = lax.broadcasted_iota(jnp.int32, (nb, W_GRP), 1) >> 6
        width = jnp.left_shift(2, cp)
        u = [cs_ref[0], cs_ref[1]]
        ext = [pb_ref[i] for i in range(POOL_BUF)]
        for t in range(DEC_SEQ):
            rows = slice(t * nb, (t + 1) * nb)
            u.append(proj_scr[rows, W_GRP:2 * W_GRP] * proj_scr[rows, 2 * W_GRP:3 * W_GRP])
            ext.append(proj_scr[rows, 3 * W_GRP:4 * W_GRP])
        cnew_ref[0] = u[DEC_SEQ]
        cnew_ref[1] = u[DEC_SEQ + 1]
        for i in range(POOL_BUF):
            pnew_ref[i] = ext[DEC_SEQ + i]
        for t in range(DEC_SEQ):
            rows = slice(t * nb, (t + 1) * nb)
            yc = u[t] * cw[0:1, :] + u[t + 1] * cw[1:2, :] + u[t + 2] * cw[2:3, :]
            mix_scr[rows, 2 * W_GRP:3 * W_GRP] = proj_scr[rows, 0:W_GRP] * yc
            top = POOL_BUF + t
            acc = ext[top]
            sums = {}
            for jj in range(1, 16):
                acc = acc + ext[top - jj]
                if jj + 1 in POOL_WINDOWS:
                    sums[jj + 1] = acc
            win = jnp.where(cp == 0, sums[2], jnp.where(cp == 1, sums[4], jnp.where(cp == 2, sums[8], sums[16])))
            cnt = jnp.minimum(PAST_LEN + t + 1, width).astype(F32)
            pooled = win / cnt - ext[top]
            mix_scr[rows, 3 * W_GRP:4 * W_GRP] = _dot(pooled.astype(BF16), pw_ref[...]) * ps

    _gather_block(ab_scr, a_scr, 4, gi, sb, nb)
    lane = lax.broadcasted_iota(jnp.int32, (PAIR_ROWS, LANES), 1)
    lo = lane < HEAD_DIM
    n_rows = 4 * PAIR_ROWS
    r_c = lax.broadcasted_iota(jnp.int32, (n_rows, 2 * WINDOW), 0)
    c_c = lax.broadcasted_iota(jnp.int32, (n_rows, 2 * WINDOW), 1)
    rel_c = ((r_c >> 1) & 3) + WINDOW - (c_c & (WINDOW - 1))
    ok_c = ((r_c & 1) == (c_c >> 7)) & (rel_c <= WINDOW)
    bias_c = jnp.where(ok_c, -jnp.exp2(-2.0 * ((r_c >> 3).astype(F32) + 1.0)) * rel_c.astype(F32), MASK_VALUE)
    r_n = lax.broadcasted_iota(jnp.int32, (n_rows, PAIR_ROWS), 0)
    c_n = lax.broadcasted_iota(jnp.int32, (n_rows, PAIR_ROWS), 1)
    rel_n = ((r_n >> 1) & 3) - (c_n >> 1)
    ok_n = ((r_n & 1) == (c_n & 1)) & (rel_n >= 0)
    bias_n = jnp.where(ok_n, -jnp.exp2(-2.0 * ((r_n >> 3).astype(F32) + 1.0)) * rel_n.astype(F32), MASK_VALUE)
    pair_rows = lambda p: pl.ds(p, PAIR_ROWS, stride=half)
    for p in range(half):
        q0 = ab_scr[0, pair_rows(p), :] * (HEAD_DIM ** -0.5)
        q1 = ab_scr[1, pair_rows(p), :] * (HEAD_DIM ** -0.5)
        q0r = pltpu.roll(q0, HEAD_DIM, axis=1)
        q1r = pltpu.roll(q1, HEAD_DIM, axis=1)
        zero = jnp.zeros_like(q0)
        q4 = jnp.concatenate([jnp.where(lo, q0, zero), jnp.where(lo, q0r, zero),
                              jnp.where(lo, zero, q1r), jnp.where(lo, zero, q1)], axis=0).astype(BF16)
        kt2 = jnp.concatenate([ck_ref[p], ck_ref[p + half]], axis=1).astype(BF16)
        kn8 = ab_scr[2, pair_rows(p), :].astype(BF16)
        sc_scr[n_rows * p:n_rows * (p + 1), :] = _dot(q4, kt2) + bias_c
        sn_scr[n_rows * p:n_rows * (p + 1), :] = _nt_dot(q4, kn8) + bias_n
    sink_col = _sink_select(sink_ref, l, (lax.broadcasted_iota(jnp.int32, (n_rows * half, 1), 0) >> 3) & 3)
    s_c = sc_scr[...]
    s_n = sn_scr[...]
    m = jnp.maximum(jnp.maximum(jnp.max(s_c, axis=-1, keepdims=True), jnp.max(s_n, axis=-1, keepdims=True)),
                    sink_col)
    p_c = jnp.exp(s_c - m)
    p_n = jnp.exp(s_n - m)
    inv = 1.0 / (jnp.sum(p_c, axis=-1, keepdims=True) + jnp.sum(p_n, axis=-1, keepdims=True)
                 + jnp.exp(sink_col - m))
    pc_scr[...] = (p_c * inv).astype(BF16)
    pn_scr[...] = (p_n * inv).astype(BF16)
    for p in range(half):
        vt2 = jnp.concatenate([cv_ref[p], cv_ref[p + half]], axis=1).astype(BF16)
        vn8 = ab_scr[3, pair_rows(p), :].astype(BF16)
        o_all = (_nt_dot(pc_scr[n_rows * p:n_rows * (p + 1), :], vt2)
                 + _dot(pn_scr[n_rows * p:n_rows * (p + 1), :], vn8))
        o = [o_all[h * PAIR_ROWS:(h + 1) * PAIR_ROWS, :] for h in range(4)]
        oa_scr[0, pair_rows(p), :] = jnp.where(lo, o[0], pltpu.roll(o[1], HEAD_DIM, axis=1))
        oa_scr[1, pair_rows(p), :] = jnp.where(lo, pltpu.roll(o[2], HEAD_DIM, axis=1), o[3])
    _scatter_block(mix_scr, oa_scr, 2, gi, sb, nb)
    lane_w = lax.broadcasted_iota(jnp.int32, (LANES, LANES), 1)
    for tile, old_ref, out_ref in ((2, ck_ref, knew_ref), (3, cv_ref, vnew_ref)):
        for t in range(DEC_SEQ):
            r_scr[pl.ds(t, sb, stride=SUBLANES), :] = ab_scr[tile, t * sb:(t + 1) * sb, :]
        new_t = r_scr[...].T
        for b in range(sb):
            new_cols = pltpu.roll(new_t, (WINDOW - DEC_SEQ - SUBLANES * b) % LANES, axis=1)
            old_cols = pltpu.roll(old_ref[b], WINDOW - DEC_SEQ, axis=1)
            out_ref[b] = jnp.where(lane_w >= WINDOW - DEC_SEQ, new_cols, old_cols)

    k_per_step = s_ref.shape[0] // HEAD_DIM
    steps_per_head = HEAD_DIM // k_per_step
    head = gi // steps_per_head
    k_base = head * HEAD_DIM + (gi % steps_per_head) * k_per_step
    v_rows = pl.ds(pl.multiple_of(head * HEAD_DIM, HEAD_DIM), HEAD_DIM)

    def k_body(k8, accs):
        rows8 = pl.ds(pl.multiple_of(k_base + k8 * SUBLANES, SUBLANES), SUBLANES)
        accs = list(accs)
        for j in range(SUBLANES):
            st_rows = pl.ds(pl.multiple_of((k8 * SUBLANES + j) * HEAD_DIM, HEAD_DIM), HEAD_DIM)
            st = s_ref[st_rows, :]
            for t in range(DEC_SEQ):
                cols = slice(t * nb, (t + 1) * nb)
                f8, k8v, q8 = f_scr[rows8, cols], k_scr[rows8, cols], q_scr[rows8, cols]
                st = f8[j:j + 1, :] * st + k8v[j:j + 1, :] * v_scr[v_rows, cols]
                accs[t] = accs[t] + q8[j:j + 1, :] * st
            snew_ref[st_rows, :] = st
        return tuple(accs)

    zero_acc = jnp.zeros((HEAD_DIM, nb), F32)
    accs = lax.fori_loop(0, k_per_step // SUBLANES, k_body, (zero_acc,) * DEC_SEQ)
    for t in range(DEC_SEQ):
        cols = slice(t * nb, (t + 1) * nb)
        o_scr[v_rows, cols] = o_scr[v_rows, cols] + accs[t]

    @pl.when(gi == pl.num_programs(0) - 1)
    def _():
        o = o_scr[...]
        parts = []
        for h in range(N_HG):
            oh = o[h * HEAD_DIM:(h + 1) * HEAD_DIM, :]
            parts.append(oh * lax.rsqrt(jnp.mean(oh * oh, axis=0, keepdims=True) + RMS_EPS))
        ob = jnp.concatenate(parts, axis=0) * ngc_ref[...] * gate_scr[...]
        y = (_dot(mix_scr[:, 0:W_GRP].astype(BF16), w_o_ref[0:W_GRP, :])
             + _tn_dot(ob.astype(BF16), w_o_ref[W_GRP:2 * W_GRP, :])
             + _dot(mix_scr[:, 2 * W_GRP:4 * W_GRP].astype(BF16), w_o_ref[2 * W_GRP:4 * W_GRP, :]))
        y_ref[...] = _layer_norm(ALPHA * xt_ref[...] + y, g_ref[...], b_ref[...])


def _sample_mixer_call(l, xt, ck, cv, s, cs_tm, pb_tm, w, prev):
    m = xt.shape[0]
    nb = m // DEC_SEQ
    sb = SEQ_BLOCK
    n_steps = nb // sb
    n_sc = 4 * PAIR_ROWS * (sb // 2)
    s_rows = HG_ROWS // n_steps
    single = pl.Buffered(1)
    const2 = lambda i: (0, 0)
    out_shape = (
        jax.ShapeDtypeStruct((m, D_MODEL), F32),
        jax.ShapeDtypeStruct((DEPTH, nb, LANES, WINDOW), F32),
        jax.ShapeDtypeStruct((DEPTH, nb, LANES, WINDOW), F32),
        jax.ShapeDtypeStruct((DEPTH, HG_ROWS, nb), F32),
        jax.ShapeDtypeStruct((CONV_W - 1, nb, W_GRP), F32),
        jax.ShapeDtypeStruct((POOL_BUF, nb, W_GRP), F32),
    )
    seq_blk = lambda i: (l, i, 0, 0)
    in_specs = [pl.BlockSpec((m, D_MODEL), const2, pipeline_mode=single),
                pl.BlockSpec((None, sb, LANES, WINDOW), seq_blk),
                pl.BlockSpec((None, sb, LANES, WINDOW), seq_blk),
                pl.BlockSpec((None, s_rows, nb), lambda i: (l, i, 0)),
                _layer_spec((CONV_W - 1, nb, W_GRP), l, single=True),
                _layer_spec((POOL_BUF, nb, W_GRP), l, single=True),
                _layer_spec((D_MODEL, D_IN), l, single=True),
                pl.BlockSpec(memory_space=pltpu.SMEM),
                _layer_spec((W_GRP, 1), l), _layer_spec((W_GRP, 1), l),
                _layer_spec((CONV_W, W_GRP), l), _layer_spec((W_GRP, W_GRP), l), _layer_spec((1, W_GRP), l),
                _layer_spec((D_MODEL, D_MODEL), l, single=True),
                _layer_spec((1, D_MODEL), l), _layer_spec((1, D_MODEL), l)]
    args = [xt, ck, cv, s, cs_tm, pb_tm, w["w_in"], w["sink"], w["lb_col"], w["ng_col"],
            w["cw"], w["pw"], w["ps"], w["w_o"], w["ln1_g"], w["ln1_b"]]
    aliases = {}
    if prev is not None:
        for j, buf in enumerate(prev):
            aliases[len(args)] = 1 + j
            args.append(buf)
            in_specs.append(pl.BlockSpec(memory_space=pl.ANY))
    f32 = lambda *shape: pltpu.VMEM(shape, F32)
    return pl.pallas_call(
        functools.partial(_sample_mixer_kernel, l),
        name="mixer_sample",
        grid=(n_steps,),
        in_specs=in_specs,
        out_specs=(pl.BlockSpec((m, D_MODEL), const2),
                   pl.BlockSpec((None, sb, LANES, WINDOW), seq_blk),
                   pl.BlockSpec((None, sb, LANES, WINDOW), seq_blk),
                   pl.BlockSpec((None, s_rows, nb), lambda i: (l, i, 0)),
                   pl.BlockSpec((CONV_W - 1, nb, W_GRP), lambda i: (0, 0, 0)),
                   pl.BlockSpec((POOL_BUF, nb, W_GRP), lambda i: (0, 0, 0))),
        out_shape=out_shape,
        input_output_aliases=aliases,
        scratch_shapes=[f32(m, 4 * W_GRP),
                        f32(4, m, LANES),
                        f32(SUBLANES * sb, LANES),
                        f32(W_GRP, m), f32(W_GRP, m), f32(W_GRP, m), f32(W_GRP, m), f32(W_GRP, m),
                        f32(W_GRP, m),
                        f32(m, D_MODEL),
                        f32(4, sb * DEC_SEQ, LANES), f32(2, sb * DEC_SEQ, LANES),
                        f32(n_sc, 2 * WINDOW), f32(n_sc, PAIR_ROWS),
                        pltpu.VMEM((n_sc, 2 * WINDOW), BF16), pltpu.VMEM((n_sc, PAIR_ROWS), BF16)],
        compiler_params=_compiler_params(("arbitrary",)),
    )(*args)


def _xattn_sample_kernel(xt_ref, mk_ref, mv_ref, wq_ref, wo_ref, g_ref, b_ref, o_ref,
                         q_scr, qb_scr, ob_scr, att_scr, s_scr, p_scr):
    gi = pl.program_id(0)
    sb = mk_ref.shape[0]
    nb = xt_ref.shape[0] // DEC_SEQ
    half = sb // 2
    n_rows = N_XH * PAIR_ROWS

    @pl.when(gi == 0)
    def _():
        q = _dot(xt_ref[...].astype(BF16), wq_ref[...]) * (HEAD_DIM ** -0.5)
        q_scr[0] = q[:, 0:LANES]
        q_scr[1] = q[:, LANES:2 * LANES]

    _gather_block(qb_scr, q_scr, 2, gi, sb, nb)
    lane_head = lax.broadcasted_iota(jnp.int32, (PAIR_ROWS, D_X), 1) >> 6
    r_s = lax.broadcasted_iota(jnp.int32, (n_rows, 2 * N_MEM), 0)
    c_s = lax.broadcasted_iota(jnp.int32, (n_rows, 2 * N_MEM), 1)
    own = (r_s & 1) == (c_s >> 8)
    pair_rows = lambda p: pl.ds(p, PAIR_ROWS, stride=half)
    for p in range(half):
        q8 = jnp.concatenate([qb_scr[0, pair_rows(p), :], qb_scr[1, pair_rows(p), :]], axis=1)
        zero = jnp.zeros_like(q8)
        q4 = jnp.concatenate([jnp.where(lane_head == h, q8, zero) for h in range(N_XH)], axis=0).astype(BF16)
        kt2 = jnp.concatenate([mk_ref[p], mk_ref[p + half]], axis=1).astype(BF16)
        s_scr[n_rows * p:n_rows * (p + 1), :] = jnp.where(own, _dot(q4, kt2), MASK_VALUE)
    s_all = s_scr[...]
    m = jnp.max(s_all, axis=-1, keepdims=True)
    pr = jnp.exp(s_all - m)
    p_scr[...] = (pr * (1.0 / jnp.sum(pr, axis=-1, keepdims=True))).astype(BF16)
    for p in range(half):
        vt2 = jnp.concatenate([mv_ref[p], mv_ref[p + half]], axis=1).astype(BF16)
        o_all = _nt_dot(p_scr[n_rows * p:n_rows * (p + 1), :], vt2)
        o8 = jnp.zeros((PAIR_ROWS, D_X), F32)
        for h in range(N_XH):
            o8 = jnp.where(lane_head == h, o_all[h * PAIR_ROWS:(h + 1) * PAIR_ROWS, :], o8)
        ob_scr[0, pair_rows(p), :] = o8[:, 0:LANES]
        ob_scr[1, pair_rows(p), :] = o8[:, LANES:2 * LANES]
    _scatter_block(att_scr, ob_scr, 2, gi, sb, nb)

    @pl.when(gi == pl.num_programs(0) - 1)
    def _():
        y = _dot(att_scr[...].astype(BF16), wo_ref[...])
        o_ref[...] = _layer_norm(ALPHA * xt_ref[...] + y, g_ref[...], b_ref[...])


def _xattn_sample_call(l, xt, mk, mv, w):
    m = xt.shape[0]
    nb = m // DEC_SEQ
    sb = SEQ_BLOCK
    n_sc = N_XH * PAIR_ROWS * (sb // 2)
    const2 = lambda i: (0, 0)
    return pl.pallas_call(
        _xattn_sample_kernel,
        name="xattn_sample",
        grid=(nb // sb,),
        in_specs=[pl.BlockSpec((m, D_MODEL), const2, pipeline_mode=pl.Buffered(1)),
                  pl.BlockSpec((None, sb, D_X, N_MEM), lambda i: (l, i, 0, 0)),
                  pl.BlockSpec((None, sb, D_X, N_MEM), lambda i: (l, i, 0, 0)),
                  _layer_spec((D_MODEL, D_X), l), _layer_spec((D_X, D_MODEL), l),
                  _layer_spec((1, D_MODEL), l), _layer_spec((1, D_MODEL), l)],
        out_specs=pl.BlockSpec((m, D_MODEL), const2),
        out_shape=jax.ShapeDtypeStruct((m, D_MODEL), F32),
        scratch_shapes=[pltpu.VMEM((2, m, LANES), F32),
                        pltpu.VMEM((2, sb * DEC_SEQ, LANES), F32),
                        pltpu.VMEM((2, sb * DEC_SEQ, LANES), F32),
                        pltpu.VMEM((m, D_X), F32),
                        pltpu.VMEM((n_sc, 2 * N_MEM), F32),
                        pltpu.VMEM((n_sc, 2 * N_MEM), BF16)],
        compiler_params=_compiler_params(("arbitrary",)),
    )(xt, mk, mv, w["w_xq"], w["w_xo"], w["ln2_g"], w["ln2_b"])


def _hgrn_lower_bounds(lb_param):
    p = jax.nn.softmax(lb_param.astype(F32), axis=0)
    return jnp.cumsum(p, axis=0) - p[0:1]


def _prepare_weights(w_in, attn_sink, hgrn_lb, hgrn_norm_g, conv_w, pool_w, pool_scale, w_o, ln1_g, ln1_b,
                     w_xq, w_xo, ln2_g, ln2_b, w_gate, w_up, w_down, ln3_g, ln3_b):
    bf = lambda a: a.astype(BF16)
    rows = lambda a: a.astype(F32)[:, None, :]
    cols = lambda a: a.astype(F32)[:, :, None]
    lb_all = _hgrn_lower_bounds(hgrn_lb)
    eye = jnp.eye(len(POOL_WINDOWS), dtype=pool_w.dtype)
    pw = (pool_w[:, :, :, None, :] * eye[None, :, None, :, None]).reshape(DEPTH, W_GRP, W_GRP)
    return dict(
        w_in=bf(w_in),
        sink=attn_sink.astype(F32),
        lb=rows(lb_all), lb_col=cols(lb_all), ng=rows(hgrn_norm_g), ng_col=cols(hgrn_norm_g),
        cw=conv_w.astype(F32), pw=bf(pw), ps=rows(pool_scale),
        w_o=bf(w_o), ln1_g=rows(ln1_g), ln1_b=rows(ln1_b),
        w_xq=bf(w_xq), w_xo=bf(w_xo), ln2_g=rows(ln2_g), ln2_b=rows(ln2_b),
        w_gate=bf(w_gate), w_up=bf(w_up), w_down=bf(w_down), ln3_g=rows(ln3_g), ln3_b=rows(ln3_b))


def _prompt_layer(l, x, mkv, w):
    bsz, t, _ = x.shape
    x, kn, vn, sn, cn, pn = _prompt_mixer_call(l, x, w)
    x = _xattn_prompt_call(l, x, mkv, w)
    x = _ffn_call(l, x.reshape(bsz * t, D_MODEL), w).reshape(bsz, t, D_MODEL)
    return x, kn, vn, sn, cn, pn


def _sample_layer(l, xt, ck, cv, s, cs_tm, pb_tm, mk, mv, w, prev):
    xt, kn, vn, sn, cn, pn = _sample_mixer_call(l, xt, ck, cv, s, cs_tm, pb_tm, w, prev)
    xt = _xattn_sample_call(l, xt, mk, mv, w)
    xt = _ffn_call(l, xt, w)
    return xt, kn, vn, sn, cn, pn


def kernel(x_prompt, x_sample, cache_swa_k, cache_swa_v, state_hgrn, state_conv, state_pool, cache_mem_k,
           cache_mem_v, mem_prompt, emb_ln_g, emb_ln_b, w_in, attn_sink, hgrn_lb, hgrn_norm_g, conv_w, pool_w,
           pool_scale, w_o, ln1_g, ln1_b, w_xq, w_xk, w_xv, w_xo, ln2_g, ln2_b, w_gate, w_up, w_down, ln3_g,
           ln3_b):
    bp, t, _ = x_prompt.shape
    bs, ts, _ = x_sample.shape
    w = _prepare_weights(w_in, attn_sink, hgrn_lb, hgrn_norm_g, conv_w, pool_w, pool_scale, w_o, ln1_g, ln1_b,
                         w_xq, w_xo, ln2_g, ln2_b, w_gate, w_up, w_down, ln3_g, ln3_b)
    emb_g = emb_ln_g.reshape(1, D_MODEL).astype(F32)
    emb_b = emb_ln_b.reshape(1, D_MODEL).astype(F32)
    w = dict(w, emb_g=emb_g, emb_b=emb_b)
    hp = x_prompt
    w_kv = jnp.concatenate([w_xk, w_xv], axis=2).transpose(1, 0, 2).reshape(D_MODEL, DEPTH * 2 * D_X).astype(BF16)
    mkv = _matmul_call(mem_prompt.reshape(bp * N_MEM, D_MODEL), w_kv).reshape(bp, N_MEM, DEPTH * 2 * D_X)
    hs = _ln_call(x_sample.transpose(1, 0, 2).reshape(ts * bs, D_MODEL), emb_g, emb_b)
    ck = cache_swa_k.transpose(0, 1, 3, 4, 2).reshape(DEPTH, bs, LANES, WINDOW)
    cv = cache_swa_v.transpose(0, 1, 3, 4, 2).reshape(DEPTH, bs, LANES, WINDOW)
    st = state_hgrn.transpose(0, 2, 3, 4, 1).reshape(DEPTH, HG_ROWS, bs)
    cs_tm = state_conv.transpose(0, 2, 1, 3)
    pb_tm = state_pool.transpose(0, 2, 1, 3)
    mk_s = cache_mem_k.transpose(0, 1, 3, 4, 2).reshape(DEPTH, bs, D_X, N_MEM)
    mv_s = cache_mem_v.transpose(0, 1, 3, 4, 2).reshape(DEPTH, bs, D_X, N_MEM)
    outs = [[] for _ in range(5)]
    souts = [[] for _ in range(2)]
    prev = None
    for l in range(DEPTH):
        res = _prompt_layer(l, hp, mkv, w)
        hp = res[0]
        for acc, r in zip(outs, res[1:]):
            acc.append(r)
        sres = _sample_layer(l, hs, ck, cv, st, cs_tm, pb_tm, mk_s, mv_s, w, prev)
        hs = sres[0]
        prev = sres[1:4]
        for acc, r in zip(souts, sres[4:]):
            acc.append(r)
    pk, pv, ps, pc, pp = [jnp.stack(o) for o in outs]
    sk, sv, ss = prev
    sc, sp = [jnp.stack(o) for o in souts]
    mem_out = mkv.reshape(bp, N_MEM, DEPTH, 2, N_XH, HEAD_DIM).transpose(3, 2, 0, 1, 4, 5)
    swa_out = lambda a: a.reshape(DEPTH, bs, N_KV, HEAD_DIM, WINDOW).transpose(0, 1, 4, 2, 3)
    return (hp, hs.reshape(ts, bs, D_MODEL).transpose(1, 0, 2),
            pk.reshape(DEPTH, bp, WINDOW, N_KV, HEAD_DIM), pv.reshape(DEPTH, bp, WINDOW, N_KV, HEAD_DIM),
            ps.reshape(DEPTH, bp, N_HG, HEAD_DIM, HEAD_DIM), pc, pp,
            mem_out[0], mem_out[1],
            swa_out(sk), swa_out(sv),
            ss.reshape(DEPTH, N_HG, HEAD_DIM, HEAD_DIM, bs).transpose(0, 4, 1, 2, 3),
            sc.transpose(0, 2, 1, 3), sp.transpose(0, 2, 1, 3))
```

```python
import functools

import jax
import jax.numpy as jnp
from jax import lax
from jax.experimental import pallas as pl
from jax.experimental.pallas import tpu as pltpu

F32 = jnp.float32
BF16 = jnp.bfloat16

D_MODEL = 1024
DEPTH = 4
HEAD_DIM = 64
W_GRP = 256
N_KV = 2
WINDOW = 128
N_HG = 4
CONV_W = 3
POOL_WINDOWS = (2, 4, 8, 16)
POOL_BUF = 15
N_MEM = 256
N_XH = 4
D_X = 256
D_FF = 2816
D_IN = 2560
DEC_SEQ = 4
ALPHA = (2 * DEPTH) ** 0.25
LN_EPS = 1e-5
RMS_EPS = 1e-6
MASK_VALUE = -1e30
LB_FLOOR = 1e-30
PAST_LEN = 8192

C_AQ, C_AK, C_AV = 0, 256, 384
C_BQ, C_BF, C_BI, C_BG = 512, 768, 1024, 1280
C_CB, C_CC, C_CH = 1536, 1792, 2048
C_DV = 2304

LANES = 128
SUBLANES = 8
VMEM_LIMIT_BYTES = 56 * 1024 * 1024

TOKEN_BLOCK = 512
SUB_BLOCKS = 2
HG_CHUNK = 64
HG_MID = HG_CHUNK // 2 - 1
HG_MAX_EXPONENT = 80.0
HG_ROWS = N_HG * HEAD_DIM * HEAD_DIM
SEQ_BLOCK = 16
PAIR_ROWS = 2 * DEC_SEQ


def _nt_dot(a, b):
    return lax.dot_general(a, b, (((1,), (1,)), ((), ())), preferred_element_type=F32)


def _tn_dot(a, b):
    return lax.dot_general(a, b, (((0,), (0,)), ((), ())), preferred_element_type=F32)


def _dot(a, b):
    return jnp.dot(a, b, preferred_element_type=F32)


def _layer_norm(x, g, b):
    mu = jnp.mean(x, axis=-1, keepdims=True)
    xc = x - mu
    var = jnp.mean(xc * xc, axis=-1, keepdims=True)
    return xc * lax.rsqrt(var + LN_EPS) * g + b


def _sigmoid_pair(z):
    e = jnp.exp(-jnp.abs(z))
    inv = 1.0 / (1.0 + e)
    small = e * inv
    pos = z >= 0
    return jnp.where(pos, inv, small), jnp.where(pos, small, inv)


def _silu_tanh(z):
    return z * (0.5 + 0.5 * jnp.tanh(0.5 * z))


def _full_spec(shape):
    nd = len(shape)
    return pl.BlockSpec(shape, lambda *_: (0,) * nd)


def _layer_spec(shape, l, single=False):
    nd = len(shape)
    index = lambda *_: (l,) + (0,) * nd
    if single:
        return pl.BlockSpec((None,) + tuple(shape), index, pipeline_mode=pl.Buffered(1))
    return pl.BlockSpec((None,) + tuple(shape), index)


def _compiler_params(sem):
    return pltpu.CompilerParams(dimension_semantics=sem, vmem_limit_bytes=VMEM_LIMIT_BYTES)


class _MatmulQueue:
    def __init__(self):
        self.items = []

    def add(self, fn, *args):
        self.items.append((fn, args))

    def issue(self, n=1):
        for _ in range(min(n, len(self.items))):
            fn, args = self.items.pop(0)
            fn(*args)

    def flush(self):
        self.issue(len(self.items))


def _ln_kernel(x_ref, g_ref, b_ref, o_ref):
    o_ref[...] = _layer_norm(x_ref[...], g_ref[...], b_ref[...])


def _ln_call(x2d, g, b):
    m = x2d.shape[0]
    return pl.pallas_call(
        _ln_kernel,
        name="input_ln",
        grid=(m // TOKEN_BLOCK,),
        in_specs=[pl.BlockSpec((TOKEN_BLOCK, D_MODEL), lambda i: (i, 0)),
                  _full_spec((1, D_MODEL)), _full_spec((1, D_MODEL))],
        out_specs=pl.BlockSpec((TOKEN_BLOCK, D_MODEL), lambda i: (i, 0)),
        out_shape=jax.ShapeDtypeStruct((m, D_MODEL), F32),
        compiler_params=_compiler_params(("arbitrary",)),
    )(x2d, g, b)


def _matmul_kernel(x_ref, w_ref, o_ref):
    o_ref[...] = _dot(x_ref[...].astype(BF16), w_ref[...])


def _matmul_call(x2d, w):
    m, k = x2d.shape
    n = w.shape[1]
    return pl.pallas_call(
        _matmul_kernel,
        name="mem_proj",
        grid=(m // TOKEN_BLOCK,),
        in_specs=[pl.BlockSpec((TOKEN_BLOCK, k), lambda i: (i, 0)), _full_spec((k, n))],
        out_specs=pl.BlockSpec((TOKEN_BLOCK, n), lambda i: (i, 0)),
        out_shape=jax.ShapeDtypeStruct((m, n), F32),
        compiler_params=_compiler_params(("arbitrary",)),
    )(x2d, w)


def _ffn_kernel(x_ref, wg_ref, wu_ref, wd_ref, g_ref, b_ref, o_ref):
    x = x_ref[...]
    xb = x.astype(BF16)
    h = _silu_tanh(_dot(xb, wg_ref[...])) * _dot(xb, wu_ref[...])
    y = _dot(h.astype(BF16), wd_ref[...])
    o_ref[...] = _layer_norm(ALPHA * x + y, g_ref[...], b_ref[...])


def _ffn_call(l, x2d, w):
    m = x2d.shape[0]
    tb = min(2 * TOKEN_BLOCK, m)
    return pl.pallas_call(
        _ffn_kernel,
        name="ffn",
        grid=(m // tb,),
        in_specs=[pl.BlockSpec((tb, D_MODEL), lambda i: (i, 0)),
                  _layer_spec((D_MODEL, D_FF), l, single=True),
                  _layer_spec((D_MODEL, D_FF), l, single=True),
                  _layer_spec((D_FF, D_MODEL), l, single=True),
                  _layer_spec((1, D_MODEL), l), _layer_spec((1, D_MODEL), l)],
        out_specs=pl.BlockSpec((tb, D_MODEL), lambda i: (i, 0)),
        out_shape=jax.ShapeDtypeStruct((m, D_MODEL), F32),
        compiler_params=_compiler_params(("arbitrary",)),
    )(x2d, w["w_gate"], w["w_up"], w["w_down"], w["ln3_g"], w["ln3_b"])


def _xattn_prompt_kernel(x_ref, mk_ref, mv_ref, wq_ref, wo_ref, g_ref, b_ref, o_ref, q_scr, att_scr, yo_scr,
                         s_scr, p_scr):
    tb = x_ref.shape[0]
    hb = tb // SUB_BLOCKS
    mk = mk_ref[...].astype(BF16)
    mv = mv_ref[...].astype(BF16)
    lane_head = lax.broadcasted_iota(jnp.int32, (hb, D_X), 1) >> 6
    sub = lambda r: slice(r * hb, (r + 1) * hb)

    def project(r):
        q_scr[sub(r), :] = _dot(x_ref[sub(r), :].astype(BF16), wq_ref[...]) * (HEAD_DIM ** -0.5)

    def finish(r, c0, c1):
        yo_scr[sub(r), c0:c1] = _dot(att_scr[sub(r), :], wo_ref[:, c0:c1])

    def finish_ln(r):
        o_ref[sub(r), :] = _layer_norm(ALPHA * x_ref[sub(r), :] + yo_scr[sub(r), :], g_ref[...], b_ref[...])

    out_cols = ((0, D_MODEL // 2), (D_MODEL // 2, D_MODEL))
    project(0)
    queue = _MatmulQueue()
    for r in range(SUB_BLOCKS):
        if r >= 1:
            for cols in out_cols:
                queue.add(finish, r - 1, *cols)
        if r + 1 < SUB_BLOCKS:
            queue.add(project, r + 1)
        q = q_scr[sub(r), :]
        o = jnp.zeros_like(q)
        for h in range(N_XH):
            qh = jnp.where(lane_head == h, q, 0.0).astype(BF16)
            s_scr[h] = _nt_dot(mk, qh)
        queue.issue()
        for h in range(N_XH):
            s = s_scr[h]
            m = jnp.max(s, axis=0, keepdims=True)
            p = jnp.exp(s - m)
            den = jnp.sum(p, axis=0, keepdims=True)
            p_scr[h] = (p * (1.0 / den)).astype(BF16)
            queue.issue()
        for h in range(N_XH):
            o = jnp.where(lane_head == h, _tn_dot(p_scr[h], mv), o)
        att_scr[sub(r), :] = o.astype(BF16)
        queue.flush()
        if r >= 1:
            finish_ln(r - 1)
    for cols in out_cols:
        finish(SUB_BLOCKS - 1, *cols)
    finish_ln(SUB_BLOCKS - 1)


def _xattn_prompt_call(l, x, mkv, w):
    bsz, t, _ = x.shape
    tb = min(4 * TOKEN_BLOCK, t)
    return pl.pallas_call(
        _xattn_prompt_kernel,
        name="xattn_prompt",
        grid=(bsz, t // tb),
        in_specs=[pl.BlockSpec((None, tb, D_MODEL), lambda i, j: (i, j, 0)),
                  pl.BlockSpec((None, N_MEM, D_X), lambda i, j: (i, 0, 2 * l)),
                  pl.BlockSpec((None, N_MEM, D_X), lambda i, j: (i, 0, 2 * l + 1)),
                  _layer_spec((D_MODEL, D_X), l), _layer_spec((D_X, D_MODEL), l),
                  _layer_spec((1, D_MODEL), l), _layer_spec((1, D_MODEL), l)],
        out_specs=pl.BlockSpec((None, tb, D_MODEL), lambda i, j: (i, j, 0)),
        out_shape=jax.ShapeDtypeStruct((bsz, t, D_MODEL), F32),
        scratch_shapes=[pltpu.VMEM((tb, D_X), F32), pltpu.VMEM((tb, D_X), BF16), pltpu.VMEM((tb, D_MODEL), F32),
                        pltpu.VMEM((N_XH, N_MEM, tb // SUB_BLOCKS), F32),
                        pltpu.VMEM((N_XH, N_MEM, tb // SUB_BLOCKS), BF16)],
        compiler_params=_compiler_params(("arbitrary", "arbitrary")),
    )(x, mkv, mkv, w["w_xq"], w["w_xo"], w["ln2_g"], w["ln2_b"])


def _swa_bias_table():
    c = lax.broadcasted_iota(jnp.int32, (2 * WINDOW, 4 * WINDOW), 0)
    r = lax.broadcasted_iota(jnp.int32, (2 * WINDOW, 4 * WINDOW), 1)
    head = r >> 7
    rel = (r & (WINDOW - 1)) + WINDOW - c
    slope = jnp.exp2(-2.0 * (head.astype(F32) + 1.0))
    valid = (rel >= 0) & (rel <= WINDOW)
    return jnp.where(valid, -slope * rel.astype(F32), MASK_VALUE)


def _sink_select(sink_ref, l, head):
    return jnp.where(head == 0, sink_ref[l, 0],
                     jnp.where(head == 1, sink_ref[l, 1], jnp.where(head == 2, sink_ref[l, 2], sink_ref[l, 3])))


def _prompt_mixer_kernel(l, x_ref, eg_ref, eb_ref, w_in_ref, sink_ref, lb_ref, ng_ref, cw_ref, pw_ref, ps_ref,
                         w_o_ref, g_ref, b_ref,
                         y_ref, knew_ref, vnew_ref, snew_ref, cnew_ref, pnew_ref,
                         proj_scr, kext_scr, vext_scr, st_scr, u_scr, p_scr, bias_scr, mix_scr, hg_scr,
                         hq_scr, hk_scr, hv_scr, ghl_scr, cum_scr, qp_scr, kp_scr, qs_scr, ks_scr, dec_scr, inc_scr,
                         stb_scr, a_scr, yo_scr, st0_scr):
    tb = x_ref.shape[0]
    hb = tb // SUB_BLOCKS
    n_qb = hb // WINDOW
    n_ch = hb // HG_CHUNK
    bi = pl.program_id(0)
    ti = pl.program_id(1)
    last = ti == pl.num_programs(1) - 1

    @pl.when((bi == 0) & (ti == 0))
    def _():
        bias_scr[...] = _swa_bias_table()

    @pl.when(ti == 0)
    def _():
        kext_scr[0:WINDOW, :] = jnp.zeros((WINDOW, LANES), BF16)
        vext_scr[0:WINDOW, :] = jnp.zeros((WINDOW, LANES), BF16)
        st_scr[...] = jnp.zeros(st_scr.shape, F32)
        u_scr[0:SUBLANES, :] = jnp.zeros((SUBLANES, W_GRP), F32)
        p_scr[0:16, :] = jnp.zeros((16, W_GRP), F32)

    embed = l == 0

    def normalize(r):
        rows = slice(r * hb, (r + 1) * hb)
        y_ref[rows, :] = _layer_norm(x_ref[rows, :], eg_ref[...], eb_ref[...])

    def stream(rows):
        return y_ref[rows, :] if embed else x_ref[rows, :]

    def project(r, c0, c1):
        rows = slice(r * hb, (r + 1) * hb)
        proj_scr[rows, c0:c1] = _dot(stream(rows).astype(BF16), w_in_ref[:, c0:c1])

    def finish(r, c0, c1):
        rows = slice(r * hb, (r + 1) * hb)
        yo_scr[rows, c0:c1] = _dot(mix_scr[rows, :], w_o_ref[:, c0:c1])

    def finish_ln(r, redo=False):
        rows = slice(r * hb, (r + 1) * hb)
        if embed and redo:
            base = _layer_norm(x_ref[rows, :], eg_ref[...], eb_ref[...])
        else:
            base = stream(rows)
        y_ref[rows, :] = _layer_norm(ALPHA * base + yo_scr[rows, :], g_ref[...], b_ref[...])

    proj_cols = ((0, C_BQ), (C_BQ, C_BI), (C_BI, C_CB), (C_CB, C_CH), (C_CH, D_IN))
    out_cols = ((0, D_MODEL // 2), (D_MODEL // 2, D_MODEL))
    if embed:
        normalize(0)
    for cols in proj_cols:
        project(0, *cols)
    queue = _MatmulQueue()

    lane = lax.broadcasted_iota(jnp.int32, (WINDOW, LANES), 1)
    lo = lane < HEAD_DIM
    key_row = lax.broadcasted_iota(jnp.int32, (2 * WINDOW, 4 * WINDOW), 0)
    sink_row = _sink_select(sink_ref, l, lax.broadcasted_iota(jnp.int32, (1, 4 * WINDOW), 1) >> 7)
    lb = lb_ref[...]
    lbf = jnp.maximum(lb, LB_FLOOR)
    one_m_lb = 1.0 - lb
    ng = ng_ref[...]
    r256 = lax.broadcasted_iota(jnp.int32, (W_GRP, W_GRP), 0)
    c256 = lax.broadcasted_iota(jnp.int32, (W_GRP, W_GRP), 1)
    same_head = (r256 >> 6) == (c256 >> 6)
    head_ones = jnp.where(same_head, 1.0, 0.0).astype(BF16)
    same_head_b = head_ones > 0
    zero_b = jnp.zeros((W_GRP, W_GRP), BF16)
    rc = lax.broadcasted_iota(jnp.int32, (HG_CHUNK, W_GRP), 0)
    cc = lax.broadcasted_iota(jnp.int32, (HG_CHUNK, W_GRP), 1)
    causal = (cc & (HG_CHUNK - 1)) <= rc
    r64 = lax.broadcasted_iota(jnp.int32, (HG_CHUNK, HG_CHUNK), 0)
    c64 = lax.broadcasted_iota(jnp.int32, (HG_CHUNK, HG_CHUNK), 1)
    tril = jnp.where(c64 <= r64, 1.0, 0.0).astype(BF16)
    cw = cw_ref[...]
    rp = lax.broadcasted_iota(jnp.int32, (hb, W_GRP), 0)
    grp = lax.broadcasted_iota(jnp.int32, (hb, W_GRP), 1) >> 6
    width = jnp.left_shift(2, grp)
    chunk = lambda c: slice(c * HG_CHUNK, (c + 1) * HG_CHUNK)

    def hgrn_output(r):
        rows = slice(r * hb, (r + 1) * hb)
        o = hg_scr[...]
        ms = _dot((o * o).astype(BF16), head_ones) * (1.0 / HEAD_DIM)
        o = o * lax.rsqrt(ms + RMS_EPS) * ng
        mix_scr[rows, W_GRP:2 * W_GRP] = (o * _silu_tanh(proj_scr[rows, C_BG:C_BG + W_GRP])).astype(BF16)

    st0_scr[...] = st_scr[...]
    span = jnp.zeros((1, W_GRP), F32)
    for r in range(SUB_BLOCKS):
        r0 = r * hb
        sub = slice(r0, r0 + hb)
        if r >= 1:
            for cols in out_cols:
                queue.add(finish, r - 1, *cols)
        if r + 1 < SUB_BLOCKS:
            if embed:
                queue.add(normalize, r + 1)
            for cols in proj_cols:
                queue.add(project, r + 1, *cols)

        kext_scr[WINDOW:WINDOW + hb, :] = proj_scr[sub, C_AK:C_AK + LANES].astype(BF16)
        vext_scr[WINDOW:WINDOW + hb, :] = proj_scr[sub, C_AV:C_AV + LANES].astype(BF16)
        for j in range(n_qb):
            rows = slice(r0 + j * WINDOW, r0 + (j + 1) * WINDOW)
            q0 = proj_scr[rows, 0:LANES] * (HEAD_DIM ** -0.5)
            q1 = proj_scr[rows, LANES:2 * LANES] * (HEAD_DIM ** -0.5)
            q0r = pltpu.roll(q0, HEAD_DIM, axis=1)
            q1r = pltpu.roll(q1, HEAD_DIM, axis=1)
            zero = jnp.zeros_like(q0)
            q4 = jnp.concatenate([jnp.where(lo, q0, zero), jnp.where(lo, q0r, zero),
                                  jnp.where(lo, zero, q1r), jnp.where(lo, zero, q1)], axis=0).astype(BF16)
            kj = kext_scr[j * WINDOW:(j + 2) * WINDOW, :]
            vj = vext_scr[j * WINDOW:(j + 2) * WINDOW, :]
            s = _nt_dot(kj, q4) + bias_scr[...]
            if r == 0 and j == 0:
                s = jnp.where((ti == 0) & (key_row < WINDOW), MASK_VALUE, s)
            m = jnp.maximum(jnp.max(s, axis=0, keepdims=True), sink_row)
            p = jnp.exp(s - m)
            den = jnp.sum(p, axis=0, keepdims=True) + jnp.exp(sink_row - m)
            p = (p * (1.0 / den)).astype(BF16)
            o_all = _tn_dot(p, vj)
            o = [o_all[h * WINDOW:(h + 1) * WINDOW, :] for h in range(4)]
            mix_scr[rows, 0:LANES] = jnp.where(lo, o[0], pltpu.roll(o[1], HEAD_DIM, axis=1)).astype(BF16)
            mix_scr[rows, LANES:2 * LANES] = jnp.where(lo, pltpu.roll(o[2], HEAD_DIM, axis=1), o[3]).astype(BF16)
            queue.issue()
        kext_scr[0:WINDOW, :] = kext_scr[hb:hb + WINDOW, :]
        vext_scr[0:WINDOW, :] = vext_scr[hb:hb + WINDOW, :]

        hq_scr[...] = _silu_tanh(proj_scr[sub, C_BQ:C_BQ + W_GRP])
        queue.issue()
        sig_pos, sig_neg = _sigmoid_pair(proj_scr[sub, C_BF:C_BF + W_GRP])
        g = jnp.log(sig_pos + lbf * sig_neg)
        hk_scr[...] = one_m_lb * sig_neg
        g_hi = g.astype(BF16)
        ghl_scr[:, 0:W_GRP] = g_hi
        ghl_scr[:, W_GRP:2 * W_GRP] = (g - g_hi.astype(F32)).astype(BF16)
        hv_scr[...] = proj_scr[sub, C_BI:C_BI + W_GRP].astype(BF16)
        for c in range(n_ch):
            cum2 = _dot(tril, ghl_scr[chunk(c), :])
            cum_scr[chunk(c), :] = cum2[:, 0:W_GRP] + cum2[:, W_GRP:2 * W_GRP]
        for c in range(n_ch):
            cum = cum_scr[chunk(c), :]
            ref = cum[HG_MID:HG_MID + 1, :]
            tot = cum[HG_CHUNK - 1:HG_CHUNK, :]
            span = jnp.maximum(span, jnp.maximum(cum[0:1, :] - ref, ref - tot))
            qp = hq_scr[chunk(c), :] * jnp.exp(cum - ref)
            kp = hk_scr[chunk(c), :] * jnp.exp(ref - cum)
            qp_scr[chunk(c), :] = qp.astype(BF16)
            kp_scr[chunk(c), :] = kp.astype(BF16)
            qs_scr[chunk(c), :] = (qp * jnp.exp(ref)).astype(BF16)
            ks_scr[chunk(c), :] = (kp * jnp.exp(tot - ref)).astype(BF16)
            dec_scr[c] = jnp.broadcast_to(jnp.exp(tot), (SUBLANES, W_GRP))
        queue.issue()
        for c in range(n_ch):
            bk = jnp.where(same_head_b, jnp.concatenate([kp_scr[chunk(c), :]] * N_HG, axis=0), zero_b)
            a = jnp.where(causal, _nt_dot(qp_scr[chunk(c), :], bk), 0.0)
            a_scr[chunk(c), :] = a.astype(BF16)
        for c in range(n_ch):
            inc_scr[c] = jnp.where(same_head, _tn_dot(hv_scr[chunk(c), :], ks_scr[chunk(c), :]), 0.0)
        for c in range(n_ch):
            bv = jnp.where(same_head_b, jnp.concatenate([hv_scr[chunk(c), :]] * N_HG, axis=0), zero_b)
            hg_scr[chunk(c), :] = _dot(a_scr[chunk(c), :], bv)
        st = st_scr[...]
        for c in range(n_ch):
            stb_scr[c] = st.astype(BF16)
            st = st * dec_scr[c, 0:1, :] + inc_scr[c]
        st_scr[...] = st
        queue.issue()
        for c in range(n_ch):
            hg_scr[chunk(c), :] = hg_scr[chunk(c), :] + _nt_dot(qs_scr[chunk(c), :], stb_scr[c])

        hgrn_output(r)
        queue.issue()

        u_scr[SUBLANES:SUBLANES + hb, :] = proj_scr[sub, C_CC:C_CC + W_GRP] * proj_scr[sub, C_CH:C_CH + W_GRP]
        yc = (u_scr[SUBLANES - 2:SUBLANES - 2 + hb, :] * cw[0:1, :]
              + u_scr[SUBLANES - 1:SUBLANES - 1 + hb, :] * cw[1:2, :]
              + u_scr[SUBLANES:SUBLANES + hb, :] * cw[2:3, :])
        mix_scr[sub, 2 * W_GRP:3 * W_GRP] = (proj_scr[sub, C_CB:C_CB + W_GRP] * yc).astype(BF16)
        u_scr[0:SUBLANES, :] = u_scr[hb:hb + SUBLANES, :]
        queue.issue()

        dv = proj_scr[sub, C_DV:C_DV + W_GRP]
        p_scr[16:16 + hb, :] = dv
        ext = p_scr[...]
        s2 = ext + pltpu.roll(ext, 1, axis=0)
        s4 = s2 + pltpu.roll(s2, 2, axis=0)
        s8 = s4 + pltpu.roll(s4, 4, axis=0)
        s16 = s8 + pltpu.roll(s8, 8, axis=0)
        win = jnp.where(grp == 0, s2[16:], jnp.where(grp == 1, s4[16:], jnp.where(grp == 2, s8[16:], s16[16:])))
        cnt = jnp.minimum(ti * tb + r0 + rp + 1, width).astype(F32)
        pooled = win / cnt - dv
        yd = _dot(pooled.astype(BF16), pw_ref[...]) * ps_ref[...]
        mix_scr[sub, 3 * W_GRP:4 * W_GRP] = yd.astype(BF16)
        p_scr[0:16, :] = p_scr[hb:hb + 16, :]
        queue.flush()
        if r >= 1:
            finish_ln(r - 1)

    for cols in out_cols:
        finish(SUB_BLOCKS - 1, *cols)
    finish_ln(SUB_BLOCKS - 1)

    @pl.when(jnp.max(span) > HG_MAX_EXPONENT)
    def _():
        st_scr[...] = st0_scr[...]
        sub8 = lax.broadcasted_iota(jnp.int32, (SUBLANES, W_GRP), 0)
        for r in range(SUB_BLOCKS):
            def tile_body(i, carry, r=r):
                rows8 = pl.ds(pl.multiple_of(i * SUBLANES, SUBLANES), SUBLANES)
                prow = pl.ds(pl.multiple_of(r * hb + i * SUBLANES, SUBLANES), SUBLANES)
                sp8, sn8 = _sigmoid_pair(proj_scr[prow, C_BF:C_BF + W_GRP])
                f8 = sp8 + lbf * sn8
                k8 = (one_m_lb * sn8).astype(BF16)
                v8 = proj_scr[prow, C_BI:C_BI + W_GRP]
                q8 = _silu_tanh(proj_scr[prow, C_BQ:C_BQ + W_GRP]).astype(BF16)
                o8 = jnp.zeros((SUBLANES, W_GRP), F32)
                for j in range(SUBLANES):
                    vj = jnp.where(sub8 == j, v8, 0.0).astype(BF16)
                    stj = st_scr[...] * f8[j:j + 1, :] + jnp.where(same_head, _tn_dot(vj, k8), 0.0)
                    st_scr[...] = stj
                    o8 = jnp.where(sub8 == j, _nt_dot(q8, stj.astype(BF16)), o8)
                hg_scr[rows8, :] = o8
                return carry

            lax.fori_loop(0, hb // SUBLANES, tile_body, 0)
            hgrn_output(r)
            for cols in out_cols:
                finish(r, *cols)
            finish_ln(r, redo=True)

    @pl.when(last)
    def _():
        knew_ref[...] = proj_scr[tb - WINDOW:tb, C_AK:C_AK + LANES]
        vnew_ref[...] = proj_scr[tb - WINDOW:tb, C_AV:C_AV + LANES]
        s_t = st_scr[...].T
        for h in range(N_HG):
            snew_ref[h * HEAD_DIM:(h + 1) * HEAD_DIM, :] = (
                s_t[h * HEAD_DIM:(h + 1) * HEAD_DIM, h * HEAD_DIM:(h + 1) * HEAD_DIM])
        cnew_ref[...] = u_scr[SUBLANES - 2:SUBLANES, :]
        pnew_ref[...] = p_scr[1:16, :]


def _prompt_mixer_call(l, x, w):
    bsz, t, _ = x.shape
    tb = min(2 * TOKEN_BLOCK, t)
    hb = tb // SUB_BLOCKS
    n_ch = hb // HG_CHUNK
    row = lambda i, j: (i, 0, 0)
    out_shape = (
        jax.ShapeDtypeStruct((bsz, t, D_MODEL), F32),
        jax.ShapeDtypeStruct((bsz, WINDOW, LANES), F32),
        jax.ShapeDtypeStruct((bsz, WINDOW, LANES), F32),
        jax.ShapeDtypeStruct((bsz, W_GRP, HEAD_DIM), F32),
        jax.ShapeDtypeStruct((bsz, CONV_W - 1, W_GRP), F32),
        jax.ShapeDtypeStruct((bsz, POOL_BUF, W_GRP), F32),
    )
    f32 = lambda *shape: pltpu.VMEM(shape, F32)
    bf16 = lambda *shape: pltpu.VMEM(shape, BF16)
    return pl.pallas_call(
        functools.partial(_prompt_mixer_kernel, l),
        name="mixer_prompt",
        grid=(bsz, t // tb),
        in_specs=[pl.BlockSpec((None, tb, D_MODEL), lambda i, j: (i, j, 0)),
                  _full_spec((1, D_MODEL)), _full_spec((1, D_MODEL)),
                  _layer_spec((D_MODEL, D_IN), l, single=True),
                  pl.BlockSpec(memory_space=pltpu.SMEM),
                  _layer_spec((1, W_GRP), l), _layer_spec((1, W_GRP), l), _layer_spec((CONV_W, W_GRP), l),
                  _layer_spec((W_GRP, W_GRP), l), _layer_spec((1, W_GRP), l),
                  _layer_spec((D_MODEL, D_MODEL), l, single=True),
                  _layer_spec((1, D_MODEL), l), _layer_spec((1, D_MODEL), l)],
        out_specs=(pl.BlockSpec((None, tb, D_MODEL), lambda i, j: (i, j, 0)),
                   pl.BlockSpec((None, WINDOW, LANES), row),
                   pl.BlockSpec((None, WINDOW, LANES), row),
                   pl.BlockSpec((None, W_GRP, HEAD_DIM), row),
                   pl.BlockSpec((None, CONV_W - 1, W_GRP), row),
                   pl.BlockSpec((None, POOL_BUF, W_GRP), row)),
        out_shape=out_shape,
        scratch_shapes=[
            f32(tb, D_IN),
            bf16(WINDOW + hb, LANES),
            bf16(WINDOW + hb, LANES),
            f32(W_GRP, W_GRP),
            f32(SUBLANES + hb, W_GRP),
            f32(16 + hb, W_GRP),
            f32(2 * WINDOW, 4 * WINDOW),
            bf16(tb, D_MODEL),
            f32(hb, W_GRP),
            f32(hb, W_GRP), f32(hb, W_GRP),
            bf16(hb, W_GRP),
            bf16(hb, 2 * W_GRP),
            f32(hb, W_GRP),
            bf16(hb, W_GRP), bf16(hb, W_GRP),
            bf16(hb, W_GRP), bf16(hb, W_GRP),
            f32(n_ch, SUBLANES, W_GRP),
            f32(n_ch, W_GRP, W_GRP),
            bf16(n_ch, W_GRP, W_GRP),
            bf16(hb, W_GRP),
            f32(tb, D_MODEL),
            f32(W_GRP, W_GRP),
        ],
        compiler_params=_compiler_params(("arbitrary", "arbitrary")),
    )(x, w["emb_g"], w["emb_b"], w["w_in"], w["sink"], w["lb"], w["ng"], w["cw"], w["pw"], w["ps"], w["w_o"],
      w["ln1_g"], w["ln1_b"])


def _gather_block(dst_scr, src_scr, n_tiles, gi, sb, nb):
    for t in range(DEC_SEQ):
        start = pl.multiple_of(t * nb + gi * sb, sb)
        for j in range(n_tiles):
            dst_scr[j, t * sb:(t + 1) * sb, :] = src_scr[j, pl.ds(start, sb), :]


def _scatter_block(dst_scr, src_scr, n_tiles, gi, sb, nb):
    for t in range(DEC_SEQ):
        start = pl.multiple_of(t * nb + gi * sb, sb)
        for j in range(n_tiles):
            dst_scr[pl.ds(start, sb), j * LANES:(j + 1) * LANES] = src_scr[j, t * sb:(t + 1) * sb, :]


def _sample_mixer_kernel(l, xt_ref, ck_ref, cv_ref, s_ref, cs_ref, pb_ref,
                         w_in_ref, sink_ref, lbc_ref, ngc_ref, cw_ref, pw_ref, ps_ref,
                         w_o_ref, g_ref, b_ref, *rest):
    (y_ref, knew_ref, vnew_ref, snew_ref, cnew_ref, pnew_ref,
     proj_scr, a_scr, r_scr, q_scr, f_scr, k_scr, v_scr, gate_scr, o_scr, mix_scr,
     ab_scr, oa_scr, sc_scr, sn_scr, pc_scr, pn_scr) = rest[-22:]
    gi = pl.program_id(0)
    sb = ck_ref.shape[0]
    nb = xt_ref.shape[0] // DEC_SEQ
    half = sb // 2

    @pl.when(gi == 0)
    def _():
        xt = xt_ref[...].astype(BF16)
        pa = _dot(xt, w_in_ref[:, 0:C_BQ])
        for j in range(4):
            a_scr[j] = pa[:, j * LANES:(j + 1) * LANES]
        r_scr[...] = jnp.zeros(r_scr.shape, F32)
        hgt =_dot(xt, w_in_ref[:, C_BQ:C_CB]).T
        lbc = lbc_ref[...]
        q_scr[...] = _silu_tanh(hgt[0:W_GRP, :])
        sig_pos, sig_neg = _sigmoid_pair(hgt[W_GRP:2 * W_GRP, :])
        f_scr[...] = sig_pos + jnp.maximum(lbc, LB_FLOOR) * sig_neg
        k_scr[...] = (1.0 - lbc) * sig_neg
        v_scr[...] = hgt[2 * W_GRP:3 * W_GRP, :]
        gate_scr[...] = _silu_tanh(hgt[3 * W_GRP:4 * W_GRP, :])
        o_scr[...] = jnp.zeros(o_scr.shape, F32)
        proj_scr[...] = _dot(xt, w_in_ref[:, C_CB:D_IN])
        cw = cw_ref[...]
        ps = ps_ref[...]
        cp = lax.broadcasted_iota(jnp.int32, (nb, W_GRP), 1) >> 6
        width = jnp.left_shift(2, cp)
        u = [cs_ref[0], cs_ref[1]]
        ext = [pb_ref[i] for i in range(POOL_BUF)]
        for t in range(DEC_SEQ):
            rows = slice(t * nb, (t + 1) * nb)
            u.append(proj_scr[rows, W_GRP:2 * W_GRP] * proj_scr[rows, 2 * W_GRP:3 * W_GRP])
            ext.append(proj_scr[rows, 3 * W_GRP:4 * W_GRP])
        cnew_ref[0] = u[DEC_SEQ]
        cnew_ref[1] = u[DEC_SEQ + 1]
        for i in range(POOL_BUF):
            pnew_ref[i] = ext[DEC_SEQ + i]
        for t in range(DEC_SEQ):
            rows = slice(t * nb, (t + 1) * nb)
            yc = u[t] * cw[0:1, :] + u[t + 1] * cw[1:2, :] + u[t + 2] * cw[2:3, :]
            mix_scr[rows, 2 * W_GRP:3 * W_GRP] = proj_scr[rows, 0:W_GRP] * yc
            top = POOL_BUF + t
            acc = ext[top]
            sums = {}
            for jj in range(1, 16):
                acc = acc + ext[top - jj]
                if jj + 1 in POOL_WINDOWS:
                    sums[jj + 1] = acc
            win = jnp.where(cp == 0, sums[2], jnp.where(cp == 1, sums[4], jnp.where(cp == 2, sums[8], sums[16])))
            cnt = jnp.minimum(PAST_LEN + t + 1, width).astype(F32)
            pooled = win / cnt - ext[top]
            mix_scr[rows, 3 * W_GRP:4 * W_GRP] = _dot(pooled.astype(BF16), pw_ref[...]) * ps

    _gather_block(ab_scr, a_scr, 4, gi, sb, nb)
    lane = lax.broadcasted_iota(jnp.int32, (PAIR_ROWS, LANES), 1)
    lo = lane < HEAD_DIM
    n_rows = 4 * PAIR_ROWS
    r_c = lax.broadcasted_iota(jnp.int32, (n_rows, 2 * WINDOW), 0)
    c_c = lax.broadcasted_iota(jnp.int32, (n_rows, 2 * WINDOW), 1)
    rel_c = ((r_c >> 1) & 3) + WINDOW - (c_c & (WINDOW - 1))
    ok_c = ((r_c & 1) == (c_c >> 7)) & (rel_c <= WINDOW)
    bias_c = jnp.where(ok_c, -jnp.exp2(-2.0 * ((r_c >> 3).astype(F32) + 1.0)) * rel_c.astype(F32), MASK_VALUE)
    r_n = lax.broadcasted_iota(jnp.int32, (n_rows, PAIR_ROWS), 0)
    c_n = lax.broadcasted_iota(jnp.int32, (n_rows, PAIR_ROWS), 1)
    rel_n = ((r_n >> 1) & 3) - (c_n >> 1)
    ok_n = ((r_n & 1) == (c_n & 1)) & (rel_n >= 0)
    bias_n = jnp.where(ok_n, -jnp.exp2(-2.0 * ((r_n >> 3).astype(F32) + 1.0)) * rel_n.astype(F32), MASK_VALUE)
    pair_rows = lambda p: pl.ds(p, PAIR_ROWS, stride=half)
    for p in range(half):
        q0 = ab_scr[0, pair_rows(p), :] * (HEAD_DIM ** -0.5)
        q1 = ab_scr[1, pair_rows(p), :] * (HEAD_DIM ** -0.5)
        q0r = pltpu.roll(q0, HEAD_DIM, axis=1)
        q1r = pltpu.roll(q1, HEAD_DIM, axis=1)
        zero = jnp.zeros_like(q0)
        q4 = jnp.concatenate([jnp.where(lo, q0, zero), jnp.where(lo, q0r, zero),
                              jnp.where(lo, zero, q1r), jnp.where(lo, zero, q1)], axis=0).astype(BF16)
        kt2 = jnp.concatenate([ck_ref[p], ck_ref[p + half]], axis=1).astype(BF16)
        kn8 = ab_scr[2, pair_rows(p), :].astype(BF16)
        sc_scr[n_rows * p:n_rows * (p + 1), :] = _dot(q4, kt2) + bias_c
        sn_scr[n_rows * p:n_rows * (p + 1), :] = _nt_dot(q4, kn8) + bias_n
    sink_col = _sink_select(sink_ref, l, (lax.broadcasted_iota(jnp.int32, (n_rows * half, 1), 0) >> 3) & 3)
    s_c = sc_scr[...]
    s_n = sn_scr[...]
    m = jnp.maximum(jnp.maximum(jnp.max(s_c, axis=-1, keepdims=True), jnp.max(s_n, axis=-1, keepdims=True)),
                    sink_col)
    p_c = jnp.exp(s_c - m)
    p_n = jnp.exp(s_n - m)
    inv = 1.0 / (jnp.sum(p_c, axis=-1, keepdims=True) + jnp.sum(p_n, axis=-1, keepdims=True)
                 + jnp.exp(sink_col - m))
    pc_scr[...] = (p_c * inv).astype(BF16)
    pn_scr[...] = (p_n * inv).astype(BF16)
    for p in range(half):
        vt2 = jnp.concatenate([cv_ref[p], cv_ref[p + half]], axis=1).astype(BF16)
        vn8 = ab_scr[3, pair_rows(p), :].astype(BF16)
        o_all = (_nt_dot(pc_scr[n_rows * p:n_rows * (p + 1), :], vt2)
                 + _dot(pn_scr[n_rows * p:n_rows * (p + 1), :], vn8))
        o = [o_all[h * PAIR_ROWS:(h + 1) * PAIR_ROWS, :] for h in range(4)]
        oa_scr[0, pair_rows(p), :] = jnp.where(lo, o[0], pltpu.roll(o[1], HEAD_DIM, axis=1))
        oa_scr[1, pair_rows(p), :] = jnp.where(lo, pltpu.roll(o[2], HEAD_DIM, axis=1), o[3])
    _scatter_block(mix_scr, oa_scr, 2, gi, sb, nb)
    lane_w = lax.broadcasted_iota(jnp.int32, (LANES, LANES), 1)
    for tile, old_ref, out_ref in ((2, ck_ref, knew_ref), (3, cv_ref, vnew_ref)):
        for t in range(DEC_SEQ):
            r_scr[pl.ds(t, sb, stride=SUBLANES), :] = ab_scr[tile, t * sb:(t + 1) * sb, :]
        new_t = r_scr[...].T
        for b in range(sb):
            new_cols = pltpu.roll(new_t, (WINDOW - DEC_SEQ - SUBLANES * b) % LANES, axis=1)
            old_cols = pltpu.roll(old_ref[b], WINDOW - DEC_SEQ, axis=1)
            out_ref[b] = jnp.where(lane_w >= WINDOW - DEC_SEQ, new_cols, old_cols)

    k_per_step = s_ref.shape[0] // HEAD_DIM
    steps_per_head = HEAD_DIM // k_per_step
    head = gi // steps_per_head
    k_base = head * HEAD_DIM + (gi % steps_per_head) * k_per_step
    v_rows = pl.ds(pl.multiple_of(head * HEAD_DIM, HEAD_DIM), HEAD_DIM)

    def k_body(k8, accs):
        rows8 = pl.ds(pl.multiple_of(k_base + k8 * SUBLANES, SUBLANES), SUBLANES)
        accs = list(accs)
        for j in range(SUBLANES):
            st_rows = pl.ds(pl.multiple_of((k8 * SUBLANES + j) * HEAD_DIM, HEAD_DIM), HEAD_DIM)
            st = s_ref[st_rows, :]
            for t in range(DEC_SEQ):
                cols = slice(t * nb, (t + 1) * nb)
                f8, k8v, q8 = f_scr[rows8, cols], k_scr[rows8, cols], q_scr[rows8, cols]
                st = f8[j:j + 1, :] * st + k8v[j:j + 1, :] * v_scr[v_rows, cols]
                accs[t] = accs[t] + q8[j:j + 1, :] * st
            snew_ref[st_rows, :] = st
        return tuple(accs)

    zero_acc = jnp.zeros((HEAD_DIM, nb), F32)
    accs = lax.fori_loop(0, k_per_step // SUBLANES, k_body, (zero_acc,) * DEC_SEQ)
    for t in range(DEC_SEQ):
        cols = slice(t * nb, (t + 1) * nb)
        o_scr[v_rows, cols] = o_scr[v_rows, cols] + accs[t]

    @pl.when(gi == pl.num_programs(0) - 1)
    def _():
        o = o_scr[...]
        parts = []
        for h in range(N_HG):
            oh = o[h * HEAD_DIM:(h + 1) * HEAD_DIM, :]
            parts.append(oh * lax.rsqrt(jnp.mean(oh * oh, axis=0, keepdims=True) + RMS_EPS))
        ob = jnp.concatenate(parts, axis=0) * ngc_ref[...] * gate_scr[...]
        y = (_dot(mix_scr[:, 0:W_GRP].astype(BF16), w_o_ref[0:W_GRP, :])
             + _tn_dot(ob.astype(BF16), w_o_ref[W_GRP:2 * W_GRP, :])
             + _dot(mix_scr[:, 2 * W_GRP:4 * W_GRP].astype(BF16), w_o_ref[2 * W_GRP:4 * W_GRP, :]))
        y_ref[...] = _layer_norm(ALPHA * xt_ref[...] + y, g_ref[...], b_ref[...])


def _sample_mixer_call(l, xt, ck, cv, s, cs_tm, pb_tm, w, prev):
    m = xt.shape[0]
    nb = m // DEC_SEQ
    sb = SEQ_BLOCK
    n_steps = nb // sb
    n_sc = 4 * PAIR_ROWS * (sb // 2)
    s_rows = HG_ROWS // n_steps
    single = pl.Buffered(1)
    const2 = lambda i: (0, 0)
    out_shape = (
        jax.ShapeDtypeStruct((m, D_MODEL), F32),
        jax.ShapeDtypeStruct((DEPTH, nb, LANES, WINDOW), F32),
        jax.ShapeDtypeStruct((DEPTH, nb, LANES, WINDOW), F32),
        jax.ShapeDtypeStruct((DEPTH, HG_ROWS, nb), F32),
        jax.ShapeDtypeStruct((CONV_W - 1, nb, W_GRP), F32),
        jax.ShapeDtypeStruct((POOL_BUF, nb, W_GRP), F32),
    )
    seq_blk = lambda i: (l, i, 0, 0)
    in_specs = [pl.BlockSpec((m, D_MODEL), const2, pipeline_mode=single),
                pl.BlockSpec((None, sb, LANES, WINDOW), seq_blk),
                pl.BlockSpec((None, sb, LANES, WINDOW), seq_blk),
                pl.BlockSpec((None, s_rows, nb), lambda i: (l, i, 0)),
                _layer_spec((CONV_W - 1, nb, W_GRP), l, single=True),
                _layer_spec((POOL_BUF, nb, W_GRP), l, single=True),
                _layer_spec((D_MODEL, D_IN), l, single=True),
                pl.BlockSpec(memory_space=pltpu.SMEM),
                _layer_spec((W_GRP, 1), l), _layer_spec((W_GRP, 1), l),
                _layer_spec((CONV_W, W_GRP), l), _layer_spec((W_GRP, W_GRP), l), _layer_spec((1, W_GRP), l),
                _layer_spec((D_MODEL, D_MODEL), l, single=True),
                _layer_spec((1, D_MODEL), l), _layer_spec((1, D_MODEL), l)]
    args = [xt, ck, cv, s, cs_tm, pb_tm, w["w_in"], w["sink"], w["lb_col"], w["ng_col"],
            w["cw"], w["pw"], w["ps"], w["w_o"], w["ln1_g"], w["ln1_b"]]
    aliases = {}
    if prev is not None:
        for j, buf in enumerate(prev):
            aliases[len(args)] = 1 + j
            args.append(buf)
            in_specs.append(pl.BlockSpec(memory_space=pl.ANY))
    f32 = lambda *shape: pltpu.VMEM(shape, F32)
    return pl.pallas_call(
        functools.partial(_sample_mixer_kernel, l),
        name="mixer_sample",
        grid=(n_steps,),
        in_specs=in_specs,
        out_specs=(pl.BlockSpec((m, D_MODEL), const2),
                   pl.BlockSpec((None, sb, LANES, WINDOW), seq_blk),
                   pl.BlockSpec((None, sb, LANES, WINDOW), seq_blk),
                   pl.BlockSpec((None, s_rows, nb), lambda i: (l, i, 0)),
                   pl.BlockSpec((CONV_W - 1, nb, W_GRP), lambda i: (0, 0, 0)),
                   pl.BlockSpec((POOL_BUF, nb, W_GRP), lambda i: (0, 0, 0))),
        out_shape=out_shape,
        input_output_aliases=aliases,
        scratch_shapes=[f32(m, 4 * W_GRP),
                        f32(4, m, LANES),
                        f32(SUBLANES * sb, LANES),
                        f32(W_GRP, m), f32(W_GRP, m), f32(W_GRP, m), f32(W_GRP, m), f32(W_GRP, m),
                        f32(W_GRP, m),
                        f32(m, D_MODEL),
                        f32(4, sb * DEC_SEQ, LANES), f32(2, sb * DEC_SEQ, LANES),
                        f32(n_sc, 2 * WINDOW), f32(n_sc, PAIR_ROWS),
                        pltpu.VMEM((n_sc, 2 * WINDOW), BF16), pltpu.VMEM((n_sc, PAIR_ROWS), BF16)],
        compiler_params=_compiler_params(("arbitrary",)),
    )(*args)


def _xattn_sample_kernel(xt_ref, mk_ref, mv_ref, wq_ref, wo_ref, g_ref, b_ref, o_ref,
                         q_scr, qb_scr, ob_scr, att_scr, s_scr, p_scr):
    gi = pl.program_id(0)
    sb = mk_ref.shape[0]
    nb = xt_ref.shape[0] // DEC_SEQ
    half = sb // 2
    n_rows = N_XH * PAIR_ROWS

    @pl.when(gi == 0)
    def _():
        q = _dot(xt_ref[...].astype(BF16), wq_ref[...]) * (HEAD_DIM ** -0.5)
        q_scr[0] = q[:, 0:LANES]
        q_scr[1] = q[:, LANES:2 * LANES]

    _gather_block(qb_scr, q_scr, 2, gi, sb, nb)
    lane_head = lax.broadcasted_iota(jnp.int32, (PAIR_ROWS, D_X), 1) >> 6
    r_s = lax.broadcasted_iota(jnp.int32, (n_rows, 2 * N_MEM), 0)
    c_s = lax.broadcasted_iota(jnp.int32, (n_rows, 2 * N_MEM), 1)
    own = (r_s & 1) == (c_s >> 8)
    pair_rows = lambda p: pl.ds(p, PAIR_ROWS, stride=half)
    for p in range(half):
        q8 = jnp.concatenate([qb_scr[0, pair_rows(p), :], qb_scr[1, pair_rows(p), :]], axis=1)
        zero = jnp.zeros_like(q8)
        q4 = jnp.concatenate([jnp.where(lane_head == h, q8, zero) for h in range(N_XH)], axis=0).astype(BF16)
        kt2 = jnp.concatenate([mk_ref[p], mk_ref[p + half]], axis=1).astype(BF16)
        s_scr[n_rows * p:n_rows * (p + 1), :] = jnp.where(own, _dot(q4, kt2), MASK_VALUE)
    s_all = s_scr[...]
    m = jnp.max(s_all, axis=-1, keepdims=True)
    pr = jnp.exp(s_all - m)
    p_scr[...] = (pr * (1.0 / jnp.sum(pr, axis=-1, keepdims=True))).astype(BF16)
    for p in range(half):
        vt2 = jnp.concatenate([mv_ref[p], mv_ref[p + half]], axis=1).astype(BF16)
        o_all = _nt_dot(p_scr[n_rows * p:n_rows * (p + 1), :], vt2)
        o8 = jnp.zeros((PAIR_ROWS, D_X), F32)
        for h in range(N_XH):
            o8 = jnp.where(lane_head == h, o_all[h * PAIR_ROWS:(h + 1) * PAIR_ROWS, :], o8)
        ob_scr[0, pair_rows(p), :] = o8[:, 0:LANES]
        ob_scr[1, pair_rows(p), :] = o8[:, LANES:2 * LANES]
    _scatter_block(att_scr, ob_scr, 2, gi, sb, nb)

    @pl.when(gi == pl.num_programs(0) - 1)
    def _():
        y = _dot(att_scr[...].astype(BF16), wo_ref[...])
        o_ref[...] = _layer_norm(ALPHA * xt_ref[...] + y, g_ref[...], b_ref[...])


def _xattn_sample_call(l, xt, mk, mv, w):
    m = xt.shape[0]
    nb = m // DEC_SEQ
    sb = SEQ_BLOCK
    n_sc = N_XH * PAIR_ROWS * (sb // 2)
    const2 = lambda i: (0, 0)
    return pl.pallas_call(
        _xattn_sample_kernel,
        name="xattn_sample",
        grid=(nb // sb,),
        in_specs=[pl.BlockSpec((m, D_MODEL), const2, pipeline_mode=pl.Buffered(1)),
                  pl.BlockSpec((None, sb, D_X, N_MEM), lambda i: (l, i, 0, 0)),
                  pl.BlockSpec((None, sb, D_X, N_MEM), lambda i: (l, i, 0, 0)),
                  _layer_spec((D_MODEL, D_X), l), _layer_spec((D_X, D_MODEL), l),
                  _layer_spec((1, D_MODEL), l), _layer_spec((1, D_MODEL), l)],
        out_specs=pl.BlockSpec((m, D_MODEL), const2),
        out_shape=jax.ShapeDtypeStruct((m, D_MODEL), F32),
        scratch_shapes=[pltpu.VMEM((2, m, LANES), F32),
                        pltpu.VMEM((2, sb * DEC_SEQ, LANES), F32),
                        pltpu.VMEM((2, sb * DEC_SEQ, LANES), F32),
                        pltpu.VMEM((m, D_X), F32),
                        pltpu.VMEM((n_sc, 2 * N_MEM), F32),
                        pltpu.VMEM((n_sc, 2 * N_MEM), BF16)],
        compiler_params=_compiler_params(("arbitrary",)),
    )(xt, mk, mv, w["w_xq"], w["w_xo"], w["ln2_g"], w["ln2_b"])


def _hgrn_lower_bounds(lb_param):
    p = jax.nn.softmax(lb_param.astype(F32), axis=0)
    return jnp.cumsum(p, axis=0) - p[0:1]


def _prepare_weights(w_in, attn_sink, hgrn_lb, hgrn_norm_g, conv_w, pool_w, pool_scale, w_o, ln1_g, ln1_b,
                     w_xq, w_xo, ln2_g, ln2_b, w_gate, w_up, w_down, ln3_g, ln3_b):
    bf = lambda a: a.astype(BF16)
    rows = lambda a: a.astype(F32)[:, None, :]
    cols = lambda a: a.astype(F32)[:, :, None]
    lb_all = _hgrn_lower_bounds(hgrn_lb)
    eye = jnp.eye(len(POOL_WINDOWS), dtype=pool_w.dtype)
    pw = (pool_w[:, :, :, None, :] * eye[None, :, None, :, None]).reshape(DEPTH, W_GRP, W_GRP)
    return dict(
        w_in=bf(w_in),
        sink=attn_sink.astype(F32),
        lb=rows(lb_all), lb_col=cols(lb_all), ng=rows(hgrn_norm_g), ng_col=cols(hgrn_norm_g),
        cw=conv_w.astype(F32), pw=bf(pw), ps=rows(pool_scale),
        w_o=bf(w_o), ln1_g=rows(ln1_g), ln1_b=rows(ln1_b),
        w_xq=bf(w_xq), w_xo=bf(w_xo), ln2_g=rows(ln2_g), ln2_b=rows(ln2_b),
        w_gate=bf(w_gate), w_up=bf(w_up), w_down=bf(w_down), ln3_g=rows(ln3_g), ln3_b=rows(ln3_b))


def _prompt_layer(l, x, mkv, w):
    bsz, t, _ = x.shape
    x, kn, vn, sn, cn, pn = _prompt_mixer_call(l, x, w)
    x = _xattn_prompt_call(l, x, mkv, w)
    x = _ffn_call(l, x.reshape(bsz * t, D_MODEL), w).reshape(bsz, t, D_MODEL)
    return x, kn, vn, sn, cn, pn


def _sample_layer(l, xt, ck, cv, s, cs_tm, pb_tm, mk, mv, w, prev):
    xt, kn, vn, sn, cn, pn = _sample_mixer_call(l, xt, ck, cv, s, cs_tm, pb_tm, w, prev)
    xt = _xattn_sample_call(l, xt, mk, mv, w)
    xt = _ffn_call(l, xt, w)
    return xt, kn, vn, sn, cn, pn


def kernel(x_prompt, x_sample, cache_swa_k, cache_swa_v, state_hgrn, state_conv, state_pool, cache_mem_k,
           cache_mem_v, mem_prompt, emb_ln_g, emb_ln_b, w_in, attn_sink, hgrn_lb, hgrn_norm_g, conv_w, pool_w,
           pool_scale, w_o, ln1_g, ln1_b, w_xq, w_xk, w_xv, w_xo, ln2_g, ln2_b, w_gate, w_up, w_down, ln3_g,
           ln3_b):
    bp, t, _ = x_prompt.shape
    bs, ts, _ = x_sample.shape
    w = _prepare_weights(w_in, attn_sink, hgrn_lb, hgrn_norm_g, conv_w, pool_w, pool_scale, w_o, ln1_g, ln1_b,
                         w_xq, w_xo, ln2_g, ln2_b, w_gate, w_up, w_down, ln3_g, ln3_b)
    emb_g = emb_ln_g.reshape(1, D_MODEL).astype(F32)
    emb_b = emb_ln_b.reshape(1, D_MODEL).astype(F32)
    w = dict(w, emb_g=emb_g, emb_b=emb_b)
    hp = x_prompt
    w_kv = jnp.concatenate([w_xk, w_xv], axis=2).transpose(1, 0, 2).reshape(D_MODEL, DEPTH * 2 * D_X).astype(BF16)
    mkv = _matmul_call(mem_prompt.reshape(bp * N_MEM, D_MODEL), w_kv).reshape(bp, N_MEM, DEPTH * 2 * D_X)
    hs = _ln_call(x_sample.transpose(1, 0, 2).reshape(ts * bs, D_MODEL), emb_g, emb_b)
    ck = cache_swa_k.transpose(0, 1, 3, 4, 2).reshape(DEPTH, bs, LANES, WINDOW)
    cv = cache_swa_v.transpose(0, 1, 3, 4, 2).reshape(DEPTH, bs, LANES, WINDOW)
    st = state_hgrn.transpose(0, 2, 3, 4, 1).reshape(DEPTH, HG_ROWS, bs)
    cs_tm = state_conv.transpose(0, 2, 1, 3)
    pb_tm = state_pool.transpose(0, 2, 1, 3)
    mk_s = cache_mem_k.transpose(0, 1, 3, 4, 2).reshape(DEPTH, bs, D_X, N_MEM)
    mv_s = cache_mem_v.transpose(0, 1, 3, 4, 2).reshape(DEPTH, bs, D_X, N_MEM)
    outs = [[] for _ in range(5)]
    souts = [[] for _ in range(2)]
    prev = None
    for l in range(DEPTH):
        res = _prompt_layer(l, hp, mkv, w)
        hp = res[0]
        for acc, r in zip(outs, res[1:]):
            acc.append(r)
        sres = _sample_layer(l, hs, ck, cv, st, cs_tm, pb_tm, mk_s, mv_s, w, prev)
        hs = sres[0]
        prev = sres[1:4]
        for acc, r in zip(souts, sres[4:]):
            acc.append(r)
    pk, pv, ps, pc, pp = [jnp.stack(o) for o in outs]
    sk, sv, ss = prev
    sc, sp = [jnp.stack(o) for o in souts]
    mem_out = mkv.reshape(bp, N_MEM, DEPTH, 2, N_XH, HEAD_DIM).transpose(3, 2, 0, 1, 4, 5)
    swa_out = lambda a: a.reshape(DEPTH, bs, N_KV, HEAD_DIM, WINDOW).transpose(0, 1, 4, 2, 3)
    return (hp, hs.reshape(ts, bs, D_MODEL).transpose(1, 0, 2),
            pk.reshape(DEPTH, bp, WINDOW, N_KV, HEAD_DIM), pv.reshape(DEPTH, bp, WINDOW, N_KV, HEAD_DIM),
            ps.reshape(DEPTH, bp, N_HG, HEAD_DIM, HEAD_DIM), pc, pp,
            mem_out[0], mem_out[1],
            swa_out(sk), swa_out(sv),
            ss.reshape(DEPTH, N_HG, HEAD_DIM, HEAD_DIM, bs).transpose(0, 4, 1, 2, 3),
            sc.transpose(0, 2, 1, 3), sp.transpose(0, 2, 1, 3))
```
